```python
import jax
import jax.numpy as jnp
from jax import lax

D_MODEL = 1024
BATCH = 16
SEQ = 256
DEPTH = 1
DEC_BATCH = 4
DEC_SEQ = 2048
PAST_LEN = 256

GRID_W = 64
N_RET_HEADS = 4
RET_DK = 256
RET_DV = 512
D_RET_QK = N_RET_HEADS * RET_DK
D_RET_V = N_RET_HEADS * RET_DV
RET_CHUNK = 128
N_LRU_BLOCKS = 10
LRU_BLOCK = 128
D_LRU = N_LRU_BLOCKS * LRU_BLOCK
LRU_C = 8.0
CONV_W = 4
CONV_LEFT = 2
EPS = 1e-6
D_IN = 2 * D_RET_QK + 2 * D_RET_V + 2 * D_LRU + 2 * D_MODEL

kernel_name = 'hybrid_retention_rglru_diffusion_step'

F32 = jnp.float32


def rmsnorm(x, w):
    x32 = x.astype(F32)
    return x32 * lax.rsqrt(jnp.mean(x32 * x32, axis=-1, keepdims=True) + EPS) * w.astype(F32)


def short_conv(x, w, b):
    L = x.shape[-2]
    pad = [(0, 0)] * (x.ndim - 2) + [(CONV_LEFT, CONV_W - 1 - CONV_LEFT), (0, 0)]
    xp = jnp.pad(x, pad)
    out = b.astype(F32)
    for t in range(CONV_W):
        out = out + xp[..., t:t + L, :] * w[t].astype(F32)
    return out


def retention_dir(q, k, v, log_g, s0, strict):
    B, L, H, DK = q.shape
    DV = v.shape[-1]
    C = min(RET_CHUNK, L)
    nc = L // C
    qc = q.reshape(B, nc, C, H, DK)
    kc = k.reshape(B, nc, C, H, DK)
    vc = v.reshape(B, nc, C, H, DV)
    pos = jnp.arange(C, dtype=F32)
    diff = pos[:, None] - pos[None, :]
    mask = (diff > 0) if strict else (diff >= 0)
    decay = jnp.where(mask[None], jnp.exp(log_g[:, None, None] * jnp.where(mask, diff, 0.0)[None]), 0.0)
    scores = jnp.einsum('bnihd,bnjhd->bnhij', qc, kc) * decay
    intra = jnp.einsum('bnhij,bnjhe->bnihe', scores, vc)
    k_dec = jnp.exp((C - 1 - pos)[:, None] * log_g[None])
    kv = jnp.einsum('bnjhd,jh,bnjhe->nbhde', kc, k_dec, vc)
    c_dec = jnp.exp(C * log_g)[:, None, None]

    def step(S, kv_n):
        return c_dec * S + kv_n, S

    s_fin, s_prev = lax.scan(step, s0.astype(F32), kv)
    q_dec = jnp.exp((pos + 1)[:, None] * log_g[None])
    inter = jnp.einsum('bnihd,ih,nbhde->bnihe', qc, q_dec, s_prev)
    return (intra + inter).reshape(B, L, H, DV), s_fin


def rglru(x, wa, ba, wx, bx, a_param, h0, reverse):
    B, L, _ = x.shape
    xb = x.reshape(B, L, N_LRU_BLOCKS, LRU_BLOCK)
    r = jax.nn.sigmoid(jnp.einsum('blhi,hij->blhj', xb, wa.astype(F32)).reshape(B, L, D_LRU) + ba.astype(F32))
    gi = jax.nn.sigmoid(jnp.einsum('blhi,hij->blhj', xb, wx.astype(F32)).reshape(B, L, D_LRU) + bx.astype(F32))
    log_a = -LRU_C * r * jax.nn.softplus(-a_param.astype(F32))
    a = jnp.exp(log_a)
    b = jnp.sqrt(-jnp.expm1(2.0 * log_a)) * (gi * x)

    def combine(e1, e2):
        a1, b1 = e1
        a2, b2 = e2
        return a1 * a2, a2 * b1 + b2

    a_cum, h = lax.associative_scan(combine, (a, b), axis=1, reverse=reverse)
    h = h + a_cum * h0.astype(F32)[:, None]
    fin = h[:, 0] if reverse else h[:, -1]
    return h, fin


def branch_mixer(h, s0_ret, h0_lru, on_grid, w_in, decay_logit, gn_w, w_ret_down, conv_w, conv_b,
                 wa, ba, wx, bx, a_param, w_lru_down, w_out):
    B, L, _ = h.shape
    z = h @ w_in.astype(F32)
    sizes = [D_RET_QK, D_RET_QK, D_RET_V, D_RET_V, D_LRU, D_LRU, D_MODEL, D_MODEL]
    idx = []
    acc = 0
    for s in sizes[:-1]:
        acc += s
        idx.append(acc)
    q, k, v, g_ret, x_lru, g_lru, m_ret, m_lru = jnp.split(z, idx, axis=-1)
    q = q.reshape(B, L, N_RET_HEADS, RET_DK)
    k = k.reshape(B, L, N_RET_HEADS, RET_DK) * (RET_DK ** -0.5)
    v = v.reshape(B, L, N_RET_HEADS, RET_DV)
    log_g = jax.nn.log_sigmoid(decay_logit.astype(F32))
    o_f, s_f = retention_dir(q, k, v, log_g[0], s0_ret[:, 0], False)
    o_b, s_b = retention_dir(q[:, ::-1], k[:, ::-1], v[:, ::-1], log_g[1], s0_ret[:, 1], True)
    o = o_f + o_b[:, ::-1]
    o = o * lax.rsqrt(jnp.mean(o * o, axis=-1, keepdims=True) + EPS)
    o = o.reshape(B, L, D_RET_V) * gn_w.astype(F32) * jax.nn.silu(g_ret)
    ret_out = o @ w_ret_down.astype(F32)
    if on_grid:
        rows = L // GRID_W
        xc = short_conv(x_lru.reshape(B, rows, GRID_W, D_LRU), conv_w, conv_b).reshape(B, L, D_LRU)
    else:
        xc = short_conv(x_lru, conv_w, conv_b)
    hf, fin_f = rglru(xc, wa[0], ba[0], wx[0], bx[0], a_param[0], h0_lru[:, 0], False)
    hb, fin_b = rglru(xc, wa[1], ba[1], wx[1], bx[1], a_param[1], h0_lru[:, 1], True)
    lru_out = ((hf + hb) * jax.nn.silu(g_lru)) @ w_lru_down.astype(F32)
    merged = jax.nn.sigmoid(m_ret) * ret_out + jax.nn.sigmoid(m_lru) * lru_out
    out = merged @ w_out.astype(F32)
    return out, jnp.stack([s_f, s_b], axis=1), jnp.stack([fin_f, fin_b], axis=1)


def trunk_layer(x, cond, s0_ret, h0_lru, on_grid, norm_w, w_ada, b_ada, w_in, decay_logit, gn_w,
                w_ret_down, conv_w, conv_b, wa, ba, wx, bx, a_param, w_lru_down, w_out):
    mod = jax.nn.silu(cond.astype(F32)) @ w_ada.astype(F32) + b_ada.astype(F32)
    shift, scale, gate = jnp.split(mod, 3, axis=-1)
    h = rmsnorm(x, norm_w) * (1.0 + scale[:, None]) + shift[:, None]
    out, s_ret, s_lru = branch_mixer(h, s0_ret, h0_lru, on_grid, w_in, decay_logit, gn_w, w_ret_down,
                                     conv_w, conv_b, wa, ba, wx, bx, a_param, w_lru_down, w_out)
    return x + gate[:, None] * out, s_ret, s_lru


def setup_inputs(seed: int = 0) -> dict:
    key = jax.random.key(seed)
    ks = jax.random.split(key, 24)
    n = jax.random.normal
    heads = jnp.arange(N_RET_HEADS, dtype=F32)
    base_logit = jnp.log(jnp.exp2(5.0 + heads) - 1.0)
    u = jax.random.uniform(ks[19], (DEPTH, 2, D_LRU), F32, 0.9, 0.999)
    s = u ** (1.0 / LRU_C)
    return {
        'x_prompt': n(ks[0], (BATCH, SEQ, D_MODEL), F32),
        'x_sample': n(ks[1], (DEC_BATCH, DEC_SEQ, D_MODEL), F32),
        'state_ret': 0.5 * n(ks[2], (DEC_BATCH, DEPTH, 2, N_RET_HEADS, RET_DK, RET_DV), F32),
        'state_lru': 0.5 * n(ks[3], (DEC_BATCH, DEPTH, 2, D_LRU), F32),
        'c': n(ks[4], (DEC_BATCH, D_MODEL), F32),
        'c_ctx': n(ks[5], (D_MODEL,), F32),
        'norm_w': 1.0 + 0.05 * n(ks[6], (DEPTH, D_MODEL), F32),
        'w_ada': n(ks[7], (DEPTH, D_MODEL, 3 * D_MODEL), F32) * (0.5 * D_MODEL ** -0.5),
        'b_ada': 0.02 * n(ks[8], (DEPTH, 3 * D_MODEL), F32),
        'w_in': n(ks[9], (DEPTH, D_MODEL, D_IN), F32) * D_MODEL ** -0.5,
        'ret_decay_logit': base_logit + 0.1 * n(ks[10], (DEPTH, 2, N_RET_HEADS), F32),
        'ret_gn_w': 1.0 + 0.05 * n(ks[11], (DEPTH, D_RET_V), F32),
        'w_ret_down': n(ks[12], (DEPTH, D_RET_V, D_MODEL), F32) * D_RET_V ** -0.5,
        'conv_w': n(ks[13], (DEPTH, CONV_W, D_LRU), F32) * CONV_W ** -0.5,
        'conv_b': 0.02 * n(ks[14], (DEPTH, D_LRU), F32),
        'lru_wa': n(ks[15], (DEPTH, 2, N_LRU_BLOCKS, LRU_BLOCK, LRU_BLOCK), F32) * LRU_BLOCK ** -0.5,
        'lru_ba': 0.02 * n(ks[16], (DEPTH, 2, D_LRU), F32),
        'lru_wx': n(ks[17], (DEPTH, 2, N_LRU_BLOCKS, LRU_BLOCK, LRU_BLOCK), F32) * LRU_BLOCK ** -0.5,
        'lru_bx': 0.02 * n(ks[18], (DEPTH, 2, D_LRU), F32),
        'lru_a_param': jnp.log(s) - jnp.log1p(-s),
        'w_lru_down': n(ks[20], (DEPTH, D_LRU, D_MODEL), F32) * D_LRU ** -0.5,
        'w_out': n(ks[21], (DEPTH, D_MODEL, D_MODEL), F32) * D_MODEL ** -0.5,
        'final_norm_w': 1.0 + 0.05 * n(ks[22], (D_MODEL,), F32),
    }


def reference(x_prompt, x_sample, state_ret, state_lru, c, c_ctx, norm_w, w_ada, b_ada, w_in,
              ret_decay_logit, ret_gn_w, w_ret_down, conv_w, conv_b, lru_wa, lru_ba, lru_wx, lru_bx,
              lru_a_param, w_lru_down, w_out, final_norm_w):
    x = x_prompt.astype(F32)
    y = x_sample.astype(F32)
    b_ctx = x.shape[0]
    cond_ctx = jnp.broadcast_to(c_ctx.astype(F32), (b_ctx, D_MODEL))
    zero_ret = jnp.zeros((b_ctx, 2, N_RET_HEADS, RET_DK, RET_DV), F32)
    zero_lru = jnp.zeros((b_ctx, 2, D_LRU), F32)
    new_ret = []
    new_lru = []
    for l in range(DEPTH):
        params = (norm_w[l], w_ada[l], b_ada[l], w_in[l], ret_decay_logit[l], ret_gn_w[l], w_ret_down[l],
                  conv_w[l], conv_b[l], lru_wa[l], lru_ba[l], lru_wx[l], lru_bx[l], lru_a_param[l],
                  w_lru_down[l], w_out[l])
        x, s_ret, s_lru = trunk_layer(x, cond_ctx, zero_ret, zero_lru, False, *params)
        new_ret.append(s_ret)
        new_lru.append(s_lru)
        y, _, _ = trunk_layer(y, c, state_ret[:, l], state_lru[:, l], True, *params)
    y_prompt = rmsnorm(x, final_norm_w).astype(x_prompt.dtype)
    y_sample = rmsnorm(y, final_norm_w).astype(x_sample.dtype)
    new_state_ret = jnp.stack(new_ret, axis=1).astype(state_ret.dtype)
    new_state_lru = jnp.stack(new_lru, axis=1).astype(state_lru.dtype)
    return (y_prompt, y_sample, new_state_ret, new_state_lru)
```

```python
import functools

import jax
import jax.numpy as jnp
from jax import lax
from jax.experimental import pallas as pl
from jax.experimental.pallas import tpu as pltpu

F32 = jnp.float32
BF16 = jnp.bfloat16

D_MODEL = 1024
N_HEADS = 4
DK = 256
DV = 512
D_QK = N_HEADS * DK
D_V = N_HEADS * DV
N_LRU_BLOCKS = 10
LRU_BLOCK = 128
D_LRU = N_LRU_BLOCKS * LRU_BLOCK
LRU_C = 8.0
CONV_W = 4
CONV_LEFT = 2
GRID_W = 64
EPS = 1e-6
D_IN = 2 * D_QK + 2 * D_V + 2 * D_LRU + 2 * D_MODEL

OFF_Q = 0
OFF_K = OFF_Q + D_QK
OFF_V = OFF_K + D_QK
OFF_GRET = OFF_V + D_V
OFF_XLRU = OFF_GRET + D_V
OFF_GLRU = OFF_XLRU + D_LRU
OFF_MRET = OFF_GLRU + D_LRU
OFF_MLRU = OFF_MRET + D_MODEL

RET_CHUNK = 256
LRU_SEQS = 4
LRU_ROWS = 2 * LRU_SEQS
LRU_TC = 64
VMEM_LIMIT = 56 * 1024 * 1024


def _sigmoid(x):
    return 0.5 * jnp.tanh(0.5 * x) + 0.5


def _silu(x):
    return x * _sigmoid(x)


def _softplus(x):
    return jnp.maximum(x, 0.0) + jnp.log1p(jnp.exp(-jnp.abs(x)))


def _ada_kernel(c_ref, w_ref, b_ref, o_ref):
    cond = _silu(c_ref[...]).astype(BF16)
    o_ref[...] = jnp.dot(cond, w_ref[...].astype(BF16), preferred_element_type=F32) + b_ref[...]


def _ada(cond8, w_ada, b_ada):
    tn = 768
    return pl.pallas_call(
        _ada_kernel,
        grid=(3 * D_MODEL // tn,),
        in_specs=[pl.BlockSpec((8, D_MODEL), lambda j: (0, 0)),
                  pl.BlockSpec((D_MODEL, tn), lambda j: (0, j)),
                  pl.BlockSpec((1, tn), lambda j: (0, j))],
        out_specs=pl.BlockSpec((8, tn), lambda j: (0, j)),
        out_shape=jax.ShapeDtypeStruct((8, 3 * D_MODEL), F32),
        name="ada",
    )(cond8, w_ada, b_ada.reshape(1, -1))


def _inproj_kernel(x_ref, nw_ref, sc_ref, sh_ref, w_ref, z_ref, h_scr):
    @pl.when(pl.program_id(1) == 0)
    def _():
        x = x_ref[...]
        ms = jnp.mean(x * x, axis=-1, keepdims=True)
        hn = x * lax.rsqrt(ms + EPS) * nw_ref[...]
        h_scr[...] = (hn * (1.0 + sc_ref[0]) + sh_ref[0]).astype(BF16)

    z_ref[...] = jnp.dot(h_scr[...], w_ref[...], preferred_element_type=F32).astype(z_ref.dtype)


def _inproj(x2d, norm_w, scale, shift, w_in_b, tiles_per_cond):
    m = x2d.shape[0]
    tm, tn = 1024, 1536
    return pl.pallas_call(
        _inproj_kernel,
        grid=(m // tm, D_IN // tn),
        in_specs=[pl.BlockSpec((tm, D_MODEL), lambda i, j: (i, 0)),
                  pl.BlockSpec((1, D_MODEL), lambda i, j: (0, 0)),
                  pl.BlockSpec((1, 1, D_MODEL), lambda i, j: (i // tiles_per_cond, 0, 0)),
                  pl.BlockSpec((1, 1, D_MODEL), lambda i, j: (i // tiles_per_cond, 0, 0)),
                  pl.BlockSpec((D_MODEL, tn), lambda i, j: (0, j))],
        out_specs=pl.BlockSpec((tm, tn), lambda i, j: (i, j)),
        out_shape=jax.ShapeDtypeStruct((m, D_IN), BF16),
        scratch_shapes=[pltpu.VMEM((tm, D_MODEL), BF16)],
        compiler_params=pltpu.CompilerParams(
            dimension_semantics=("parallel", "arbitrary"), vmem_limit_bytes=VMEM_LIMIT),
        name="inproj",
    )(x2d, norm_w.reshape(1, -1), scale, shift, w_in_b)


def _dot_tn(a, b):
    return lax.dot_general(a, b, (((0,), (0,)), ((), ())), preferred_element_type=F32)


def _dot_nt(a, b):
    return lax.dot_general(a, b, (((1,), (1,)), ((), ())), preferred_element_type=F32)


def _ret_kernel(*refs, nc, has_state, emit_state):
    dl_ref, q_ref, k_ref, v_ref, g_ref, gnw_ref = refs[:6]
    pos = 6
    s0_ref = None
    if has_state:
        s0_ref = refs[pos]
        pos += 1
    o_ref = refs[pos]
    pos += 1
    sfin_ref = None
    if emit_state:
        sfin_ref = refs[pos]
        pos += 1
    ob_scr, sf_scr, sb_scr = refs[pos:pos + 3]

    c = RET_CHUNK
    head = pl.program_id(1)
    lgf = -_softplus(-jnp.full((1, 1), dl_ref[0, head], F32))
    lgb = -_softplus(-jnp.full((1, 1), dl_ref[1, head], F32))
    ii = lax.broadcasted_iota(jnp.int32, (c, c), 0)
    jj = lax.broadcasted_iota(jnp.int32, (c, c), 1)
    diff = (ii - jj).astype(F32)
    decay = jnp.exp(jnp.where(diff >= 0, lgf * diff, -lgb * diff))
    p = lax.broadcasted_iota(jnp.int32, (c, 1), 0).astype(F32)
    kscale = DK ** -0.5
    qdf = jnp.exp(lgf * (p + 1.0))
    qdb = jnp.exp(lgb * (c - p))
    kdf = jnp.exp(lgf * (c - 1.0 - p)) * kscale
    kdb = jnp.exp(lgb * p) * kscale
    cdf = jnp.exp(lgf * c)
    cdb = jnp.exp(lgb * c)
    carry_states = has_state or nc > 1

    if has_state:
        sf_scr[...] = s0_ref[0, 0, 0]
        sb_scr[...] = s0_ref[0, 1, 0]
    else:
        sf_scr[...] = jnp.zeros_like(sf_scr)
        sb_scr[...] = jnp.zeros_like(sb_scr)

    def rows_of(n):
        return pl.ds(pl.multiple_of(n * c, c), c)

    def rev_body(idx, carry):
        rows = rows_of(nc - 1 - idx)
        qn = q_ref[0, rows, :]
        if carry_states:
            ob_scr[rows, :] = jnp.dot(qn, sb_scr[...].astype(BF16), preferred_element_type=F32) * qdb
        kb = (k_ref[0, rows, :].astype(F32) * kdb).astype(BF16)
        kv = _dot_tn(kb, v_ref[0, rows, :])
        sb_scr[...] = (cdb * sb_scr[...] + kv) if carry_states else kv
        return carry

    lax.fori_loop(0, nc, rev_body, 0)

    def fwd_body(n, carry):
        rows = rows_of(n)
        qn = q_ref[0, rows, :]
        kn = k_ref[0, rows, :]
        vn = v_ref[0, rows, :]
        s = _dot_nt(qn, kn) * (decay * kscale)
        o = jnp.dot(s.astype(BF16), vn, preferred_element_type=F32)
        if carry_states:
            o = o + jnp.dot(qn, sf_scr[...].astype(BF16), preferred_element_type=F32) * qdf + ob_scr[rows, :]
        ms = jnp.mean(o * o, axis=-1, keepdims=True)
        on = o * lax.rsqrt(ms + EPS)
        gate = _silu(g_ref[0, rows, :].astype(F32))
        o_ref[0, rows, :] = (on * gnw_ref[0] * gate).astype(o_ref.dtype)
        kf = (kn.astype(F32) * kdf).astype(BF16)
        kv = _dot_tn(kf, vn)
        sf_scr[...] = (cdf * sf_scr[...] + kv) if carry_states else kv
        return carry

    lax.fori_loop(0, nc, fwd_body, 0)

    if emit_state:
        sfin_ref[0, 0, 0] = sf_scr[...]
        sfin_ref[0, 1, 0] = sb_scr[...]


def _retention(z3, decay_logit, gn_w, s0, emit_state):
    b, l, _ = z3.shape
    nc = l // RET_CHUNK
    has_state = s0 is not None
    kq, kv_ = OFF_K // DK, OFF_V // DV
    kg = OFF_GRET // DV
    in_specs = [pl.BlockSpec(memory_space=pltpu.SMEM),
                pl.BlockSpec((1, l, DK), lambda i, h: (i, 0, h)),
                pl.BlockSpec((1, l, DK), lambda i, h: (i, 0, kq + h)),
                pl.BlockSpec((1, l, DV), lambda i, h: (i, 0, kv_ + h)),
                pl.BlockSpec((1, l, DV), lambda i, h: (i, 0, kg + h)),
                pl.BlockSpec((1, 1, DV), lambda i, h: (h, 0, 0))]
    args = [decay_logit, z3, z3, z3, z3, gn_w.reshape(N_HEADS, 1, DV)]
    if has_state:
        in_specs.append(pl.BlockSpec((1, 2, 1, DK, DV), lambda i, h: (i, 0, h, 0, 0)))
        args.append(s0)
    out_specs = [pl.BlockSpec((1, l, DV), lambda i, h: (i, 0, h))]
    out_shape = [jax.ShapeDtypeStruct((b, l, D_V), BF16)]
    if emit_state:
        out_specs.append(pl.BlockSpec((1, 2, 1, DK, DV), lambda i, h: (i, 0, h, 0, 0)))
        out_shape.append(jax.ShapeDtypeStruct((b, 2, N_HEADS, DK, DV), F32))
    res = pl.pallas_call(
        functools.partial(_ret_kernel, nc=nc, has_state=has_state, emit_state=emit_state),
        grid=(b, N_HEADS),
        in_specs=in_specs,
        out_specs=out_specs,
        out_shape=out_shape,
        scratch_shapes=[pltpu.VMEM((l, DV), F32), pltpu.VMEM((DK, DV), F32), pltpu.VMEM((DK, DV), F32)],
        compiler_params=pltpu.CompilerParams(
            dimension_semantics=("parallel", "parallel"), vmem_limit_bytes=VMEM_LIMIT),
        name="retention",
    )(*args)
    return res


def _lru_kernel(x_ref, g_ref, cw_ref, cb_ref, w_ref, bias_ref, ap_ref, h0_ref, out_ref, fin_ref,
                hs_scr, a_scr, b_scr, *, steps, width):
    r8 = LRU_ROWS
    tc = LRU_TC
    rows = tc * r8
    nsp = -LRU_C * _softplus(-ap_ref[...])
    bias = bias_ref[0]
    cb = cb_ref[...]
    row_id = lax.broadcasted_iota(jnp.int32, (rows, LRU_BLOCK), 0)
    fwd_rows = (row_id & (r8 - 1)) < LRU_SEQS
    step_in_chunk = row_id >> 3

    def chunk_body(ci, h):
        t0 = ci * tc
        tw = (t0 + step_in_chunk) & (width - 1)
        xc = jnp.zeros((rows, LRU_BLOCK), F32)
        for d in range(-2, 3):
            xs = x_ref[0, pl.ds(pl.multiple_of((t0 + d + 2) * r8, r8), rows), :]
            valid = (tw + d >= 0) & (tw + d < width)
            term = jnp.where(valid, xs, 0.0).reshape(tc, r8, LRU_BLOCK) * cw_ref[d + 2][None]
            xc = xc + term.reshape(rows, LRU_BLOCK)
        xc = xc + cb
        lhs = jnp.concatenate([jnp.where(fwd_rows, xc, 0.0), jnp.where(fwd_rows, 0.0, xc)], axis=1)
        pre = jnp.dot(lhs.astype(BF16), w_ref[0], preferred_element_type=F32)
        pre = pre.reshape(tc, r8, 2 * LRU_BLOCK) + bias[None]
        r = _sigmoid(pre[:, :, :LRU_BLOCK])
        gi = _sigmoid(pre[:, :, LRU_BLOCK:])
        a = jnp.exp(nsp[None] * r)
        bco = jnp.sqrt(1.0 - a * a) * gi * xc.reshape(tc, r8, LRU_BLOCK)
        a_scr[...] = a.reshape(rows, LRU_BLOCK)
        b_scr[...] = bco.reshape(rows, LRU_BLOCK)

        def step(t, hh):
            rr = pl.ds(pl.multiple_of(t * r8, r8), r8)
            hh = a_scr[rr, :] * hh + b_scr[rr, :]
            hs_scr[pl.ds(pl.multiple_of((t0 + t) * r8, r8), r8), :] = hh
            return hh

        return lax.fori_loop(0, tc, step, h, unroll=8)

    h_fin = lax.fori_loop(0, steps // tc, chunk_body, h0_ref[0])
    fin_ref[0] = h_fin

    def fold(t, carry):
        ra = pl.ds(pl.multiple_of(t * r8, r8), r8)
        rb = pl.ds(pl.multiple_of((steps - 1 - t) * r8, r8), r8)
        s = hs_scr[ra, :] + pltpu.roll(hs_scr[rb, :], LRU_SEQS, axis=0)
        out_ref[0, ra, :] = s * _silu(g_ref[0, ra, :])
        return carry

    lax.fori_loop(0, steps // 2, fold, 0, unroll=8)


def _lru(x2p, g2, wsel, conv_b, w_blk, bias_blk, ap8, h0, steps, width):
    groups = x2p.shape[0]
    r8 = LRU_ROWS
    out, fin = pl.pallas_call(
        functools.partial(_lru_kernel, steps=steps, width=width),
        grid=(groups, N_LRU_BLOCKS),
        in_specs=[pl.BlockSpec((1, (steps + 4) * r8, LRU_BLOCK), lambda g, c: (g, 0, c)),
                  pl.BlockSpec((1, steps // 2 * r8, LRU_BLOCK), lambda g, c: (g, 0, c)),
                  pl.BlockSpec((5, r8, LRU_BLOCK), lambda g, c: (0, 0, c)),
                  pl.BlockSpec((1, LRU_BLOCK), lambda g, c: (0, c)),
                  pl.BlockSpec((1, 2 * LRU_BLOCK, 2 * LRU_BLOCK), lambda g, c: (c, 0, 0)),
                  pl.BlockSpec((1, r8, 2 * LRU_BLOCK), lambda g, c: (c, 0, 0)),
                  pl.BlockSpec((r8, LRU_BLOCK), lambda g, c: (0, c)),
                  pl.BlockSpec((1, r8, LRU_BLOCK), lambda g, c: (g, 0, c))],
        out_specs=[pl.BlockSpec((1, steps // 2 * r8, LRU_BLOCK), lambda g, c: (g, 0, c)),
                   pl.BlockSpec((1, r8, LRU_BLOCK), lambda g, c: (g, 0, c))],
        out_shape=[jax.ShapeDtypeStruct((groups, steps // 2 * r8, D_LRU), F32),
                   jax.ShapeDtypeStruct((groups, r8, D_LRU), F32)],
        scratch_shapes=[pltpu.VMEM((steps * r8, LRU_BLOCK), F32),
                        pltpu.VMEM((LRU_TC * r8, LRU_BLOCK), F32),
                        pltpu.VMEM((LRU_TC * r8, LRU_BLOCK), F32)],
        compiler_params=pltpu.CompilerParams(
            dimension_semantics=("parallel", "parallel"), vmem_limit_bytes=VMEM_LIMIT),
        name="lru",
    )(x2p, g2, wsel, conv_b.reshape(1, -1), w_blk, bias_blk, ap8, h0)
    return out, fin


def _lru_branch(z3, conv_w, conv_b, wa, ba, wx, bx, a_param, h0_state, width):
    b, l, _ = z3.shape
    groups = b // LRU_SEQS
    x = z3[:, :, OFF_XLRU:OFF_XLRU + D_LRU].astype(F32).reshape(groups, LRU_SEQS, l, D_LRU)
    g = z3[:, :, OFF_GLRU:OFF_GLRU + D_LRU].astype(F32).reshape(groups, LRU_SEQS, l, D_LRU)
    x2 = jnp.concatenate([x.transpose(0, 2, 1, 3), x[:, :, ::-1].transpose(0, 2, 1, 3)], axis=2)
    x2p = jnp.pad(x2, ((0, 0), (2, 2), (0, 0), (0, 0))).reshape(groups, (l + 4) * LRU_ROWS, D_LRU)
    half = l // 2
    g2 = jnp.concatenate([g[:, :, :half].transpose(0, 2, 1, 3),
                          g[:, :, half:][:, :, ::-1].transpose(0, 2, 1, 3)], axis=2)
    g2 = g2.reshape(groups, half * LRU_ROWS, D_LRU)

    zero_tap = jnp.zeros((D_LRU,), F32)
    taps_f = [conv_w[d + CONV_LEFT] if d + CONV_LEFT < CONV_W else zero_tap for d in range(-2, 3)]
    taps_b = [conv_w[CONV_LEFT - d] if 0 <= CONV_LEFT - d < CONV_W else zero_tap for d in range(-2, 3)]
    wsel = jnp.stack([jnp.concatenate([jnp.broadcast_to(tf, (LRU_SEQS, D_LRU)),
                                       jnp.broadcast_to(tb, (LRU_SEQS, D_LRU))], axis=0)
                      for tf, tb in zip(taps_f, taps_b)], axis=0)
    w_blk = jnp.concatenate([jnp.concatenate([wa[0], wx[0]], axis=2),
                             jnp.concatenate([wa[1], wx[1]], axis=2)], axis=1).astype(BF16)
    ba_r = ba.reshape(2, N_LRU_BLOCKS, LRU_BLOCK)
    bx_r = bx.reshape(2, N_LRU_BLOCKS, LRU_BLOCK)
    bias_dir = jnp.concatenate([ba_r, bx_r], axis=2)
    bias_blk = jnp.repeat(bias_dir.transpose(1, 0, 2), LRU_SEQS, axis=1)
    ap8 = jnp.repeat(a_param, LRU_SEQS, axis=0)
    if h0_state is None:
        h0 = jnp.zeros((groups, LRU_ROWS, D_LRU), F32)
    else:
        h0 = h0_state.reshape(groups, LRU_SEQS, 2, D_LRU).transpose(0, 2, 1, 3).reshape(groups, LRU_ROWS, D_LRU)

    out2, fin = _lru(x2p, g2, wsel, conv_b, w_blk, bias_blk, ap8, h0, l, width)
    out2 = out2.reshape(groups, half, 2, LRU_SEQS, D_LRU)
    first = out2[:, :, 0].transpose(0, 2, 1, 3)
    second = out2[:, :, 1].transpose(0, 2, 1, 3)[:, :, ::-1]
    lru_pre = jnp.concatenate([first, second], axis=2).reshape(b, l, D_LRU).astype(BF16)
    fin = fin.reshape(groups, 2, LRU_SEQS, D_LRU).transpose(0, 2, 1, 3).reshape(b, 2, D_LRU)
    return lru_pre, fin


def _tail_kernel(x_ref, o_ref, l_ref, mr0_ref, mr1_ref, ml0_ref, ml1_ref, gate_ref, fnw_ref,
                 wrd_ref, wld_ref, wo_ref, y_ref):
    ret_out = jnp.dot(o_ref[...], wrd_ref[...], preferred_element_type=F32)
    lru_out = jnp.dot(l_ref[...], wld_ref[...], preferred_element_type=F32)
    m_ret = jnp.concatenate([mr0_ref[...], mr1_ref[...]], axis=1).astype(F32)
    m_lru = jnp.concatenate([ml0_ref[...], ml1_ref[...]], axis=1).astype(F32)
    merged = _sigmoid(m_ret) * ret_out + _sigmoid(m_lru) * lru_out
    out = jnp.dot(merged.astype(BF16), wo_ref[...], preferred_element_type=F32)
    y = x_ref[...] + gate_ref[0] * out
    ms = jnp.mean(y * y, axis=-1, keepdims=True)
    y_ref[...] = y * lax.rsqrt(ms + EPS) * fnw_ref[...]


def _tail(x2d, o2d, l2d, z2d, gate, fnw, wrd_b, wld_b, wo_b, tiles_per_cond):
    m = x2d.shape[0]
    tm = 512
    half = D_MODEL // 2
    kmr, kml = OFF_MRET // half, OFF_MLRU // half
    const = lambda i: (0, 0)
    return pl.pallas_call(
        _tail_kernel,
        grid=(m // tm,),
        in_specs=[pl.BlockSpec((tm, D_MODEL), lambda i: (i, 0)),
                  pl.BlockSpec((tm, D_V), lambda i: (i, 0)),
                  pl.BlockSpec((tm, D_LRU), lambda i: (i, 0)),
                  pl.BlockSpec((tm, half), lambda i: (i, kmr)),
                  pl.BlockSpec((tm, half), lambda i: (i, kmr + 1)),
                  pl.BlockSpec((tm, half), lambda i: (i, kml)),
                  pl.BlockSpec((tm, half), lambda i: (i, kml + 1)),
                  pl.BlockSpec((1, 1, D_MODEL), lambda i: (i // tiles_per_cond, 0, 0)),
                  pl.BlockSpec((1, D_MODEL), const),
                  pl.BlockSpec((D_V, D_MODEL), const),
                  pl.BlockSpec((D_LRU, D_MODEL), const),
                  pl.BlockSpec((D_MODEL, D_MODEL), const)],
        out_specs=pl.BlockSpec((tm, D_MODEL), lambda i: (i, 0)),
        out_shape=jax.ShapeDtypeStruct((m, D_MODEL), F32),
        compiler_params=pltpu.CompilerParams(
            dimension_semantics=("parallel",), vmem_limit_bytes=VMEM_LIMIT),
        name="tail",
    )(x2d, o2d, l2d, z2d, z2d, z2d, z2d, gate, fnw.reshape(1, -1), wrd_b, wld_b, wo_b)


def _trunk(x, mod_rows, s0_ret, h0_lru, width, params, final_norm_w, emit_state):
    (norm_w, w_in_b, decay_logit, gn_w, wrd_b, conv_w, conv_b, wa, ba, wx, bx, a_param, wld_b, wo_b) = params
    b, l, _ = x.shape
    x2d = x.reshape(b * l, D_MODEL)
    n_cond = mod_rows.shape[0]
    shift = mod_rows[:, None, 0:D_MODEL]
    scale = mod_rows[:, None, D_MODEL:2 * D_MODEL]
    gate = mod_rows[:, None, 2 * D_MODEL:]
    tokens_per_cond = b * l // n_cond
    z2d = _inproj(x2d, norm_w, scale, shift, w_in_b, tokens_per_cond // 1024)
    z3 = z2d.reshape(b, l, D_IN)
    ret = _retention(z3, decay_logit, gn_w, s0_ret, emit_state)
    lru_pre, lru_fin = _lru_branch(z3, conv_w, conv_b, wa, ba, wx, bx, a_param, h0_lru, width)
    y = _tail(x2d, ret[0].reshape(b * l, D_V), lru_pre.reshape(b * l, D_LRU), z2d, gate, final_norm_w,
              wrd_b, wld_b, wo_b, tokens_per_cond // 512)
    return y.reshape(b, l, D_MODEL), (ret[1] if emit_state else None), lru_fin


def kernel(x_prompt, x_sample, state_ret, state_lru, c, c_ctx, norm_w, w_ada, b_ada, w_in, ret_decay_logit,
           ret_gn_w, w_ret_down, conv_w, conv_b, lru_wa, lru_ba, lru_wx, lru_bx, lru_a_param, w_lru_down,
           w_out, final_norm_w):
    assert norm_w.shape[0] == 1, "single-layer step"
    n_dec = c.shape[0]
    cond8 = jnp.concatenate([c.astype(F32), c_ctx.astype(F32)[None],
                             jnp.zeros((8 - n_dec - 1, D_MODEL), F32)], axis=0)
    mod = _ada(cond8, w_ada[0], b_ada[0])
    params = (norm_w[0], w_in[0].astype(BF16), ret_decay_logit[0], ret_gn_w[0], w_ret_down[0].astype(BF16),
              conv_w[0], conv_b[0], lru_wa[0], lru_ba[0], lru_wx[0], lru_bx[0], lru_a_param[0],
              w_lru_down[0].astype(BF16), w_out[0].astype(BF16))
    y_prompt, new_ret, new_lru = _trunk(x_prompt.astype(F32), mod[n_dec:n_dec + 1], None, None,
                                        x_prompt.shape[1], params, final_norm_w, True)
    y_sample, _, _ = _trunk(x_sample.astype(F32), mod[:n_dec], state_ret[:, 0], state_lru[:, 0],
                            GRID_W, params, final_norm_w, False)
    return (y_prompt.astype(x_prompt.dtype), y_sample.astype(x_sample.dtype),
            new_ret[:, None].astype(state_ret.dtype), new_lru[:, None].astype(state_lru.dtype))
```

```python
import functools

import jax
import jax.numpy as jnp
import numpy as np
from jax import lax
from jax.experimental import pallas as pl
from jax.experimental.pallas import tpu as pltpu

F32 = jnp.float32
BF16 = jnp.bfloat16

D_MODEL = 1024
N_HEADS = 4
DK = 256
DV = 512
D_QK = N_HEADS * DK
D_V = N_HEADS * DV
N_LRU_BLOCKS = 10
LRU_BLOCK = 128
D_LRU = N_LRU_BLOCKS * LRU_BLOCK
LRU_C = 8.0
CONV_W = 4
CONV_LEFT = 2
GRID_W = 64
EPS = 1e-6

OFF_Q = 0
OFF_K = OFF_Q + D_QK
OFF_V = OFF_K + D_QK
OFF_GRET = OFF_V + D_V
OFF_XLRU = OFF_GRET + D_V
OFF_GLRU = OFF_XLRU + D_LRU
OFF_MRET = OFF_GLRU + D_LRU
OFF_MLRU = OFF_MRET + D_MODEL
D_IN = OFF_MLRU + D_MODEL
D_MAIN = OFF_XLRU + 2 * D_MODEL
MAIN_MRET = OFF_XLRU
MAIN_MLRU = OFF_XLRU + D_MODEL

RET_CHUNK = 256
LRU_SEQS = 4
LRU_ROWS = 2 * LRU_SEQS
LRU_TC = 64
PERM_ROWS = LRU_TC * LRU_ROWS
VMEM_LIMIT = 56 * 1024 * 1024


def _sigmoid(x):
    return 0.5 * jnp.tanh(0.5 * x) + 0.5


def _silu(x):
    return x * _sigmoid(x)


def _softplus(x):
    return jnp.maximum(x, 0.0) + jnp.log1p(jnp.exp(-jnp.abs(x)))


def _slab_permutation():
    p = np.zeros((PERM_ROWS, PERM_ROWS), np.float32)
    for t in range(LRU_TC):
        for r in range(LRU_ROWS):
            if r < LRU_SEQS:
                src = r * LRU_TC + t
            else:
                src = LRU_SEQS * LRU_TC + (r - LRU_SEQS) * LRU_TC + (LRU_TC - 1 - t)
            p[t * LRU_ROWS + r, src] = 1.0
    return p


def _ada_kernel(c_ref, w_ref, b_ref, o_ref):
    cond = _silu(c_ref[...]).astype(BF16)
    o_ref[...] = jnp.dot(cond, w_ref[...].astype(BF16), preferred_element_type=F32) + b_ref[...]


def _ada(cond8, w_ada, b_ada):
    tn = 768
    return pl.pallas_call(
        _ada_kernel,
        grid=(3 * D_MODEL // tn,),
        in_specs=[pl.BlockSpec((8, D_MODEL), lambda j: (0, 0)),
                  pl.BlockSpec((D_MODEL, tn), lambda j: (0, j)),
                  pl.BlockSpec((1, tn), lambda j: (0, j))],
        out_specs=pl.BlockSpec((8, tn), lambda j: (0, j)),
        out_shape=jax.ShapeDtypeStruct((8, 3 * D_MODEL), F32),
        name="ada",
    )(cond8, w_ada, b_ada.reshape(1, -1))


def _modulated_norm(x, nw, scale, shift):
    ms = jnp.mean(x * x, axis=-1, keepdims=True)
    return (x * lax.rsqrt(ms + EPS) * nw) * (1.0 + scale) + shift


def _inproj_kernel(x_ref, nw_ref, sc_ref, sh_ref, w_ref, z_ref, h_scr):
    @pl.when(pl.program_id(1) == 0)
    def _():
        h_scr[...] = _modulated_norm(x_ref[...], nw_ref[...], sc_ref[0], sh_ref[0]).astype(BF16)

    z_ref[...] = jnp.dot(h_scr[...], w_ref[...], preferred_element_type=F32).astype(z_ref.dtype)


def _inproj(x2d, norm_w, scale, shift, w_main, tiles_per_cond):
    m = x2d.shape[0]
    tm, tn = 1024, 2048
    return pl.pallas_call(
        _inproj_kernel,
        grid=(m // tm, D_MAIN // tn),
        in_specs=[pl.BlockSpec((tm, D_MODEL), lambda i, j: (i, 0)),
                  pl.BlockSpec((1, D_MODEL), lambda i, j: (0, 0)),
                  pl.BlockSpec((1, 1, D_MODEL), lambda i, j: (i // tiles_per_cond, 0, 0)),
                  pl.BlockSpec((1, 1, D_MODEL), lambda i, j: (i // tiles_per_cond, 0, 0)),
                  pl.BlockSpec((D_MODEL, tn), lambda i, j: (0, j))],
        out_specs=pl.BlockSpec((tm, tn), lambda i, j: (i, j)),
        out_shape=jax.ShapeDtypeStruct((m, D_MAIN), BF16),
        scratch_shapes=[pltpu.VMEM((tm, D_MODEL), BF16)],
        compiler_params=pltpu.CompilerParams(
            dimension_semantics=("parallel", "arbitrary"), vmem_limit_bytes=VMEM_LIMIT),
        name="inproj",
    )(x2d, norm_w.reshape(1, -1), scale, shift, w_main)


def _inproj_lru_kernel(xa_ref, xb_ref, nw_ref, sc_ref, sh_ref, p_ref, w_ref, xs_ref, gs_ref):
    nw = nw_ref[...]
    sc = sc_ref[0]
    sh = sh_ref[0]
    half_rows = LRU_SEQS * LRU_TC
    ha = _modulated_norm(xa_ref[0], nw, sc, sh).astype(BF16).reshape(half_rows, D_MODEL)
    hb = _modulated_norm(xb_ref[0], nw, sc, sh).astype(BF16).reshape(half_rows, D_MODEL)
    h = jnp.concatenate([ha, hb], axis=0)
    hp = jnp.dot(p_ref[...], h, preferred_element_type=F32).astype(BF16)
    z = jnp.dot(hp, w_ref[...], preferred_element_type=F32)
    xs_ref[0] = z[:, :D_LRU].astype(xs_ref.dtype)
    gs_ref[0] = z[:, D_LRU:].astype(gs_ref.dtype)


def _inproj_lru(x4, norm_w, scale4, shift4, perm, w_lru):
    groups, _, l, _ = x4.shape
    nt = l // 2 // LRU_TC
    ntb = l // LRU_TC
    out_sds = jax.ShapeDtypeStruct((groups, l // 2 * LRU_ROWS, D_LRU), BF16)
    return pl.pallas_call(
        _inproj_lru_kernel,
        grid=(groups, nt),
        in_specs=[pl.BlockSpec((1, LRU_SEQS, LRU_TC, D_MODEL), lambda g, i: (g, 0, i, 0)),
                  pl.BlockSpec((1, LRU_SEQS, LRU_TC, D_MODEL), lambda g, i: (g, 0, ntb - 1 - i, 0)),
                  pl.BlockSpec((1, D_MODEL), lambda g, i: (0, 0)),
                  pl.BlockSpec((1, LRU_SEQS, 1, D_MODEL), lambda g, i: (0, 0, 0, 0)),
                  pl.BlockSpec((1, LRU_SEQS, 1, D_MODEL), lambda g, i: (0, 0, 0, 0)),
                  pl.BlockSpec((PERM_ROWS, PERM_ROWS), lambda g, i: (0, 0)),
                  pl.BlockSpec((D_MODEL, 2 * D_LRU), lambda g, i: (0, 0))],
        out_specs=[pl.BlockSpec((1, PERM_ROWS, D_LRU), lambda g, i: (g, i, 0)),
                   pl.BlockSpec((1, PERM_ROWS, D_LRU), lambda g, i: (g, i, 0))],
        out_shape=[out_sds, out_sds],
        compiler_params=pltpu.CompilerParams(
            dimension_semantics=("parallel", "parallel"), vmem_limit_bytes=VMEM_LIMIT),
        name="inproj_lru",
    )(x4, x4, norm_w.reshape(1, -1), scale4, shift4, perm, w_lru)


def _dot_tn(a, b):
    return lax.dot_general(a, b, (((0,), (0,)), ((), ())), preferred_element_type=F32)


def _dot_nt(a, b):
    return lax.dot_general(a, b, (((1,), (1,)), ((), ())), preferred_element_type=F32)


def _ret_kernel(*refs, nc, has_state, emit_state):
    dl_ref, q_ref, k_ref, v_ref, g_ref, gnw_ref = refs[:6]
    pos = 6
    s0_ref = None
    if has_state:
        s0_ref = refs[pos]
        pos += 1
    o_ref = refs[pos]
    pos += 1
    sfin_ref = None
    if emit_state:
        sfin_ref = refs[pos]
        pos += 1
    ob_scr, sf_scr, sb_scr = refs[pos:pos + 3]

    c = RET_CHUNK
    head = pl.program_id(1)
    lgf = -_softplus(-jnp.full((1, 1), dl_ref[0, head], F32))
    lgb = -_softplus(-jnp.full((1, 1), dl_ref[1, head], F32))
    ii = lax.broadcasted_iota(jnp.int32, (c, c), 0)
    jj = lax.broadcasted_iota(jnp.int32, (c, c), 1)
    diff = (ii - jj).astype(F32)
    decay = jnp.exp(jnp.where(diff >= 0, lgf * diff, -lgb * diff))
    p = lax.broadcasted_iota(jnp.int32, (c, 1), 0).astype(F32)
    kscale = DK ** -0.5
    qdf = jnp.exp(lgf * (p + 1.0))
    qdb = jnp.exp(lgb * (c - p))
    kdf = jnp.exp(lgf * (c - 1.0 - p)) * kscale
    kdb = jnp.exp(lgb * p) * kscale
    cdf = jnp.exp(lgf * c)
    cdb = jnp.exp(lgb * c)
    carry_states = has_state or nc > 1

    if has_state:
        sf_scr[...] = s0_ref[0, 0, 0]
        sb_scr[...] = s0_ref[0, 1, 0]
    else:
        sf_scr[...] = jnp.zeros_like(sf_scr)
        sb_scr[...] = jnp.zeros_like(sb_scr)

    def rows_of(n):
        return pl.ds(pl.multiple_of(n * c, c), c)

    def rev_body(idx, carry):
        rows = rows_of(nc - 1 - idx)
        qn = q_ref[0, rows, :]
        if carry_states:
            ob_scr[rows, :] = jnp.dot(qn, sb_scr[...].astype(BF16), preferred_element_type=F32) * qdb
        kb = (k_ref[0, rows, :].astype(F32) * kdb).astype(BF16)
        kv = _dot_tn(kb, v_ref[0, rows, :])
        sb_scr[...] = (cdb * sb_scr[...] + kv) if carry_states else kv
        return carry

    lax.fori_loop(0, nc, rev_body, 0)

    def fwd_body(n, carry):
        rows = rows_of(n)
        qn = q_ref[0, rows, :]
        kn = k_ref[0, rows, :]
        vn = v_ref[0, rows, :]
        s = _dot_nt(qn, kn) * (decay * kscale)
        o = jnp.dot(s.astype(BF16), vn, preferred_element_type=F32)
        if carry_states:
            o = o + jnp.dot(qn, sf_scr[...].astype(BF16), preferred_element_type=F32) * qdf + ob_scr[rows, :]
        ms = jnp.mean(o * o, axis=-1, keepdims=True)
        on = o * lax.rsqrt(ms + EPS)
        gate = _silu(g_ref[0, rows, :].astype(F32))
        o_ref[0, rows, :] = (on * gnw_ref[0] * gate).astype(o_ref.dtype)
        kf = (kn.astype(F32) * kdf).astype(BF16)
        kv = _dot_tn(kf, vn)
        sf_scr[...] = (cdf * sf_scr[...] + kv) if carry_states else kv
        return carry

    lax.fori_loop(0, nc, fwd_body, 0)

    if emit_state:
        sfin_ref[0, 0, 0] = sf_scr[...]
        sfin_ref[0, 1, 0] = sb_scr[...]


def _retention(z3, decay_logit, gn_w, s0, emit_state):
    b, l, _ = z3.shape
    nc = l // RET_CHUNK
    has_state = s0 is not None
    kq, kv_ = OFF_K // DK, OFF_V // DV
    kg = OFF_GRET // DV
    in_specs = [pl.BlockSpec(memory_space=pltpu.SMEM),
                pl.BlockSpec((1, l, DK), lambda i, h: (i, 0, h)),
                pl.BlockSpec((1, l, DK), lambda i, h: (i, 0, kq + h)),
                pl.BlockSpec((1, l, DV), lambda i, h: (i, 0, kv_ + h)),
                pl.BlockSpec((1, l, DV), lambda i, h: (i, 0, kg + h)),
                pl.BlockSpec((1, 1, DV), lambda i, h: (h, 0, 0))]
    args = [decay_logit, z3, z3, z3, z3, gn_w.reshape(N_HEADS, 1, DV)]
    if has_state:
        in_specs.append(pl.BlockSpec((1, 2, 1, DK, DV), lambda i, h: (i, 0, h, 0, 0)))
        args.append(s0)
    out_specs = [pl.BlockSpec((1, l, DV), lambda i, h: (i, 0, h))]
    out_shape = [jax.ShapeDtypeStruct((b, l, D_V), BF16)]
    if emit_state:
        out_specs.append(pl.BlockSpec((1, 2, 1, DK, DV), lambda i, h: (i, 0, h, 0, 0)))
        out_shape.append(jax.ShapeDtypeStruct((b, 2, N_HEADS, DK, DV), F32))
    return pl.pallas_call(
        functools.partial(_ret_kernel, nc=nc, has_state=has_state, emit_state=emit_state),
        grid=(b, N_HEADS),
        in_specs=in_specs,
        out_specs=out_specs,
        out_shape=out_shape,
        scratch_shapes=[pltpu.VMEM((l, DV), F32), pltpu.VMEM((DK, DV), F32), pltpu.VMEM((DK, DV), F32)],
        compiler_params=pltpu.CompilerParams(
            dimension_semantics=("parallel", "parallel"), vmem_limit_bytes=VMEM_LIMIT),
        name="retention",
    )(*args)


def _lru_kernel(xs_ref, gs_ref, cw_ref, cb_ref, w_ref, bias_ref, ap_ref, h0_ref, pt_ref, out_ref, fin_ref,
                xp_scr, s1_scr, a_scr, b_scr, *, slabs, seq_len, width, nblk):
    r8 = LRU_ROWS
    tc = LRU_TC
    rows = PERM_ROWS
    cw = nblk * LRU_BLOCK
    nchunks = slabs // tc

    def roll4(v):
        return pltpu.roll(v, LRU_SEQS, axis=0)

    xp_scr[pl.ds(0, 2 * r8), :] = jnp.zeros((2 * r8, cw), F32)
    xp_scr[pl.ds(2 * r8, slabs * r8), :] = xs_ref[0].astype(F32)
    xp_scr[pl.ds((slabs + 2) * r8, r8), :] = roll4(xs_ref[0, pl.ds((slabs - 1) * r8, r8), :].astype(F32))
    xp_scr[pl.ds((slabs + 3) * r8, r8), :] = roll4(xs_ref[0, pl.ds((slabs - 2) * r8, r8), :].astype(F32))

    nsp1 = -LRU_C * _softplus(-ap_ref[...])
    bias1 = bias_ref[...]
    step_in_chunk = lax.broadcasted_iota(jnp.int32, (rows, cw), 0) >> 3
    low_rows = (lax.broadcasted_iota(jnp.int32, (rows, LRU_BLOCK), 0) & (r8 - 1)) < LRU_SEQS
    high_rows = jnp.logical_not(low_rows)

    def gates(t0, sweep2):
        tw = (t0 + step_in_chunk) & (width - 1)
        xc = jnp.zeros((rows, cw), F32)
        for s in range(-2, 3):
            xs = xp_scr[pl.ds(pl.multiple_of((t0 + s + 2) * r8, r8), rows), :]
            valid = (tw + s >= 0) & (tw + s < width)
            term = jnp.where(valid, xs, 0.0).reshape(tc, r8, cw) * cw_ref[s + 2][None]
            xc = xc + term.reshape(rows, cw)
        xc = xc + cb_ref[...]
        fj = high_rows if sweep2 else low_rows
        nsp = roll4(nsp1) if sweep2 else nsp1
        for j in range(nblk):
            cols = slice(j * LRU_BLOCK, (j + 1) * LRU_BLOCK)
            xj = xc[:, cols]
            lhs = jnp.concatenate([jnp.where(fj, xj, 0.0), jnp.where(fj, 0.0, xj)], axis=1)
            pre = jnp.dot(lhs.astype(BF16), w_ref[j], preferred_element_type=F32)
            bj = roll4(bias1[j]) if sweep2 else bias1[j]
            pre = pre.reshape(tc, r8, 2 * LRU_BLOCK) + bj[None]
            r = _sigmoid(pre[:, :, :LRU_BLOCK])
            gi = _sigmoid(pre[:, :, LRU_BLOCK:])
            a = jnp.exp(nsp[:, cols][None] * r)
            bco = jnp.sqrt(1.0 - a * a) * gi * xj.reshape(tc, r8, LRU_BLOCK)
            a_scr[:, cols] = a.reshape(rows, LRU_BLOCK)
            b_scr[:, cols] = bco.reshape(rows, LRU_BLOCK)

    def slab(t):
        return pl.ds(pl.multiple_of(t * r8, r8), r8)

    def sweep1_chunk(ci, h):
        t0 = ci * tc
        gates(t0, False)

        def step(t, hh):
            hh = a_scr[slab(t), :] * hh + b_scr[slab(t), :]
            s1_scr[slab(t0 + t), :] = hh
            return hh

        return lax.fori_loop(0, tc, step, h, unroll=8)

    h_mid = lax.fori_loop(0, nchunks, sweep1_chunk, h0_ref[0])

    def sweep2_chunk(idx, h):
        ci = nchunks - 1 - idx
        t0 = ci * tc
        gates(t0, True)

        def step(i, hh):
            t = tc - 1 - i
            hh = a_scr[slab(t), :] * hh + b_scr[slab(t), :]
            s1_scr[slab(t0 + t), :] = s1_scr[slab(t0 + t), :] + hh
            return hh

        h = lax.fori_loop(0, tc, step, h, unroll=8)
        crows = pl.ds(pl.multiple_of(t0 * r8, rows), rows)
        o = (s1_scr[crows, :] * _silu(gs_ref[0, crows, :].astype(F32))).astype(BF16)
        nat = jnp.dot(pt_ref[...], o, preferred_element_type=F32).astype(out_ref.dtype)
        lo = pl.multiple_of(t0, tc)
        hi = pl.multiple_of(seq_len - tc - t0, tc)
        for b in range(LRU_SEQS):
            out_ref[0, b, pl.ds(lo, tc), :] = nat[b * tc:(b + 1) * tc]
            out_ref[0, b, pl.ds(hi, tc), :] = nat[(LRU_SEQS + b) * tc:(LRU_SEQS + b + 1) * tc]
        return h

    h_fin = lax.fori_loop(0, nchunks, sweep2_chunk, roll4(h_mid))
    fin_ref[0] = h_fin


def _lru(xs, gs, wsel, conv_b, w_blk, bias_blk, ap8, h0, perm_t, seq_len, width, nblk):
    groups = xs.shape[0]
    r8 = LRU_ROWS
    slabs = seq_len // 2
    cw = nblk * LRU_BLOCK
    return pl.pallas_call(
        functools.partial(_lru_kernel, slabs=slabs, seq_len=seq_len, width=width, nblk=nblk),
        grid=(groups, N_LRU_BLOCKS // nblk),
        in_specs=[pl.BlockSpec((1, slabs * r8, cw), lambda g, c: (g, 0, c)),
                  pl.BlockSpec((1, slabs * r8, cw), lambda g, c: (g, 0, c)),
                  pl.BlockSpec((5, r8, cw), lambda g, c: (0, 0, c)),
                  pl.BlockSpec((1, cw), lambda g, c: (0, c)),
                  pl.BlockSpec((nblk, 2 * LRU_BLOCK, 2 * LRU_BLOCK), lambda g, c: (c, 0, 0)),
                  pl.BlockSpec((nblk, r8, 2 * LRU_BLOCK), lambda g, c: (c, 0, 0)),
                  pl.BlockSpec((r8, cw), lambda g, c: (0, c)),
                  pl.BlockSpec((1, r8, cw), lambda g, c: (g, 0, c)),
                  pl.BlockSpec((PERM_ROWS, PERM_ROWS), lambda g, c: (0, 0))],
        out_specs=[pl.BlockSpec((1, LRU_SEQS, seq_len, cw), lambda g, c: (g, 0, 0, c)),
                   pl.BlockSpec((1, r8, cw), lambda g, c: (g, 0, c))],
        out_shape=[jax.ShapeDtypeStruct((groups, LRU_SEQS, seq_len, D_LRU), BF16),
                   jax.ShapeDtypeStruct((groups, r8, D_LRU), F32)],
        scratch_shapes=[pltpu.VMEM(((slabs + 4) * r8, cw), F32),
                        pltpu.VMEM((slabs * r8, cw), F32),
                        pltpu.VMEM((PERM_ROWS, cw), F32),
                        pltpu.VMEM((PERM_ROWS, cw), F32)],
        compiler_params=pltpu.CompilerParams(
            dimension_semantics=("parallel", "parallel"), vmem_limit_bytes=VMEM_LIMIT),
        name="lru",
    )(xs, gs, wsel, conv_b.reshape(1, -1), w_blk, bias_blk, ap8, h0, perm_t)


def _lru_params(conv_w, wa, ba, wx, bx, a_param):
    zero_tap = jnp.zeros((D_LRU,), F32)
    taps_lo = [conv_w[s + CONV_LEFT] if s + CONV_LEFT < CONV_W else zero_tap for s in range(-2, 3)]
    taps_hi = [conv_w[CONV_LEFT - s] if 0 <= CONV_LEFT - s < CONV_W else zero_tap for s in range(-2, 3)]
    wsel = jnp.stack([jnp.concatenate([jnp.broadcast_to(tl, (LRU_SEQS, D_LRU)),
                                       jnp.broadcast_to(th, (LRU_SEQS, D_LRU))], axis=0)
                      for tl, th in zip(taps_lo, taps_hi)], axis=0)
    w_blk = jnp.concatenate([jnp.concatenate([wa[0], wx[0]], axis=2),
                             jnp.concatenate([wa[1], wx[1]], axis=2)], axis=1).astype(BF16)
    ba_r = ba.reshape(2, N_LRU_BLOCKS, LRU_BLOCK)
    bx_r = bx.reshape(2, N_LRU_BLOCKS, LRU_BLOCK)
    bias_dir = jnp.concatenate([ba_r, bx_r], axis=2)
    bias_blk = jnp.repeat(bias_dir.transpose(1, 0, 2), LRU_SEQS, axis=1)
    ap8 = jnp.repeat(a_param, LRU_SEQS, axis=0)
    return wsel, w_blk, bias_blk, ap8


def _tail_kernel(x_ref, o_ref, l_ref, mr_ref, ml_ref, gate_ref, fnw_ref, wrd_ref, wld_ref, wo_ref, y_ref):
    ret_out = jnp.dot(o_ref[...], wrd_ref[...], preferred_element_type=F32)
    lru_out = jnp.dot(l_ref[...], wld_ref[...], preferred_element_type=F32)
    merged = _sigmoid(mr_ref[...].astype(F32)) * ret_out + _sigmoid(ml_ref[...].astype(F32)) * lru_out
    out = jnp.dot(merged.astype(BF16), wo_ref[...], preferred_element_type=F32)
    y = x_ref[...] + gate_ref[0] * out
    ms = jnp.mean(y * y, axis=-1, keepdims=True)
    y_ref[...] = y * lax.rsqrt(ms + EPS) * fnw_ref[...]


def _tail(x2d, o2d, l2d, z2d, gate, fnw, wrd_b, wld_b, wo_b, tiles_per_cond):
    m = x2d.shape[0]
    tm = 512
    kmr, kml = MAIN_MRET // D_MODEL, MAIN_MLRU // D_MODEL
    const = lambda i: (0, 0)
    return pl.pallas_call(
        _tail_kernel,
        grid=(m // tm,),
        in_specs=[pl.BlockSpec((tm, D_MODEL), lambda i: (i, 0)),
                  pl.BlockSpec((tm, D_V), lambda i: (i, 0)),
                  pl.BlockSpec((tm, D_LRU), lambda i: (i, 0)),
                  pl.BlockSpec((tm, D_MODEL), lambda i: (i, kmr)),
                  pl.BlockSpec((tm, D_MODEL), lambda i: (i, kml)),
                  pl.BlockSpec((1, 1, D_MODEL), lambda i: (i // tiles_per_cond, 0, 0)),
                  pl.BlockSpec((1, D_MODEL), const),
                  pl.BlockSpec((D_V, D_MODEL), const),
                  pl.BlockSpec((D_LRU, D_MODEL), const),
                  pl.BlockSpec((D_MODEL, D_MODEL), const)],
        out_specs=pl.BlockSpec((tm, D_MODEL), lambda i: (i, 0)),
        out_shape=jax.ShapeDtypeStruct((m, D_MODEL), F32),
        compiler_params=pltpu.CompilerParams(
            dimension_semantics=("parallel",), vmem_limit_bytes=VMEM_LIMIT),
        name="tail",
    )(x2d, o2d, l2d, z2d, z2d, gate, fnw.reshape(1, -1), wrd_b, wld_b, wo_b)


def _trunk(x, mod_rows, s0_ret, h0_lru, width, lru_nblk, params, final_norm_w, emit_state):
    (norm_w, w_main, w_lru, decay_logit, gn_w, wrd_b, conv_b, lru_par, wld_b, wo_b, perm, perm_t) = params
    wsel, w_blk, bias_blk, ap8 = lru_par
    b, l, _ = x.shape
    groups = b // LRU_SEQS
    x2d = x.reshape(b * l, D_MODEL)
    n_cond = mod_rows.shape[0]
    shift = mod_rows[:, None, 0:D_MODEL]
    scale = mod_rows[:, None, D_MODEL:2 * D_MODEL]
    gate = mod_rows[:, None, 2 * D_MODEL:]
    tokens_per_cond = b * l // n_cond
    z2d = _inproj(x2d, norm_w, scale, shift, w_main, tokens_per_cond // 1024)
    z3 = z2d.reshape(b, l, D_MAIN)
    ret = _retention(z3, decay_logit, gn_w, s0_ret, emit_state)

    scale4 = jnp.broadcast_to(scale, (LRU_SEQS, 1, D_MODEL))[None]
    shift4 = jnp.broadcast_to(shift, (LRU_SEQS, 1, D_MODEL))[None]
    xs, gs = _inproj_lru(x.reshape(groups, LRU_SEQS, l, D_MODEL), norm_w, scale4, shift4, perm, w_lru)
    if h0_lru is None:
        h0 = jnp.zeros((groups, LRU_ROWS, D_LRU), F32)
    else:
        h0 = h0_lru.reshape(groups, LRU_SEQS, 2, D_LRU).transpose(0, 2, 1, 3).reshape(groups, LRU_ROWS, D_LRU)
    lru_pre, fin = _lru(xs, gs, wsel, conv_b, w_blk, bias_blk, ap8, h0, perm_t, l, width, lru_nblk)
    fin = fin.reshape(groups, 2, LRU_SEQS, D_LRU)
    lru_fin = jnp.stack([fin[:, 1], fin[:, 0]], axis=2).reshape(b, 2, D_LRU)

    y = _tail(x2d, ret[0].reshape(b * l, D_V), lru_pre.reshape(b * l, D_LRU), z2d, gate, final_norm_w,
              wrd_b, wld_b, wo_b, tokens_per_cond // 512)
    return y.reshape(b, l, D_MODEL), (ret[1] if emit_state else None), lru_fin


def kernel(x_prompt, x_sample, state_ret, state_lru, c, c_ctx, norm_w, w_ada, b_ada, w_in, ret_decay_logit,
           ret_gn_w, w_ret_down, conv_w, conv_b, lru_wa, lru_ba, lru_wx, lru_bx, lru_a_param, w_lru_down,
           w_out, final_norm_w):
    assert norm_w.shape[0] == 1, "single-layer step"
    n_dec = c.shape[0]
    cond8 = jnp.concatenate([c.astype(F32), c_ctx.astype(F32)[None],
                             jnp.zeros((8 - n_dec - 1, D_MODEL), F32)], axis=0)
    mod = _ada(cond8, w_ada[0], b_ada[0])
    w = w_in[0]
    w_main = jnp.concatenate([w[:, :OFF_XLRU], w[:, OFF_MRET:]], axis=1).astype(BF16)
    w_lru = w[:, OFF_XLRU:OFF_MRET].astype(BF16)
    perm_np = _slab_permutation()
    perm = jnp.asarray(perm_np, BF16)
    perm_t = jnp.asarray(perm_np.T, BF16)
    lru_par = _lru_params(conv_w[0], lru_wa[0], lru_ba[0], lru_wx[0], lru_bx[0], lru_a_param[0])
    params = (norm_w[0], w_main, w_lru, ret_decay_logit[0], ret_gn_w[0], w_ret_down[0].astype(BF16),
              conv_b[0], lru_par, w_lru_down[0].astype(BF16), w_out[0].astype(BF16), perm, perm_t)
    y_prompt, new_ret, new_lru = _trunk(x_prompt.astype(F32), mod[n_dec:n_dec + 1], None, None,
                                        x_prompt.shape[1], 5, params, final_norm_w, True)
    y_sample, _, _ = _trunk(x_sample.astype(F32), mod[:n_dec], state_ret[:, 0], state_lru[:, 0],
                            GRID_W, 2, params, final_norm_w, False)
    return (y_prompt.astype(x_prompt.dtype), y_sample.astype(x_sample.dtype),
            new_ret[:, None].astype(state_ret.dtype), new_lru[:, None].astype(state_lru.dtype))
```

```python
import functools

import jax
import jax.numpy as jnp
import numpy as np
from jax import lax
from jax.experimental import pallas as pl
from jax.experimental.pallas import tpu as pltpu

F32 = jnp.float32
BF16 = jnp.bfloat16

D_MODEL = 1024
N_HEADS = 4
DK = 256
DV = 512
D_QK = N_HEADS * DK
D_V = N_HEADS * DV
N_LRU_BLOCKS = 10
LRU_BLOCK = 128
D_LRU = N_LRU_BLOCKS * LRU_BLOCK
LRU_C = 8.0
CONV_W = 4
CONV_LEFT = 2
GRID_W = 64
EPS = 1e-6

OFF_Q = 0
OFF_K = OFF_Q + D_QK
OFF_V = OFF_K + D_QK
OFF_GRET = OFF_V + D_V
OFF_XLRU = OFF_GRET + D_V
OFF_GLRU = OFF_XLRU + D_LRU
OFF_MRET = OFF_GLRU + D_LRU
OFF_MLRU = OFF_MRET + D_MODEL
D_IN = OFF_MLRU + D_MODEL
D_MAIN = OFF_XLRU + 2 * D_MODEL
MAIN_MRET = OFF_XLRU
MAIN_MLRU = OFF_XLRU + D_MODEL

RET_CHUNK = 256
LRU_SEQS = 4
LRU_ROWS = 2 * LRU_SEQS
LRU_TC = 64
PERM_ROWS = LRU_TC * LRU_ROWS
VMEM_LIMIT = 56 * 1024 * 1024


def _sigmoid(x):
    return 0.5 * jnp.tanh(0.5 * x) + 0.5


def _silu(x):
    return x * _sigmoid(x)


def _softplus(x):
    return jnp.maximum(x, 0.0) + jnp.log1p(jnp.exp(-jnp.abs(x)))


def _slab_permutation():
    p = np.zeros((PERM_ROWS, PERM_ROWS), np.float32)
    for t in range(LRU_TC):
        for r in range(LRU_ROWS):
            if r < LRU_SEQS:
                src = r * LRU_TC + t
            else:
                src = LRU_SEQS * LRU_TC + (r - LRU_SEQS) * LRU_TC + (LRU_TC - 1 - t)
            p[t * LRU_ROWS + r, src] = 1.0
    return p


def _ada_kernel(c_ref, w_ref, b_ref, o_ref):
    cond = _silu(c_ref[...]).astype(BF16)
    o_ref[...] = jnp.dot(cond, w_ref[...].astype(BF16), preferred_element_type=F32) + b_ref[...]


def _ada(cond8, w_ada, b_ada):
    tn = 768
    return pl.pallas_call(
        _ada_kernel,
        grid=(3 * D_MODEL // tn,),
        in_specs=[pl.BlockSpec((8, D_MODEL), lambda j: (0, 0)),
                  pl.BlockSpec((D_MODEL, tn), lambda j: (0, j)),
                  pl.BlockSpec((1, tn), lambda j: (0, j))],
        out_specs=pl.BlockSpec((8, tn), lambda j: (0, j)),
        out_shape=jax.ShapeDtypeStruct((8, 3 * D_MODEL), F32),
        name="ada",
    )(cond8, w_ada, b_ada.reshape(1, -1))


def _modulated_norm(x, nw, scale, shift):
    ms = jnp.mean(x * x, axis=-1, keepdims=True)
    return (x * lax.rsqrt(ms + EPS) * nw) * (1.0 + scale) + shift


def _inproj_kernel(x_ref, nw_ref, sc_ref, sh_ref, w_ref, z_ref, h_scr):
    @pl.when(pl.program_id(1) == 0)
    def _():
        h_scr[...] = _modulated_norm(x_ref[...], nw_ref[...], sc_ref[0], sh_ref[0]).astype(BF16)

    z_ref[...] = jnp.dot(h_scr[...], w_ref[...], preferred_element_type=F32).astype(z_ref.dtype)


def _inproj(x2d, norm_w, scale, shift, w_main, tiles_per_cond):
    m = x2d.shape[0]
    tm, tn = 1024, 2048
    return pl.pallas_call(
        _inproj_kernel,
        grid=(m // tm, D_MAIN // tn),
        in_specs=[pl.BlockSpec((tm, D_MODEL), lambda i, j: (i, 0)),
                  pl.BlockSpec((1, D_MODEL), lambda i, j: (0, 0)),
                  pl.BlockSpec((1, 1, D_MODEL), lambda i, j: (i // tiles_per_cond, 0, 0)),
                  pl.BlockSpec((1, 1, D_MODEL), lambda i, j: (i // tiles_per_cond, 0, 0)),
                  pl.BlockSpec((D_MODEL, tn), lambda i, j: (0, j))],
        out_specs=pl.BlockSpec((tm, tn), lambda i, j: (i, j)),
        out_shape=jax.ShapeDtypeStruct((m, D_MAIN), BF16),
        scratch_shapes=[pltpu.VMEM((tm, D_MODEL), BF16)],
        compiler_params=pltpu.CompilerParams(
            dimension_semantics=("parallel", "arbitrary"), vmem_limit_bytes=VMEM_LIMIT),
        name="inproj",
    )(x2d, norm_w.reshape(1, -1), scale, shift, w_main)


def _inproj_lru_kernel(xa_ref, xb_ref, nw_ref, sc_ref, sh_ref, p_ref, w_ref, xs_ref, gs_ref):
    nw = nw_ref[...]
    sc = sc_ref[0]
    sh = sh_ref[0]
    half_rows = LRU_SEQS * LRU_TC
    ha = _modulated_norm(xa_ref[0], nw, sc, sh).astype(BF16).reshape(half_rows, D_MODEL)
    hb = _modulated_norm(xb_ref[0], nw, sc, sh).astype(BF16).reshape(half_rows, D_MODEL)
    h = jnp.concatenate([ha, hb], axis=0)
    hp = jnp.dot(p_ref[...], h, preferred_element_type=F32).astype(BF16)
    z = jnp.dot(hp, w_ref[...], preferred_element_type=F32)
    xs_ref[0] = z[:, :D_LRU].astype(xs_ref.dtype)
    gs_ref[0] = _silu(z[:, D_LRU:]).astype(gs_ref.dtype)


def _inproj_lru(x4, norm_w, scale4, shift4, perm, w_lru):
    groups, _, l, _ = x4.shape
    nt = l // 2 // LRU_TC
    ntb = l // LRU_TC
    out_sds = jax.ShapeDtypeStruct((groups, l // 2 * LRU_ROWS, D_LRU), BF16)
    return pl.pallas_call(
        _inproj_lru_kernel,
        grid=(groups, nt),
        in_specs=[pl.BlockSpec((1, LRU_SEQS, LRU_TC, D_MODEL), lambda g, i: (g, 0, i, 0)),
                  pl.BlockSpec((1, LRU_SEQS, LRU_TC, D_MODEL), lambda g, i: (g, 0, ntb - 1 - i, 0)),
                  pl.BlockSpec((1, D_MODEL), lambda g, i: (0, 0)),
                  pl.BlockSpec((1, LRU_SEQS, 1, D_MODEL), lambda g, i: (0, 0, 0, 0)),
                  pl.BlockSpec((1, LRU_SEQS, 1, D_MODEL), lambda g, i: (0, 0, 0, 0)),
                  pl.BlockSpec((PERM_ROWS, PERM_ROWS), lambda g, i: (0, 0)),
                  pl.BlockSpec((D_MODEL, 2 * D_LRU), lambda g, i: (0, 0))],
        out_specs=[pl.BlockSpec((1, PERM_ROWS, D_LRU), lambda g, i: (g, i, 0)),
                   pl.BlockSpec((1, PERM_ROWS, D_LRU), lambda g, i: (g, i, 0))],
        out_shape=[out_sds, out_sds],
        compiler_params=pltpu.CompilerParams(
            dimension_semantics=("parallel", "parallel"), vmem_limit_bytes=VMEM_LIMIT),
        name="inproj_lru",
    )(x4, x4, norm_w.reshape(1, -1), scale4, shift4, perm, w_lru)


def _dot_tn(a, b):
    return lax.dot_general(a, b, (((0,), (0,)), ((), ())), preferred_element_type=F32)


def _dot_nt(a, b):
    return lax.dot_general(a, b, (((1,), (1,)), ((), ())), preferred_element_type=F32)


def _ret_kernel(*refs, nc, has_state, emit_state):
    dl_ref, q_ref, k_ref, v_ref, g_ref, gnw_ref = refs[:6]
    pos = 6
    s0_ref = None
    if has_state:
        s0_ref = refs[pos]
        pos += 1
    o_ref = refs[pos]
    pos += 1
    sfin_ref = None
    if emit_state:
        sfin_ref = refs[pos]
        pos += 1
    ob_scr, sf_scr, sb_scr = refs[pos:pos + 3]

    c = RET_CHUNK
    head = pl.program_id(1)
    lgf = -_softplus(-jnp.full((1, 1), dl_ref[0, head], F32))
    lgb = -_softplus(-jnp.full((1, 1), dl_ref[1, head], F32))
    ii = lax.broadcasted_iota(jnp.int32, (c, c), 0)
    jj = lax.broadcasted_iota(jnp.int32, (c, c), 1)
    diff = (ii - jj).astype(F32)
    decay = jnp.exp(jnp.where(diff >= 0, lgf * diff, -lgb * diff))
    p = lax.broadcasted_iota(jnp.int32, (c, 1), 0).astype(F32)
    kscale = DK ** -0.5
    qdf = jnp.exp(lgf * (p + 1.0))
    qdb = jnp.exp(lgb * (c - p))
    kdf = jnp.exp(lgf * (c - 1.0 - p)) * kscale
    kdb = jnp.exp(lgb * p) * kscale
    cdf = jnp.exp(lgf * c)
    cdb = jnp.exp(lgb * c)
    carry_states = has_state or nc > 1

    if has_state:
        sf_scr[...] = s0_ref[0, 0, 0]
        sb_scr[...] = s0_ref[0, 1, 0]
    else:
        sf_scr[...] = jnp.zeros_like(sf_scr)
        sb_scr[...] = jnp.zeros_like(sb_scr)

    def rows_of(n):
        return pl.ds(pl.multiple_of(n * c, c), c)

    def rev_body(idx, carry):
        rows = rows_of(nc - 1 - idx)
        qn = q_ref[0, rows, :]
        if carry_states:
            ob_scr[rows, :] = jnp.dot(qn, sb_scr[...].astype(BF16), preferred_element_type=F32) * qdb
        kb = (k_ref[0, rows, :].astype(F32) * kdb).astype(BF16)
        kv = _dot_tn(kb, v_ref[0, rows, :])
        sb_scr[...] = (cdb * sb_scr[...] + kv) if carry_states else kv
        return carry

    lax.fori_loop(0, nc, rev_body, 0)

    def fwd_body(n, carry):
        rows = rows_of(n)
        qn = q_ref[0, rows, :]
        kn = k_ref[0, rows, :]
        vn = v_ref[0, rows, :]
        s = _dot_nt(qn, kn) * (decay * kscale)
        o = jnp.dot(s.astype(BF16), vn, preferred_element_type=F32)
        if carry_states:
            o = o + jnp.dot(qn, sf_scr[...].astype(BF16), preferred_element_type=F32) * qdf + ob_scr[rows, :]
        ms = jnp.mean(o * o, axis=-1, keepdims=True)
        on = o * lax.rsqrt(ms + EPS)
        gate = _silu(g_ref[0, rows, :].astype(F32))
        o_ref[0, rows, :] = (on * gnw_ref[0] * gate).astype(o_ref.dtype)
        kf = (kn.astype(F32) * kdf).astype(BF16)
        kv = _dot_tn(kf, vn)
        sf_scr[...] = (cdf * sf_scr[...] + kv) if carry_states else kv
        return carry

    lax.fori_loop(0, nc, fwd_body, 0)

    if emit_state:
        sfin_ref[0, 0, 0] = sf_scr[...]
        sfin_ref[0, 1, 0] = sb_scr[...]


def _retention(z3, decay_logit, gn_w, s0, emit_state):
    b, l, _ = z3.shape
    nc = l // RET_CHUNK
    has_state = s0 is not None
    kq, kv_ = OFF_K // DK, OFF_V // DV
    kg = OFF_GRET // DV
    in_specs = [pl.BlockSpec(memory_space=pltpu.SMEM),
                pl.BlockSpec((1, l, DK), lambda i, h: (i, 0, h)),
                pl.BlockSpec((1, l, DK), lambda i, h: (i, 0, kq + h)),
                pl.BlockSpec((1, l, DV), lambda i, h: (i, 0, kv_ + h)),
                pl.BlockSpec((1, l, DV), lambda i, h: (i, 0, kg + h)),
                pl.BlockSpec((1, 1, DV), lambda i, h: (h, 0, 0))]
    args = [decay_logit, z3, z3, z3, z3, gn_w.reshape(N_HEADS, 1, DV)]
    if has_state:
        in_specs.append(pl.BlockSpec((1, 2, 1, DK, DV), lambda i, h: (i, 0, h, 0, 0)))
        args.append(s0)
    out_specs = [pl.BlockSpec((1, l, DV), lambda i, h: (i, 0, h))]
    out_shape = [jax.ShapeDtypeStruct((b, l, D_V), BF16)]
    if emit_state:
        out_specs.append(pl.BlockSpec((1, 2, 1, DK, DV), lambda i, h: (i, 0, h, 0, 0)))
        out_shape.append(jax.ShapeDtypeStruct((b, 2, N_HEADS, DK, DV), F32))
    return pl.pallas_call(
        functools.partial(_ret_kernel, nc=nc, has_state=has_state, emit_state=emit_state),
        grid=(b, N_HEADS),
        in_specs=in_specs,
        out_specs=out_specs,
        out_shape=out_shape,
        scratch_shapes=[pltpu.VMEM((l, DV), F32), pltpu.VMEM((DK, DV), F32), pltpu.VMEM((DK, DV), F32)],
        compiler_params=pltpu.CompilerParams(
            dimension_semantics=("parallel", "parallel"), vmem_limit_bytes=VMEM_LIMIT),
        name="retention",
    )(*args)


def _sqrt_unit(x):
    return x * lax.rsqrt(jnp.maximum(x, 1e-30))


def _lru_kernel(xs_ref, sg_ref, cw_ref, cb_ref, w_ref, bias_ref, ap_ref, h0_ref, pt_ref, out_ref, fin_ref,
                xp_scr, s1_scr, wm_scr, a_scr, b_scr, *, slabs, seq_len, width, nblk):
    r8 = LRU_ROWS
    tc = LRU_TC
    rows = PERM_ROWS
    cw = nblk * LRU_BLOCK
    nchunks = slabs // tc
    assert width == seq_len or width == tc

    def roll4(v):
        return pltpu.roll(v, LRU_SEQS, axis=0)

    xp_scr[pl.ds(0, 2 * r8), :] = jnp.zeros((2 * r8, cw), F32)
    xp_scr[pl.ds(2 * r8, slabs * r8), :] = xs_ref[0].astype(F32)
    xp_scr[pl.ds((slabs + 2) * r8, r8), :] = roll4(xs_ref[0, pl.ds((slabs - 1) * r8, r8), :].astype(F32))
    xp_scr[pl.ds((slabs + 3) * r8, r8), :] = roll4(xs_ref[0, pl.ds((slabs - 2) * r8, r8), :].astype(F32))

    step_in_chunk = lax.broadcasted_iota(jnp.int32, (rows, cw), 0) >> 3
    for s in range(-2, 3):
        tap = jnp.broadcast_to(cw_ref[s + 2][None], (tc, r8, cw)).reshape(rows, cw)
        if width != seq_len:
            tap = jnp.where((step_in_chunk + s >= 0) & (step_in_chunk + s < tc), tap, 0.0)
        wm_scr[s + 2] = tap

    c1_lo = (-0.5 * LRU_C / np.log(2.0)) * _softplus(-ap_ref[...])
    bias_lo = bias_ref[...]
    low_rows = (lax.broadcasted_iota(jnp.int32, (rows, LRU_BLOCK), 0) & (r8 - 1)) < LRU_SEQS
    high_rows = jnp.logical_not(low_rows)

    def conv_half(t0):
        xh = jnp.zeros((rows, cw), F32) + cb_ref[...]
        for s in range(-2, 3):
            xh = xh + xp_scr[pl.ds(pl.multiple_of((t0 + s + 2) * r8, r8), rows), :] * wm_scr[s + 2]
        xp_scr[pl.ds(pl.multiple_of(t0 * r8, rows), rows), :] = xh
        return xh

    def gates(xh, sweep2):
        fj = high_rows if sweep2 else low_rows
        c1 = roll4(c1_lo) if sweep2 else c1_lo
        for j in range(nblk):
            cols = slice(j * LRU_BLOCK, (j + 1) * LRU_BLOCK)
            xj = xh[:, cols]
            lhs = jnp.concatenate([jnp.where(fj, xj, 0.0), jnp.where(fj, 0.0, xj)], axis=1)
            pre = jnp.dot(lhs.astype(BF16), w_ref[j], preferred_element_type=F32)
            bj = roll4(bias_lo[j]) if sweep2 else bias_lo[j]
            pre = pre.reshape(tc, r8, 2 * LRU_BLOCK) + bj[None]
            tr = jnp.tanh(pre[:, :, :LRU_BLOCK])
            tg = jnp.tanh(pre[:, :, LRU_BLOCK:])
            c1j = c1[:, cols][None]
            a = jnp.exp2(c1j * tr + c1j)
            bco = _sqrt_unit(1.0 - a * a) * ((tg + 1.0) * xj.reshape(tc, r8, LRU_BLOCK))
            a_scr[:, cols] = a.reshape(rows, LRU_BLOCK)
            b_scr[:, cols] = bco.reshape(rows, LRU_BLOCK)

    def slab(t):
        return pl.ds(pl.multiple_of(t * r8, r8), r8)

    def sweep1_chunk(ci, h):
        t0 = ci * tc
        gates(conv_half(t0), False)

        def step(t, hh):
            hh = a_scr[slab(t), :] * hh + b_scr[slab(t), :]
            s1_scr[slab(t0 + t), :] = hh
            return hh

        return lax.fori_loop(0, tc, step, h, unroll=8)

    h_mid = lax.fori_loop(0, nchunks, sweep1_chunk, h0_ref[0])

    def sweep2_chunk(idx, h):
        t0 = (nchunks - 1 - idx) * tc
        crows = pl.ds(pl.multiple_of(t0 * r8, rows), rows)
        gates(xp_scr[crows, :], True)

        def step(i, hh):
            t = tc - 1 - i
            hh = a_scr[slab(t), :] * hh + b_scr[slab(t), :]
            b_scr[slab(t), :] = hh
            return hh

        h = lax.fori_loop(0, tc, step, h, unroll=8)
        o = ((s1_scr[crows, :] + b_scr[...]) * sg_ref[0, crows, :].astype(F32)).astype(BF16)
        nat = jnp.dot(pt_ref[...], o, preferred_element_type=F32).astype(out_ref.dtype)
        lo = pl.multiple_of(t0, tc)
        hi = pl.multiple_of(seq_len - tc - t0, tc)
        for b in range(LRU_SEQS):
            out_ref[0, b, pl.ds(lo, tc), :] = nat[b * tc:(b + 1) * tc]
            out_ref[0, b, pl.ds(hi, tc), :] = nat[(LRU_SEQS + b) * tc:(LRU_SEQS + b + 1) * tc]
        return h

    h_fin = lax.fori_loop(0, nchunks, sweep2_chunk, roll4(h_mid))
    fin_ref[0] = h_fin


def _lru(xs, gs, wsel, conv_b, w_blk, bias_blk, ap8, h0, perm_t, seq_len, width, nblk):
    groups = xs.shape[0]
    r8 = LRU_ROWS
    slabs = seq_len // 2
    cw = nblk * LRU_BLOCK
    return pl.pallas_call(
        functools.partial(_lru_kernel, slabs=slabs, seq_len=seq_len, width=width, nblk=nblk),
        grid=(groups, N_LRU_BLOCKS // nblk),
        in_specs=[pl.BlockSpec((1, slabs * r8, cw), lambda g, c: (g, 0, c)),
                  pl.BlockSpec((1, slabs * r8, cw), lambda g, c: (g, 0, c)),
                  pl.BlockSpec((5, r8, cw), lambda g, c: (0, 0, c)),
                  pl.BlockSpec((1, cw), lambda g, c: (0, c)),
                  pl.BlockSpec((nblk, 2 * LRU_BLOCK, 2 * LRU_BLOCK), lambda g, c: (c, 0, 0)),
                  pl.BlockSpec((nblk, r8, 2 * LRU_BLOCK), lambda g, c: (c, 0, 0)),
                  pl.BlockSpec((r8, cw), lambda g, c: (0, c)),
                  pl.BlockSpec((1, r8, cw), lambda g, c: (g, 0, c)),
                  pl.BlockSpec((PERM_ROWS, PERM_ROWS), lambda g, c: (0, 0))],
        out_specs=[pl.BlockSpec((1, LRU_SEQS, seq_len, cw), lambda g, c: (g, 0, 0, c)),
                   pl.BlockSpec((1, r8, cw), lambda g, c: (g, 0, c))],
        out_shape=[jax.ShapeDtypeStruct((groups, LRU_SEQS, seq_len, D_LRU), BF16),
                   jax.ShapeDtypeStruct((groups, r8, D_LRU), F32)],
        scratch_shapes=[pltpu.VMEM(((slabs + 4) * r8, cw), F32),
                        pltpu.VMEM((slabs * r8, cw), F32),
                        pltpu.VMEM((5, PERM_ROWS, cw), F32),
                        pltpu.VMEM((PERM_ROWS, cw), F32),
                        pltpu.VMEM((PERM_ROWS, cw), F32)],
        compiler_params=pltpu.CompilerParams(
            dimension_semantics=("parallel", "parallel"), vmem_limit_bytes=VMEM_LIMIT),
        name="lru",
    )(xs, gs, wsel, 0.5 * conv_b.reshape(1, -1), w_blk, bias_blk, ap8, h0, perm_t)


def _lru_params(conv_w, wa, ba, wx, bx, a_param):
    zero_tap = jnp.zeros((D_LRU,), F32)
    taps_lo = [conv_w[s + CONV_LEFT] if s + CONV_LEFT < CONV_W else zero_tap for s in range(-2, 3)]
    taps_hi = [conv_w[CONV_LEFT - s] if 0 <= CONV_LEFT - s < CONV_W else zero_tap for s in range(-2, 3)]
    wsel = jnp.stack([jnp.concatenate([jnp.broadcast_to(tl, (LRU_SEQS, D_LRU)),
                                       jnp.broadcast_to(th, (LRU_SEQS, D_LRU))], axis=0)
                      for tl, th in zip(taps_lo, taps_hi)], axis=0)
    w_blk = jnp.concatenate([jnp.concatenate([wa[0], wx[0]], axis=2),
                             jnp.concatenate([wa[1], wx[1]], axis=2)], axis=1).astype(BF16)
    ba_r = ba.reshape(2, N_LRU_BLOCKS, LRU_BLOCK)
    bx_r = bx.reshape(2, N_LRU_BLOCKS, LRU_BLOCK)
    bias_dir = jnp.concatenate([ba_r, bx_r], axis=2)
    bias_blk = jnp.repeat(bias_dir.transpose(1, 0, 2), LRU_SEQS, axis=1)
    ap8 = jnp.repeat(a_param, LRU_SEQS, axis=0)
    return 0.5 * wsel, w_blk, 0.5 * bias_blk, ap8


def _tail_kernel(x_ref, o_ref, l_ref, mr_ref, ml_ref, gate_ref, fnw_ref, wrd_ref, wld_ref, wo_ref, y_ref):
    ret_out = jnp.dot(o_ref[...], wrd_ref[...], preferred_element_type=F32)
    lru_out = jnp.dot(l_ref[...], wld_ref[...], preferred_element_type=F32)
    merged = _sigmoid(mr_ref[...].astype(F32)) * ret_out + _sigmoid(ml_ref[...].astype(F32)) * lru_out
    out = jnp.dot(merged.astype(BF16), wo_ref[...], preferred_element_type=F32)
    y = x_ref[...] + gate_ref[0] * out
    ms = jnp.mean(y * y, axis=-1, keepdims=True)
    y_ref[...] = y * lax.rsqrt(ms + EPS) * fnw_ref[...]


def _tail(x2d, o2d, l2d, z2d, gate, fnw, wrd_b, wld_b, wo_b, tiles_per_cond):
    m = x2d.shape[0]
    tm = 512
    kmr, kml = MAIN_MRET // D_MODEL, MAIN_MLRU // D_MODEL
    const = lambda i: (0, 0)
    return pl.pallas_call(
        _tail_kernel,
        grid=(m // tm,),
        in_specs=[pl.BlockSpec((tm, D_MODEL), lambda i: (i, 0)),
                  pl.BlockSpec((tm, D_V), lambda i: (i, 0)),
                  pl.BlockSpec((tm, D_LRU), lambda i: (i, 0)),
                  pl.BlockSpec((tm, D_MODEL), lambda i: (i, kmr)),
                  pl.BlockSpec((tm, D_MODEL), lambda i: (i, kml)),
                  pl.BlockSpec((1, 1, D_MODEL), lambda i: (i // tiles_per_cond, 0, 0)),
                  pl.BlockSpec((1, D_MODEL), const),
                  pl.BlockSpec((D_V, D_MODEL), const),
                  pl.BlockSpec((D_LRU, D_MODEL), const),
                  pl.BlockSpec((D_MODEL, D_MODEL), const)],
        out_specs=pl.BlockSpec((tm, D_MODEL), lambda i: (i, 0)),
        out_shape=jax.ShapeDtypeStruct((m, D_MODEL), F32),
        compiler_params=pltpu.CompilerParams(
            dimension_semantics=("parallel",), vmem_limit_bytes=VMEM_LIMIT),
        name="tail",
    )(x2d, o2d, l2d, z2d, z2d, gate, fnw.reshape(1, -1), wrd_b, wld_b, wo_b)


def _trunk(x, mod_rows, s0_ret, h0_lru, width, lru_nblk, params, final_norm_w, emit_state):
    (norm_w, w_main, w_lru, decay_logit, gn_w, wrd_b, conv_b, lru_par, wld_b, wo_b, perm, perm_t) = params
    wsel, w_blk, bias_blk, ap8 = lru_par
    b, l, _ = x.shape
    groups = b // LRU_SEQS
    x2d = x.reshape(b * l, D_MODEL)
    n_cond = mod_rows.shape[0]
    shift = mod_rows[:, None, 0:D_MODEL]
    scale = mod_rows[:, None, D_MODEL:2 * D_MODEL]
    gate = mod_rows[:, None, 2 * D_MODEL:]
    tokens_per_cond = b * l // n_cond
    z2d = _inproj(x2d, norm_w, scale, shift, w_main, tokens_per_cond // 1024)
    z3 = z2d.reshape(b, l, D_MAIN)
    ret = _retention(z3, decay_logit, gn_w, s0_ret, emit_state)

    scale4 = jnp.broadcast_to(scale, (LRU_SEQS, 1, D_MODEL))[None]
    shift4 = jnp.broadcast_to(shift, (LRU_SEQS, 1, D_MODEL))[None]
    xs, gs = _inproj_lru(x.reshape(groups, LRU_SEQS, l, D_MODEL), norm_w, scale4, shift4, perm, w_lru)
    if h0_lru is None:
        h0 = jnp.zeros((groups, LRU_ROWS, D_LRU), F32)
    else:
        h0 = h0_lru.reshape(groups, LRU_SEQS, 2, D_LRU).transpose(0, 2, 1, 3).reshape(groups, LRU_ROWS, D_LRU)
    lru_pre, fin = _lru(xs, gs, wsel, conv_b, w_blk, bias_blk, ap8, h0, perm_t, l, width, lru_nblk)
    fin = fin.reshape(groups, 2, LRU_SEQS, D_LRU)
    lru_fin = jnp.stack([fin[:, 1], fin[:, 0]], axis=2).reshape(b, 2, D_LRU)

    y = _tail(x2d, ret[0].reshape(b * l, D_V), lru_pre.reshape(b * l, D_LRU), z2d, gate, final_norm_w,
              wrd_b, wld_b, wo_b, tokens_per_cond // 512)
    return y.reshape(b, l, D_MODEL), (ret[1] if emit_state else None), lru_fin


def kernel(x_prompt, x_sample, state_ret, state_lru, c, c_ctx, norm_w, w_ada, b_ada, w_in, ret_decay_logit,
           ret_gn_w, w_ret_down, conv_w, conv_b, lru_wa, lru_ba, lru_wx, lru_bx, lru_a_param, w_lru_down,
           w_out, final_norm_w):
    assert norm_w.shape[0] == 1, "single-layer step"
    n_dec = c.shape[0]
    cond8 = jnp.concatenate([c.astype(F32), c_ctx.astype(F32)[None],
                             jnp.zeros((8 - n_dec - 1, D_MODEL), F32)], axis=0)
    mod = _ada(cond8, w_ada[0], b_ada[0])
    w = w_in[0]
    w_main = jnp.concatenate([w[:, :OFF_XLRU], w[:, OFF_MRET:]], axis=1).astype(BF16)
    w_lru = w[:, OFF_XLRU:OFF_MRET].astype(BF16)
    perm_np = _slab_permutation()
    perm = jnp.asarray(perm_np, BF16)
    perm_t = jnp.asarray(perm_np.T, BF16)
    lru_par = _lru_params(conv_w[0], lru_wa[0], lru_ba[0], lru_wx[0], lru_bx[0], lru_a_param[0])
    params = (norm_w[0], w_main, w_lru, ret_decay_logit[0], ret_gn_w[0], w_ret_down[0].astype(BF16),
              conv_b[0], lru_par, w_lru_down[0].astype(BF16), w_out[0].astype(BF16), perm, perm_t)
    y_prompt, new_ret, new_lru = _trunk(x_prompt.astype(F32), mod[n_dec:n_dec + 1], None, None,
                                        x_prompt.shape[1], 5, params, final_norm_w, True)
    y_sample, _, _ = _trunk(x_sample.astype(F32), mod[:n_dec], state_ret[:, 0], state_lru[:, 0],
                            GRID_W, 2, params, final_norm_w, False)
    return (y_prompt.astype(x_prompt.dtype), y_sample.astype(x_sample.dtype),
            new_ret[:, None].astype(state_ret.dtype), new_lru[:, None].astype(state_lru.dtype))
```

```python
import functools

import jax
import jax.numpy as jnp
import numpy as np
from jax import lax
from jax.experimental import pallas as pl
from jax.experimental.pallas import tpu as pltpu

F32 = jnp.float32
BF16 = jnp.bfloat16

D_MODEL = 1024
N_HEADS = 4
DK = 256
DV = 512
D_QK = N_HEADS * DK
D_V = N_HEADS * DV
N_LRU_BLOCKS = 10
LRU_BLOCK = 128
D_LRU = N_LRU_BLOCKS * LRU_BLOCK
LRU_C = 8.0
CONV_W = 4
CONV_LEFT = 2
GRID_W = 64
EPS = 1e-6

OFF_Q = 0
OFF_K = OFF_Q + D_QK
OFF_V = OFF_K + D_QK
OFF_GRET = OFF_V + D_V
OFF_XLRU = OFF_GRET + D_V
OFF_GLRU = OFF_XLRU + D_LRU
OFF_MRET = OFF_GLRU + D_LRU
OFF_MLRU = OFF_MRET + D_MODEL
D_IN = OFF_MLRU + D_MODEL
D_MAIN = OFF_XLRU + 2 * D_MODEL
MAIN_MRET = OFF_XLRU
MAIN_MLRU = OFF_XLRU + D_MODEL

RET_CHUNK = 256
LRU_SEQS = 4
LRU_ROWS = 2 * LRU_SEQS
LRU_TC = 64
PERM_ROWS = LRU_TC * LRU_ROWS
VMEM_LIMIT = 56 * 1024 * 1024


def _sigmoid(x):
    return 0.5 * jnp.tanh(0.5 * x) + 0.5


def _silu(x):
    return x * _sigmoid(x)


def _softplus(x):
    return jnp.maximum(x, 0.0) + jnp.log1p(jnp.exp(-jnp.abs(x)))


def _slab_permutation():
    p = np.zeros((PERM_ROWS, PERM_ROWS), np.float32)
    for t in range(LRU_TC):
        for r in range(LRU_ROWS):
            if r < LRU_SEQS:
                src = r * LRU_TC + t
            else:
                src = LRU_SEQS * LRU_TC + (r - LRU_SEQS) * LRU_TC + (LRU_TC - 1 - t)
            p[t * LRU_ROWS + r, src] = 1.0
    return p


def _ada_kernel(c_ref, w_ref, b_ref, o_ref):
    cond = _silu(c_ref[...]).astype(BF16)
    o_ref[...] = jnp.dot(cond, w_ref[...].astype(BF16), preferred_element_type=F32) + b_ref[...]


def _ada(cond8, w_ada, b_ada):
    tn = 768
    return pl.pallas_call(
        _ada_kernel,
        grid=(3 * D_MODEL // tn,),
        in_specs=[pl.BlockSpec((8, D_MODEL), lambda j: (0, 0)),
                  pl.BlockSpec((D_MODEL, tn), lambda j: (0, j)),
                  pl.BlockSpec((1, tn), lambda j: (0, j))],
        out_specs=pl.BlockSpec((8, tn), lambda j: (0, j)),
        out_shape=jax.ShapeDtypeStruct((8, 3 * D_MODEL), F32),
        name="ada",
    )(cond8, w_ada, b_ada.reshape(1, -1))


def _modulated_norm(x, nw, scale, shift):
    ms = jnp.mean(x * x, axis=-1, keepdims=True)
    return (x * lax.rsqrt(ms + EPS) * nw) * (1.0 + scale) + shift


INPROJ_TN = 2048
assert OFF_GRET == 2 * INPROJ_TN and MAIN_MRET == 3 * INPROJ_TN and D_MAIN == 4 * INPROJ_TN


def _inproj_kernel(x_ref, nw_ref, sc_ref, sh_ref, gnw_ref, w_ref, z_ref, h_scr):
    j = pl.program_id(1)

    @pl.when(j == 0)
    def _():
        h_scr[...] = _modulated_norm(x_ref[...], nw_ref[...], sc_ref[0], sh_ref[0]).astype(BF16)

    z = jnp.dot(h_scr[...], w_ref[...], preferred_element_type=F32)

    @pl.when(j < 2)
    def _():
        z_ref[...] = z.astype(z_ref.dtype)

    @pl.when(j == 2)
    def _():
        z_ref[...] = (_silu(z) * gnw_ref[...]).astype(z_ref.dtype)

    @pl.when(j == 3)
    def _():
        z_ref[...] = _sigmoid(z).astype(z_ref.dtype)


def _inproj(x2d, norm_w, scale, shift, gn_w, w_main, tiles_per_cond):
    m = x2d.shape[0]
    tm, tn = 1024, INPROJ_TN
    return pl.pallas_call(
        _inproj_kernel,
        grid=(m // tm, D_MAIN // tn),
        in_specs=[pl.BlockSpec((tm, D_MODEL), lambda i, j: (i, 0)),
                  pl.BlockSpec((1, D_MODEL), lambda i, j: (0, 0)),
                  pl.BlockSpec((1, 1, D_MODEL), lambda i, j: (i // tiles_per_cond, 0, 0)),
                  pl.BlockSpec((1, 1, D_MODEL), lambda i, j: (i // tiles_per_cond, 0, 0)),
                  pl.BlockSpec((1, D_V), lambda i, j: (0, 0)),
                  pl.BlockSpec((D_MODEL, tn), lambda i, j: (0, j))],
        out_specs=pl.BlockSpec((tm, tn), lambda i, j: (i, j)),
        out_shape=jax.ShapeDtypeStruct((m, D_MAIN), BF16),
        scratch_shapes=[pltpu.VMEM((tm, D_MODEL), BF16)],
        compiler_params=pltpu.CompilerParams(
            dimension_semantics=("parallel", "arbitrary"), vmem_limit_bytes=VMEM_LIMIT),
        name="inproj",
    )(x2d, norm_w.reshape(1, -1), scale, shift, gn_w.reshape(1, -1), w_main)


def _inproj_lru_kernel(xa_ref, xb_ref, nw_ref, sc_ref, sh_ref, p_ref, w_ref, xs_ref, gs_ref):
    nw = nw_ref[...]
    sc = sc_ref[0]
    sh = sh_ref[0]
    half_rows = LRU_SEQS * LRU_TC
    ha = _modulated_norm(xa_ref[0], nw, sc, sh).astype(BF16).reshape(half_rows, D_MODEL)
    hb = _modulated_norm(xb_ref[0], nw, sc, sh).astype(BF16).reshape(half_rows, D_MODEL)
    h = jnp.concatenate([ha, hb], axis=0)
    hp = jnp.dot(p_ref[...], h, preferred_element_type=F32).astype(BF16)
    z = jnp.dot(hp, w_ref[...], preferred_element_type=F32)
    xs_ref[0] = z[:, :D_LRU].astype(xs_ref.dtype)
    gs_ref[0] = _silu(z[:, D_LRU:]).astype(gs_ref.dtype)


def _inproj_lru(x4, norm_w, scale4, shift4, perm, w_lru):
    groups, _, l, _ = x4.shape
    nt = l // 2 // LRU_TC
    ntb = l // LRU_TC
    out_sds = jax.ShapeDtypeStruct((groups, l // 2 * LRU_ROWS, D_LRU), BF16)
    return pl.pallas_call(
        _inproj_lru_kernel,
        grid=(groups, nt),
        in_specs=[pl.BlockSpec((1, LRU_SEQS, LRU_TC, D_MODEL), lambda g, i: (g, 0, i, 0)),
                  pl.BlockSpec((1, LRU_SEQS, LRU_TC, D_MODEL), lambda g, i: (g, 0, ntb - 1 - i, 0)),
                  pl.BlockSpec((1, D_MODEL), lambda g, i: (0, 0)),
                  pl.BlockSpec((1, LRU_SEQS, 1, D_MODEL), lambda g, i: (0, 0, 0, 0)),
                  pl.BlockSpec((1, LRU_SEQS, 1, D_MODEL), lambda g, i: (0, 0, 0, 0)),
                  pl.BlockSpec((PERM_ROWS, PERM_ROWS), lambda g, i: (0, 0)),
                  pl.BlockSpec((D_MODEL, 2 * D_LRU), lambda g, i: (0, 0))],
        out_specs=[pl.BlockSpec((1, PERM_ROWS, D_LRU), lambda g, i: (g, i, 0)),
                   pl.BlockSpec((1, PERM_ROWS, D_LRU), lambda g, i: (g, i, 0))],
        out_shape=[out_sds, out_sds],
        compiler_params=pltpu.CompilerParams(
            dimension_semantics=("parallel", "parallel"), vmem_limit_bytes=VMEM_LIMIT),
        name="inproj_lru",
    )(x4, x4, norm_w.reshape(1, -1), scale4, shift4, perm, w_lru)


def _dot_tn(a, b):
    return lax.dot_general(a, b, (((0,), (0,)), ((), ())), preferred_element_type=F32)


def _dot_nt(a, b):
    return lax.dot_general(a, b, (((1,), (1,)), ((), ())), preferred_element_type=F32)


def _ret_kernel(*refs, nc, hp, has_state, emit_state):
    dl_ref, q_ref, k_ref, v_ref, sg_ref = refs[:5]
    pos = 5
    s0_ref = None
    if has_state:
        s0_ref = refs[pos]
        pos += 1
    o_ref = refs[pos]
    pos += 1
    sfin_ref = None
    if emit_state:
        sfin_ref = refs[pos]
        pos += 1
    sf_scr, sb_scr, sbh_scr = refs[pos:pos + 3]

    c = RET_CHUNK
    carry_states = has_state or nc > 1
    ii = lax.broadcasted_iota(jnp.int32, (c, c), 0)
    jj = lax.broadcasted_iota(jnp.int32, (c, c), 1)
    diff = (ii - jj).astype(F32)
    p = lax.broadcasted_iota(jnp.int32, (c, 1), 0).astype(F32)
    kscale = DK ** -0.5

    def rows_of(n):
        return pl.ds(pl.multiple_of(n * c, c), c)

    def normed_out(o, rows, vcols):
        ms = jnp.mean(o * o, axis=-1, keepdims=True)
        on = o * lax.rsqrt(ms + EPS)
        o_ref[0, rows, vcols] = (on * sg_ref[0, rows, vcols].astype(F32)).astype(o_ref.dtype)

    for hh in range(hp):
        head = pl.program_id(1) * hp + hh
        qcols = slice(hh * DK, (hh + 1) * DK)
        vcols = slice(hh * DV, (hh + 1) * DV)
        lgf = -_softplus(-jnp.full((1, 1), dl_ref[0, head], F32))
        lgb = -_softplus(-jnp.full((1, 1), dl_ref[1, head], F32))
        decay = jnp.exp(jnp.where(diff >= 0, lgf * diff, -lgb * diff)) * kscale
        kdf = jnp.exp(lgf * (c - 1.0 - p)) * kscale
        kdb = jnp.exp(lgb * p) * kscale

        if not carry_states:
            rows = pl.ds(0, c)
            qn = q_ref[0, rows, qcols]
            kn = k_ref[0, rows, qcols]
            vn = v_ref[0, rows, vcols]
            s = (_dot_nt(qn, kn) * decay).astype(BF16)
            normed_out(jnp.dot(s, vn, preferred_element_type=F32), rows, vcols)
            k32 = kn.astype(F32)
            if emit_state:
                sfin_ref[0, 0, hh] = _dot_tn((k32 * kdf).astype(BF16), vn)
                sfin_ref[0, 1, hh] = _dot_tn((k32 * kdb).astype(BF16), vn)
            continue

        qdf = jnp.exp(lgf * (p + 1.0))
        qdb = jnp.exp(lgb * (c - p))
        cdf = jnp.exp(lgf * c)
        cdb = jnp.exp(lgb * c)
        if has_state:
            sf_scr[...] = s0_ref[0, 0, hh]
            sb_scr[...] = s0_ref[0, 1, hh]
        else:
            sf_scr[...] = jnp.zeros_like(sf_scr)
            sb_scr[...] = jnp.zeros_like(sb_scr)

        def rev_body(idx, carry):
            n = nc - 1 - idx
            rows = rows_of(n)
            sbh_scr[n] = sb_scr[...].astype(BF16)
            kb = (k_ref[0, rows, qcols].astype(F32) * kdb).astype(BF16)
            sb_scr[...] = cdb * sb_scr[...] + _dot_tn(kb, v_ref[0, rows, vcols])
            return carry

        lax.fori_loop(0, nc, rev_body, 0, unroll=True)

        def fwd_body(n, carry):
            rows = rows_of(n)
            qn = q_ref[0, rows, qcols]
            kn = k_ref[0, rows, qcols]
            vn = v_ref[0, rows, vcols]
            s = (_dot_nt(qn, kn) * decay).astype(BF16)
            q32 = qn.astype(F32)
            o = (jnp.dot(s, vn, preferred_element_type=F32)
                 + jnp.dot((q32 * qdf).astype(BF16), sf_scr[...].astype(BF16), preferred_element_type=F32)
                 + jnp.dot((q32 * qdb).astype(BF16), sbh_scr[n], preferred_element_type=F32))
            normed_out(o, rows, vcols)
            kf = (kn.astype(F32) * kdf).astype(BF16)
            sf_scr[...] = cdf * sf_scr[...] + _dot_tn(kf, vn)
            return carry

        lax.fori_loop(0, nc, fwd_body, 0, unroll=True)

        if emit_state:
            sfin_ref[0, 0, hh] = sf_scr[...]
            sfin_ref[0, 1, hh] = sb_scr[...]


def _retention(z3, decay_logit, s0, emit_state, hp):
    b, l, _ = z3.shape
    nc = l // RET_CHUNK
    has_state = s0 is not None
    kq, kv_ = OFF_K // (hp * DK), OFF_V // (hp * DV)
    kg = OFF_GRET // (hp * DV)
    in_specs = [pl.BlockSpec(memory_space=pltpu.SMEM),
                pl.BlockSpec((1, l, hp * DK), lambda i, h: (i, 0, h)),
                pl.BlockSpec((1, l, hp * DK), lambda i, h: (i, 0, kq + h)),
                pl.BlockSpec((1, l, hp * DV), lambda i, h: (i, 0, kv_ + h)),
                pl.BlockSpec((1, l, hp * DV), lambda i, h: (i, 0, kg + h))]
    args = [decay_logit, z3, z3, z3, z3]
    if has_state:
        in_specs.append(pl.BlockSpec((1, 2, hp, DK, DV), lambda i, h: (i, 0, h, 0, 0)))
        args.append(s0)
    out_specs = [pl.BlockSpec((1, l, hp * DV), lambda i, h: (i, 0, h))]
    out_shape = [jax.ShapeDtypeStruct((b, l, D_V), BF16)]
    if emit_state:
        out_specs.append(pl.BlockSpec((1, 2, hp, DK, DV), lambda i, h: (i, 0, h, 0, 0)))
        out_shape.append(jax.ShapeDtypeStruct((b, 2, N_HEADS, DK, DV), F32))
    return pl.pallas_call(
        functools.partial(_ret_kernel, nc=nc, hp=hp, has_state=has_state, emit_state=emit_state),
        grid=(b, N_HEADS // hp),
        in_specs=in_specs,
        out_specs=out_specs,
        out_shape=out_shape,
        scratch_shapes=[pltpu.VMEM((DK, DV), F32), pltpu.VMEM((DK, DV), F32), pltpu.VMEM((nc, DK, DV), BF16)],
        compiler_params=pltpu.CompilerParams(
            dimension_semantics=("parallel", "parallel"), vmem_limit_bytes=VMEM_LIMIT),
        name="retention",
    )(*args)


def _sqrt_unit(x):
    return x * lax.rsqrt(jnp.maximum(x, 1e-30))


def _lru_kernel(xs_ref, sg_ref, cw_ref, cb_ref, w_ref, bias_ref, ap_ref, h0_ref, pt_ref, out_ref, fin_ref,
                xp_scr, s1_scr, wm_scr, a_scr, b_scr, *, slabs, seq_len, width, nblk):
    r8 = LRU_ROWS
    tc = LRU_TC
    rows = PERM_ROWS
    cw = nblk * LRU_BLOCK
    nchunks = slabs // tc
    assert width == seq_len or width == tc

    def roll4(v):
        return pltpu.roll(v, LRU_SEQS, axis=0)

    xp_scr[pl.ds(0, 2 * r8), :] = jnp.zeros((2 * r8, cw), F32)
    xp_scr[pl.ds(2 * r8, slabs * r8), :] = xs_ref[0].astype(F32)
    xp_scr[pl.ds((slabs + 2) * r8, r8), :] = roll4(xs_ref[0, pl.ds((slabs - 1) * r8, r8), :].astype(F32))
    xp_scr[pl.ds((slabs + 3) * r8, r8), :] = roll4(xs_ref[0, pl.ds((slabs - 2) * r8, r8), :].astype(F32))

    step_in_chunk = lax.broadcasted_iota(jnp.int32, (rows, cw), 0) >> 3
    for s in range(-2, 3):
        tap = jnp.broadcast_to(cw_ref[s + 2][None], (tc, r8, cw)).reshape(rows, cw)
        if width != seq_len:
            tap = jnp.where((step_in_chunk + s >= 0) & (step_in_chunk + s < tc), tap, 0.0)
        wm_scr[s + 2] = tap

    c1_lo = (-0.5 * LRU_C / np.log(2.0)) * _softplus(-ap_ref[...])
    bias_lo = bias_ref[...]
    low_rows = (lax.broadcasted_iota(jnp.int32, (rows, LRU_BLOCK), 0) & (r8 - 1)) < LRU_SEQS
    high_rows = jnp.logical_not(low_rows)

    def conv_half(t0):
        xh = jnp.zeros((rows, cw), F32) + cb_ref[...]
        for s in range(-2, 3):
            xh = xh + xp_scr[pl.ds(pl.multiple_of((t0 + s + 2) * r8, r8), rows), :] * wm_scr[s + 2]
        xp_scr[pl.ds(pl.multiple_of(t0 * r8, rows), rows), :] = xh
        return xh

    def gates(xh, sweep2):
        fj = high_rows if sweep2 else low_rows
        c1 = roll4(c1_lo) if sweep2 else c1_lo
        for j in range(nblk):
            cols = slice(j * LRU_BLOCK, (j + 1) * LRU_BLOCK)
            xj = xh[:, cols]
            lhs = jnp.concatenate([jnp.where(fj, xj, 0.0), jnp.where(fj, 0.0, xj)], axis=1)
            pre = jnp.dot(lhs.astype(BF16), w_ref[j], preferred_element_type=F32)
            bj = roll4(bias_lo[j]) if sweep2 else bias_lo[j]
            pre = pre.reshape(tc, r8, 2 * LRU_BLOCK) + bj[None]
            tr = jnp.tanh(pre[:, :, :LRU_BLOCK])
            tg = jnp.tanh(pre[:, :, LRU_BLOCK:])
            c1j = c1[:, cols][None]
            a = jnp.exp2(c1j * tr + c1j)
            bco = _sqrt_unit(1.0 - a * a) * ((tg + 1.0) * xj.reshape(tc, r8, LRU_BLOCK))
            a_scr[:, cols] = a.reshape(rows, LRU_BLOCK)
            b_scr[:, cols] = bco.reshape(rows, LRU_BLOCK)

    def slab(t):
        return pl.ds(pl.multiple_of(t * r8, r8), r8)

    def sweep1_chunk(ci, h):
        t0 = ci * tc
        gates(conv_half(t0), False)

        def step(t, hh):
            hh = a_scr[slab(t), :] * hh + b_scr[slab(t), :]
            s1_scr[slab(t0 + t), :] = hh
            return hh

        return lax.fori_loop(0, tc, step, h, unroll=8)

    h_mid = lax.fori_loop(0, nchunks, sweep1_chunk, h0_ref[0])

    def sweep2_chunk(idx, h):
        t0 = (nchunks - 1 - idx) * tc
        crows = pl.ds(pl.multiple_of(t0 * r8, rows), rows)
        gates(xp_scr[crows, :], True)

        def step(i, hh):
            t = tc - 1 - i
            hh = a_scr[slab(t), :] * hh + b_scr[slab(t), :]
            b_scr[slab(t), :] = hh
            return hh

        h = lax.fori_loop(0, tc, step, h, unroll=8)
        o = ((s1_scr[crows, :] + b_scr[...]) * sg_ref[0, crows, :].astype(F32)).astype(BF16)
        nat = jnp.dot(pt_ref[...], o, preferred_element_type=F32).astype(out_ref.dtype)
        lo = pl.multiple_of(t0, tc)
        hi = pl.multiple_of(seq_len - tc - t0, tc)
        for b in range(LRU_SEQS):
            out_ref[0, b, pl.ds(lo, tc), :] = nat[b * tc:(b + 1) * tc]
            out_ref[0, b, pl.ds(hi, tc), :] = nat[(LRU_SEQS + b) * tc:(LRU_SEQS + b + 1) * tc]
        return h

    h_fin = lax.fori_loop(0, nchunks, sweep2_chunk, roll4(h_mid))
    fin_ref[0] = h_fin


def _lru(xs, gs, wsel, conv_b, w_blk, bias_blk, ap8, h0, perm_t, seq_len, width, nblk):
    groups = xs.shape[0]
    r8 = LRU_ROWS
    slabs = seq_len // 2
    cw = nblk * LRU_BLOCK
    return pl.pallas_call(
        functools.partial(_lru_kernel, slabs=slabs, seq_len=seq_len, width=width, nblk=nblk),
        grid=(groups, N_LRU_BLOCKS // nblk),
        in_specs=[pl.BlockSpec((1, slabs * r8, cw), lambda g, c: (g, 0, c)),
                  pl.BlockSpec((1, slabs * r8, cw), lambda g, c: (g, 0, c)),
                  pl.BlockSpec((5, r8, cw), lambda g, c: (0, 0, c)),
                  pl.BlockSpec((1, cw), lambda g, c: (0, c)),
                  pl.BlockSpec((nblk, 2 * LRU_BLOCK, 2 * LRU_BLOCK), lambda g, c: (c, 0, 0)),
                  pl.BlockSpec((nblk, r8, 2 * LRU_BLOCK), lambda g, c: (c, 0, 0)),
                  pl.BlockSpec((r8, cw), lambda g, c: (0, c)),
                  pl.BlockSpec((1, r8, cw), lambda g, c: (g, 0, c)),
                  pl.BlockSpec((PERM_ROWS, PERM_ROWS), lambda g, c: (0, 0))],
        out_specs=[pl.BlockSpec((1, LRU_SEQS, seq_len, cw), lambda g, c: (g, 0, 0, c)),
                   pl.BlockSpec((1, r8, cw), lambda g, c: (g, 0, c))],
        out_shape=[jax.ShapeDtypeStruct((groups, LRU_SEQS, seq_len, D_LRU), BF16),
                   jax.ShapeDtypeStruct((groups, r8, D_LRU), F32)],
        scratch_shapes=[pltpu.VMEM(((slabs + 4) * r8, cw), F32),
                        pltpu.VMEM((slabs * r8, cw), F32),
                        pltpu.VMEM((5, PERM_ROWS, cw), F32),
                        pltpu.VMEM((PERM_ROWS, cw), F32),
                        pltpu.VMEM((PERM_ROWS, cw), F32)],
        compiler_params=pltpu.CompilerParams(
            dimension_semantics=("parallel", "parallel"), vmem_limit_bytes=VMEM_LIMIT),
        name="lru",
    )(xs, gs, wsel, 0.5 * conv_b.reshape(1, -1), w_blk, bias_blk, ap8, h0, perm_t)


def _lru_params(conv_w, wa, ba, wx, bx, a_param):
    zero_tap = jnp.zeros((D_LRU,), F32)
    taps_lo = [conv_w[s + CONV_LEFT] if s + CONV_LEFT < CONV_W else zero_tap for s in range(-2, 3)]
    taps_hi = [conv_w[CONV_LEFT - s] if 0 <= CONV_LEFT - s < CONV_W else zero_tap for s in range(-2, 3)]
    wsel = jnp.stack([jnp.concatenate([jnp.broadcast_to(tl, (LRU_SEQS, D_LRU)),
                                       jnp.broadcast_to(th, (LRU_SEQS, D_LRU))], axis=0)
                      for tl, th in zip(taps_lo, taps_hi)], axis=0)
    w_blk = jnp.concatenate([jnp.concatenate([wa[0], wx[0]], axis=2),
                             jnp.concatenate([wa[1], wx[1]], axis=2)], axis=1).astype(BF16)
    ba_r = ba.reshape(2, N_LRU_BLOCKS, LRU_BLOCK)
    bx_r = bx.reshape(2, N_LRU_BLOCKS, LRU_BLOCK)
    bias_dir = jnp.concatenate([ba_r, bx_r], axis=2)
    bias_blk = jnp.repeat(bias_dir.transpose(1, 0, 2), LRU_SEQS, axis=1)
    ap8 = jnp.repeat(a_param, LRU_SEQS, axis=0)
    return 0.5 * wsel, w_blk, 0.5 * bias_blk, ap8


def _tail_kernel(x_ref, o_ref, l_ref, mr_ref, ml_ref, gate_ref, fnw_ref, wrd_ref, wld_ref, wo_ref, y_ref):
    ret_out = jnp.dot(o_ref[...], wrd_ref[...], preferred_element_type=F32)
    lru_out = jnp.dot(l_ref[...], wld_ref[...], preferred_element_type=F32)
    merged = mr_ref[...].astype(F32) * ret_out + ml_ref[...].astype(F32) * lru_out
    out = jnp.dot(merged.astype(BF16), wo_ref[...], preferred_element_type=F32)
    y = x_ref[...] + gate_ref[0] * out
    ms = jnp.mean(y * y, axis=-1, keepdims=True)
    y_ref[...] = y * lax.rsqrt(ms + EPS) * fnw_ref[...]


def _tail(x2d, o2d, l2d, z2d, gate, fnw, wrd_b, wld_b, wo_b, tiles_per_cond):
    m = x2d.shape[0]
    tm = 512
    kmr, kml = MAIN_MRET // D_MODEL, MAIN_MLRU // D_MODEL
    const = lambda i: (0, 0)
    return pl.pallas_call(
        _tail_kernel,
        grid=(m // tm,),
        in_specs=[pl.BlockSpec((tm, D_MODEL), lambda i: (i, 0)),
                  pl.BlockSpec((tm, D_V), lambda i: (i, 0)),
                  pl.BlockSpec((tm, D_LRU), lambda i: (i, 0)),
                  pl.BlockSpec((tm, D_MODEL), lambda i: (i, kmr)),
                  pl.BlockSpec((tm, D_MODEL), lambda i: (i, kml)),
                  pl.BlockSpec((1, 1, D_MODEL), lambda i: (i // tiles_per_cond, 0, 0)),
                  pl.BlockSpec((1, D_MODEL), const),
                  pl.BlockSpec((D_V, D_MODEL), const),
                  pl.BlockSpec((D_LRU, D_MODEL), const),
                  pl.BlockSpec((D_MODEL, D_MODEL), const)],
        out_specs=pl.BlockSpec((tm, D_MODEL), lambda i: (i, 0)),
        out_shape=jax.ShapeDtypeStruct((m, D_MODEL), F32),
        compiler_params=pltpu.CompilerParams(
            dimension_semantics=("parallel",), vmem_limit_bytes=VMEM_LIMIT),
        name="tail",
    )(x2d, o2d, l2d, z2d, z2d, gate, fnw.reshape(1, -1), wrd_b, wld_b, wo_b)


def _trunk(x, mod_rows, s0_ret, h0_lru, width, lru_nblk, params, final_norm_w, emit_state):
    (norm_w, w_main, w_lru, decay_logit, gn_w, wrd_b, conv_b, lru_par, wld_b, wo_b, perm, perm_t) = params
    wsel, w_blk, bias_blk, ap8 = lru_par
    b, l, _ = x.shape
    groups = b // LRU_SEQS
    x2d = x.reshape(b * l, D_MODEL)
    n_cond = mod_rows.shape[0]
    shift = mod_rows[:, None, 0:D_MODEL]
    scale = mod_rows[:, None, D_MODEL:2 * D_MODEL]
    gate = mod_rows[:, None, 2 * D_MODEL:]
    tokens_per_cond = b * l // n_cond
    z2d = _inproj(x2d, norm_w, scale, shift, gn_w, w_main, tokens_per_cond // 1024)
    z3 = z2d.reshape(b, l, D_MAIN)
    ret = _retention(z3, decay_logit, s0_ret, emit_state, N_HEADS if l == RET_CHUNK else 1)

    scale4 = jnp.broadcast_to(scale, (LRU_SEQS, 1, D_MODEL))[None]
    shift4 = jnp.broadcast_to(shift, (LRU_SEQS, 1, D_MODEL))[None]
    xs, gs = _inproj_lru(x.reshape(groups, LRU_SEQS, l, D_MODEL), norm_w, scale4, shift4, perm, w_lru)
    if h0_lru is None:
        h0 = jnp.zeros((groups, LRU_ROWS, D_LRU), F32)
    else:
        h0 = h0_lru.reshape(groups, LRU_SEQS, 2, D_LRU).transpose(0, 2, 1, 3).reshape(groups, LRU_ROWS, D_LRU)
    lru_pre, fin = _lru(xs, gs, wsel, conv_b, w_blk, bias_blk, ap8, h0, perm_t, l, width, lru_nblk)
    fin = fin.reshape(groups, 2, LRU_SEQS, D_LRU)
    lru_fin = jnp.stack([fin[:, 1], fin[:, 0]], axis=2).reshape(b, 2, D_LRU)

    y = _tail(x2d, ret[0].reshape(b * l, D_V), lru_pre.reshape(b * l, D_LRU), z2d, gate, final_norm_w,
              wrd_b, wld_b, wo_b, tokens_per_cond // 512)
    return y.reshape(b, l, D_MODEL), (ret[1] if emit_state else None), lru_fin


def kernel(x_prompt, x_sample, state_ret, state_lru, c, c_ctx, norm_w, w_ada, b_ada, w_in, ret_decay_logit,
           ret_gn_w, w_ret_down, conv_w, conv_b, lru_wa, lru_ba, lru_wx, lru_bx, lru_a_param, w_lru_down,
           w_out, final_norm_w):
    assert norm_w.shape[0] == 1, "single-layer step"
    n_dec = c.shape[0]
    cond8 = jnp.concatenate([c.astype(F32), c_ctx.astype(F32)[None],
                             jnp.zeros((8 - n_dec - 1, D_MODEL), F32)], axis=0)
    mod = _ada(cond8, w_ada[0], b_ada[0])
    w = w_in[0]
    w_main = jnp.concatenate([w[:, :OFF_XLRU], w[:, OFF_MRET:]], axis=1).astype(BF16)
    w_lru = w[:, OFF_XLRU:OFF_MRET].astype(BF16)
    perm_np = _slab_permutation()
    perm = jnp.asarray(perm_np, BF16)
    perm_t = jnp.asarray(perm_np.T, BF16)
    lru_par = _lru_params(conv_w[0], lru_wa[0], lru_ba[0], lru_wx[0], lru_bx[0], lru_a_param[0])
    params = (norm_w[0], w_main, w_lru, ret_decay_logit[0], ret_gn_w[0], w_ret_down[0].astype(BF16),
              conv_b[0], lru_par, w_lru_down[0].astype(BF16), w_out[0].astype(BF16), perm, perm_t)
    y_prompt, new_ret, new_lru = _trunk(x_prompt.astype(F32), mod[n_dec:n_dec + 1], None, None,
                                        x_prompt.shape[1], 5, params, final_norm_w, True)
    y_sample, _, _ = _trunk(x_sample.astype(F32), mod[:n_dec], state_ret[:, 0], state_lru[:, 0],
                            GRID_W, 2, params, final_norm_w, False)
    return (y_prompt.astype(x_prompt.dtype), y_sample.astype(x_sample.dtype),
            new_ret[:, None].astype(state_ret.dtype), new_lru[:, None].astype(state_lru.dtype))
```

```python
import functools

import jax
import jax.numpy as jnp
import numpy as np
from jax import lax
from jax.experimental import pallas as pl
from jax.experimental.pallas import tpu as pltpu

F32 = jnp.float32
BF16 = jnp.bfloat16

D_MODEL = 1024
N_HEADS = 4
DK = 256
DV = 512
D_QK = N_HEADS * DK
D_V = N_HEADS * DV
N_LRU_BLOCKS = 10
LRU_BLOCK = 128
D_LRU = N_LRU_BLOCKS * LRU_BLOCK
LRU_C = 8.0
CONV_W = 4
CONV_LEFT = 2
GRID_W = 64
EPS = 1e-6

OFF_Q = 0
OFF_K = OFF_Q + D_QK
OFF_V = OFF_K + D_QK
OFF_GRET = OFF_V + D_V
OFF_XLRU = OFF_GRET + D_V
OFF_GLRU = OFF_XLRU + D_LRU
OFF_MRET = OFF_GLRU + D_LRU
OFF_MLRU = OFF_MRET + D_MODEL
D_IN = OFF_MLRU + D_MODEL
D_MAIN = OFF_XLRU + 2 * D_MODEL
MAIN_MRET = OFF_XLRU
MAIN_MLRU = OFF_XLRU + D_MODEL

RET_CHUNK = 256
LRU_SEQS = 4
LRU_ROWS = 2 * LRU_SEQS
LRU_TC = 64
PERM_ROWS = LRU_TC * LRU_ROWS
VMEM_LIMIT = 56 * 1024 * 1024


def _sigmoid(x):
    return 0.5 * jnp.tanh(0.5 * x) + 0.5


def _silu(x):
    return x * _sigmoid(x)


def _softplus(x):
    return jnp.maximum(x, 0.0) + jnp.log1p(jnp.exp(-jnp.abs(x)))


def _slab_permutation():
    p = np.zeros((PERM_ROWS, PERM_ROWS), np.float32)
    for t in range(LRU_TC):
        for r in range(LRU_ROWS):
            if r < LRU_SEQS:
                src = r * LRU_TC + t
            else:
                src = LRU_SEQS * LRU_TC + (r - LRU_SEQS) * LRU_TC + (LRU_TC - 1 - t)
            p[t * LRU_ROWS + r, src] = 1.0
    return p


def _ada_kernel(c_ref, w_ref, b_ref, o_ref):
    cond = _silu(c_ref[...]).astype(BF16)
    o_ref[...] = jnp.dot(cond, w_ref[...].astype(BF16), preferred_element_type=F32) + b_ref[...]


def _ada(cond8, w_ada, b_ada):
    tn = 768
    return pl.pallas_call(
        _ada_kernel,
        grid=(3 * D_MODEL // tn,),
        in_specs=[pl.BlockSpec((8, D_MODEL), lambda j: (0, 0)),
                  pl.BlockSpec((D_MODEL, tn), lambda j: (0, j)),
                  pl.BlockSpec((1, tn), lambda j: (0, j))],
        out_specs=pl.BlockSpec((8, tn), lambda j: (0, j)),
        out_shape=jax.ShapeDtypeStruct((8, 3 * D_MODEL), F32),
        name="ada",
    )(cond8, w_ada, b_ada.reshape(1, -1))


def _modulated_norm(x, nw, scale, shift):
    ms = jnp.mean(x * x, axis=-1, keepdims=True)
    return (x * lax.rsqrt(ms + EPS) * nw) * (1.0 + scale) + shift


def _inproj_kernel(x_ref, nw_ref, sc_ref, sh_ref, w_ref, z_ref, h_scr):
    @pl.when(pl.program_id(1) == 0)
    def _():
        h_scr[...] = _modulated_norm(x_ref[...], nw_ref[...], sc_ref[0], sh_ref[0]).astype(BF16)

    z_ref[...] = jnp.dot(h_scr[...], w_ref[...], preferred_element_type=F32).astype(z_ref.dtype)


def _inproj(x2d, norm_w, scale, shift, w_main, tiles_per_cond):
    m = x2d.shape[0]
    tm, tn = 1024, 2048
    return pl.pallas_call(
        _inproj_kernel,
        grid=(m // tm, D_MAIN // tn),
        in_specs=[pl.BlockSpec((tm, D_MODEL), lambda i, j: (i, 0)),
                  pl.BlockSpec((1, D_MODEL), lambda i, j: (0, 0)),
                  pl.BlockSpec((1, 1, D_MODEL), lambda i, j: (i // tiles_per_cond, 0, 0)),
                  pl.BlockSpec((1, 1, D_MODEL), lambda i, j: (i // tiles_per_cond, 0, 0)),
                  pl.BlockSpec((D_MODEL, tn), lambda i, j: (0, j))],
        out_specs=pl.BlockSpec((tm, tn), lambda i, j: (i, j)),
        out_shape=jax.ShapeDtypeStruct((m, D_MAIN), BF16),
        scratch_shapes=[pltpu.VMEM((tm, D_MODEL), BF16)],
        compiler_params=pltpu.CompilerParams(
            dimension_semantics=("parallel", "arbitrary"), vmem_limit_bytes=VMEM_LIMIT),
        name="inproj",
    )(x2d, norm_w.reshape(1, -1), scale, shift, w_main)


def _inproj_lru_kernel(xa_ref, xb_ref, nw_ref, sc_ref, sh_ref, p_ref, w_ref, xs_ref, gs_ref):
    nw = nw_ref[...]
    sc = sc_ref[0]
    sh = sh_ref[0]
    half_rows = LRU_SEQS * LRU_TC
    ha = _modulated_norm(xa_ref[0], nw, sc, sh).astype(BF16).reshape(half_rows, D_MODEL)
    hb = _modulated_norm(xb_ref[0], nw, sc, sh).astype(BF16).reshape(half_rows, D_MODEL)
    h = jnp.concatenate([ha, hb], axis=0)
    hp = jnp.dot(p_ref[...], h, preferred_element_type=F32).astype(BF16)
    z = jnp.dot(hp, w_ref[...], preferred_element_type=F32)
    xs_ref[0] = z[:, :D_LRU].astype(xs_ref.dtype)
    gs_ref[0] = _silu(z[:, D_LRU:]).astype(gs_ref.dtype)


def _inproj_lru(x4, norm_w, scale4, shift4, perm, w_lru):
    groups, _, l, _ = x4.shape
    nt = l // 2 // LRU_TC
    ntb = l // LRU_TC
    out_sds = jax.ShapeDtypeStruct((groups, l // 2 * LRU_ROWS, D_LRU), BF16)
    return pl.pallas_call(
        _inproj_lru_kernel,
        grid=(groups, nt),
        in_specs=[pl.BlockSpec((1, LRU_SEQS, LRU_TC, D_MODEL), lambda g, i: (g, 0, i, 0)),
                  pl.BlockSpec((1, LRU_SEQS, LRU_TC, D_MODEL), lambda g, i: (g, 0, ntb - 1 - i, 0)),
                  pl.BlockSpec((1, D_MODEL), lambda g, i: (0, 0)),
                  pl.BlockSpec((1, LRU_SEQS, 1, D_MODEL), lambda g, i: (0, 0, 0, 0)),
                  pl.BlockSpec((1, LRU_SEQS, 1, D_MODEL), lambda g, i: (0, 0, 0, 0)),
                  pl.BlockSpec((PERM_ROWS, PERM_ROWS), lambda g, i: (0, 0)),
                  pl.BlockSpec((D_MODEL, 2 * D_LRU), lambda g, i: (0, 0))],
        out_specs=[pl.BlockSpec((1, PERM_ROWS, D_LRU), lambda g, i: (g, i, 0)),
                   pl.BlockSpec((1, PERM_ROWS, D_LRU), lambda g, i: (g, i, 0))],
        out_shape=[out_sds, out_sds],
        compiler_params=pltpu.CompilerParams(
            dimension_semantics=("parallel", "parallel"), vmem_limit_bytes=VMEM_LIMIT),
        name="inproj_lru",
    )(x4, x4, norm_w.reshape(1, -1), scale4, shift4, perm, w_lru)


def _dot_tn(a, b):
    return lax.dot_general(a, b, (((0,), (0,)), ((), ())), preferred_element_type=F32)


def _dot_nt(a, b):
    return lax.dot_general(a, b, (((1,), (1,)), ((), ())), preferred_element_type=F32)


def _ret_kernel(*refs, nc, hp, has_state, emit_state):
    dl_ref, q_ref, k_ref, v_ref, g_ref, gnw_ref = refs[:6]
    pos = 6
    s0_ref = None
    if has_state:
        s0_ref = refs[pos]
        pos += 1
    o_ref = refs[pos]
    pos += 1
    sfin_ref = None
    if emit_state:
        sfin_ref = refs[pos]
        pos += 1
    sf_scr, sb_scr, sbh_scr = refs[pos:pos + 3]

    c = RET_CHUNK
    carry_states = has_state or nc > 1
    ii = lax.broadcasted_iota(jnp.int32, (c, c), 0)
    jj = lax.broadcasted_iota(jnp.int32, (c, c), 1)
    diff = (ii - jj).astype(F32)
    p = lax.broadcasted_iota(jnp.int32, (c, 1), 0).astype(F32)
    kscale = DK ** -0.5

    def rows_of(n):
        return pl.ds(pl.multiple_of(n * c, c), c)

    def normed_out(o, rows, vcols):
        ms = jnp.mean(o * o, axis=-1, keepdims=True)
        on = o * lax.rsqrt(ms + EPS)
        gate = _silu(g_ref[0, rows, vcols].astype(F32)) * gnw_ref[:, vcols]
        o_ref[0, rows, vcols] = (on * gate).astype(o_ref.dtype)

    for hh in range(hp):
        head = pl.program_id(1) * hp + hh
        qcols = slice(hh * DK, (hh + 1) * DK)
        vcols = slice(hh * DV, (hh + 1) * DV)
        lgf = -_softplus(-jnp.full((1, 1), dl_ref[0, head], F32))
        lgb = -_softplus(-jnp.full((1, 1), dl_ref[1, head], F32))
        decay = jnp.exp(jnp.where(diff >= 0, lgf * diff, -lgb * diff)) * kscale
        kdf = jnp.exp(lgf * (c - 1.0 - p)) * kscale
        kdb = jnp.exp(lgb * p) * kscale

        if not carry_states:
            rows = pl.ds(0, c)
            qn = q_ref[0, rows, qcols]
            kn = k_ref[0, rows, qcols]
            vn = v_ref[0, rows, vcols]
            s = (_dot_nt(qn, kn) * decay).astype(BF16)
            normed_out(jnp.dot(s, vn, preferred_element_type=F32), rows, vcols)
            k32 = kn.astype(F32)
            if emit_state:
                sfin_ref[0, 0, hh] = _dot_tn((k32 * kdf).astype(BF16), vn)
                sfin_ref[0, 1, hh] = _dot_tn((k32 * kdb).astype(BF16), vn)
            continue

        qdf = jnp.exp(lgf * (p + 1.0))
        qdb = jnp.exp(lgb * (c - p))
        cdf = jnp.exp(lgf * c)
        cdb = jnp.exp(lgb * c)
        if has_state:
            sf_scr[...] = s0_ref[0, 0, hh]
            sb_scr[...] = s0_ref[0, 1, hh]
        else:
            sf_scr[...] = jnp.zeros_like(sf_scr)
            sb_scr[...] = jnp.zeros_like(sb_scr)

        def rev_body(idx, carry):
            n = nc - 1 - idx
            rows = rows_of(n)
            sbh_scr[n] = sb_scr[...].astype(BF16)
            kb = (k_ref[0, rows, qcols].astype(F32) * kdb).astype(BF16)
            sb_scr[...] = cdb * sb_scr[...] + _dot_tn(kb, v_ref[0, rows, vcols])
            return carry

        lax.fori_loop(0, nc, rev_body, 0, unroll=True)

        def fwd_body(n, carry):
            rows = rows_of(n)
            qn = q_ref[0, rows, qcols]
            kn = k_ref[0, rows, qcols]
            vn = v_ref[0, rows, vcols]
            s = (_dot_nt(qn, kn) * decay).astype(BF16)
            q32 = qn.astype(F32)
            o = (jnp.dot(s, vn, preferred_element_type=F32)
                 + jnp.dot((q32 * qdf).astype(BF16), sf_scr[...].astype(BF16), preferred_element_type=F32)
                 + jnp.dot((q32 * qdb).astype(BF16), sbh_scr[n], preferred_element_type=F32))
            normed_out(o, rows, vcols)
            kf = (kn.astype(F32) * kdf).astype(BF16)
            sf_scr[...] = cdf * sf_scr[...] + _dot_tn(kf, vn)
            return carry

        lax.fori_loop(0, nc, fwd_body, 0, unroll=True)

        if emit_state:
            sfin_ref[0, 0, hh] = sf_scr[...]
            sfin_ref[0, 1, hh] = sb_scr[...]


def _retention(z3, decay_logit, gn_w, s0, emit_state, hp):
    b, l, _ = z3.shape
    nc = l // RET_CHUNK
    has_state = s0 is not None
    kq, kv_ = OFF_K // (hp * DK), OFF_V // (hp * DV)
    kg = OFF_GRET // (hp * DV)
    in_specs = [pl.BlockSpec(memory_space=pltpu.SMEM),
                pl.BlockSpec((1, l, hp * DK), lambda i, h: (i, 0, h)),
                pl.BlockSpec((1, l, hp * DK), lambda i, h: (i, 0, kq + h)),
                pl.BlockSpec((1, l, hp * DV), lambda i, h: (i, 0, kv_ + h)),
                pl.BlockSpec((1, l, hp * DV), lambda i, h: (i, 0, kg + h)),
                pl.BlockSpec((1, hp * DV), lambda i, h: (0, h))]
    args = [decay_logit, z3, z3, z3, z3, gn_w.reshape(1, D_V)]
    if has_state:
        in_specs.append(pl.BlockSpec((1, 2, hp, DK, DV), lambda i, h: (i, 0, h, 0, 0)))
        args.append(s0)
    out_specs = [pl.BlockSpec((1, l, hp * DV), lambda i, h: (i, 0, h))]
    out_shape = [jax.ShapeDtypeStruct((b, l, D_V), BF16)]
    if emit_state:
        out_specs.append(pl.BlockSpec((1, 2, hp, DK, DV), lambda i, h: (i, 0, h, 0, 0)))
        out_shape.append(jax.ShapeDtypeStruct((b, 2, N_HEADS, DK, DV), F32))
    return pl.pallas_call(
        functools.partial(_ret_kernel, nc=nc, hp=hp, has_state=has_state, emit_state=emit_state),
        grid=(b, N_HEADS // hp),
        in_specs=in_specs,
        out_specs=out_specs,
        out_shape=out_shape,
        scratch_shapes=[pltpu.VMEM((DK, DV), F32), pltpu.VMEM((DK, DV), F32), pltpu.VMEM((nc, DK, DV), BF16)],
        compiler_params=pltpu.CompilerParams(
            dimension_semantics=("parallel", "parallel"), vmem_limit_bytes=VMEM_LIMIT),
        name="retention",
    )(*args)


def _sqrt_unit(x):
    return x * lax.rsqrt(jnp.maximum(x, 1e-30))


def _lru_kernel(xs_ref, sg_ref, cw_ref, cb_ref, w_ref, bias_ref, ap_ref, h0_ref, pt_ref, out_ref, fin_ref,
                xp_scr, s1_scr, wm_scr, a0_scr, b0_scr, a1_scr, b1_scr, *, slabs, seq_len, width, nblk):
    r8 = LRU_ROWS
    tc = LRU_TC
    rows = PERM_ROWS
    cw = nblk * LRU_BLOCK
    nchunks = slabs // tc
    assert width == seq_len or width == tc

    def roll4(v):
        return pltpu.roll(v, LRU_SEQS, axis=0)

    xp_scr[pl.ds(0, 2 * r8), :] = jnp.zeros((2 * r8, cw), F32)
    xp_scr[pl.ds(2 * r8, slabs * r8), :] = xs_ref[0].astype(F32)
    xp_scr[pl.ds((slabs + 2) * r8, r8), :] = roll4(xs_ref[0, pl.ds((slabs - 1) * r8, r8), :].astype(F32))
    xp_scr[pl.ds((slabs + 3) * r8, r8), :] = roll4(xs_ref[0, pl.ds((slabs - 2) * r8, r8), :].astype(F32))

    step_in_chunk = lax.broadcasted_iota(jnp.int32, (rows, cw), 0) >> 3
    for s in range(-2, 3):
        tap = jnp.broadcast_to(cw_ref[s + 2][None], (tc, r8, cw)).reshape(rows, cw)
        if width != seq_len:
            tap = jnp.where((step_in_chunk + s >= 0) & (step_in_chunk + s < tc), tap, 0.0)
        wm_scr[s + 2] = tap

    c1_lo = (-0.5 * LRU_C / np.log(2.0)) * _softplus(-ap_ref[...])
    bias_lo = bias_ref[...]
    low_rows = (lax.broadcasted_iota(jnp.int32, (rows, LRU_BLOCK), 0) & (r8 - 1)) < LRU_SEQS
    high_rows = jnp.logical_not(low_rows)

    def conv_half(t0):
        xh = jnp.zeros((rows, cw), F32) + cb_ref[...]
        for s in range(-2, 3):
            xh = xh + xp_scr[pl.ds(pl.multiple_of((t0 + s + 2) * r8, r8), rows), :] * wm_scr[s + 2]
        xp_scr[pl.ds(pl.multiple_of(t0 * r8, rows), rows), :] = xh
        return xh

    def gates(xh, sweep2, a_scr, b_scr):
        fj = high_rows if sweep2 else low_rows
        c1 = roll4(c1_lo) if sweep2 else c1_lo
        for j in range(nblk):
            cols = slice(j * LRU_BLOCK, (j + 1) * LRU_BLOCK)
            xj = xh[:, cols]
            lhs = jnp.concatenate([jnp.where(fj, xj, 0.0), jnp.where(fj, 0.0, xj)], axis=1)
            pre = jnp.dot(lhs.astype(BF16), w_ref[j], preferred_element_type=F32)
            bj = roll4(bias_lo[j]) if sweep2 else bias_lo[j]
            pre = pre.reshape(tc, r8, 2 * LRU_BLOCK) + bj[None]
            tr = jnp.tanh(pre[:, :, :LRU_BLOCK])
            tg = jnp.tanh(pre[:, :, LRU_BLOCK:])
            c1j = c1[:, cols][None]
            a = jnp.exp2(c1j * tr + c1j)
            bco = _sqrt_unit(1.0 - a * a) * ((tg + 1.0) * xj.reshape(tc, r8, LRU_BLOCK))
            a_scr[:, cols] = a.reshape(rows, LRU_BLOCK)
            b_scr[:, cols] = bco.reshape(rows, LRU_BLOCK)

    bufs = ((a0_scr, b0_scr), (a1_scr, b1_scr))

    def chunk_rows(t0):
        return pl.ds(pl.multiple_of(t0 * r8, rows), rows)

    def scan(buf, h, dst_ref, t0, descending):
        a_ref, b_ref = bufs[buf]
        for i in range(tc):
            t = tc - 1 - i if descending else i
            h = a_ref[t * r8:(t + 1) * r8, :] * h + b_ref[t * r8:(t + 1) * r8, :]
            if dst_ref is None:
                b_ref[t * r8:(t + 1) * r8, :] = h
            else:
                dst_ref[pl.ds(pl.multiple_of(t0 * r8, rows) + t * r8, r8), :] = h
        return h

    def gates1(ci, buf):
        gates(conv_half(ci * tc), False, *bufs[buf])

    def scan1(ci, buf, h):
        return scan(buf, h, s1_scr, ci * tc, False)

    npairs = nchunks // 2
    assert nchunks == 2 * npairs
    gates1(0, 0)

    def sweep1_pair(i, h):
        gates1(2 * i + 1, 1)
        h = scan1(2 * i, 0, h)
        gates1(2 * i + 2, 0)
        return scan1(2 * i + 1, 1, h)

    h = lax.fori_loop(0, npairs - 1, sweep1_pair, h0_ref[0])
    gates1(nchunks - 1, 1)
    h = scan1(nchunks - 2, 0, h)
    h = scan1(nchunks - 1, 1, h)

    def gates2(p, buf):
        gates(xp_scr[chunk_rows((nchunks - 1 - p) * tc), :], True, *bufs[buf])

    def scan2(p, buf, h):
        return scan(buf, h, None, 0, True)

    def finish(p, buf):
        t0 = (nchunks - 1 - p) * tc
        crows = chunk_rows(t0)
        o = ((s1_scr[crows, :] + bufs[buf][1][...]) * sg_ref[0, crows, :].astype(F32)).astype(BF16)
        nat = jnp.dot(pt_ref[...], o, preferred_element_type=F32).astype(out_ref.dtype)
        lo = pl.multiple_of(t0, tc)
        hi = pl.multiple_of(seq_len - tc - t0, tc)
        for b in range(LRU_SEQS):
            out_ref[0, b, pl.ds(lo, tc), :] = nat[b * tc:(b + 1) * tc]
            out_ref[0, b, pl.ds(hi, tc), :] = nat[(LRU_SEQS + b) * tc:(LRU_SEQS + b + 1) * tc]

    h = roll4(h)
    gates2(0, 0)
    gates2(1, 1)
    h = scan2(0, 0, h)

    def sweep2_pair(i, h):
        finish(2 * i, 0)
        gates2(2 * i + 2, 0)
        h = scan2(2 * i + 1, 1, h)
        finish(2 * i + 1, 1)
        gates2(2 * i + 3, 1)
        return scan2(2 * i + 2, 0, h)

    h = lax.fori_loop(0, npairs - 1, sweep2_pair, h)
    finish(nchunks - 2, 0)
    h = scan2(nchunks - 1, 1, h)
    finish(nchunks - 1, 1)
    fin_ref[0] = h


def _lru(xs, gs, wsel, conv_b, w_blk, bias_blk, ap8, h0, perm_t, seq_len, width, nblk):
    groups = xs.shape[0]
    r8 = LRU_ROWS
    slabs = seq_len // 2
    cw = nblk * LRU_BLOCK
    return pl.pallas_call(
        functools.partial(_lru_kernel, slabs=slabs, seq_len=seq_len, width=width, nblk=nblk),
        grid=(groups, N_LRU_BLOCKS // nblk),
        in_specs=[pl.BlockSpec((1, slabs * r8, cw), lambda g, c: (g, 0, c)),
                  pl.BlockSpec((1, slabs * r8, cw), lambda g, c: (g, 0, c)),
                  pl.BlockSpec((5, r8, cw), lambda g, c: (0, 0, c)),
                  pl.BlockSpec((1, cw), lambda g, c: (0, c)),
                  pl.BlockSpec((nblk, 2 * LRU_BLOCK, 2 * LRU_BLOCK), lambda g, c: (c, 0, 0)),
                  pl.BlockSpec((nblk, r8, 2 * LRU_BLOCK), lambda g, c: (c, 0, 0)),
                  pl.BlockSpec((r8, cw), lambda g, c: (0, c)),
                  pl.BlockSpec((1, r8, cw), lambda g, c: (g, 0, c)),
                  pl.BlockSpec((PERM_ROWS, PERM_ROWS), lambda g, c: (0, 0))],
        out_specs=[pl.BlockSpec((1, LRU_SEQS, seq_len, cw), lambda g, c: (g, 0, 0, c)),
                   pl.BlockSpec((1, r8, cw), lambda g, c: (g, 0, c))],
        out_shape=[jax.ShapeDtypeStruct((groups, LRU_SEQS, seq_len, D_LRU), BF16),
                   jax.ShapeDtypeStruct((groups, r8, D_LRU), F32)],
        scratch_shapes=[pltpu.VMEM(((slabs + 4) * r8, cw), F32),
                        pltpu.VMEM((slabs * r8, cw), F32),
                        pltpu.VMEM((5, PERM_ROWS, cw), F32),
                        pltpu.VMEM((PERM_ROWS, cw), F32),
                        pltpu.VMEM((PERM_ROWS, cw), F32),
                        pltpu.VMEM((PERM_ROWS, cw), F32),
                        pltpu.VMEM((PERM_ROWS, cw), F32)],
        compiler_params=pltpu.CompilerParams(
            dimension_semantics=("parallel", "parallel"), vmem_limit_bytes=VMEM_LIMIT),
        name="lru",
    )(xs, gs, wsel, 0.5 * conv_b.reshape(1, -1), w_blk, bias_blk, ap8, h0, perm_t)


def _lru_params(conv_w, wa, ba, wx, bx, a_param):
    zero_tap = jnp.zeros((D_LRU,), F32)
    taps_lo = [conv_w[s + CONV_LEFT] if s + CONV_LEFT < CONV_W else zero_tap for s in range(-2, 3)]
    taps_hi = [conv_w[CONV_LEFT - s] if 0 <= CONV_LEFT - s < CONV_W else zero_tap for s in range(-2, 3)]
    wsel = jnp.stack([jnp.concatenate([jnp.broadcast_to(tl, (LRU_SEQS, D_LRU)),
                                       jnp.broadcast_to(th, (LRU_SEQS, D_LRU))], axis=0)
                      for tl, th in zip(taps_lo, taps_hi)], axis=0)
    w_blk = jnp.concatenate([jnp.concatenate([wa[0], wx[0]], axis=2),
                             jnp.concatenate([wa[1], wx[1]], axis=2)], axis=1).astype(BF16)
    ba_r = ba.reshape(2, N_LRU_BLOCKS, LRU_BLOCK)
    bx_r = bx.reshape(2, N_LRU_BLOCKS, LRU_BLOCK)
    bias_dir = jnp.concatenate([ba_r, bx_r], axis=2)
    bias_blk = jnp.repeat(bias_dir.transpose(1, 0, 2), LRU_SEQS, axis=1)
    ap8 = jnp.repeat(a_param, LRU_SEQS, axis=0)
    return 0.5 * wsel, w_blk, 0.5 * bias_blk, ap8


def _tail_kernel(x_ref, o_ref, l_ref, mr_ref, ml_ref, gate_ref, fnw_ref, wrd_ref, wld_ref, wo_ref, y_ref):
    ret_out = jnp.dot(o_ref[...], wrd_ref[...], preferred_element_type=F32)
    lru_out = jnp.dot(l_ref[...], wld_ref[...], preferred_element_type=F32)
    merged = _sigmoid(mr_ref[...].astype(F32)) * ret_out + _sigmoid(ml_ref[...].astype(F32)) * lru_out
    out = jnp.dot(merged.astype(BF16), wo_ref[...], preferred_element_type=F32)
    y = x_ref[...] + gate_ref[0] * out
    ms = jnp.mean(y * y, axis=-1, keepdims=True)
    y_ref[...] = y * lax.rsqrt(ms + EPS) * fnw_ref[...]


def _tail(x2d, o2d, l2d, z2d, gate, fnw, wrd_b, wld_b, wo_b, tiles_per_cond):
    m = x2d.shape[0]
    tm = 512
    kmr, kml = MAIN_MRET // D_MODEL, MAIN_MLRU // D_MODEL
    const = lambda i: (0, 0)
    return pl.pallas_call(
        _tail_kernel,
        grid=(m // tm,),
        in_specs=[pl.BlockSpec((tm, D_MODEL), lambda i: (i, 0)),
                  pl.BlockSpec((tm, D_V), lambda i: (i, 0)),
                  pl.BlockSpec((tm, D_LRU), lambda i: (i, 0)),
                  pl.BlockSpec((tm, D_MODEL), lambda i: (i, kmr)),
                  pl.BlockSpec((tm, D_MODEL), lambda i: (i, kml)),
                  pl.BlockSpec((1, 1, D_MODEL), lambda i: (i // tiles_per_cond, 0, 0)),
                  pl.BlockSpec((1, D_MODEL), const),
                  pl.BlockSpec((D_V, D_MODEL), const),
                  pl.BlockSpec((D_LRU, D_MODEL), const),
                  pl.BlockSpec((D_MODEL, D_MODEL), const)],
        out_specs=pl.BlockSpec((tm, D_MODEL), lambda i: (i, 0)),
        out_shape=jax.ShapeDtypeStruct((m, D_MODEL), F32),
        compiler_params=pltpu.CompilerParams(
            dimension_semantics=("parallel",), vmem_limit_bytes=VMEM_LIMIT),
        name="tail",
    )(x2d, o2d, l2d, z2d, z2d, gate, fnw.reshape(1, -1), wrd_b, wld_b, wo_b)


def _trunk(x, mod_rows, s0_ret, h0_lru, width, lru_nblk, params, final_norm_w, emit_state):
    (norm_w, w_main, w_lru, decay_logit, gn_w, wrd_b, conv_b, lru_par, wld_b, wo_b, perm, perm_t) = params
    wsel, w_blk, bias_blk, ap8 = lru_par
    b, l, _ = x.shape
    groups = b // LRU_SEQS
    x2d = x.reshape(b * l, D_MODEL)
    n_cond = mod_rows.shape[0]
    shift = mod_rows[:, None, 0:D_MODEL]
    scale = mod_rows[:, None, D_MODEL:2 * D_MODEL]
    gate = mod_rows[:, None, 2 * D_MODEL:]
    tokens_per_cond = b * l // n_cond
    z2d = _inproj(x2d, norm_w, scale, shift, w_main, tokens_per_cond // 1024)
    z3 = z2d.reshape(b, l, D_MAIN)
    ret = _retention(z3, decay_logit, gn_w, s0_ret, emit_state, N_HEADS if l == RET_CHUNK else 1)

    scale4 = jnp.broadcast_to(scale, (LRU_SEQS, 1, D_MODEL))[None]
    shift4 = jnp.broadcast_to(shift, (LRU_SEQS, 1, D_MODEL))[None]
    xs, gs = _inproj_lru(x.reshape(groups, LRU_SEQS, l, D_MODEL), norm_w, scale4, shift4, perm, w_lru)
    if h0_lru is None:
        h0 = jnp.zeros((groups, LRU_ROWS, D_LRU), F32)
    else:
        h0 = h0_lru.reshape(groups, LRU_SEQS, 2, D_LRU).transpose(0, 2, 1, 3).reshape(groups, LRU_ROWS, D_LRU)
    lru_pre, fin = _lru(xs, gs, wsel, conv_b, w_blk, bias_blk, ap8, h0, perm_t, l, width, lru_nblk)
    fin = fin.reshape(groups, 2, LRU_SEQS, D_LRU)
    lru_fin = jnp.stack([fin[:, 1], fin[:, 0]], axis=2).reshape(b, 2, D_LRU)

    y = _tail(x2d, ret[0].reshape(b * l, D_V), lru_pre.reshape(b * l, D_LRU), z2d, gate, final_norm_w,
              wrd_b, wld_b, wo_b, tokens_per_cond // 512)
    return y.reshape(b, l, D_MODEL), (ret[1] if emit_state else None), lru_fin


def kernel(x_prompt, x_sample, state_ret, state_lru, c, c_ctx, norm_w, w_ada, b_ada, w_in, ret_decay_logit,
           ret_gn_w, w_ret_down, conv_w, conv_b, lru_wa, lru_ba, lru_wx, lru_bx, lru_a_param, w_lru_down,
           w_out, final_norm_w):
    assert norm_w.shape[0] == 1, "single-layer step"
    n_dec = c.shape[0]
    cond8 = jnp.concatenate([c.astype(F32), c_ctx.astype(F32)[None],
                             jnp.zeros((8 - n_dec - 1, D_MODEL), F32)], axis=0)
    mod = _ada(cond8, w_ada[0], b_ada[0])
    w = w_in[0]
    w_main = jnp.concatenate([w[:, :OFF_XLRU], w[:, OFF_MRET:]], axis=1).astype(BF16)
    w_lru = w[:, OFF_XLRU:OFF_MRET].astype(BF16)
    perm_np = _slab_permutation()
    perm = jnp.asarray(perm_np, BF16)
    perm_t = jnp.asarray(perm_np.T, BF16)
    lru_par = _lru_params(conv_w[0], lru_wa[0], lru_ba[0], lru_wx[0], lru_bx[0], lru_a_param[0])
    params = (norm_w[0], w_main, w_lru, ret_decay_logit[0], ret_gn_w[0], w_ret_down[0].astype(BF16),
              conv_b[0], lru_par, w_lru_down[0].astype(BF16), w_out[0].astype(BF16), perm, perm_t)
    y_prompt, new_ret, new_lru = _trunk(x_prompt.astype(F32), mod[n_dec:n_dec + 1], None, None,
                                        x_prompt.shape[1], 5, params, final_norm_w, True)
    y_sample, _, _ = _trunk(x_sample.astype(F32), mod[:n_dec], state_ret[:, 0], state_lru[:, 0],
                            GRID_W, 2, params, final_norm_w, False)
    return (y_prompt.astype(x_prompt.dtype), y_sample.astype(x_sample.dtype),
            new_ret[:, None].astype(state_ret.dtype), new_lru[:, None].astype(state_lru.dtype))
```

```python
import functools

import jax
import jax.numpy as jnp
import numpy as np
from jax import lax
from jax.experimental import pallas as pl
from jax.experimental.pallas import tpu as pltpu

F32 = jnp.float32
BF16 = jnp.bfloat16

D_MODEL = 1024
N_HEADS = 4
DK = 256
DV = 512
D_QK = N_HEADS * DK
D_V = N_HEADS * DV
N_LRU_BLOCKS = 10
LRU_BLOCK = 128
D_LRU = N_LRU_BLOCKS * LRU_BLOCK
LRU_C = 8.0
CONV_W = 4
CONV_LEFT = 2
GRID_W = 64
EPS = 1e-6

OFF_Q = 0
OFF_K = OFF_Q + D_QK
OFF_V = OFF_K + D_QK
OFF_GRET = OFF_V + D_V
OFF_XLRU = OFF_GRET + D_V
OFF_GLRU = OFF_XLRU + D_LRU
OFF_MRET = OFF_GLRU + D_LRU
OFF_MLRU = OFF_MRET + D_MODEL
D_IN = OFF_MLRU + D_MODEL
D_MAIN = OFF_XLRU + 2 * D_MODEL
MAIN_MRET = OFF_XLRU
MAIN_MLRU = OFF_XLRU + D_MODEL

RET_CHUNK = 256
LRU_SEQS = 4
LRU_ROWS = 2 * LRU_SEQS
LRU_TC = 64
PERM_ROWS = LRU_TC * LRU_ROWS
VMEM_LIMIT = 56 * 1024 * 1024


def _sigmoid(x):
    return 0.5 * jnp.tanh(0.5 * x) + 0.5


def _silu(x):
    return x * _sigmoid(x)


def _softplus(x):
    return jnp.maximum(x, 0.0) + jnp.log1p(jnp.exp(-jnp.abs(x)))


def _slab_permutation():
    p = np.zeros((PERM_ROWS, PERM_ROWS), np.float32)
    for t in range(LRU_TC):
        for r in range(LRU_ROWS):
            if r < LRU_SEQS:
                src = r * LRU_TC + t
            else:
                src = LRU_SEQS * LRU_TC + (r - LRU_SEQS) * LRU_TC + (LRU_TC - 1 - t)
            p[t * LRU_ROWS + r, src] = 1.0
    return p


def _ada_kernel(c_ref, w_ref, b_ref, o_ref):
    cond = _silu(c_ref[...]).astype(BF16)
    o_ref[...] = jnp.dot(cond, w_ref[...].astype(BF16), preferred_element_type=F32) + b_ref[...]


def _ada(cond8, w_ada, b_ada):
    tn = 768
    return pl.pallas_call(
        _ada_kernel,
        grid=(3 * D_MODEL // tn,),
        in_specs=[pl.BlockSpec((8, D_MODEL), lambda j: (0, 0)),
                  pl.BlockSpec((D_MODEL, tn), lambda j: (0, j)),
                  pl.BlockSpec((1, tn), lambda j: (0, j))],
        out_specs=pl.BlockSpec((8, tn), lambda j: (0, j)),
        out_shape=jax.ShapeDtypeStruct((8, 3 * D_MODEL), F32),
        name="ada",
    )(cond8, w_ada, b_ada.reshape(1, -1))


def _modulated_norm(x, nw, scale, shift):
    ms = jnp.mean(x * x, axis=-1, keepdims=True)
    return (x * lax.rsqrt(ms + EPS) * nw) * (1.0 + scale) + shift


INPROJ_TM = 1024
INPROJ_TN = 2048
INPROJ_NJ = 4
assert D_MAIN == INPROJ_NJ * INPROJ_TN and INPROJ_TM % INPROJ_NJ == 0


def _inproj_kernel(x0_ref, xn_ref, nw_ref, sc0_ref, sh0_ref, scn_ref, shn_ref, w_ref, z_ref, ha_scr, hb_scr):
    i = pl.program_id(0)
    j = pl.program_id(1)
    part = INPROJ_TM // INPROJ_NJ

    @pl.when((i == 0) & (j == 0))
    def _():
        ha_scr[...] = _modulated_norm(x0_ref[...], nw_ref[...], sc0_ref[0], sh0_ref[0]).astype(BF16)

    def step(h_cur, h_next):
        r = pl.ds(pl.multiple_of(j * part, part), part)
        h_next[r, :] = _modulated_norm(xn_ref[r, :], nw_ref[...], scn_ref[0], shn_ref[0]).astype(BF16)
        z_ref[...] = jnp.dot(h_cur[...], w_ref[...], preferred_element_type=F32).astype(z_ref.dtype)

    @pl.when(i % 2 == 0)
    def _():
        step(ha_scr, hb_scr)

    @pl.when(i % 2 == 1)
    def _():
        step(hb_scr, ha_scr)


def _inproj(x2d, norm_w, scale, shift, w_main, tiles_per_cond):
    m = x2d.shape[0]
    tm, tn = INPROJ_TM, INPROJ_TN
    nt = m // tm

    def nxt(i):
        return jnp.minimum(i + 1, nt - 1)

    return pl.pallas_call(
        _inproj_kernel,
        grid=(nt, D_MAIN // tn),
        in_specs=[pl.BlockSpec((tm, D_MODEL), lambda i, j: (0, 0)),
                  pl.BlockSpec((tm, D_MODEL), lambda i, j: (nxt(i), 0)),
                  pl.BlockSpec((1, D_MODEL), lambda i, j: (0, 0)),
                  pl.BlockSpec((1, 1, D_MODEL), lambda i, j: (0, 0, 0)),
                  pl.BlockSpec((1, 1, D_MODEL), lambda i, j: (0, 0, 0)),
                  pl.BlockSpec((1, 1, D_MODEL), lambda i, j: (nxt(i) // tiles_per_cond, 0, 0)),
                  pl.BlockSpec((1, 1, D_MODEL), lambda i, j: (nxt(i) // tiles_per_cond, 0, 0)),
                  pl.BlockSpec((D_MODEL, tn), lambda i, j: (0, j))],
        out_specs=pl.BlockSpec((tm, tn), lambda i, j: (i, j)),
        out_shape=jax.ShapeDtypeStruct((m, D_MAIN), BF16),
        scratch_shapes=[pltpu.VMEM((tm, D_MODEL), BF16), pltpu.VMEM((tm, D_MODEL), BF16)],
        compiler_params=pltpu.CompilerParams(
            dimension_semantics=("arbitrary", "arbitrary"), vmem_limit_bytes=VMEM_LIMIT),
        name="inproj",
    )(x2d, x2d, norm_w.reshape(1, -1), scale, shift, scale, shift, w_main)


def _inproj_lru_kernel(xa_ref, xb_ref, nw_ref, sc_ref, sh_ref, p_ref, w_ref, xs_ref, gs_ref):
    nw = nw_ref[...]
    sc = sc_ref[0]
    sh = sh_ref[0]
    half_rows = LRU_SEQS * LRU_TC
    ha = _modulated_norm(xa_ref[0], nw, sc, sh).astype(BF16).reshape(half_rows, D_MODEL)
    hb = _modulated_norm(xb_ref[0], nw, sc, sh).astype(BF16).reshape(half_rows, D_MODEL)
    h = jnp.concatenate([ha, hb], axis=0)
    hp = jnp.dot(p_ref[...], h, preferred_element_type=F32).astype(BF16)
    z = jnp.dot(hp, w_ref[...], preferred_element_type=F32)
    xs_ref[0] = z[:, :D_LRU].astype(xs_ref.dtype)
    gs_ref[0] = _silu(z[:, D_LRU:]).astype(gs_ref.dtype)


def _inproj_lru(x4, norm_w, scale4, shift4, perm, w_lru):
    groups, _, l, _ = x4.shape
    nt = l // 2 // LRU_TC
    ntb = l // LRU_TC
    out_sds = jax.ShapeDtypeStruct((groups, l // 2 * LRU_ROWS, D_LRU), BF16)
    return pl.pallas_call(
        _inproj_lru_kernel,
        grid=(groups, nt),
        in_specs=[pl.BlockSpec((1, LRU_SEQS, LRU_TC, D_MODEL), lambda g, i: (g, 0, i, 0)),
                  pl.BlockSpec((1, LRU_SEQS, LRU_TC, D_MODEL), lambda g, i: (g, 0, ntb - 1 - i, 0)),
                  pl.BlockSpec((1, D_MODEL), lambda g, i: (0, 0)),
                  pl.BlockSpec((1, LRU_SEQS, 1, D_MODEL), lambda g, i: (0, 0, 0, 0)),
                  pl.BlockSpec((1, LRU_SEQS, 1, D_MODEL), lambda g, i: (0, 0, 0, 0)),
                  pl.BlockSpec((PERM_ROWS, PERM_ROWS), lambda g, i: (0, 0)),
                  pl.BlockSpec((D_MODEL, 2 * D_LRU), lambda g, i: (0, 0))],
        out_specs=[pl.BlockSpec((1, PERM_ROWS, D_LRU), lambda g, i: (g, i, 0)),
                   pl.BlockSpec((1, PERM_ROWS, D_LRU), lambda g, i: (g, i, 0))],
        out_shape=[out_sds, out_sds],
        compiler_params=pltpu.CompilerParams(
            dimension_semantics=("parallel", "parallel"), vmem_limit_bytes=VMEM_LIMIT),
        name="inproj_lru",
    )(x4, x4, norm_w.reshape(1, -1), scale4, shift4, perm, w_lru)


def _dot_tn(a, b):
    return lax.dot_general(a, b, (((0,), (0,)), ((), ())), preferred_element_type=F32)


def _dot_nt(a, b):
    return lax.dot_general(a, b, (((1,), (1,)), ((), ())), preferred_element_type=F32)


def _ret_kernel(*refs, nc, hp, has_state, emit_state):
    dl_ref, q_ref, k_ref, v_ref, g_ref, gnw_ref = refs[:6]
    pos = 6
    s0_ref = None
    if has_state:
        s0_ref = refs[pos]
        pos += 1
    o_ref = refs[pos]
    pos += 1
    sfin_ref = None
    if emit_state:
        sfin_ref = refs[pos]
        pos += 1
    sf_scr, sb_scr, sbh_scr = refs[pos:pos + 3]

    c = RET_CHUNK
    carry_states = has_state or nc > 1
    ii = lax.broadcasted_iota(jnp.int32, (c, c), 0)
    jj = lax.broadcasted_iota(jnp.int32, (c, c), 1)
    diff = (ii - jj).astype(F32)
    p = lax.broadcasted_iota(jnp.int32, (c, 1), 0).astype(F32)
    kscale = DK ** -0.5

    def rows_of(n):
        return pl.ds(pl.multiple_of(n * c, c), c)

    def normed_out(o, rows, vcols):
        ms = jnp.mean(o * o, axis=-1, keepdims=True)
        on = o * lax.rsqrt(ms + EPS)
        gate = _silu(g_ref[0, rows, vcols].astype(F32)) * gnw_ref[:, vcols]
        o_ref[0, rows, vcols] = (on * gate).astype(o_ref.dtype)

    for hh in range(hp):
        head = pl.program_id(1) * hp + hh
        qcols = slice(hh * DK, (hh + 1) * DK)
        vcols = slice(hh * DV, (hh + 1) * DV)
        lgf = -_softplus(-jnp.full((1, 1), dl_ref[0, head], F32))
        lgb = -_softplus(-jnp.full((1, 1), dl_ref[1, head], F32))
        decay = jnp.exp(jnp.where(diff >= 0, lgf * diff, -lgb * diff)) * kscale
        kdf = jnp.exp(lgf * (c - 1.0 - p)) * kscale
        kdb = jnp.exp(lgb * p) * kscale

        if not carry_states:
            rows = pl.ds(0, c)
            qn = q_ref[0, rows, qcols]
            kn = k_ref[0, rows, qcols]
            vn = v_ref[0, rows, vcols]
            s = (_dot_nt(qn, kn) * decay).astype(BF16)
            normed_out(jnp.dot(s, vn, preferred_element_type=F32), rows, vcols)
            k32 = kn.astype(F32)
            if emit_state:
                sfin_ref[0, 0, hh] = _dot_tn((k32 * kdf).astype(BF16), vn)
                sfin_ref[0, 1, hh] = _dot_tn((k32 * kdb).astype(BF16), vn)
            continue

        qdf = jnp.exp(lgf * (p + 1.0))
        qdb = jnp.exp(lgb * (c - p))
        cdf = jnp.exp(lgf * c)
        cdb = jnp.exp(lgb * c)
        if has_state:
            sf_scr[...] = s0_ref[0, 0, hh]
            sb_scr[...] = s0_ref[0, 1, hh]
        else:
            sf_scr[...] = jnp.zeros_like(sf_scr)
            sb_scr[...] = jnp.zeros_like(sb_scr)

        def rev_body(idx, carry):
            n = nc - 1 - idx
            rows = rows_of(n)
            sbh_scr[n] = sb_scr[...].astype(BF16)
            kb = (k_ref[0, rows, qcols].astype(F32) * kdb).astype(BF16)
            sb_scr[...] = cdb * sb_scr[...] + _dot_tn(kb, v_ref[0, rows, vcols])
            return carry

        lax.fori_loop(0, nc, rev_body, 0, unroll=True)

        def fwd_body(n, carry):
            rows = rows_of(n)
            qn = q_ref[0, rows, qcols]
            kn = k_ref[0, rows, qcols]
            vn = v_ref[0, rows, vcols]
            s = (_dot_nt(qn, kn) * decay).astype(BF16)
            q32 = qn.astype(F32)
            o = (jnp.dot(s, vn, preferred_element_type=F32)
                 + jnp.dot((q32 * qdf).astype(BF16), sf_scr[...].astype(BF16), preferred_element_type=F32)
                 + jnp.dot((q32 * qdb).astype(BF16), sbh_scr[n], preferred_element_type=F32))
            normed_out(o, rows, vcols)
            kf = (kn.astype(F32) * kdf).astype(BF16)
            sf_scr[...] = cdf * sf_scr[...] + _dot_tn(kf, vn)
            return carry

        lax.fori_loop(0, nc, fwd_body, 0, unroll=True)

        if emit_state:
            sfin_ref[0, 0, hh] = sf_scr[...]
            sfin_ref[0, 1, hh] = sb_scr[...]


def _retention(z3, decay_logit, gn_w, s0, emit_state, hp):
    b, l, _ = z3.shape
    nc = l // RET_CHUNK
    has_state = s0 is not None
    kq, kv_ = OFF_K // (hp * DK), OFF_V // (hp * DV)
    kg = OFF_GRET // (hp * DV)
    in_specs = [pl.BlockSpec(memory_space=pltpu.SMEM),
                pl.BlockSpec((1, l, hp * DK), lambda i, h: (i, 0, h)),
                pl.BlockSpec((1, l, hp * DK), lambda i, h: (i, 0, kq + h)),
                pl.BlockSpec((1, l, hp * DV), lambda i, h: (i, 0, kv_ + h)),
                pl.BlockSpec((1, l, hp * DV), lambda i, h: (i, 0, kg + h)),
                pl.BlockSpec((1, hp * DV), lambda i, h: (0, h))]
    args = [decay_logit, z3, z3, z3, z3, gn_w.reshape(1, D_V)]
    if has_state:
        in_specs.append(pl.BlockSpec((1, 2, hp, DK, DV), lambda i, h: (i, 0, h, 0, 0)))
        args.append(s0)
    out_specs = [pl.BlockSpec((1, l, hp * DV), lambda i, h: (i, 0, h))]
    out_shape = [jax.ShapeDtypeStruct((b, l, D_V), BF16)]
    if emit_state:
        out_specs.append(pl.BlockSpec((1, 2, hp, DK, DV), lambda i, h: (i, 0, h, 0, 0)))
        out_shape.append(jax.ShapeDtypeStruct((b, 2, N_HEADS, DK, DV), F32))
    return pl.pallas_call(
        functools.partial(_ret_kernel, nc=nc, hp=hp, has_state=has_state, emit_state=emit_state),
        grid=(b, N_HEADS // hp),
        in_specs=in_specs,
        out_specs=out_specs,
        out_shape=out_shape,
        scratch_shapes=[pltpu.VMEM((DK, DV), F32), pltpu.VMEM((DK, DV), F32), pltpu.VMEM((nc, DK, DV), BF16)],
        compiler_params=pltpu.CompilerParams(
            dimension_semantics=("parallel", "parallel"), vmem_limit_bytes=VMEM_LIMIT),
        name="retention",
    )(*args)


def _sqrt_unit(x):
    return x * lax.rsqrt(jnp.maximum(x, 1e-30))


def _lru_kernel(xs_ref, sg_ref, cw_ref, cb_ref, w_ref, bias_ref, ap_ref, h0_ref, pt_ref, out_ref, fin_ref,
                xp_scr, s1_scr, wm_scr, a0_scr, b0_scr, a1_scr, b1_scr, *, slabs, seq_len, width, nblk):
    r8 = LRU_ROWS
    tc = LRU_TC
    rows = PERM_ROWS
    cw = nblk * LRU_BLOCK
    nchunks = slabs // tc
    assert width == seq_len or width == tc

    def roll4(v):
        return pltpu.roll(v, LRU_SEQS, axis=0)

    xp_scr[pl.ds(0, 2 * r8), :] = jnp.zeros((2 * r8, cw), F32)
    xp_scr[pl.ds(2 * r8, slabs * r8), :] = xs_ref[0].astype(F32)
    xp_scr[pl.ds((slabs + 2) * r8, r8), :] = roll4(xs_ref[0, pl.ds((slabs - 1) * r8, r8), :].astype(F32))
    xp_scr[pl.ds((slabs + 3) * r8, r8), :] = roll4(xs_ref[0, pl.ds((slabs - 2) * r8, r8), :].astype(F32))

    step_in_chunk = lax.broadcasted_iota(jnp.int32, (rows, cw), 0) >> 3
    for s in range(-2, 3):
        tap = jnp.broadcast_to(cw_ref[s + 2][None], (tc, r8, cw)).reshape(rows, cw)
        if width != seq_len:
            tap = jnp.where((step_in_chunk + s >= 0) & (step_in_chunk + s < tc), tap, 0.0)
        wm_scr[s + 2] = tap

    c1_lo = (-0.5 * LRU_C / np.log(2.0)) * _softplus(-ap_ref[...])
    bias_lo = bias_ref[...]
    low_rows = (lax.broadcasted_iota(jnp.int32, (rows, LRU_BLOCK), 0) & (r8 - 1)) < LRU_SEQS
    high_rows = jnp.logical_not(low_rows)

    def conv_half(t0):
        xh = jnp.zeros((rows, cw), F32) + cb_ref[...]
        for s in range(-2, 3):
            xh = xh + xp_scr[pl.ds(pl.multiple_of((t0 + s + 2) * r8, r8), rows), :] * wm_scr[s + 2]
        xp_scr[pl.ds(pl.multiple_of(t0 * r8, rows), rows), :] = xh
        return xh

    def gates(xh, sweep2, a_scr, b_scr):
        fj = high_rows if sweep2 else low_rows
        c1 = roll4(c1_lo) if sweep2 else c1_lo
        for j in range(nblk):
            cols = slice(j * LRU_BLOCK, (j + 1) * LRU_BLOCK)
            xj = xh[:, cols]
            lhs = jnp.concatenate([jnp.where(fj, xj, 0.0), jnp.where(fj, 0.0, xj)], axis=1)
            pre = jnp.dot(lhs.astype(BF16), w_ref[j], preferred_element_type=F32)
            bj = roll4(bias_lo[j]) if sweep2 else bias_lo[j]
            pre = pre.reshape(tc, r8, 2 * LRU_BLOCK) + bj[None]
            tr = jnp.tanh(pre[:, :, :LRU_BLOCK])
            tg = jnp.tanh(pre[:, :, LRU_BLOCK:])
            c1j = c1[:, cols][None]
            a = jnp.exp2(c1j * tr + c1j)
            bco = _sqrt_unit(1.0 - a * a) * ((tg + 1.0) * xj.reshape(tc, r8, LRU_BLOCK))
            a_scr[:, cols] = a.reshape(rows, LRU_BLOCK)
            b_scr[:, cols] = bco.reshape(rows, LRU_BLOCK)

    bufs = ((a0_scr, b0_scr), (a1_scr, b1_scr))

    def chunk_rows(t0):
        return pl.ds(pl.multiple_of(t0 * r8, rows), rows)

    def scan(buf, h, dst_ref, t0, descending):
        a_ref, b_ref = bufs[buf]
        for i in range(tc):
            t = tc - 1 - i if descending else i
            h = a_ref[t * r8:(t + 1) * r8, :] * h + b_ref[t * r8:(t + 1) * r8, :]
            if dst_ref is None:
                b_ref[t * r8:(t + 1) * r8, :] = h
            else:
                dst_ref[pl.ds(pl.multiple_of(t0 * r8, rows) + t * r8, r8), :] = h
        return h

    def gates1(ci, buf):
        gates(conv_half(ci * tc), False, *bufs[buf])

    def scan1(ci, buf, h):
        return scan(buf, h, s1_scr, ci * tc, False)

    npairs = nchunks // 2
    assert nchunks == 2 * npairs
    gates1(0, 0)

    def sweep1_pair(i, h):
        gates1(2 * i + 1, 1)
        h = scan1(2 * i, 0, h)
        gates1(2 * i + 2, 0)
        return scan1(2 * i + 1, 1, h)

    h = lax.fori_loop(0, npairs - 1, sweep1_pair, h0_ref[0])
    gates1(nchunks - 1, 1)
    h = scan1(nchunks - 2, 0, h)
    h = scan1(nchunks - 1, 1, h)

    def gates2(p, buf):
        gates(xp_scr[chunk_rows((nchunks - 1 - p) * tc), :], True, *bufs[buf])

    def scan2(p, buf, h):
        return scan(buf, h, None, 0, True)

    def finish(p, buf):
        t0 = (nchunks - 1 - p) * tc
        crows = chunk_rows(t0)
        o = ((s1_scr[crows, :] + bufs[buf][1][...]) * sg_ref[0, crows, :].astype(F32)).astype(BF16)
        nat = jnp.dot(pt_ref[...], o, preferred_element_type=F32).astype(out_ref.dtype)
        lo = pl.multiple_of(t0, tc)
        hi = pl.multiple_of(seq_len - tc - t0, tc)
        for b in range(LRU_SEQS):
            out_ref[0, b, pl.ds(lo, tc), :] = nat[b * tc:(b + 1) * tc]
            out_ref[0, b, pl.ds(hi, tc), :] = nat[(LRU_SEQS + b) * tc:(LRU_SEQS + b + 1) * tc]

    h = roll4(h)
    gates2(0, 0)
    gates2(1, 1)
    h = scan2(0, 0, h)

    def sweep2_pair(i, h):
        finish(2 * i, 0)
        gates2(2 * i + 2, 0)
        h = scan2(2 * i + 1, 1, h)
        finish(2 * i + 1, 1)
        gates2(2 * i + 3, 1)
        return scan2(2 * i + 2, 0, h)

    h = lax.fori_loop(0, npairs - 1, sweep2_pair, h)
    finish(nchunks - 2, 0)
    h = scan2(nchunks - 1, 1, h)
    finish(nchunks - 1, 1)
    fin_ref[0] = h


def _lru(xs, gs, wsel, conv_b, w_blk, bias_blk, ap8, h0, perm_t, seq_len, width, nblk):
    groups = xs.shape[0]
    r8 = LRU_ROWS
    slabs = seq_len // 2
    cw = nblk * LRU_BLOCK
    return pl.pallas_call(
        functools.partial(_lru_kernel, slabs=slabs, seq_len=seq_len, width=width, nblk=nblk),
        grid=(groups, N_LRU_BLOCKS // nblk),
        in_specs=[pl.BlockSpec((1, slabs * r8, cw), lambda g, c: (g, 0, c)),
                  pl.BlockSpec((1, slabs * r8, cw), lambda g, c: (g, 0, c)),
                  pl.BlockSpec((5, r8, cw), lambda g, c: (0, 0, c)),
                  pl.BlockSpec((1, cw), lambda g, c: (0, c)),
                  pl.BlockSpec((nblk, 2 * LRU_BLOCK, 2 * LRU_BLOCK), lambda g, c: (c, 0, 0)),
                  pl.BlockSpec((nblk, r8, 2 * LRU_BLOCK), lambda g, c: (c, 0, 0)),
                  pl.BlockSpec((r8, cw), lambda g, c: (0, c)),
                  pl.BlockSpec((1, r8, cw), lambda g, c: (g, 0, c)),
                  pl.BlockSpec((PERM_ROWS, PERM_ROWS), lambda g, c: (0, 0))],
        out_specs=[pl.BlockSpec((1, LRU_SEQS, seq_len, cw), lambda g, c: (g, 0, 0, c)),
                   pl.BlockSpec((1, r8, cw), lambda g, c: (g, 0, c))],
        out_shape=[jax.ShapeDtypeStruct((groups, LRU_SEQS, seq_len, D_LRU), BF16),
                   jax.ShapeDtypeStruct((groups, r8, D_LRU), F32)],
        scratch_shapes=[pltpu.VMEM(((slabs + 4) * r8, cw), F32),
                        pltpu.VMEM((slabs * r8, cw), F32),
                        pltpu.VMEM((5, PERM_ROWS, cw), F32),
                        pltpu.VMEM((PERM_ROWS, cw), F32),
                        pltpu.VMEM((PERM_ROWS, cw), F32),
                        pltpu.VMEM((PERM_ROWS, cw), F32),
                        pltpu.VMEM((PERM_ROWS, cw), F32)],
        compiler_params=pltpu.CompilerParams(
            dimension_semantics=("parallel", "parallel"), vmem_limit_bytes=VMEM_LIMIT),
        name="lru",
    )(xs, gs, wsel, 0.5 * conv_b.reshape(1, -1), w_blk, bias_blk, ap8, h0, perm_t)


def _lru_params(conv_w, wa, ba, wx, bx, a_param):
    zero_tap = jnp.zeros((D_LRU,), F32)
    taps_lo = [conv_w[s + CONV_LEFT] if s + CONV_LEFT < CONV_W else zero_tap for s in range(-2, 3)]
    taps_hi = [conv_w[CONV_LEFT - s] if 0 <= CONV_LEFT - s < CONV_W else zero_tap for s in range(-2, 3)]
    wsel = jnp.stack([jnp.concatenate([jnp.broadcast_to(tl, (LRU_SEQS, D_LRU)),
                                       jnp.broadcast_to(th, (LRU_SEQS, D_LRU))], axis=0)
                      for tl, th in zip(taps_lo, taps_hi)], axis=0)
    w_blk = jnp.concatenate([jnp.concatenate([wa[0], wx[0]], axis=2),
                             jnp.concatenate([wa[1], wx[1]], axis=2)], axis=1).astype(BF16)
    ba_r = ba.reshape(2, N_LRU_BLOCKS, LRU_BLOCK)
    bx_r = bx.reshape(2, N_LRU_BLOCKS, LRU_BLOCK)
    bias_dir = jnp.concatenate([ba_r, bx_r], axis=2)
    bias_blk = jnp.repeat(bias_dir.transpose(1, 0, 2), LRU_SEQS, axis=1)
    ap8 = jnp.repeat(a_param, LRU_SEQS, axis=0)
    return 0.5 * wsel, w_blk, 0.5 * bias_blk, ap8


def _tail_kernel(x_ref, o_ref, l_ref, mr_ref, ml_ref, gate_ref, fnw_ref, wrd_ref, wld_ref, wo_ref, y_ref):
    ret_out = jnp.dot(o_ref[...], wrd_ref[...], preferred_element_type=F32)
    lru_out = jnp.dot(l_ref[...], wld_ref[...], preferred_element_type=F32)
    merged = _sigmoid(mr_ref[...].astype(F32)) * ret_out + _sigmoid(ml_ref[...].astype(F32)) * lru_out
    out = jnp.dot(merged.astype(BF16), wo_ref[...], preferred_element_type=F32)
    y = x_ref[...] + gate_ref[0] * out
    ms = jnp.mean(y * y, axis=-1, keepdims=True)
    y_ref[...] = y * lax.rsqrt(ms + EPS) * fnw_ref[...]


TAIL_TM = 1024


def _tail(x2d, o2d, l2d, z2d, gate, fnw, wrd_b, wld_b, wo_b, tiles_per_cond):
    m = x2d.shape[0]
    tm = TAIL_TM
    kmr, kml = MAIN_MRET // D_MODEL, MAIN_MLRU // D_MODEL
    const = lambda i: (0, 0)
    once = dict(pipeline_mode=pl.Buffered(1))
    return pl.pallas_call(
        _tail_kernel,
        grid=(m // tm,),
        in_specs=[pl.BlockSpec((tm, D_MODEL), lambda i: (i, 0)),
                  pl.BlockSpec((tm, D_V), lambda i: (i, 0)),
                  pl.BlockSpec((tm, D_LRU), lambda i: (i, 0)),
                  pl.BlockSpec((tm, D_MODEL), lambda i: (i, kmr)),
                  pl.BlockSpec((tm, D_MODEL), lambda i: (i, kml)),
                  pl.BlockSpec((1, 1, D_MODEL), lambda i: (i // tiles_per_cond, 0, 0)),
                  pl.BlockSpec((1, D_MODEL), const),
                  pl.BlockSpec((D_V, D_MODEL), const, **once),
                  pl.BlockSpec((D_LRU, D_MODEL), const, **once),
                  pl.BlockSpec((D_MODEL, D_MODEL), const, **once)],
        out_specs=pl.BlockSpec((tm, D_MODEL), lambda i: (i, 0)),
        out_shape=jax.ShapeDtypeStruct((m, D_MODEL), F32),
        compiler_params=pltpu.CompilerParams(
            dimension_semantics=("parallel",), vmem_limit_bytes=VMEM_LIMIT),
        name="tail",
    )(x2d, o2d, l2d, z2d, z2d, gate, fnw.reshape(1, -1), wrd_b, wld_b, wo_b)


def _trunk(x, mod_rows, s0_ret, h0_lru, width, lru_nblk, params, final_norm_w, emit_state):
    (norm_w, w_main, w_lru, decay_logit, gn_w, wrd_b, conv_b, lru_par, wld_b, wo_b, perm, perm_t) = params
    wsel, w_blk, bias_blk, ap8 = lru_par
    b, l, _ = x.shape
    groups = b // LRU_SEQS
    x2d = x.reshape(b * l, D_MODEL)
    n_cond = mod_rows.shape[0]
    shift = mod_rows[:, None, 0:D_MODEL]
    scale = mod_rows[:, None, D_MODEL:2 * D_MODEL]
    gate = mod_rows[:, None, 2 * D_MODEL:]
    tokens_per_cond = b * l // n_cond
    z2d = _inproj(x2d, norm_w, scale, shift, w_main, tokens_per_cond // INPROJ_TM)
    z3 = z2d.reshape(b, l, D_MAIN)
    ret = _retention(z3, decay_logit, gn_w, s0_ret, emit_state, N_HEADS if l == RET_CHUNK else 1)

    scale4 = jnp.broadcast_to(scale, (LRU_SEQS, 1, D_MODEL))[None]
    shift4 = jnp.broadcast_to(shift, (LRU_SEQS, 1, D_MODEL))[None]
    xs, gs = _inproj_lru(x.reshape(groups, LRU_SEQS, l, D_MODEL), norm_w, scale4, shift4, perm, w_lru)
    if h0_lru is None:
        h0 = jnp.zeros((groups, LRU_ROWS, D_LRU), F32)
    else:
        h0 = h0_lru.reshape(groups, LRU_SEQS, 2, D_LRU).transpose(0, 2, 1, 3).reshape(groups, LRU_ROWS, D_LRU)
    lru_pre, fin = _lru(xs, gs, wsel, conv_b, w_blk, bias_blk, ap8, h0, perm_t, l, width, lru_nblk)
    fin = fin.reshape(groups, 2, LRU_SEQS, D_LRU)
    lru_fin = jnp.stack([fin[:, 1], fin[:, 0]], axis=2).reshape(b, 2, D_LRU)

    y = _tail(x2d, ret[0].reshape(b * l, D_V), lru_pre.reshape(b * l, D_LRU), z2d, gate, final_norm_w,
              wrd_b, wld_b, wo_b, tokens_per_cond // TAIL_TM)
    return y.reshape(b, l, D_MODEL), (ret[1] if emit_state else None), lru_fin


def kernel(x_prompt, x_sample, state_ret, state_lru, c, c_ctx, norm_w, w_ada, b_ada, w_in, ret_decay_logit,
           ret_gn_w, w_ret_down, conv_w, conv_b, lru_wa, lru_ba, lru_wx, lru_bx, lru_a_param, w_lru_down,
           w_out, final_norm_w):
    assert norm_w.shape[0] == 1, "single-layer step"
    n_dec = c.shape[0]
    cond8 = jnp.concatenate([c.astype(F32), c_ctx.astype(F32)[None],
                             jnp.zeros((8 - n_dec - 1, D_MODEL), F32)], axis=0)
    mod = _ada(cond8, w_ada[0], b_ada[0])
    w = w_in[0]
    w_main = jnp.concatenate([w[:, :OFF_XLRU], w[:, OFF_MRET:]], axis=1).astype(BF16)
    w_lru = w[:, OFF_XLRU:OFF_MRET].astype(BF16)
    perm_np = _slab_permutation()
    perm = jnp.asarray(perm_np, BF16)
    perm_t = jnp.asarray(perm_np.T, BF16)
    lru_par = _lru_params(conv_w[0], lru_wa[0], lru_ba[0], lru_wx[0], lru_bx[0], lru_a_param[0])
    params = (norm_w[0], w_main, w_lru, ret_decay_logit[0], ret_gn_w[0], w_ret_down[0].astype(BF16),
              conv_b[0], lru_par, w_lru_down[0].astype(BF16), w_out[0].astype(BF16), perm, perm_t)
    y_prompt, new_ret, new_lru = _trunk(x_prompt.astype(F32), mod[n_dec:n_dec + 1], None, None,
                                        x_prompt.shape[1], 5, params, final_norm_w, True)
    y_sample, _, _ = _trunk(x_sample.astype(F32), mod[:n_dec], state_ret[:, 0], state_lru[:, 0],
                            GRID_W, 2, params, final_norm_w, False)
    return (y_prompt.astype(x_prompt.dtype), y_sample.astype(x_sample.dtype),
            new_ret[:, None].astype(state_ret.dtype), new_lru[:, None].astype(state_lru.dtype))
```

```python
import functools

import jax
import jax.numpy as jnp
import numpy as np
from jax import lax
from jax.experimental import pallas as pl
from jax.experimental.pallas import tpu as pltpu

F32 = jnp.float32
BF16 = jnp.bfloat16

D_MODEL = 1024
N_HEADS = 4
DK = 256
DV = 512
D_QK = N_HEADS * DK
D_V = N_HEADS * DV
N_LRU_BLOCKS = 10
LRU_BLOCK = 128
D_LRU = N_LRU_BLOCKS * LRU_BLOCK
LRU_C = 8.0
CONV_W = 4
CONV_LEFT = 2
GRID_W = 64
EPS = 1e-6

OFF_Q = 0
OFF_K = OFF_Q + D_QK
OFF_V = OFF_K + D_QK
OFF_GRET = OFF_V + D_V
OFF_XLRU = OFF_GRET + D_V
OFF_GLRU = OFF_XLRU + D_LRU
OFF_MRET = OFF_GLRU + D_LRU
OFF_MLRU = OFF_MRET + D_MODEL
D_IN = OFF_MLRU + D_MODEL
D_MAIN = OFF_XLRU + 2 * D_MODEL
MAIN_MRET = OFF_XLRU
MAIN_MLRU = OFF_XLRU + D_MODEL

RET_CHUNK = 256
LRU_SEQS = 4
LRU_ROWS = 2 * LRU_SEQS
LRU_TC = 64
PERM_ROWS = LRU_TC * LRU_ROWS
VMEM_LIMIT = 56 * 1024 * 1024


def _sigmoid(x):
    return 0.5 * jnp.tanh(0.5 * x) + 0.5


def _silu(x):
    return x * _sigmoid(x)


def _softplus(x):
    return jnp.maximum(x, 0.0) + jnp.log1p(jnp.exp(-jnp.abs(x)))


def _slab_permutation():
    p = np.zeros((PERM_ROWS, PERM_ROWS), np.float32)
    for t in range(LRU_TC):
        for r in range(LRU_ROWS):
            if r < LRU_SEQS:
                src = r * LRU_TC + t
            else:
                src = LRU_SEQS * LRU_TC + (r - LRU_SEQS) * LRU_TC + (LRU_TC - 1 - t)
            p[t * LRU_ROWS + r, src] = 1.0
    return p


def _ada_kernel(c_ref, w_ref, b_ref, o_ref):
    cond = _silu(c_ref[...]).astype(BF16)
    o_ref[...] = jnp.dot(cond, w_ref[...].astype(BF16), preferred_element_type=F32) + b_ref[...]


def _ada(cond8, w_ada, b_ada):
    tn = 768
    return pl.pallas_call(
        _ada_kernel,
        grid=(3 * D_MODEL // tn,),
        in_specs=[pl.BlockSpec((8, D_MODEL), lambda j: (0, 0)),
                  pl.BlockSpec((D_MODEL, tn), lambda j: (0, j)),
                  pl.BlockSpec((1, tn), lambda j: (0, j))],
        out_specs=pl.BlockSpec((8, tn), lambda j: (0, j)),
        out_shape=jax.ShapeDtypeStruct((8, 3 * D_MODEL), F32),
        name="ada",
    )(cond8, w_ada, b_ada.reshape(1, -1))


def _modulated_norm(x, nw, scale, shift):
    ms = jnp.mean(x * x, axis=-1, keepdims=True)
    return (x * lax.rsqrt(ms + EPS) * nw) * (1.0 + scale) + shift


INPROJ_TM = 1024


def _inproj_kernel(x_ref, nw_ref, sc_ref, sh_ref, w_ref, z_ref, h_scr):
    @pl.when(pl.program_id(1) == 0)
    def _():
        h_scr[...] = _modulated_norm(x_ref[...], nw_ref[...], sc_ref[0], sh_ref[0]).astype(BF16)

    z_ref[...] = jnp.dot(h_scr[...], w_ref[...], preferred_element_type=F32).astype(z_ref.dtype)


def _inproj(x2d, norm_w, scale, shift, w_all, tiles_per_cond):
    m = x2d.shape[0]
    tm, tn = INPROJ_TM, 2048
    n_lead = OFF_XLRU // tn
    assert OFF_XLRU == n_lead * tn and D_MAIN - OFF_XLRU == tn

    def w_col(j):
        return jnp.where(j < n_lead, j * tn, OFF_MRET)

    return pl.pallas_call(
        _inproj_kernel,
        grid=(m // tm, D_MAIN // tn),
        in_specs=[pl.BlockSpec((tm, D_MODEL), lambda i, j: (i, 0)),
                  pl.BlockSpec((1, D_MODEL), lambda i, j: (0, 0)),
                  pl.BlockSpec((1, 1, D_MODEL), lambda i, j: (i // tiles_per_cond, 0, 0)),
                  pl.BlockSpec((1, 1, D_MODEL), lambda i, j: (i // tiles_per_cond, 0, 0)),
                  pl.BlockSpec((pl.Element(D_MODEL), pl.Element(tn)), lambda i, j: (0, w_col(j)))],
        out_specs=pl.BlockSpec((tm, tn), lambda i, j: (i, j)),
        out_shape=jax.ShapeDtypeStruct((m, D_MAIN), BF16),
        scratch_shapes=[pltpu.VMEM((tm, D_MODEL), BF16)],
        compiler_params=pltpu.CompilerParams(
            dimension_semantics=("parallel", "arbitrary"), vmem_limit_bytes=VMEM_LIMIT),
        name="inproj",
    )(x2d, norm_w.reshape(1, -1), scale, shift, w_all)


def _inproj_lru_kernel(xa_ref, xb_ref, nw_ref, sc_ref, sh_ref, p_ref, w_ref, xs_ref, gs_ref):
    nw = nw_ref[...]
    sc = sc_ref[0]
    sh = sh_ref[0]
    half_rows = LRU_SEQS * LRU_TC
    ha = _modulated_norm(xa_ref[0], nw, sc, sh).astype(BF16).reshape(half_rows, D_MODEL)
    hb = _modulated_norm(xb_ref[0], nw, sc, sh).astype(BF16).reshape(half_rows, D_MODEL)
    h = jnp.concatenate([ha, hb], axis=0)
    hp = jnp.dot(p_ref[...], h, preferred_element_type=F32).astype(BF16)
    z = jnp.dot(hp, w_ref[...], preferred_element_type=F32)
    xs_ref[0] = z[:, :D_LRU].astype(xs_ref.dtype)
    gs_ref[0] = _silu(z[:, D_LRU:]).astype(gs_ref.dtype)


def _inproj_lru(x4, norm_w, scale4, shift4, perm, w_all):
    groups, _, l, _ = x4.shape
    nt = l // 2 // LRU_TC
    ntb = l // LRU_TC
    out_sds = jax.ShapeDtypeStruct((groups, l // 2 * LRU_ROWS, D_LRU), BF16)
    return pl.pallas_call(
        _inproj_lru_kernel,
        grid=(groups, nt),
        in_specs=[pl.BlockSpec((1, LRU_SEQS, LRU_TC, D_MODEL), lambda g, i: (g, 0, i, 0)),
                  pl.BlockSpec((1, LRU_SEQS, LRU_TC, D_MODEL), lambda g, i: (g, 0, ntb - 1 - i, 0)),
                  pl.BlockSpec((1, D_MODEL), lambda g, i: (0, 0)),
                  pl.BlockSpec((1, LRU_SEQS, 1, D_MODEL), lambda g, i: (0, 0, 0, 0)),
                  pl.BlockSpec((1, LRU_SEQS, 1, D_MODEL), lambda g, i: (0, 0, 0, 0)),
                  pl.BlockSpec((PERM_ROWS, PERM_ROWS), lambda g, i: (0, 0)),
                  pl.BlockSpec((pl.Element(D_MODEL), pl.Element(2 * D_LRU)), lambda g, i: (0, OFF_XLRU))],
        out_specs=[pl.BlockSpec((1, PERM_ROWS, D_LRU), lambda g, i: (g, i, 0)),
                   pl.BlockSpec((1, PERM_ROWS, D_LRU), lambda g, i: (g, i, 0))],
        out_shape=[out_sds, out_sds],
        compiler_params=pltpu.CompilerParams(
            dimension_semantics=("parallel", "parallel"), vmem_limit_bytes=VMEM_LIMIT),
        name="inproj_lru",
    )(x4, x4, norm_w.reshape(1, -1), scale4, shift4, perm, w_all)


def _dot_tn(a, b):
    return lax.dot_general(a, b, (((0,), (0,)), ((), ())), preferred_element_type=F32)


def _dot_nt(a, b):
    return lax.dot_general(a, b, (((1,), (1,)), ((), ())), preferred_element_type=F32)


def _ret_kernel(*refs, nc, hp, has_state, emit_state):
    dl_ref, q_ref, k_ref, v_ref, hg_ref = refs[:5]
    pos = 5
    s0_ref = None
    if has_state:
        s0_ref = refs[pos]
        pos += 1
    o_ref = refs[pos]
    pos += 1
    sfin_ref = None
    if emit_state:
        sfin_ref = refs[pos]
        pos += 1
    sf_scr, sb_scr, sbh_scr = refs[pos:pos + 3]

    c = RET_CHUNK
    carry_states = has_state or nc > 1
    ii = lax.broadcasted_iota(jnp.int32, (c, c), 0)
    jj = lax.broadcasted_iota(jnp.int32, (c, c), 1)
    diff = (ii - jj).astype(F32)
    p = lax.broadcasted_iota(jnp.int32, (c, 1), 0).astype(F32)
    kscale = DK ** -0.5

    def rows_of(n):
        return pl.ds(pl.multiple_of(n * c, c), c)

    def normed_out(o, rows, vcols):
        ms = jnp.mean(o * o, axis=-1, keepdims=True)
        on = o * lax.rsqrt(ms + EPS)
        hg = hg_ref[0, rows, vcols].astype(F32)
        gate = hg * (jnp.tanh(hg) + 1.0)
        o_ref[0, rows, vcols] = (on * gate).astype(o_ref.dtype)

    for hh in range(hp):
        head = pl.program_id(1) * hp + hh
        qcols = slice(hh * DK, (hh + 1) * DK)
        vcols = slice(hh * DV, (hh + 1) * DV)
        lgf = -_softplus(-jnp.full((1, 1), dl_ref[0, head], F32))
        lgb = -_softplus(-jnp.full((1, 1), dl_ref[1, head], F32))
        decay = jnp.exp(jnp.where(diff >= 0, lgf * diff, -lgb * diff)) * kscale
        kdf = jnp.exp(lgf * (c - 1.0 - p)) * kscale
        kdb = jnp.exp(lgb * p) * kscale

        if not carry_states:
            rows = pl.ds(0, c)
            qn = q_ref[0, rows, qcols]
            kn = k_ref[0, rows, qcols]
            vn = v_ref[0, rows, vcols]
            s = (_dot_nt(qn, kn) * decay).astype(BF16)
            normed_out(jnp.dot(s, vn, preferred_element_type=F32), rows, vcols)
            k32 = kn.astype(F32)
            if emit_state:
                sfin_ref[0, 0, hh] = _dot_tn((k32 * kdf).astype(BF16), vn)
                sfin_ref[0, 1, hh] = _dot_tn((k32 * kdb).astype(BF16), vn)
            continue

        qdf = jnp.exp(lgf * (p + 1.0))
        qdb = jnp.exp(lgb * (c - p))
        cdf = jnp.exp(lgf * c)
        cdb = jnp.exp(lgb * c)
        if has_state:
            sf_scr[...] = s0_ref[0, 0, hh]
            sb_scr[...] = s0_ref[0, 1, hh]
        else:
            sf_scr[...] = jnp.zeros_like(sf_scr)
            sb_scr[...] = jnp.zeros_like(sb_scr)

        def rev_body(idx, carry):
            n = nc - 1 - idx
            rows = rows_of(n)
            sbh_scr[n] = sb_scr[...].astype(BF16)
            kb = (k_ref[0, rows, qcols].astype(F32) * kdb).astype(BF16)
            sb_scr[...] = cdb * sb_scr[...] + _dot_tn(kb, v_ref[0, rows, vcols])
            return carry

        lax.fori_loop(0, nc, rev_body, 0, unroll=True)

        def fwd_body(n, carry):
            rows = rows_of(n)
            qn = q_ref[0, rows, qcols]
            kn = k_ref[0, rows, qcols]
            vn = v_ref[0, rows, vcols]
            s = (_dot_nt(qn, kn) * decay).astype(BF16)
            q32 = qn.astype(F32)
            o = (jnp.dot(s, vn, preferred_element_type=F32)
                 + jnp.dot((q32 * qdf).astype(BF16), sf_scr[...].astype(BF16), preferred_element_type=F32)
                 + jnp.dot((q32 * qdb).astype(BF16), sbh_scr[n], preferred_element_type=F32))
            normed_out(o, rows, vcols)
            kf = (kn.astype(F32) * kdf).astype(BF16)
            sf_scr[...] = cdf * sf_scr[...] + _dot_tn(kf, vn)
            return carry

        lax.fori_loop(0, nc, fwd_body, 0, unroll=True)

        if emit_state:
            sfin_ref[0, 0, hh] = sf_scr[...]
            sfin_ref[0, 1, hh] = sb_scr[...]


def _retention(z3, decay_logit, s0, emit_state, hp):
    b, l, _ = z3.shape
    nc = l // RET_CHUNK
    has_state = s0 is not None
    kq, kv_ = OFF_K // (hp * DK), OFF_V // (hp * DV)
    kg = OFF_GRET // (hp * DV)
    in_specs = [pl.BlockSpec(memory_space=pltpu.SMEM),
                pl.BlockSpec((1, l, hp * DK), lambda i, h: (i, 0, h)),
                pl.BlockSpec((1, l, hp * DK), lambda i, h: (i, 0, kq + h)),
                pl.BlockSpec((1, l, hp * DV), lambda i, h: (i, 0, kv_ + h)),
                pl.BlockSpec((1, l, hp * DV), lambda i, h: (i, 0, kg + h))]
    args = [decay_logit, z3, z3, z3, z3]
    if has_state:
        in_specs.append(pl.BlockSpec((1, 2, hp, DK, DV), lambda i, h: (i, 0, h, 0, 0)))
        args.append(s0)
    out_specs = [pl.BlockSpec((1, l, hp * DV), lambda i, h: (i, 0, h))]
    out_shape = [jax.ShapeDtypeStruct((b, l, D_V), BF16)]
    if emit_state:
        out_specs.append(pl.BlockSpec((1, 2, hp, DK, DV), lambda i, h: (i, 0, h, 0, 0)))
        out_shape.append(jax.ShapeDtypeStruct((b, 2, N_HEADS, DK, DV), F32))
    return pl.pallas_call(
        functools.partial(_ret_kernel, nc=nc, hp=hp, has_state=has_state, emit_state=emit_state),
        grid=(b, N_HEADS // hp),
        in_specs=in_specs,
        out_specs=out_specs,
        out_shape=out_shape,
        scratch_shapes=[pltpu.VMEM((DK, DV), F32), pltpu.VMEM((DK, DV), F32), pltpu.VMEM((nc, DK, DV), BF16)],
        compiler_params=pltpu.CompilerParams(
            dimension_semantics=("parallel", "parallel"), vmem_limit_bytes=VMEM_LIMIT),
        name="retention",
    )(*args)


def _sqrt_unit(x):
    return x * lax.rsqrt(jnp.maximum(x, 1e-30))


def _lru_kernel(xs_ref, sg_ref, cw_ref, cb_ref, w_ref, bias_ref, ap_ref, h0_ref, pt_ref, out_ref, fin_ref,
                xp_scr, s1_scr, wm_scr, a0_scr, b0_scr, a1_scr, b1_scr, *, slabs, seq_len, width, nblk):
    r8 = LRU_ROWS
    tc = LRU_TC
    rows = PERM_ROWS
    cw = nblk * LRU_BLOCK
    nchunks = slabs // tc
    assert width == seq_len or width == tc

    def roll4(v):
        return pltpu.roll(v, LRU_SEQS, axis=0)

    xp_scr[pl.ds(0, 2 * r8), :] = jnp.zeros((2 * r8, cw), F32)
    xp_scr[pl.ds(2 * r8, slabs * r8), :] = xs_ref[0].astype(F32)
    xp_scr[pl.ds((slabs + 2) * r8, r8), :] = roll4(xs_ref[0, pl.ds((slabs - 1) * r8, r8), :].astype(F32))
    xp_scr[pl.ds((slabs + 3) * r8, r8), :] = roll4(xs_ref[0, pl.ds((slabs - 2) * r8, r8), :].astype(F32))

    step_in_chunk = lax.broadcasted_iota(jnp.int32, (rows, cw), 0) >> 3
    for s in range(-2, 3):
        tap = jnp.broadcast_to(cw_ref[s + 2][None], (tc, r8, cw)).reshape(rows, cw)
        if width != seq_len:
            tap = jnp.where((step_in_chunk + s >= 0) & (step_in_chunk + s < tc), tap, 0.0)
        wm_scr[s + 2] = tap

    c1_lo = (-0.5 * LRU_C / np.log(2.0)) * _softplus(-ap_ref[...])
    bias_lo = bias_ref[...]
    low_rows = (lax.broadcasted_iota(jnp.int32, (rows, LRU_BLOCK), 0) & (r8 - 1)) < LRU_SEQS
    high_rows = jnp.logical_not(low_rows)

    def conv_half(t0):
        xh = jnp.zeros((rows, cw), F32) + cb_ref[...]
        for s in range(-2, 3):
            xh = xh + xp_scr[pl.ds(pl.multiple_of((t0 + s + 2) * r8, r8), rows), :] * wm_scr[s + 2]
        xp_scr[pl.ds(pl.multiple_of(t0 * r8, rows), rows), :] = xh
        return xh

    def gates(xh, sweep2, a_scr, b_scr):
        fj = high_rows if sweep2 else low_rows
        c1 = roll4(c1_lo) if sweep2 else c1_lo
        for j in range(nblk):
            cols = slice(j * LRU_BLOCK, (j + 1) * LRU_BLOCK)
            xj = xh[:, cols]
            lhs = jnp.concatenate([jnp.where(fj, xj, 0.0), jnp.where(fj, 0.0, xj)], axis=1)
            pre = jnp.dot(lhs.astype(BF16), w_ref[j], preferred_element_type=F32)
            bj = roll4(bias_lo[j]) if sweep2 else bias_lo[j]
            pre = pre.reshape(tc, r8, 2 * LRU_BLOCK) + bj[None]
            tr = jnp.tanh(pre[:, :, :LRU_BLOCK])
            tg = jnp.tanh(pre[:, :, LRU_BLOCK:])
            c1j = c1[:, cols][None]
            a = jnp.exp2(c1j * tr + c1j)
            bco = _sqrt_unit(1.0 - a * a) * ((tg + 1.0) * xj.reshape(tc, r8, LRU_BLOCK))
            a_scr[:, cols] = a.reshape(rows, LRU_BLOCK)
            b_scr[:, cols] = bco.reshape(rows, LRU_BLOCK)

    bufs = ((a0_scr, b0_scr), (a1_scr, b1_scr))

    def chunk_rows(t0):
        return pl.ds(pl.multiple_of(t0 * r8, rows), rows)

    def scan(buf, h, dst_ref, t0, descending):
        a_ref, b_ref = bufs[buf]
        for i in range(tc):
            t = tc - 1 - i if descending else i
            h = a_ref[t * r8:(t + 1) * r8, :] * h + b_ref[t * r8:(t + 1) * r8, :]
            if dst_ref is None:
                b_ref[t * r8:(t + 1) * r8, :] = h
            else:
                dst_ref[pl.ds(pl.multiple_of(t0 * r8, rows) + t * r8, r8), :] = h
        return h

    def gates1(ci, buf):
        gates(conv_half(ci * tc), False, *bufs[buf])

    def scan1(ci, buf, h):
        return scan(buf, h, s1_scr, ci * tc, False)

    npairs = nchunks // 2
    assert nchunks == 2 * npairs
    gates1(0, 0)

    def sweep1_pair(i, h):
        gates1(2 * i + 1, 1)
        h = scan1(2 * i, 0, h)
        gates1(2 * i + 2, 0)
        return scan1(2 * i + 1, 1, h)

    h = lax.fori_loop(0, npairs - 1, sweep1_pair, h0_ref[0])
    gates1(nchunks - 1, 1)
    h = scan1(nchunks - 2, 0, h)
    h = scan1(nchunks - 1, 1, h)

    def gates2(p, buf):
        gates(xp_scr[chunk_rows((nchunks - 1 - p) * tc), :], True, *bufs[buf])

    def scan2(p, buf, h):
        return scan(buf, h, None, 0, True)

    def finish(p, buf):
        t0 = (nchunks - 1 - p) * tc
        crows = chunk_rows(t0)
        o = ((s1_scr[crows, :] + bufs[buf][1][...]) * sg_ref[0, crows, :].astype(F32)).astype(BF16)
        nat = jnp.dot(pt_ref[...], o, preferred_element_type=F32).astype(out_ref.dtype)
        lo = pl.multiple_of(t0, tc)
        hi = pl.multiple_of(seq_len - tc - t0, tc)
        for b in range(LRU_SEQS):
            out_ref[0, b, pl.ds(lo, tc), :] = nat[b * tc:(b + 1) * tc]
            out_ref[0, b, pl.ds(hi, tc), :] = nat[(LRU_SEQS + b) * tc:(LRU_SEQS + b + 1) * tc]

    h = roll4(h)
    gates2(0, 0)
    gates2(1, 1)
    h = scan2(0, 0, h)

    def sweep2_pair(i, h):
        finish(2 * i, 0)
        gates2(2 * i + 2, 0)
        h = scan2(2 * i + 1, 1, h)
        finish(2 * i + 1, 1)
        gates2(2 * i + 3, 1)
        return scan2(2 * i + 2, 0, h)

    h = lax.fori_loop(0, npairs - 1, sweep2_pair, h)
    finish(nchunks - 2, 0)
    h = scan2(nchunks - 1, 1, h)
    finish(nchunks - 1, 1)
    fin_ref[0] = h


def _lru(xs, gs, wsel, conv_b, w_blk, bias_blk, ap8, h0, perm_t, seq_len, width, nblk):
    groups = xs.shape[0]
    r8 = LRU_ROWS
    slabs = seq_len // 2
    cw = nblk * LRU_BLOCK
    return pl.pallas_call(
        functools.partial(_lru_kernel, slabs=slabs, seq_len=seq_len, width=width, nblk=nblk),
        grid=(groups, N_LRU_BLOCKS // nblk),
        in_specs=[pl.BlockSpec((1, slabs * r8, cw), lambda g, c: (g, 0, c)),
                  pl.BlockSpec((1, slabs * r8, cw), lambda g, c: (g, 0, c)),
                  pl.BlockSpec((5, r8, cw), lambda g, c: (0, 0, c)),
                  pl.BlockSpec((1, cw), lambda g, c: (0, c)),
                  pl.BlockSpec((nblk, 2 * LRU_BLOCK, 2 * LRU_BLOCK), lambda g, c: (c, 0, 0)),
                  pl.BlockSpec((nblk, r8, 2 * LRU_BLOCK), lambda g, c: (c, 0, 0)),
                  pl.BlockSpec((r8, cw), lambda g, c: (0, c)),
                  pl.BlockSpec((1, r8, cw), lambda g, c: (g, 0, c)),
                  pl.BlockSpec((PERM_ROWS, PERM_ROWS), lambda g, c: (0, 0))],
        out_specs=[pl.BlockSpec((1, LRU_SEQS, seq_len, cw), lambda g, c: (g, 0, 0, c)),
                   pl.BlockSpec((1, r8, cw), lambda g, c: (g, 0, c))],
        out_shape=[jax.ShapeDtypeStruct((groups, LRU_SEQS, seq_len, D_LRU), BF16),
                   jax.ShapeDtypeStruct((groups, r8, D_LRU), F32)],
        scratch_shapes=[pltpu.VMEM(((slabs + 4) * r8, cw), F32),
                        pltpu.VMEM((slabs * r8, cw), F32),
                        pltpu.VMEM((5, PERM_ROWS, cw), F32),
                        pltpu.VMEM((PERM_ROWS, cw), F32),
                        pltpu.VMEM((PERM_ROWS, cw), F32),
                        pltpu.VMEM((PERM_ROWS, cw), F32),
                        pltpu.VMEM((PERM_ROWS, cw), F32)],
        compiler_params=pltpu.CompilerParams(
            dimension_semantics=("parallel", "parallel"), vmem_limit_bytes=VMEM_LIMIT),
        name="lru",
    )(xs, gs, wsel, 0.5 * conv_b.reshape(1, -1), w_blk, bias_blk, ap8, h0, perm_t)


def _lru_params(conv_w, wa, ba, wx, bx, a_param):
    zero_tap = jnp.zeros((D_LRU,), F32)
    taps_lo = [conv_w[s + CONV_LEFT] if s + CONV_LEFT < CONV_W else zero_tap for s in range(-2, 3)]
    taps_hi = [conv_w[CONV_LEFT - s] if 0 <= CONV_LEFT - s < CONV_W else zero_tap for s in range(-2, 3)]
    wsel = jnp.stack([jnp.concatenate([jnp.broadcast_to(tl, (LRU_SEQS, D_LRU)),
                                       jnp.broadcast_to(th, (LRU_SEQS, D_LRU))], axis=0)
                      for tl, th in zip(taps_lo, taps_hi)], axis=0)
    w_blk = jnp.concatenate([jnp.concatenate([wa[0], wx[0]], axis=2),
                             jnp.concatenate([wa[1], wx[1]], axis=2)], axis=1).astype(BF16)
    ba_r = ba.reshape(2, N_LRU_BLOCKS, LRU_BLOCK)
    bx_r = bx.reshape(2, N_LRU_BLOCKS, LRU_BLOCK)
    bias_dir = jnp.concatenate([ba_r, bx_r], axis=2)
    bias_blk = jnp.repeat(bias_dir.transpose(1, 0, 2), LRU_SEQS, axis=1)
    ap8 = jnp.repeat(a_param, LRU_SEQS, axis=0)
    return 0.5 * wsel, w_blk, 0.5 * bias_blk, ap8


def _tail_kernel(x_ref, o_ref, l_ref, mr_ref, ml_ref, gate_ref, fnw_ref, wrd_ref, wld_ref, wo_ref, y_ref):
    ret_out = jnp.dot(o_ref[...], wrd_ref[...], preferred_element_type=F32)
    lru_out = jnp.dot(l_ref[...], wld_ref[...], preferred_element_type=F32)
    merged = _sigmoid(mr_ref[...].astype(F32)) * ret_out + _sigmoid(ml_ref[...].astype(F32)) * lru_out
    out = jnp.dot(merged.astype(BF16), wo_ref[...], preferred_element_type=F32)
    y = x_ref[...] + gate_ref[0] * out
    ms = jnp.mean(y * y, axis=-1, keepdims=True)
    y_ref[...] = y * lax.rsqrt(ms + EPS) * fnw_ref[...]


TAIL_TM = 512


def _tail(x2d, o2d, l2d, z2d, gate, fnw, wrd_b, wld_b, wo_b, tiles_per_cond):
    m = x2d.shape[0]
    tm = TAIL_TM
    kmr, kml = MAIN_MRET // D_MODEL, MAIN_MLRU // D_MODEL
    const = lambda i: (0, 0)
    return pl.pallas_call(
        _tail_kernel,
        grid=(m // tm,),
        in_specs=[pl.BlockSpec((tm, D_MODEL), lambda i: (i, 0)),
                  pl.BlockSpec((tm, D_V), lambda i: (i, 0)),
                  pl.BlockSpec((tm, D_LRU), lambda i: (i, 0)),
                  pl.BlockSpec((tm, D_MODEL), lambda i: (i, kmr)),
                  pl.BlockSpec((tm, D_MODEL), lambda i: (i, kml)),
                  pl.BlockSpec((1, 1, D_MODEL), lambda i: (i // tiles_per_cond, 0, 0)),
                  pl.BlockSpec((1, D_MODEL), const),
                  pl.BlockSpec((D_V, D_MODEL), const),
                  pl.BlockSpec((D_LRU, D_MODEL), const),
                  pl.BlockSpec((D_MODEL, D_MODEL), const)],
        out_specs=pl.BlockSpec((tm, D_MODEL), lambda i: (i, 0)),
        out_shape=jax.ShapeDtypeStruct((m, D_MODEL), F32),
        compiler_params=pltpu.CompilerParams(
            dimension_semantics=("parallel",), vmem_limit_bytes=VMEM_LIMIT),
        name="tail",
    )(x2d, o2d, l2d, z2d, z2d, gate, fnw.reshape(1, -1), wrd_b, wld_b, wo_b)


def _trunk(x, mod_rows, s0_ret, h0_lru, width, lru_nblk, params, final_norm_w, emit_state):
    (norm_w, w_all, decay_logit, wrd_b, conv_b, lru_par, wld_b, wo_b, perm, perm_t) = params
    wsel, w_blk, bias_blk, ap8 = lru_par
    b, l, _ = x.shape
    groups = b // LRU_SEQS
    x2d = x.reshape(b * l, D_MODEL)
    n_cond = mod_rows.shape[0]
    shift = mod_rows[:, None, 0:D_MODEL]
    scale = mod_rows[:, None, D_MODEL:2 * D_MODEL]
    gate = mod_rows[:, None, 2 * D_MODEL:]
    tokens_per_cond = b * l // n_cond
    z2d = _inproj(x2d, norm_w, scale, shift, w_all, tokens_per_cond // INPROJ_TM)
    z3 = z2d.reshape(b, l, D_MAIN)
    ret = _retention(z3, decay_logit, s0_ret, emit_state, N_HEADS if l == RET_CHUNK else 1)

    scale4 = jnp.broadcast_to(scale, (LRU_SEQS, 1, D_MODEL))[None]
    shift4 = jnp.broadcast_to(shift, (LRU_SEQS, 1, D_MODEL))[None]
    xs, gs = _inproj_lru(x.reshape(groups, LRU_SEQS, l, D_MODEL), norm_w, scale4, shift4, perm, w_all)
    if h0_lru is None:
        h0 = jnp.zeros((groups, LRU_ROWS, D_LRU), F32)
    else:
        h0 = h0_lru.reshape(groups, LRU_SEQS, 2, D_LRU).transpose(0, 2, 1, 3).reshape(groups, LRU_ROWS, D_LRU)
    lru_pre, fin = _lru(xs, gs, wsel, conv_b, w_blk, bias_blk, ap8, h0, perm_t, l, width, lru_nblk)
    fin = fin.reshape(groups, 2, LRU_SEQS, D_LRU)
    lru_fin = jnp.stack([fin[:, 1], fin[:, 0]], axis=2).reshape(b, 2, D_LRU)

    y = _tail(x2d, ret[0].reshape(b * l, D_V), lru_pre.reshape(b * l, D_LRU), z2d, gate, final_norm_w,
              wrd_b, wld_b, wo_b, tokens_per_cond // TAIL_TM)
    return y.reshape(b, l, D_MODEL), (ret[1] if emit_state else None), lru_fin


def kernel(x_prompt, x_sample, state_ret, state_lru, c, c_ctx, norm_w, w_ada, b_ada, w_in, ret_decay_logit,
           ret_gn_w, w_ret_down, conv_w, conv_b, lru_wa, lru_ba, lru_wx, lru_bx, lru_a_param, w_lru_down,
           w_out, final_norm_w):
    assert norm_w.shape[0] == 1, "single-layer step"
    n_dec = c.shape[0]
    cond8 = jnp.concatenate([c.astype(F32), c_ctx.astype(F32)[None],
                             jnp.zeros((8 - n_dec - 1, D_MODEL), F32)], axis=0)
    mod = _ada(cond8, w_ada[0], b_ada[0])
    col = jnp.arange(D_IN)
    col_scale = jnp.where((col >= OFF_GRET) & (col < OFF_XLRU), 0.5, 1.0).astype(F32)
    w_all = (w_in[0] * col_scale[None, :]).astype(BF16)
    wrd_b = (ret_gn_w[0][:, None] * w_ret_down[0]).astype(BF16)
    perm_np = _slab_permutation()
    perm = jnp.asarray(perm_np, BF16)
    perm_t = jnp.asarray(perm_np.T, BF16)
    lru_par = _lru_params(conv_w[0], lru_wa[0], lru_ba[0], lru_wx[0], lru_bx[0], lru_a_param[0])
    params = (norm_w[0], w_all, ret_decay_logit[0], wrd_b,
              conv_b[0], lru_par, w_lru_down[0].astype(BF16), w_out[0].astype(BF16), perm, perm_t)
    y_prompt, new_ret, new_lru = _trunk(x_prompt.astype(F32), mod[n_dec:n_dec + 1], None, None,
                                        x_prompt.shape[1], 5, params, final_norm_w, True)
    y_sample, _, _ = _trunk(x_sample.astype(F32), mod[:n_dec], state_ret[:, 0], state_lru[:, 0],
                            GRID_W, 2, params, final_norm_w, False)
    return (y_prompt.astype(x_prompt.dtype), y_sample.astype(x_sample.dtype),
            new_ret[:, None].astype(state_ret.dtype), new_lru[:, None].astype(state_lru.dtype))
```

```python
import functools

import jax
import jax.numpy as jnp
import numpy as np
from jax import lax
from jax.experimental import pallas as pl
from jax.experimental.pallas import tpu as pltpu

F32 = jnp.float32
BF16 = jnp.bfloat16

D_MODEL = 1024
N_HEADS = 4
DK = 256
DV = 512
D_QK = N_HEADS * DK
D_V = N_HEADS * DV
N_LRU_BLOCKS = 10
LRU_BLOCK = 128
D_LRU = N_LRU_BLOCKS * LRU_BLOCK
LRU_C = 8.0
CONV_W = 4
CONV_LEFT = 2
GRID_W = 64
EPS = 1e-6

OFF_Q = 0
OFF_K = OFF_Q + D_QK
OFF_V = OFF_K + D_QK
OFF_GRET = OFF_V + D_V
OFF_XLRU = OFF_GRET + D_V
OFF_GLRU = OFF_XLRU + D_LRU
OFF_MRET = OFF_GLRU + D_LRU
OFF_MLRU = OFF_MRET + D_MODEL
D_IN = OFF_MLRU + D_MODEL
D_MAIN = OFF_XLRU + 2 * D_MODEL
MAIN_MRET = OFF_XLRU
MAIN_MLRU = OFF_XLRU + D_MODEL

RET_CHUNK = 256
LRU_SEQS = 4
LRU_ROWS = 2 * LRU_SEQS
LRU_TC = 64
PERM_ROWS = LRU_TC * LRU_ROWS
PERM_UNIT = 32
VMEM_LIMIT = 56 * 1024 * 1024


def _sigmoid(x):
    return 0.5 * jnp.tanh(0.5 * x) + 0.5


def _silu(x):
    return x * _sigmoid(x)


def _softplus(x):
    return jnp.maximum(x, 0.0) + jnp.log1p(jnp.exp(-jnp.abs(x)))


def _slab_permutation(tc):
    n = tc * LRU_ROWS
    p = np.zeros((n, n), np.float32)
    for t in range(tc):
        for r in range(LRU_ROWS):
            if r < LRU_SEQS:
                src = r * tc + t
            else:
                src = LRU_SEQS * tc + (r - LRU_SEQS) * tc + (tc - 1 - t)
            p[t * LRU_ROWS + r, src] = 1.0
    return p


def _ada_kernel(c_ref, w_ref, b_ref, o_ref):
    cond = _silu(c_ref[...]).astype(BF16)
    o_ref[...] = jnp.dot(cond, w_ref[...].astype(BF16), preferred_element_type=F32) + b_ref[...]


def _ada(cond8, w_ada, b_ada):
    tn = 768
    return pl.pallas_call(
        _ada_kernel,
        grid=(3 * D_MODEL // tn,),
        in_specs=[pl.BlockSpec((8, D_MODEL), lambda j: (0, 0)),
                  pl.BlockSpec((D_MODEL, tn), lambda j: (0, j)),
                  pl.BlockSpec((1, tn), lambda j: (0, j))],
        out_specs=pl.BlockSpec((8, tn), lambda j: (0, j)),
        out_shape=jax.ShapeDtypeStruct((8, 3 * D_MODEL), F32),
        name="ada",
    )(cond8, w_ada, b_ada.reshape(1, -1))


def _modulated_norm(x, nw, scale, shift):
    ms = jnp.mean(x * x, axis=-1, keepdims=True)
    return (x * lax.rsqrt(ms + EPS)) * (nw * (1.0 + scale)) + shift


INPROJ_TM = 1024


def _inproj_kernel(x_ref, nw_ref, sc_ref, sh_ref, w_ref, z_ref, h_scr):
    @pl.when(pl.program_id(1) == 0)
    def _():
        h_scr[...] = _modulated_norm(x_ref[...], nw_ref[...], sc_ref[0], sh_ref[0]).astype(BF16)

    z_ref[...] = jnp.dot(h_scr[...], w_ref[...], preferred_element_type=F32).astype(z_ref.dtype)


def _inproj(x2d, norm_w, scale, shift, w_all, tiles_per_cond):
    m = x2d.shape[0]
    tm, tn = INPROJ_TM, 2048
    n_lead = OFF_XLRU // tn
    assert OFF_XLRU == n_lead * tn and D_MAIN - OFF_XLRU == tn

    def w_col(j):
        return jnp.where(j < n_lead, j * tn, OFF_MRET)

    return pl.pallas_call(
        _inproj_kernel,
        grid=(m // tm, D_MAIN // tn),
        in_specs=[pl.BlockSpec((tm, D_MODEL), lambda i, j: (i, 0)),
                  pl.BlockSpec((1, D_MODEL), lambda i, j: (0, 0)),
                  pl.BlockSpec((1, 1, D_MODEL), lambda i, j: (i // tiles_per_cond, 0, 0)),
                  pl.BlockSpec((1, 1, D_MODEL), lambda i, j: (i // tiles_per_cond, 0, 0)),
                  pl.BlockSpec((pl.Element(D_MODEL), pl.Element(tn)), lambda i, j: (0, w_col(j)))],
        out_specs=pl.BlockSpec((tm, tn), lambda i, j: (i, j)),
        out_shape=jax.ShapeDtypeStruct((m, D_MAIN), BF16),
        scratch_shapes=[pltpu.VMEM((tm, D_MODEL), BF16)],
        compiler_params=pltpu.CompilerParams(
            dimension_semantics=("parallel", "arbitrary"), vmem_limit_bytes=VMEM_LIMIT),
        name="inproj",
    )(x2d, norm_w.reshape(1, -1), scale, shift, w_all)


def _inproj_lru_kernel(xa_ref, xb_ref, nw_ref, sc_ref, sh_ref, p_ref, w_ref, xs_ref, gs_ref):
    nw = nw_ref[...]
    sc = sc_ref[0]
    sh = sh_ref[0]
    ha = _modulated_norm(xa_ref[0], nw, sc, sh).astype(BF16)
    hb = _modulated_norm(xb_ref[0], nw, sc, sh).astype(BF16)
    units = LRU_TC // PERM_UNIT
    pieces = []
    for u in range(units):
        lo, hi = u * PERM_UNIT, (units - 1 - u) * PERM_UNIT
        src = jnp.concatenate([ha[b, lo:lo + PERM_UNIT] for b in range(LRU_SEQS)]
                              + [hb[b, hi:hi + PERM_UNIT] for b in range(LRU_SEQS)], axis=0)
        pieces.append(jnp.dot(p_ref[...], src, preferred_element_type=F32).astype(BF16))
    hp = jnp.concatenate(pieces, axis=0)
    z = jnp.dot(hp, w_ref[...], preferred_element_type=F32)
    xs_ref[0] = z[:, :D_LRU].astype(xs_ref.dtype)
    gs_ref[0] = _silu(z[:, D_LRU:]).astype(gs_ref.dtype)


def _inproj_lru(x4, norm_w, scale4, shift4, perm, w_all):
    groups, _, l, _ = x4.shape
    nt = l // 2 // LRU_TC
    ntb = l // LRU_TC
    out_sds = jax.ShapeDtypeStruct((groups, l // 2 * LRU_ROWS, D_LRU), BF16)
    return pl.pallas_call(
        _inproj_lru_kernel,
        grid=(groups, nt),
        in_specs=[pl.BlockSpec((1, LRU_SEQS, LRU_TC, D_MODEL), lambda g, i: (g, 0, i, 0)),
                  pl.BlockSpec((1, LRU_SEQS, LRU_TC, D_MODEL), lambda g, i: (g, 0, ntb - 1 - i, 0)),
                  pl.BlockSpec((1, D_MODEL), lambda g, i: (0, 0)),
                  pl.BlockSpec((1, LRU_SEQS, 1, D_MODEL), lambda g, i: (0, 0, 0, 0)),
                  pl.BlockSpec((1, LRU_SEQS, 1, D_MODEL), lambda g, i: (0, 0, 0, 0)),
                  pl.BlockSpec((PERM_UNIT * LRU_ROWS, PERM_UNIT * LRU_ROWS), lambda g, i: (0, 0)),
                  pl.BlockSpec((pl.Element(D_MODEL), pl.Element(2 * D_LRU)), lambda g, i: (0, OFF_XLRU))],
        out_specs=[pl.BlockSpec((1, PERM_ROWS, D_LRU), lambda g, i: (g, i, 0)),
                   pl.BlockSpec((1, PERM_ROWS, D_LRU), lambda g, i: (g, i, 0))],
        out_shape=[out_sds, out_sds],
        compiler_params=pltpu.CompilerParams(
            dimension_semantics=("parallel", "parallel"), vmem_limit_bytes=VMEM_LIMIT),
        name="inproj_lru",
    )(x4, x4, norm_w.reshape(1, -1), scale4, shift4, perm, w_all)


def _dot_tn(a, b):
    return lax.dot_general(a, b, (((0,), (0,)), ((), ())), preferred_element_type=F32)


def _dot_nt(a, b):
    return lax.dot_general(a, b, (((1,), (1,)), ((), ())), preferred_element_type=F32)


def _ret_kernel(*refs, nc, hp, has_state, emit_state):
    dl_ref, q_ref, k_ref, v_ref, hg_ref = refs[:5]
    pos = 5
    s0_ref = None
    if has_state:
        s0_ref = refs[pos]
        pos += 1
    o_ref = refs[pos]
    pos += 1
    sfin_ref = None
    if emit_state:
        sfin_ref = refs[pos]
        pos += 1
    sf_scr, sb_scr, sbh_scr = refs[pos:pos + 3]

    c = RET_CHUNK
    carry_states = has_state or nc > 1
    ii = lax.broadcasted_iota(jnp.int32, (c, c), 0)
    jj = lax.broadcasted_iota(jnp.int32, (c, c), 1)
    diff = (ii - jj).astype(F32)
    p = lax.broadcasted_iota(jnp.int32, (c, 1), 0).astype(F32)
    kscale = DK ** -0.5

    def rows_of(n):
        return pl.ds(pl.multiple_of(n * c, c), c)

    def normed_out(o, rows, vcols):
        ms = jnp.mean(o * o, axis=-1, keepdims=True)
        on = o * lax.rsqrt(ms + EPS)
        hg = hg_ref[0, rows, vcols].astype(F32)
        gate = hg * (jnp.tanh(hg) + 1.0)
        o_ref[0, rows, vcols] = (on * gate).astype(o_ref.dtype)

    for hh in range(hp):
        head = pl.program_id(1) * hp + hh
        qcols = slice(hh * DK, (hh + 1) * DK)
        vcols = slice(hh * DV, (hh + 1) * DV)
        lgf = -_softplus(-jnp.full((1, 1), dl_ref[0, head], F32))
        lgb = -_softplus(-jnp.full((1, 1), dl_ref[1, head], F32))
        decay = jnp.exp(jnp.where(diff >= 0, lgf * diff, -lgb * diff)) * kscale
        kdf = jnp.exp(lgf * (c - 1.0 - p)) * kscale
        kdb = jnp.exp(lgb * p) * kscale

        if not carry_states:
            rows = pl.ds(0, c)
            qn = q_ref[0, rows, qcols]
            kn = k_ref[0, rows, qcols]
            vn = v_ref[0, rows, vcols]
            s = (_dot_nt(qn, kn) * decay).astype(BF16)
            normed_out(jnp.dot(s, vn, preferred_element_type=F32), rows, vcols)
            k32 = kn.astype(F32)
            if emit_state:
                sfin_ref[0, 0, hh] = _dot_tn((k32 * kdf).astype(BF16), vn)
                sfin_ref[0, 1, hh] = _dot_tn((k32 * kdb).astype(BF16), vn)
            continue

        qdf = jnp.exp(lgf * (p + 1.0))
        qdb = jnp.exp(lgb * (c - p))
        cdf = jnp.exp(lgf * c)
        cdb = jnp.exp(lgb * c)
        if has_state:
            sf_scr[...] = s0_ref[0, 0, hh]
            sb_scr[...] = s0_ref[0, 1, hh]
        else:
            sf_scr[...] = jnp.zeros_like(sf_scr)
            sb_scr[...] = jnp.zeros_like(sb_scr)

        def rev_body(idx, carry):
            n = nc - 1 - idx
            rows = rows_of(n)
            sbh_scr[n] = sb_scr[...].astype(BF16)
            kb = (k_ref[0, rows, qcols].astype(F32) * kdb).astype(BF16)
            sb_scr[...] = cdb * sb_scr[...] + _dot_tn(kb, v_ref[0, rows, vcols])
            return carry

        lax.fori_loop(0, nc, rev_body, 0, unroll=True)

        def fwd_body(n, carry):
            rows = rows_of(n)
            qn = q_ref[0, rows, qcols]
            kn = k_ref[0, rows, qcols]
            vn = v_ref[0, rows, vcols]
            s = (_dot_nt(qn, kn) * decay).astype(BF16)
            q32 = qn.astype(F32)
            o = (jnp.dot(s, vn, preferred_element_type=F32)
                 + jnp.dot((q32 * qdf).astype(BF16), sf_scr[...].astype(BF16), preferred_element_type=F32)
                 + jnp.dot((q32 * qdb).astype(BF16), sbh_scr[n], preferred_element_type=F32))
            normed_out(o, rows, vcols)
            kf = (kn.astype(F32) * kdf).astype(BF16)
            sf_scr[...] = cdf * sf_scr[...] + _dot_tn(kf, vn)
            return carry

        lax.fori_loop(0, nc, fwd_body, 0, unroll=True)

        if emit_state:
            sfin_ref[0, 0, hh] = sf_scr[...]
            sfin_ref[0, 1, hh] = sb_scr[...]


def _retention(z3, decay_logit, s0, emit_state, hp):
    b, l, _ = z3.shape
    nc = l // RET_CHUNK
    has_state = s0 is not None
    kq, kv_ = OFF_K // (hp * DK), OFF_V // (hp * DV)
    kg = OFF_GRET // (hp * DV)
    in_specs = [pl.BlockSpec(memory_space=pltpu.SMEM),
                pl.BlockSpec((1, l, hp * DK), lambda i, h: (i, 0, h)),
                pl.BlockSpec((1, l, hp * DK), lambda i, h: (i, 0, kq + h)),
                pl.BlockSpec((1, l, hp * DV), lambda i, h: (i, 0, kv_ + h)),
                pl.BlockSpec((1, l, hp * DV), lambda i, h: (i, 0, kg + h))]
    args = [decay_logit, z3, z3, z3, z3]
    if has_state:
        in_specs.append(pl.BlockSpec((1, 2, hp, DK, DV), lambda i, h: (i, 0, h, 0, 0)))
        args.append(s0)
    out_specs = [pl.BlockSpec((1, l, hp * DV), lambda i, h: (i, 0, h))]
    out_shape = [jax.ShapeDtypeStruct((b, l, D_V), BF16)]
    if emit_state:
        out_specs.append(pl.BlockSpec((1, 2, hp, DK, DV), lambda i, h: (i, 0, h, 0, 0)))
        out_shape.append(jax.ShapeDtypeStruct((b, 2, N_HEADS, DK, DV), F32))
    return pl.pallas_call(
        functools.partial(_ret_kernel, nc=nc, hp=hp, has_state=has_state, emit_state=emit_state),
        grid=(b, N_HEADS // hp),
        in_specs=in_specs,
        out_specs=out_specs,
        out_shape=out_shape,
        scratch_shapes=[pltpu.VMEM((DK, DV), F32), pltpu.VMEM((DK, DV), F32), pltpu.VMEM((nc, DK, DV), BF16)],
        compiler_params=pltpu.CompilerParams(
            dimension_semantics=("parallel", "parallel"), vmem_limit_bytes=VMEM_LIMIT),
        name="retention",
    )(*args)


def _sqrt_unit(x):
    return x * lax.rsqrt(jnp.maximum(x, 1e-30))


def _lru_kernel(xs_ref, sg_ref, cw_ref, cb_ref, w_ref, bias_ref, ap_ref, h0_ref, pt_ref, out_ref, fin_ref,
                xp_scr, s1_scr, wm_scr, a0_scr, b0_scr, a1_scr, b1_scr, *, slabs, seq_len, width, nblk):
    r8 = LRU_ROWS
    tc = LRU_TC
    rows = PERM_ROWS
    cw = nblk * LRU_BLOCK
    nchunks = slabs // tc
    assert width == seq_len or width == tc

    def roll4(v):
        return pltpu.roll(v, LRU_SEQS, axis=0)

    xp_scr[pl.ds(0, 2 * r8), :] = jnp.zeros((2 * r8, cw), F32)
    xp_scr[pl.ds(2 * r8, slabs * r8), :] = xs_ref[0].astype(F32)
    xp_scr[pl.ds((slabs + 2) * r8, r8), :] = roll4(xs_ref[0, pl.ds((slabs - 1) * r8, r8), :].astype(F32))
    xp_scr[pl.ds((slabs + 3) * r8, r8), :] = roll4(xs_ref[0, pl.ds((slabs - 2) * r8, r8), :].astype(F32))

    row_cw = lax.broadcasted_iota(jnp.int32, (rows, cw), 0)
    step_in_chunk = row_cw >> 3
    low_cw = (row_cw & (r8 - 1)) < LRU_SEQS

    def tap_table(s):
        tap = jnp.broadcast_to(cw_ref[s + 2][None], (tc, r8, cw)).reshape(rows, cw)
        if width != seq_len:
            tap = jnp.where((step_in_chunk + s >= 0) & (step_in_chunk + s < tc), tap, 0.0)
        return tap

    wm_scr[0] = tap_table(-2) + tap_table(2)
    for s in range(-1, 2):
        wm_scr[s + 2] = tap_table(s)

    c1_lo = (-0.5 * LRU_C / np.log(2.0)) * _softplus(-ap_ref[...])
    bias_lo = bias_ref[...]
    low_rows = (lax.broadcasted_iota(jnp.int32, (rows, LRU_BLOCK), 0) & (r8 - 1)) < LRU_SEQS
    high_rows = jnp.logical_not(low_rows)

    def conv_half(t0):
        def shifted(s):
            return xp_scr[pl.ds(pl.multiple_of((t0 + s + 2) * r8, r8), rows), :]

        xh = jnp.where(low_cw, shifted(-2), shifted(2)) * wm_scr[0] + cb_ref[...]
        for s in range(-1, 2):
            xh = xh + shifted(s) * wm_scr[s + 2]
        xp_scr[pl.ds(pl.multiple_of(t0 * r8, rows), rows), :] = xh
        return xh

    def gates(xh, sweep2, a_scr, b_scr):
        fj = high_rows if sweep2 else low_rows
        c1 = roll4(c1_lo) if sweep2 else c1_lo
        for j in range(nblk):
            cols = slice(j * LRU_BLOCK, (j + 1) * LRU_BLOCK)
            xj = xh[:, cols]
            lhs = jnp.concatenate([jnp.where(fj, xj, 0.0), jnp.where(fj, 0.0, xj)], axis=1)
            pre = jnp.dot(lhs.astype(BF16), w_ref[j], preferred_element_type=F32)
            bj = roll4(bias_lo[j]) if sweep2 else bias_lo[j]
            pre = pre.reshape(tc, r8, 2 * LRU_BLOCK) + bj[None]
            tr = jnp.tanh(pre[:, :, :LRU_BLOCK])
            tg = jnp.tanh(pre[:, :, LRU_BLOCK:])
            c1j = c1[:, cols][None]
            a = jnp.exp2(c1j * tr + c1j)
            bco = _sqrt_unit(1.0 - a * a) * ((tg + 1.0) * xj.reshape(tc, r8, LRU_BLOCK))
            a_scr[:, cols] = a.reshape(rows, LRU_BLOCK)
            b_scr[:, cols] = bco.reshape(rows, LRU_BLOCK)

    bufs = ((a0_scr, b0_scr), (a1_scr, b1_scr))

    def chunk_rows(t0):
        return pl.ds(pl.multiple_of(t0 * r8, rows), rows)

    def scan(buf, h, dst_ref, t0, descending):
        a_ref, b_ref = bufs[buf]
        for i in range(tc):
            t = tc - 1 - i if descending else i
            h = a_ref[t * r8:(t + 1) * r8, :] * h + b_ref[t * r8:(t + 1) * r8, :]
            if dst_ref is None:
                b_ref[t * r8:(t + 1) * r8, :] = h
            else:
                dst_ref[pl.ds(pl.multiple_of(t0 * r8, rows) + t * r8, r8), :] = h
        return h

    def gates1(ci, buf):
        gates(conv_half(ci * tc), False, *bufs[buf])

    def scan1(ci, buf, h):
        return scan(buf, h, s1_scr, ci * tc, False)

    npairs = nchunks // 2
    assert nchunks == 2 * npairs
    gates1(0, 0)

    def sweep1_pair(i, h):
        gates1(2 * i + 1, 1)
        h = scan1(2 * i, 0, h)
        gates1(2 * i + 2, 0)
        return scan1(2 * i + 1, 1, h)

    h = lax.fori_loop(0, npairs - 1, sweep1_pair, h0_ref[0])
    gates1(nchunks - 1, 1)
    h = scan1(nchunks - 2, 0, h)
    h = scan1(nchunks - 1, 1, h)

    def gates2(p, buf):
        gates(xp_scr[chunk_rows((nchunks - 1 - p) * tc), :], True, *bufs[buf])

    def scan2(p, buf, h):
        return scan(buf, h, None, 0, True)

    def finish(p, buf):
        t0 = (nchunks - 1 - p) * tc
        crows = chunk_rows(t0)
        o = ((s1_scr[crows, :] + bufs[buf][1][...]) * sg_ref[0, crows, :].astype(F32)).astype(BF16)
        nat = jnp.dot(pt_ref[...], o, preferred_element_type=F32).astype(out_ref.dtype)
        lo = pl.multiple_of(t0, tc)
        hi = pl.multiple_of(seq_len - tc - t0, tc)
        for b in range(LRU_SEQS):
            out_ref[0, b, pl.ds(lo, tc), :] = nat[b * tc:(b + 1) * tc]
            out_ref[0, b, pl.ds(hi, tc), :] = nat[(LRU_SEQS + b) * tc:(LRU_SEQS + b + 1) * tc]

    h = roll4(h)
    gates2(0, 0)
    gates2(1, 1)
    h = scan2(0, 0, h)

    def sweep2_pair(i, h):
        finish(2 * i, 0)
        gates2(2 * i + 2, 0)
        h = scan2(2 * i + 1, 1, h)
        finish(2 * i + 1, 1)
        gates2(2 * i + 3, 1)
        return scan2(2 * i + 2, 0, h)

    h = lax.fori_loop(0, npairs - 1, sweep2_pair, h)
    finish(nchunks - 2, 0)
    h = scan2(nchunks - 1, 1, h)
    finish(nchunks - 1, 1)
    fin_ref[0] = h


def _lru(xs, gs, wsel, conv_b, w_blk, bias_blk, ap8, h0, perm_t, seq_len, width, nblk):
    groups = xs.shape[0]
    r8 = LRU_ROWS
    slabs = seq_len // 2
    cw = nblk * LRU_BLOCK
    return pl.pallas_call(
        functools.partial(_lru_kernel, slabs=slabs, seq_len=seq_len, width=width, nblk=nblk),
        grid=(groups, N_LRU_BLOCKS // nblk),
        in_specs=[pl.BlockSpec((1, slabs * r8, cw), lambda g, c: (g, 0, c)),
                  pl.BlockSpec((1, slabs * r8, cw), lambda g, c: (g, 0, c)),
                  pl.BlockSpec((5, r8, cw), lambda g, c: (0, 0, c)),
                  pl.BlockSpec((1, cw), lambda g, c: (0, c)),
                  pl.BlockSpec((nblk, 2 * LRU_BLOCK, 2 * LRU_BLOCK), lambda g, c: (c, 0, 0)),
                  pl.BlockSpec((nblk, r8, 2 * LRU_BLOCK), lambda g, c: (c, 0, 0)),
                  pl.BlockSpec((r8, cw), lambda g, c: (0, c)),
                  pl.BlockSpec((1, r8, cw), lambda g, c: (g, 0, c)),
                  pl.BlockSpec((PERM_ROWS, PERM_ROWS), lambda g, c: (0, 0))],
        out_specs=[pl.BlockSpec((1, LRU_SEQS, seq_len, cw), lambda g, c: (g, 0, 0, c)),
                   pl.BlockSpec((1, r8, cw), lambda g, c: (g, 0, c))],
        out_shape=[jax.ShapeDtypeStruct((groups, LRU_SEQS, seq_len, D_LRU), BF16),
                   jax.ShapeDtypeStruct((groups, r8, D_LRU), F32)],
        scratch_shapes=[pltpu.VMEM(((slabs + 4) * r8, cw), F32),
                        pltpu.VMEM((slabs * r8, cw), F32),
                        pltpu.VMEM((4, PERM_ROWS, cw), F32),
                        pltpu.VMEM((PERM_ROWS, cw), F32),
                        pltpu.VMEM((PERM_ROWS, cw), F32),
                        pltpu.VMEM((PERM_ROWS, cw), F32),
                        pltpu.VMEM((PERM_ROWS, cw), F32)],
        compiler_params=pltpu.CompilerParams(
            dimension_semantics=("parallel", "parallel"), vmem_limit_bytes=VMEM_LIMIT),
        name="lru",
    )(xs, gs, wsel, 0.5 * conv_b.reshape(1, -1), w_blk, bias_blk, ap8, h0, perm_t)


def _lru_params(conv_w, wa, ba, wx, bx, a_param):
    zero_tap = jnp.zeros((D_LRU,), F32)
    taps_lo = [conv_w[s + CONV_LEFT] if s + CONV_LEFT < CONV_W else zero_tap for s in range(-2, 3)]
    taps_hi = [conv_w[CONV_LEFT - s] if 0 <= CONV_LEFT - s < CONV_W else zero_tap for s in range(-2, 3)]
    wsel = jnp.stack([jnp.concatenate([jnp.broadcast_to(tl, (LRU_SEQS, D_LRU)),
                                       jnp.broadcast_to(th, (LRU_SEQS, D_LRU))], axis=0)
                      for tl, th in zip(taps_lo, taps_hi)], axis=0)
    w_blk = jnp.concatenate([jnp.concatenate([wa[0], wx[0]], axis=2),
                             jnp.concatenate([wa[1], wx[1]], axis=2)], axis=1).astype(BF16)
    ba_r = ba.reshape(2, N_LRU_BLOCKS, LRU_BLOCK)
    bx_r = bx.reshape(2, N_LRU_BLOCKS, LRU_BLOCK)
    bias_dir = jnp.concatenate([ba_r, bx_r], axis=2)
    bias_blk = jnp.repeat(bias_dir.transpose(1, 0, 2), LRU_SEQS, axis=1)
    ap8 = jnp.repeat(a_param, LRU_SEQS, axis=0)
    return 0.5 * wsel, w_blk, 0.5 * bias_blk, ap8


def _tail_kernel(x_ref, o_ref, l_ref, mr_ref, ml_ref, gate_ref, fnw_ref, wrd_ref, wld_ref, wo_ref, y_ref):
    ret_out = jnp.dot(o_ref[...], wrd_ref[...], preferred_element_type=F32)
    lru_out = jnp.dot(l_ref[...], wld_ref[...], preferred_element_type=F32)
    merged = _sigmoid(mr_ref[...].astype(F32)) * ret_out + _sigmoid(ml_ref[...].astype(F32)) * lru_out
    out = jnp.dot(merged.astype(BF16), wo_ref[...], preferred_element_type=F32)
    y = x_ref[...] + gate_ref[0] * out
    ms = jnp.mean(y * y, axis=-1, keepdims=True)
    y_ref[...] = y * lax.rsqrt(ms + EPS) * fnw_ref[...]


TAIL_TM = 512


def _tail(x2d, o2d, l2d, z2d, gate, fnw, wrd_b, wld_b, wo_b, tiles_per_cond):
    m = x2d.shape[0]
    tm = TAIL_TM
    kmr, kml = MAIN_MRET // D_MODEL, MAIN_MLRU // D_MODEL
    const = lambda i: (0, 0)
    return pl.pallas_call(
        _tail_kernel,
        grid=(m // tm,),
        in_specs=[pl.BlockSpec((tm, D_MODEL), lambda i: (i, 0)),
                  pl.BlockSpec((tm, D_V), lambda i: (i, 0)),
                  pl.BlockSpec((tm, D_LRU), lambda i: (i, 0)),
                  pl.BlockSpec((tm, D_MODEL), lambda i: (i, kmr)),
                  pl.BlockSpec((tm, D_MODEL), lambda i: (i, kml)),
                  pl.BlockSpec((1, 1, D_MODEL), lambda i: (i // tiles_per_cond, 0, 0)),
                  pl.BlockSpec((1, D_MODEL), const),
                  pl.BlockSpec((D_V, D_MODEL), const),
                  pl.BlockSpec((D_LRU, D_MODEL), const),
                  pl.BlockSpec((D_MODEL, D_MODEL), const)],
        out_specs=pl.BlockSpec((tm, D_MODEL), lambda i: (i, 0)),
        out_shape=jax.ShapeDtypeStruct((m, D_MODEL), F32),
        compiler_params=pltpu.CompilerParams(
            dimension_semantics=("parallel",), vmem_limit_bytes=VMEM_LIMIT),
        name="tail",
    )(x2d, o2d, l2d, z2d, z2d, gate, fnw.reshape(1, -1), wrd_b, wld_b, wo_b)


def _trunk(x, mod_rows, s0_ret, h0_lru, width, lru_nblk, params, final_norm_w, emit_state):
    (norm_w, w_all, decay_logit, wrd_b, conv_b, lru_par, wld_b, wo_b, perm, perm_t) = params
    wsel, w_blk, bias_blk, ap8 = lru_par
    b, l, _ = x.shape
    groups = b // LRU_SEQS
    x2d = x.reshape(b * l, D_MODEL)
    n_cond = mod_rows.shape[0]
    shift = mod_rows[:, None, 0:D_MODEL]
    scale = mod_rows[:, None, D_MODEL:2 * D_MODEL]
    gate = mod_rows[:, None, 2 * D_MODEL:]
    tokens_per_cond = b * l // n_cond
    z2d = _inproj(x2d, norm_w, scale, shift, w_all, tokens_per_cond // INPROJ_TM)
    z3 = z2d.reshape(b, l, D_MAIN)
    ret = _retention(z3, decay_logit, s0_ret, emit_state, N_HEADS if l == RET_CHUNK else 1)

    scale4 = jnp.broadcast_to(scale, (LRU_SEQS, 1, D_MODEL))[None]
    shift4 = jnp.broadcast_to(shift, (LRU_SEQS, 1, D_MODEL))[None]
    xs, gs = _inproj_lru(x.reshape(groups, LRU_SEQS, l, D_MODEL), norm_w, scale4, shift4, perm, w_all)
    if h0_lru is None:
        h0 = jnp.zeros((groups, LRU_ROWS, D_LRU), F32)
    else:
        h0 = h0_lru.reshape(groups, LRU_SEQS, 2, D_LRU).transpose(0, 2, 1, 3).reshape(groups, LRU_ROWS, D_LRU)
    lru_pre, fin = _lru(xs, gs, wsel, conv_b, w_blk, bias_blk, ap8, h0, perm_t, l, width, lru_nblk)
    fin = fin.reshape(groups, 2, LRU_SEQS, D_LRU)
    lru_fin = jnp.stack([fin[:, 1], fin[:, 0]], axis=2).reshape(b, 2, D_LRU)

    y = _tail(x2d, ret[0].reshape(b * l, D_V), lru_pre.reshape(b * l, D_LRU), z2d, gate, final_norm_w,
              wrd_b, wld_b, wo_b, tokens_per_cond // TAIL_TM)
    return y.reshape(b, l, D_MODEL), (ret[1] if emit_state else None), lru_fin


def kernel(x_prompt, x_sample, state_ret, state_lru, c, c_ctx, norm_w, w_ada, b_ada, w_in, ret_decay_logit,
           ret_gn_w, w_ret_down, conv_w, conv_b, lru_wa, lru_ba, lru_wx, lru_bx, lru_a_param, w_lru_down,
           w_out, final_norm_w):
    assert norm_w.shape[0] == 1, "single-layer step"
    n_dec = c.shape[0]
    cond8 = jnp.concatenate([c.astype(F32), c_ctx.astype(F32)[None],
                             jnp.zeros((8 - n_dec - 1, D_MODEL), F32)], axis=0)
    mod = _ada(cond8, w_ada[0], b_ada[0])
    col = jnp.arange(D_IN)
    col_scale = jnp.where((col >= OFF_GRET) & (col < OFF_XLRU), 0.5, 1.0).astype(F32)
    w_all = (w_in[0] * col_scale[None, :]).astype(BF16)
    wrd_b = (ret_gn_w[0][:, None] * w_ret_down[0]).astype(BF16)
    perm = jnp.asarray(_slab_permutation(PERM_UNIT), BF16)
    perm_t = jnp.asarray(_slab_permutation(LRU_TC).T, BF16)
    lru_par = _lru_params(conv_w[0], lru_wa[0], lru_ba[0], lru_wx[0], lru_bx[0], lru_a_param[0])
    params = (norm_w[0], w_all, ret_decay_logit[0], wrd_b,
              conv_b[0], lru_par, w_lru_down[0].astype(BF16), w_out[0].astype(BF16), perm, perm_t)
    y_prompt, new_ret, new_lru = _trunk(x_prompt.astype(F32), mod[n_dec:n_dec + 1], None, None,
                                        x_prompt.shape[1], 5, params, final_norm_w, True)
    y_sample, _, _ = _trunk(x_sample.astype(F32), mod[:n_dec], state_ret[:, 0], state_lru[:, 0],
                            GRID_W, 2, params, final_norm_w, False)
    return (y_prompt.astype(x_prompt.dtype), y_sample.astype(x_sample.dtype),
            new_ret[:, None].astype(state_ret.dtype), new_lru[:, None].astype(state_lru.dtype))
```

```python
import functools

import jax
import jax.numpy as jnp
import numpy as np
from jax import lax
from jax.experimental import pallas as pl
from jax.experimental.pallas import tpu as pltpu

F32 = jnp.float32
BF16 = jnp.bfloat16

D_MODEL = 1024
N_HEADS = 4
DK = 256
DV = 512
D_QK = N_HEADS * DK
D_V = N_HEADS * DV
N_LRU_BLOCKS = 10
LRU_BLOCK = 128
D_LRU = N_LRU_BLOCKS * LRU_BLOCK
LRU_C = 8.0
CONV_W = 4
CONV_LEFT = 2
GRID_W = 64
EPS = 1e-6

OFF_Q = 0
OFF_K = OFF_Q + D_QK
OFF_V = OFF_K + D_QK
OFF_GRET = OFF_V + D_V
OFF_XLRU = OFF_GRET + D_V
OFF_GLRU = OFF_XLRU + D_LRU
OFF_MRET = OFF_GLRU + D_LRU
OFF_MLRU = OFF_MRET + D_MODEL
D_IN = OFF_MLRU + D_MODEL
D_MAIN = OFF_XLRU + 2 * D_MODEL
MAIN_MRET = OFF_XLRU
MAIN_MLRU = OFF_XLRU + D_MODEL

RET_CHUNK = 256
LRU_SEQS = 4
LRU_ROWS = 2 * LRU_SEQS
LRU_TC = 64
PERM_ROWS = LRU_TC * LRU_ROWS
PERM_UNIT = 32
VMEM_LIMIT = 56 * 1024 * 1024


def _sigmoid(x):
    return 0.5 * jnp.tanh(0.5 * x) + 0.5


def _silu(x):
    return x * _sigmoid(x)


def _softplus(x):
    return jnp.maximum(x, 0.0) + jnp.log1p(jnp.exp(-jnp.abs(x)))


def _slab_permutation(tc):
    n = tc * LRU_ROWS
    p = np.zeros((n, n), np.float32)
    for t in range(tc):
        for r in range(LRU_ROWS):
            if r < LRU_SEQS:
                src = r * tc + t
            else:
                src = LRU_SEQS * tc + (r - LRU_SEQS) * tc + (tc - 1 - t)
            p[t * LRU_ROWS + r, src] = 1.0
    return p


def _ada_kernel(c_ref, w_ref, b_ref, o_ref):
    cond = _silu(c_ref[...]).astype(BF16)
    o_ref[...] = jnp.dot(cond, w_ref[...].astype(BF16), preferred_element_type=F32) + b_ref[...]


def _ada(cond8, w_ada, b_ada):
    tn = 768
    return pl.pallas_call(
        _ada_kernel,
        grid=(3 * D_MODEL // tn,),
        in_specs=[pl.BlockSpec((8, D_MODEL), lambda j: (0, 0)),
                  pl.BlockSpec((D_MODEL, tn), lambda j: (0, j)),
                  pl.BlockSpec((1, tn), lambda j: (0, j))],
        out_specs=pl.BlockSpec((8, tn), lambda j: (0, j)),
        out_shape=jax.ShapeDtypeStruct((8, 3 * D_MODEL), F32),
        name="ada",
    )(cond8, w_ada, b_ada.reshape(1, -1))


def _modulated_norm(x, nw, scale, shift):
    ms = jnp.mean(x * x, axis=-1, keepdims=True)
    return (x * lax.rsqrt(ms + EPS)) * (nw * (1.0 + scale)) + shift


INPROJ_TM = 1024


def _inproj_kernel(x_ref, nw_ref, sc_ref, sh_ref, w_ref, z_ref, h_scr):
    @pl.when(pl.program_id(1) == 0)
    def _():
        h_scr[...] = _modulated_norm(x_ref[...], nw_ref[...], sc_ref[0], sh_ref[0]).astype(BF16)

    z_ref[...] = jnp.dot(h_scr[...], w_ref[...], preferred_element_type=F32).astype(z_ref.dtype)


MOD_SHIFT, MOD_SCALE, MOD_GATE = 0, 1, 2


def _inproj(x2d, norm_w, mod3, cond_row, w_all):
    m = x2d.shape[0]
    tm, tn = INPROJ_TM, 2048
    n_lead = OFF_XLRU // tn
    assert OFF_XLRU == n_lead * tn and D_MAIN - OFF_XLRU == tn

    def w_col(j):
        return jnp.where(j < n_lead, j * tn, OFF_MRET)

    return pl.pallas_call(
        _inproj_kernel,
        grid=(m // tm, D_MAIN // tn),
        in_specs=[pl.BlockSpec((tm, D_MODEL), lambda i, j: (i, 0)),
                  pl.BlockSpec((1, D_MODEL), lambda i, j: (0, 0)),
                  pl.BlockSpec((1, 1, D_MODEL), lambda i, j: (cond_row(i), 0, MOD_SCALE)),
                  pl.BlockSpec((1, 1, D_MODEL), lambda i, j: (cond_row(i), 0, MOD_SHIFT)),
                  pl.BlockSpec((pl.Element(D_MODEL), pl.Element(tn)), lambda i, j: (0, w_col(j)))],
        out_specs=pl.BlockSpec((tm, tn), lambda i, j: (i, j)),
        out_shape=jax.ShapeDtypeStruct((m, D_MAIN), BF16),
        scratch_shapes=[pltpu.VMEM((tm, D_MODEL), BF16)],
        compiler_params=pltpu.CompilerParams(
            dimension_semantics=("parallel", "arbitrary"), vmem_limit_bytes=VMEM_LIMIT),
        name="inproj",
    )(x2d, norm_w.reshape(1, -1), mod3, mod3, w_all)


def _inproj_lru_kernel(xa_ref, xb_ref, nw_ref, sc_ref, sh_ref, p_ref, w_ref, xs_ref, gs_ref):
    nw = nw_ref[...]
    sc = sc_ref[...]
    sh = sh_ref[...]
    ha = _modulated_norm(xa_ref[0], nw, sc, sh).astype(BF16)
    hb = _modulated_norm(xb_ref[0], nw, sc, sh).astype(BF16)
    units = LRU_TC // PERM_UNIT
    pieces = []
    for u in range(units):
        lo, hi = u * PERM_UNIT, (units - 1 - u) * PERM_UNIT
        src = jnp.concatenate([ha[b, lo:lo + PERM_UNIT] for b in range(LRU_SEQS)]
                              + [hb[b, hi:hi + PERM_UNIT] for b in range(LRU_SEQS)], axis=0)
        pieces.append(jnp.dot(p_ref[...], src, preferred_element_type=F32).astype(BF16))
    hp = jnp.concatenate(pieces, axis=0)
    z = jnp.dot(hp, w_ref[...], preferred_element_type=F32)
    xs_ref[0] = z[:, :D_LRU].astype(xs_ref.dtype)
    gs_ref[0] = _silu(z[:, D_LRU:]).astype(gs_ref.dtype)


def _inproj_lru(x4, norm_w, mod3, row0, nrows, perm, w_all):
    groups, _, l, _ = x4.shape
    assert row0 % nrows == 0 and nrows in (1, LRU_SEQS)
    nt = l // 2 // LRU_TC
    ntb = l // LRU_TC
    out_sds = jax.ShapeDtypeStruct((groups, l // 2 * LRU_ROWS, D_LRU), BF16)
    return pl.pallas_call(
        _inproj_lru_kernel,
        grid=(groups, nt),
        in_specs=[pl.BlockSpec((1, LRU_SEQS, LRU_TC, D_MODEL), lambda g, i: (g, 0, i, 0)),
                  pl.BlockSpec((1, LRU_SEQS, LRU_TC, D_MODEL), lambda g, i: (g, 0, ntb - 1 - i, 0)),
                  pl.BlockSpec((1, D_MODEL), lambda g, i: (0, 0)),
                  pl.BlockSpec((nrows, 1, D_MODEL), lambda g, i: (row0 // nrows, 0, MOD_SCALE)),
                  pl.BlockSpec((nrows, 1, D_MODEL), lambda g, i: (row0 // nrows, 0, MOD_SHIFT)),
                  pl.BlockSpec((PERM_UNIT * LRU_ROWS, PERM_UNIT * LRU_ROWS), lambda g, i: (0, 0)),
                  pl.BlockSpec((pl.Element(D_MODEL), pl.Element(2 * D_LRU)), lambda g, i: (0, OFF_XLRU))],
        out_specs=[pl.BlockSpec((1, PERM_ROWS, D_LRU), lambda g, i: (g, i, 0)),
                   pl.BlockSpec((1, PERM_ROWS, D_LRU), lambda g, i: (g, i, 0))],
        out_shape=[out_sds, out_sds],
        compiler_params=pltpu.CompilerParams(
            dimension_semantics=("parallel", "parallel"), vmem_limit_bytes=VMEM_LIMIT),
        name="inproj_lru",
    )(x4, x4, norm_w.reshape(1, -1), mod3, mod3, perm, w_all)


def _dot_tn(a, b):
    return lax.dot_general(a, b, (((0,), (0,)), ((), ())), preferred_element_type=F32)


def _dot_nt(a, b):
    return lax.dot_general(a, b, (((1,), (1,)), ((), ())), preferred_element_type=F32)


def _ret_kernel(*refs, nc, hp, has_state, emit_state):
    dl_ref, q_ref, k_ref, v_ref, hg_ref = refs[:5]
    pos = 5
    s0_ref = None
    if has_state:
        s0_ref = refs[pos]
        pos += 1
    o_ref = refs[pos]
    pos += 1
    sfin_ref = None
    if emit_state:
        sfin_ref = refs[pos]
        pos += 1
    sf_scr, sb_scr, sbh_scr = refs[pos:pos + 3]

    c = RET_CHUNK
    carry_states = has_state or nc > 1
    ii = lax.broadcasted_iota(jnp.int32, (c, c), 0)
    jj = lax.broadcasted_iota(jnp.int32, (c, c), 1)
    diff = (ii - jj).astype(F32)
    p = lax.broadcasted_iota(jnp.int32, (c, 1), 0).astype(F32)
    kscale = DK ** -0.5

    def rows_of(n):
        return pl.ds(pl.multiple_of(n * c, c), c)

    def normed_out(o, rows, vcols):
        ms = jnp.mean(o * o, axis=-1, keepdims=True)
        on = o * lax.rsqrt(ms + EPS)
        hg = hg_ref[0, rows, vcols].astype(F32)
        gate = hg * (jnp.tanh(hg) + 1.0)
        o_ref[0, rows, vcols] = (on * gate).astype(o_ref.dtype)

    for hh in range(hp):
        head = pl.program_id(1) * hp + hh
        qcols = slice(hh * DK, (hh + 1) * DK)
        vcols = slice(hh * DV, (hh + 1) * DV)
        lgf = -_softplus(-jnp.full((1, 1), dl_ref[0, head], F32))
        lgb = -_softplus(-jnp.full((1, 1), dl_ref[1, head], F32))
        decay = jnp.exp(jnp.where(diff >= 0, lgf * diff, -lgb * diff)) * kscale
        kdf = jnp.exp(lgf * (c - 1.0 - p)) * kscale
        kdb = jnp.exp(lgb * p) * kscale

        if not carry_states:
            rows = pl.ds(0, c)
            qn = q_ref[0, rows, qcols]
            kn = k_ref[0, rows, qcols]
            vn = v_ref[0, rows, vcols]
            s = (_dot_nt(qn, kn) * decay).astype(BF16)
            normed_out(jnp.dot(s, vn, preferred_element_type=F32), rows, vcols)
            k32 = kn.astype(F32)
            if emit_state:
                sfin_ref[0, 0, hh] = _dot_tn((k32 * kdf).astype(BF16), vn)
                sfin_ref[0, 1, hh] = _dot_tn((k32 * kdb).astype(BF16), vn)
            continue

        qdf = jnp.exp(lgf * (p + 1.0))
        qdb = jnp.exp(lgb * (c - p))
        cdf = jnp.exp(lgf * c)
        cdb = jnp.exp(lgb * c)
        if has_state:
            sf_scr[...] = s0_ref[0, 0, hh]
            sb_scr[...] = s0_ref[0, 1, hh]
        else:
            sf_scr[...] = jnp.zeros_like(sf_scr)
            sb_scr[...] = jnp.zeros_like(sb_scr)

        def rev_body(idx, carry):
            n = nc - 1 - idx
            rows = rows_of(n)
            sbh_scr[n] = sb_scr[...].astype(BF16)
            kb = (k_ref[0, rows, qcols].astype(F32) * kdb).astype(BF16)
            sb_scr[...] = cdb * sb_scr[...] + _dot_tn(kb, v_ref[0, rows, vcols])
            return carry

        lax.fori_loop(0, nc, rev_body, 0, unroll=True)

        def fwd_body(n, carry):
            rows = rows_of(n)
            qn = q_ref[0, rows, qcols]
            kn = k_ref[0, rows, qcols]
            vn = v_ref[0, rows, vcols]
            s = (_dot_nt(qn, kn) * decay).astype(BF16)
            q32 = qn.astype(F32)
            o = (jnp.dot(s, vn, preferred_element_type=F32)
                 + jnp.dot((q32 * qdf).astype(BF16), sf_scr[...].astype(BF16), preferred_element_type=F32)
                 + jnp.dot((q32 * qdb).astype(BF16), sbh_scr[n], preferred_element_type=F32))
            normed_out(o, rows, vcols)
            kf = (kn.astype(F32) * kdf).astype(BF16)
            sf_scr[...] = cdf * sf_scr[...] + _dot_tn(kf, vn)
            return carry

        lax.fori_loop(0, nc, fwd_body, 0, unroll=True)

        if emit_state:
            sfin_ref[0, 0, hh] = sf_scr[...]
            sfin_ref[0, 1, hh] = sb_scr[...]


def _retention(z3, decay_logit, s0, emit_state, hp):
    b, l, _ = z3.shape
    nc = l // RET_CHUNK
    has_state = s0 is not None
    kq, kv_ = OFF_K // (hp * DK), OFF_V // (hp * DV)
    kg = OFF_GRET // (hp * DV)
    in_specs = [pl.BlockSpec(memory_space=pltpu.SMEM),
                pl.BlockSpec((1, l, hp * DK), lambda i, h: (i, 0, h)),
                pl.BlockSpec((1, l, hp * DK), lambda i, h: (i, 0, kq + h)),
                pl.BlockSpec((1, l, hp * DV), lambda i, h: (i, 0, kv_ + h)),
                pl.BlockSpec((1, l, hp * DV), lambda i, h: (i, 0, kg + h))]
    args = [decay_logit, z3, z3, z3, z3]
    if has_state:
        in_specs.append(pl.BlockSpec((1, 2, hp, DK, DV), lambda i, h: (i, 0, h, 0, 0)))
        args.append(s0)
    out_specs = [pl.BlockSpec((1, l, hp * DV), lambda i, h: (i, 0, h))]
    out_shape = [jax.ShapeDtypeStruct((b, l, D_V), BF16)]
    if emit_state:
        out_specs.append(pl.BlockSpec((1, 2, hp, DK, DV), lambda i, h: (i, 0, h, 0, 0)))
        out_shape.append(jax.ShapeDtypeStruct((b, 2, N_HEADS, DK, DV), F32))
    return pl.pallas_call(
        functools.partial(_ret_kernel, nc=nc, hp=hp, has_state=has_state, emit_state=emit_state),
        grid=(b, N_HEADS // hp),
        in_specs=in_specs,
        out_specs=out_specs,
        out_shape=out_shape,
        scratch_shapes=[pltpu.VMEM((DK, DV), F32), pltpu.VMEM((DK, DV), F32), pltpu.VMEM((nc, DK, DV), BF16)],
        compiler_params=pltpu.CompilerParams(
            dimension_semantics=("parallel", "parallel"), vmem_limit_bytes=VMEM_LIMIT),
        name="retention",
    )(*args)


def _sqrt_unit(x):
    return x * lax.rsqrt(jnp.maximum(x, 1e-30))


def _lru_kernel(xs_ref, sg_ref, cw_ref, cb_ref, w_ref, ba_ref, bx_ref, ap_ref, h0_ref, pt_ref, out_ref, fin_ref,
                xp_scr, s1_scr, wm_scr, a0_scr, b0_scr, a1_scr, b1_scr, *, slabs, seq_len, width, nblk):
    r8 = LRU_ROWS
    tc = LRU_TC
    rows = PERM_ROWS
    cw = nblk * LRU_BLOCK
    nchunks = slabs // tc
    assert width == seq_len or width == tc

    def roll4(v):
        return pltpu.roll(v, LRU_SEQS, axis=0)

    xp_scr[pl.ds(0, 2 * r8), :] = jnp.zeros((2 * r8, cw), F32)
    xp_scr[pl.ds(2 * r8, slabs * r8), :] = xs_ref[0].astype(F32)
    xp_scr[pl.ds((slabs + 2) * r8, r8), :] = roll4(xs_ref[0, pl.ds((slabs - 1) * r8, r8), :].astype(F32))
    xp_scr[pl.ds((slabs + 3) * r8, r8), :] = roll4(xs_ref[0, pl.ds((slabs - 2) * r8, r8), :].astype(F32))

    row_cw = lax.broadcasted_iota(jnp.int32, (rows, cw), 0)
    step_in_chunk = row_cw >> 3
    low_cw = (row_cw & (r8 - 1)) < LRU_SEQS

    def row_pattern(lo, hi):
        n = lo.shape[-1]
        return jnp.where(lax.broadcasted_iota(jnp.int32, (r8, n), 0) < LRU_SEQS, lo, hi)

    def tap_rows(s):
        zero = jnp.zeros((1, cw), F32)
        lo = cw_ref[pl.ds(s + CONV_LEFT, 1), :] if 0 <= s + CONV_LEFT < CONV_W else zero
        hi = cw_ref[pl.ds(CONV_LEFT - s, 1), :] if 0 <= CONV_LEFT - s < CONV_W else zero
        return 0.5 * row_pattern(lo, hi)

    def tap_table(s):
        tap = jnp.broadcast_to(tap_rows(s)[None], (tc, r8, cw)).reshape(rows, cw)
        if width != seq_len:
            tap = jnp.where((step_in_chunk + s >= 0) & (step_in_chunk + s < tc), tap, 0.0)
        return tap

    wm_scr[0] = tap_table(-2) + tap_table(2)
    for s in range(-1, 2):
        wm_scr[s + 2] = tap_table(s)

    ap8 = row_pattern(ap_ref[pl.ds(0, 1), :], ap_ref[pl.ds(1, 1), :])
    c1_lo = (-0.5 * LRU_C / np.log(2.0)) * _softplus(-ap8)
    cb_half = 0.5 * cb_ref[...]

    def half_bias(j):
        cols = slice(j * LRU_BLOCK, (j + 1) * LRU_BLOCK)
        return 0.5 * jnp.concatenate(
            [row_pattern(ba_ref[pl.ds(0, 1), cols], ba_ref[pl.ds(1, 1), cols]),
             row_pattern(bx_ref[pl.ds(0, 1), cols], bx_ref[pl.ds(1, 1), cols])], axis=1)

    bias_lo = [half_bias(j) for j in range(nblk)]
    low_rows = (lax.broadcasted_iota(jnp.int32, (rows, LRU_BLOCK), 0) & (r8 - 1)) < LRU_SEQS
    high_rows = jnp.logical_not(low_rows)

    def conv_half(t0):
        def shifted(s):
            return xp_scr[pl.ds(pl.multiple_of((t0 + s + 2) * r8, r8), rows), :]

        xh = jnp.where(low_cw, shifted(-2), shifted(2)) * wm_scr[0] + cb_half
        for s in range(-1, 2):
            xh = xh + shifted(s) * wm_scr[s + 2]
        xp_scr[pl.ds(pl.multiple_of(t0 * r8, rows), rows), :] = xh
        return xh

    def gates(xh, sweep2, a_scr, b_scr):
        fj = high_rows if sweep2 else low_rows
        c1 = roll4(c1_lo) if sweep2 else c1_lo
        for j in range(nblk):
            cols = slice(j * LRU_BLOCK, (j + 1) * LRU_BLOCK)
            xj = xh[:, cols]
            lhs = jnp.concatenate([jnp.where(fj, xj, 0.0), jnp.where(fj, 0.0, xj)], axis=1)
            pre = jnp.dot(lhs.astype(BF16), w_ref[j], preferred_element_type=F32)
            bj = roll4(bias_lo[j]) if sweep2 else bias_lo[j]
            pre = pre.reshape(tc, r8, 2 * LRU_BLOCK) + bj[None]
            tr = jnp.tanh(pre[:, :, :LRU_BLOCK])
            tg = jnp.tanh(pre[:, :, LRU_BLOCK:])
            c1j = c1[:, cols][None]
            a = jnp.exp2(c1j * tr + c1j)
            bco = _sqrt_unit(1.0 - a * a) * ((tg + 1.0) * xj.reshape(tc, r8, LRU_BLOCK))
            a_scr[:, cols] = a.reshape(rows, LRU_BLOCK)
            b_scr[:, cols] = bco.reshape(rows, LRU_BLOCK)

    bufs = ((a0_scr, b0_scr), (a1_scr, b1_scr))

    def chunk_rows(t0):
        return pl.ds(pl.multiple_of(t0 * r8, rows), rows)

    def scan(buf, h, dst_ref, t0, descending):
        a_ref, b_ref = bufs[buf]
        for i in range(tc):
            t = tc - 1 - i if descending else i
            h = a_ref[t * r8:(t + 1) * r8, :] * h + b_ref[t * r8:(t + 1) * r8, :]
            if dst_ref is None:
                b_ref[t * r8:(t + 1) * r8, :] = h
            else:
                dst_ref[pl.ds(pl.multiple_of(t0 * r8, rows) + t * r8, r8), :] = h
        return h

    def gates1(ci, buf):
        gates(conv_half(ci * tc), False, *bufs[buf])

    def scan1(ci, buf, h):
        return scan(buf, h, s1_scr, ci * tc, False)

    npairs = nchunks // 2
    assert nchunks == 2 * npairs
    gates1(0, 0)

    def sweep1_pair(i, h):
        gates1(2 * i + 1, 1)
        h = scan1(2 * i, 0, h)
        gates1(2 * i + 2, 0)
        return scan1(2 * i + 1, 1, h)

    h = lax.fori_loop(0, npairs - 1, sweep1_pair, h0_ref[0])
    gates1(nchunks - 1, 1)
    h = scan1(nchunks - 2, 0, h)
    h = scan1(nchunks - 1, 1, h)

    def gates2(p, buf):
        gates(xp_scr[chunk_rows((nchunks - 1 - p) * tc), :], True, *bufs[buf])

    def scan2(p, buf, h):
        return scan(buf, h, None, 0, True)

    def finish(p, buf):
        t0 = (nchunks - 1 - p) * tc
        crows = chunk_rows(t0)
        o = ((s1_scr[crows, :] + bufs[buf][1][...]) * sg_ref[0, crows, :].astype(F32)).astype(BF16)
        nat = jnp.dot(pt_ref[...], o, preferred_element_type=F32).astype(out_ref.dtype)
        lo = pl.multiple_of(t0, tc)
        hi = pl.multiple_of(seq_len - tc - t0, tc)
        for b in range(LRU_SEQS):
            out_ref[0, b, pl.ds(lo, tc), :] = nat[b * tc:(b + 1) * tc]
            out_ref[0, b, pl.ds(hi, tc), :] = nat[(LRU_SEQS + b) * tc:(LRU_SEQS + b + 1) * tc]

    h = roll4(h)
    gates2(0, 0)
    gates2(1, 1)
    h = scan2(0, 0, h)

    def sweep2_pair(i, h):
        finish(2 * i, 0)
        gates2(2 * i + 2, 0)
        h = scan2(2 * i + 1, 1, h)
        finish(2 * i + 1, 1)
        gates2(2 * i + 3, 1)
        return scan2(2 * i + 2, 0, h)

    h = lax.fori_loop(0, npairs - 1, sweep2_pair, h)
    finish(nchunks - 2, 0)
    h = scan2(nchunks - 1, 1, h)
    finish(nchunks - 1, 1)
    fin_ref[0] = h


def _lru(xs, gs, conv_w, conv_b, w_blk, ba, bx, a_param, h0, perm_t, seq_len, width, nblk):
    groups = xs.shape[0]
    r8 = LRU_ROWS
    slabs = seq_len // 2
    cw = nblk * LRU_BLOCK
    per_dir = pl.BlockSpec((2, cw), lambda g, c: (0, c))
    return pl.pallas_call(
        functools.partial(_lru_kernel, slabs=slabs, seq_len=seq_len, width=width, nblk=nblk),
        grid=(groups, N_LRU_BLOCKS // nblk),
        in_specs=[pl.BlockSpec((1, slabs * r8, cw), lambda g, c: (g, 0, c)),
                  pl.BlockSpec((1, slabs * r8, cw), lambda g, c: (g, 0, c)),
                  pl.BlockSpec((CONV_W, cw), lambda g, c: (0, c)),
                  pl.BlockSpec((1, cw), lambda g, c: (0, c)),
                  pl.BlockSpec((nblk, 2 * LRU_BLOCK, 2 * LRU_BLOCK), lambda g, c: (c, 0, 0)),
                  per_dir, per_dir, per_dir,
                  pl.BlockSpec((1, r8, cw), lambda g, c: (g, 0, c)),
                  pl.BlockSpec((PERM_ROWS, PERM_ROWS), lambda g, c: (0, 0))],
        out_specs=[pl.BlockSpec((1, LRU_SEQS, seq_len, cw), lambda g, c: (g, 0, 0, c)),
                   pl.BlockSpec((1, r8, cw), lambda g, c: (g, 0, c))],
        out_shape=[jax.ShapeDtypeStruct((groups, LRU_SEQS, seq_len, D_LRU), BF16),
                   jax.ShapeDtypeStruct((groups, r8, D_LRU), F32)],
        scratch_shapes=[pltpu.VMEM(((slabs + 4) * r8, cw), F32),
                        pltpu.VMEM((slabs * r8, cw), F32),
                        pltpu.VMEM((4, PERM_ROWS, cw), F32),
                        pltpu.VMEM((PERM_ROWS, cw), F32),
                        pltpu.VMEM((PERM_ROWS, cw), F32),
                        pltpu.VMEM((PERM_ROWS, cw), F32),
                        pltpu.VMEM((PERM_ROWS, cw), F32)],
        compiler_params=pltpu.CompilerParams(
            dimension_semantics=("parallel", "parallel"), vmem_limit_bytes=VMEM_LIMIT),
        name="lru",
    )(xs, gs, conv_w, conv_b.reshape(1, -1), w_blk, ba, bx, a_param, h0, perm_t)


def _lru_gate_weights(wa, wx):
    return jnp.concatenate([jnp.concatenate([wa[0], wx[0]], axis=2),
                            jnp.concatenate([wa[1], wx[1]], axis=2)], axis=1).astype(BF16)


def _tail_kernel(x_ref, o_ref, l_ref, mr_ref, ml_ref, gate_ref, fnw_ref, wrd_ref, wld_ref, wo_ref, y_ref):
    ret_out = jnp.dot(o_ref[...], wrd_ref[...], preferred_element_type=F32)
    lru_out = jnp.dot(l_ref[...], wld_ref[...], preferred_element_type=F32)
    merged = _sigmoid(mr_ref[...].astype(F32)) * ret_out + _sigmoid(ml_ref[...].astype(F32)) * lru_out
    out = jnp.dot(merged.astype(BF16), wo_ref[...], preferred_element_type=F32)
    y = x_ref[...] + gate_ref[0] * out
    ms = jnp.mean(y * y, axis=-1, keepdims=True)
    y_ref[...] = y * lax.rsqrt(ms + EPS) * fnw_ref[...]


TAIL_TM = 512


def _tail(x2d, o2d, l2d, z2d, mod3, cond_row, fnw, wrd_b, wld_b, wo_b):
    m = x2d.shape[0]
    tm = TAIL_TM
    kmr, kml = MAIN_MRET // D_MODEL, MAIN_MLRU // D_MODEL
    const = lambda i: (0, 0)
    return pl.pallas_call(
        _tail_kernel,
        grid=(m // tm,),
        in_specs=[pl.BlockSpec((tm, D_MODEL), lambda i: (i, 0)),
                  pl.BlockSpec((tm, D_V), lambda i: (i, 0)),
                  pl.BlockSpec((tm, D_LRU), lambda i: (i, 0)),
                  pl.BlockSpec((tm, D_MODEL), lambda i: (i, kmr)),
                  pl.BlockSpec((tm, D_MODEL), lambda i: (i, kml)),
                  pl.BlockSpec((1, 1, D_MODEL), lambda i: (cond_row(i), 0, MOD_GATE)),
                  pl.BlockSpec((1, D_MODEL), const),
                  pl.BlockSpec((D_V, D_MODEL), const),
                  pl.BlockSpec((D_LRU, D_MODEL), const),
                  pl.BlockSpec((D_MODEL, D_MODEL), const)],
        out_specs=pl.BlockSpec((tm, D_MODEL), lambda i: (i, 0)),
        out_shape=jax.ShapeDtypeStruct((m, D_MODEL), F32),
        compiler_params=pltpu.CompilerParams(
            dimension_semantics=("parallel",), vmem_limit_bytes=VMEM_LIMIT),
        name="tail",
    )(x2d, o2d, l2d, z2d, z2d, mod3, fnw.reshape(1, -1), wrd_b, wld_b, wo_b)


def _trunk(x, mod3, cond_row0, n_cond, s0_ret, h0_lru, width, lru_nblk, params, final_norm_w, emit_state):
    (norm_w, w_all, decay_logit, wrd_b, conv_w, conv_b, w_blk, ba, bx, a_param, wld_b, wo_b, perm, perm_t) = params
    b, l, _ = x.shape
    groups = b // LRU_SEQS
    x2d = x.reshape(b * l, D_MODEL)
    tokens_per_cond = b * l // n_cond

    def cond_row(tile_tokens):
        return lambda i: cond_row0 + i // (tokens_per_cond // tile_tokens)

    z2d = _inproj(x2d, norm_w, mod3, cond_row(INPROJ_TM), w_all)
    z3 = z2d.reshape(b, l, D_MAIN)
    ret = _retention(z3, decay_logit, s0_ret, emit_state, N_HEADS if l == RET_CHUNK else 1)

    xs, gs = _inproj_lru(x.reshape(groups, LRU_SEQS, l, D_MODEL), norm_w, mod3, cond_row0, n_cond, perm, w_all)
    if h0_lru is None:
        h0 = jnp.zeros((groups, LRU_ROWS, D_LRU), F32)
    else:
        h0 = h0_lru.reshape(groups, LRU_SEQS, 2, D_LRU).transpose(0, 2, 1, 3).reshape(groups, LRU_ROWS, D_LRU)
    lru_pre, fin = _lru(xs, gs, conv_w, conv_b, w_blk, ba, bx, a_param, h0, perm_t, l, width, lru_nblk)
    fin = fin.reshape(groups, 2, LRU_SEQS, D_LRU)
    lru_fin = jnp.stack([fin[:, 1], fin[:, 0]], axis=2).reshape(b, 2, D_LRU)

    y = _tail(x2d, ret[0].reshape(b * l, D_V), lru_pre.reshape(b * l, D_LRU), z2d, mod3, cond_row(TAIL_TM),
              final_norm_w, wrd_b, wld_b, wo_b)
    return y.reshape(b, l, D_MODEL), (ret[1] if emit_state else None), lru_fin


def kernel(x_prompt, x_sample, state_ret, state_lru, c, c_ctx, norm_w, w_ada, b_ada, w_in, ret_decay_logit,
           ret_gn_w, w_ret_down, conv_w, conv_b, lru_wa, lru_ba, lru_wx, lru_bx, lru_a_param, w_lru_down,
           w_out, final_norm_w):
    assert norm_w.shape[0] == 1, "single-layer step"
    n_dec = c.shape[0]
    cond8 = jnp.concatenate([c.astype(F32), c_ctx.astype(F32)[None],
                             jnp.zeros((8 - n_dec - 1, D_MODEL), F32)], axis=0)
    mod = _ada(cond8, w_ada[0], b_ada[0])
    col = jnp.arange(D_IN)
    col_scale = jnp.where((col >= OFF_GRET) & (col < OFF_XLRU), 0.5, 1.0).astype(F32)
    w_all = (w_in[0] * col_scale[None, :]).astype(BF16)
    wrd_b = (ret_gn_w[0][:, None] * w_ret_down[0]).astype(BF16)
    perm = jnp.asarray(_slab_permutation(PERM_UNIT), BF16)
    perm_t = jnp.asarray(_slab_permutation(LRU_TC).T, BF16)
    params = (norm_w[0], w_all, ret_decay_logit[0], wrd_b, conv_w[0], conv_b[0],
              _lru_gate_weights(lru_wa[0], lru_wx[0]), lru_ba[0], lru_bx[0], lru_a_param[0],
              w_lru_down[0].astype(BF16), w_out[0].astype(BF16), perm, perm_t)
    mod3 = mod.reshape(mod.shape[0], 1, 3 * D_MODEL)
    y_prompt, new_ret, new_lru = _trunk(x_prompt.astype(F32), mod3, n_dec, 1, None, None,
                                        x_prompt.shape[1], 5, params, final_norm_w, True)
    y_sample, _, _ = _trunk(x_sample.astype(F32), mod3, 0, n_dec, state_ret[:, 0], state_lru[:, 0],
                            GRID_W, 2, params, final_norm_w, False)
    return (y_prompt.astype(x_prompt.dtype), y_sample.astype(x_sample.dtype),
            new_ret[:, None].astype(state_ret.dtype), new_lru[:, None].astype(state_lru.dtype))
```

```python
import functools

import jax
import jax.numpy as jnp
import numpy as np
from jax import lax
from jax.experimental import pallas as pl
from jax.experimental.pallas import tpu as pltpu

F32 = jnp.float32
BF16 = jnp.bfloat16

D_MODEL = 1024
N_HEADS = 4
DK = 256
DV = 512
D_QK = N_HEADS * DK
D_V = N_HEADS * DV
N_LRU_BLOCKS = 10
LRU_BLOCK = 128
D_LRU = N_LRU_BLOCKS * LRU_BLOCK
LRU_C = 8.0
CONV_W = 4
CONV_LEFT = 2
GRID_W = 64
EPS = 1e-6

OFF_Q = 0
OFF_K = OFF_Q + D_QK
OFF_V = OFF_K + D_QK
OFF_GRET = OFF_V + D_V
OFF_XLRU = OFF_GRET + D_V
OFF_GLRU = OFF_XLRU + D_LRU
OFF_MRET = OFF_GLRU + D_LRU
OFF_MLRU = OFF_MRET + D_MODEL
D_IN = OFF_MLRU + D_MODEL
D_MAIN = OFF_XLRU + 2 * D_MODEL
MAIN_MRET = OFF_XLRU
MAIN_MLRU = OFF_XLRU + D_MODEL

RET_CHUNK = 256
LRU_SEQS = 4
LRU_ROWS = 2 * LRU_SEQS
LRU_TC = 64
PERM_ROWS = LRU_TC * LRU_ROWS
PERM_UNIT = 32
INPROJ_LRU_SLABS = 128
VMEM_LIMIT = 56 * 1024 * 1024


def _sigmoid(x):
    return 0.5 * jnp.tanh(0.5 * x) + 0.5


def _silu(x):
    return x * _sigmoid(x)


def _softplus(x):
    return jnp.maximum(x, 0.0) + jnp.log1p(jnp.exp(-jnp.abs(x)))


def _slab_permutation(tc):
    n = tc * LRU_ROWS
    p = np.zeros((n, n), np.float32)
    for t in range(tc):
        for r in range(LRU_ROWS):
            if r < LRU_SEQS:
                src = r * tc + t
            else:
                src = LRU_SEQS * tc + (r - LRU_SEQS) * tc + (tc - 1 - t)
            p[t * LRU_ROWS + r, src] = 1.0
    return p


def _ada_kernel(c_ref, w_ref, b_ref, o_ref):
    cond = _silu(c_ref[...]).astype(BF16)
    o_ref[...] = jnp.dot(cond, w_ref[...].astype(BF16), preferred_element_type=F32) + b_ref[...]


def _ada(cond8, w_ada, b_ada):
    tn = 768
    return pl.pallas_call(
        _ada_kernel,
        grid=(3 * D_MODEL // tn,),
        in_specs=[pl.BlockSpec((8, D_MODEL), lambda j: (0, 0)),
                  pl.BlockSpec((D_MODEL, tn), lambda j: (0, j)),
                  pl.BlockSpec((1, tn), lambda j: (0, j))],
        out_specs=pl.BlockSpec((8, tn), lambda j: (0, j)),
        out_shape=jax.ShapeDtypeStruct((8, 3 * D_MODEL), F32),
        name="ada",
    )(cond8, w_ada, b_ada.reshape(1, -1))


def _modulated_norm(x, nw, scale, shift):
    ms = jnp.mean(x * x, axis=-1, keepdims=True)
    return (x * lax.rsqrt(ms + EPS)) * (nw * (1.0 + scale)) + shift


INPROJ_TM = 1024


def _inproj_kernel(x_ref, nw_ref, sc_ref, sh_ref, w_ref, z_ref, h_scr):
    @pl.when(pl.program_id(1) == 0)
    def _():
        h_scr[...] = _modulated_norm(x_ref[...], nw_ref[...], sc_ref[0], sh_ref[0]).astype(BF16)

    z_ref[...] = jnp.dot(h_scr[...], w_ref[...], preferred_element_type=F32).astype(z_ref.dtype)


MOD_SHIFT, MOD_SCALE, MOD_GATE = 0, 1, 2


def _inproj(x2d, norm_w, mod3, cond_row, w_all):
    m = x2d.shape[0]
    tm, tn = INPROJ_TM, 2048
    n_lead = OFF_XLRU // tn
    assert OFF_XLRU == n_lead * tn and D_MAIN - OFF_XLRU == tn

    def w_col(j):
        return jnp.where(j < n_lead, j * tn, OFF_MRET)

    return pl.pallas_call(
        _inproj_kernel,
        grid=(m // tm, D_MAIN // tn),
        in_specs=[pl.BlockSpec((tm, D_MODEL), lambda i, j: (i, 0)),
                  pl.BlockSpec((1, D_MODEL), lambda i, j: (0, 0)),
                  pl.BlockSpec((1, 1, D_MODEL), lambda i, j: (cond_row(i), 0, MOD_SCALE)),
                  pl.BlockSpec((1, 1, D_MODEL), lambda i, j: (cond_row(i), 0, MOD_SHIFT)),
                  pl.BlockSpec((pl.Element(D_MODEL), pl.Element(tn)), lambda i, j: (0, w_col(j)))],
        out_specs=pl.BlockSpec((tm, tn), lambda i, j: (i, j)),
        out_shape=jax.ShapeDtypeStruct((m, D_MAIN), BF16),
        scratch_shapes=[pltpu.VMEM((tm, D_MODEL), BF16)],
        compiler_params=pltpu.CompilerParams(
            dimension_semantics=("parallel", "arbitrary"), vmem_limit_bytes=VMEM_LIMIT),
        name="inproj",
    )(x2d, norm_w.reshape(1, -1), mod3, mod3, w_all)


def _inproj_lru_kernel(xa_ref, xb_ref, nw_ref, sc_ref, sh_ref, p_ref, w_ref, xs_ref, gs_ref):
    nw = nw_ref[...]
    sc = sc_ref[...]
    sh = sh_ref[...]
    ha = _modulated_norm(xa_ref[0], nw, sc, sh).astype(BF16)
    hb = _modulated_norm(xb_ref[0], nw, sc, sh).astype(BF16)
    units = INPROJ_LRU_SLABS // PERM_UNIT
    pieces = []
    for u in range(units):
        lo, hi = u * PERM_UNIT, (units - 1 - u) * PERM_UNIT
        src = jnp.concatenate([ha[b, lo:lo + PERM_UNIT] for b in range(LRU_SEQS)]
                              + [hb[b, hi:hi + PERM_UNIT] for b in range(LRU_SEQS)], axis=0)
        pieces.append(jnp.dot(p_ref[...], src, preferred_element_type=F32).astype(BF16))
    hp = jnp.concatenate(pieces, axis=0)
    z = jnp.dot(hp, w_ref[...], preferred_element_type=F32)
    xs_ref[0] = z[:, :D_LRU].astype(xs_ref.dtype)
    gs_ref[0] = _silu(z[:, D_LRU:]).astype(gs_ref.dtype)


def _inproj_lru(x4, norm_w, mod3, row0, nrows, perm, w_all):
    groups, _, l, _ = x4.shape
    assert row0 % nrows == 0 and nrows in (1, LRU_SEQS)
    ts = INPROJ_LRU_SLABS
    nt = l // 2 // ts
    ntb = l // ts
    out_sds = jax.ShapeDtypeStruct((groups, l // 2 * LRU_ROWS, D_LRU), BF16)
    return pl.pallas_call(
        _inproj_lru_kernel,
        grid=(groups, nt),
        in_specs=[pl.BlockSpec((1, LRU_SEQS, ts, D_MODEL), lambda g, i: (g, 0, i, 0)),
                  pl.BlockSpec((1, LRU_SEQS, ts, D_MODEL), lambda g, i: (g, 0, ntb - 1 - i, 0)),
                  pl.BlockSpec((1, D_MODEL), lambda g, i: (0, 0)),
                  pl.BlockSpec((nrows, 1, D_MODEL), lambda g, i: (row0 // nrows, 0, MOD_SCALE)),
                  pl.BlockSpec((nrows, 1, D_MODEL), lambda g, i: (row0 // nrows, 0, MOD_SHIFT)),
                  pl.BlockSpec((PERM_UNIT * LRU_ROWS, PERM_UNIT * LRU_ROWS), lambda g, i: (0, 0)),
                  pl.BlockSpec((pl.Element(D_MODEL), pl.Element(2 * D_LRU)), lambda g, i: (0, OFF_XLRU))],
        out_specs=[pl.BlockSpec((1, ts * LRU_ROWS, D_LRU), lambda g, i: (g, i, 0)),
                   pl.BlockSpec((1, ts * LRU_ROWS, D_LRU), lambda g, i: (g, i, 0))],
        out_shape=[out_sds, out_sds],
        compiler_params=pltpu.CompilerParams(
            dimension_semantics=("parallel", "parallel"), vmem_limit_bytes=VMEM_LIMIT),
        name="inproj_lru",
    )(x4, x4, norm_w.reshape(1, -1), mod3, mod3, perm, w_all)


def _dot_tn(a, b):
    return lax.dot_general(a, b, (((0,), (0,)), ((), ())), preferred_element_type=F32)


def _dot_nt(a, b):
    return lax.dot_general(a, b, (((1,), (1,)), ((), ())), preferred_element_type=F32)


def _ret_kernel(*refs, nc, hp, has_state, emit_state):
    dl_ref, q_ref, k_ref, v_ref, hg_ref = refs[:5]
    pos = 5
    s0_ref = None
    if has_state:
        s0_ref = refs[pos]
        pos += 1
    o_ref = refs[pos]
    pos += 1
    sfin_ref = None
    if emit_state:
        sfin_ref = refs[pos]
        pos += 1
    sf_scr, sb_scr, sbh_scr = refs[pos:pos + 3]

    c = RET_CHUNK
    carry_states = has_state or nc > 1
    ii = lax.broadcasted_iota(jnp.int32, (c, c), 0)
    jj = lax.broadcasted_iota(jnp.int32, (c, c), 1)
    diff = (ii - jj).astype(F32)
    p = lax.broadcasted_iota(jnp.int32, (c, 1), 0).astype(F32)
    kscale = DK ** -0.5

    def rows_of(n):
        return pl.ds(pl.multiple_of(n * c, c), c)

    def normed_out(o, rows, vcols):
        ms = jnp.mean(o * o, axis=-1, keepdims=True)
        on = o * lax.rsqrt(ms + EPS)
        hg = hg_ref[0, rows, vcols].astype(F32)
        gate = hg * (jnp.tanh(hg) + 1.0)
        o_ref[0, rows, vcols] = (on * gate).astype(o_ref.dtype)

    for hh in range(hp):
        head = pl.program_id(1) * hp + hh
        qcols = slice(hh * DK, (hh + 1) * DK)
        vcols = slice(hh * DV, (hh + 1) * DV)
        lgf = -_softplus(-jnp.full((1, 1), dl_ref[0, head], F32))
        lgb = -_softplus(-jnp.full((1, 1), dl_ref[1, head], F32))
        decay = jnp.exp(jnp.where(diff >= 0, lgf * diff, -lgb * diff)) * kscale
        kdf = jnp.exp(lgf * (c - 1.0 - p)) * kscale
        kdb = jnp.exp(lgb * p) * kscale

        if not carry_states:
            rows = pl.ds(0, c)
            qn = q_ref[0, rows, qcols]
            kn = k_ref[0, rows, qcols]
            vn = v_ref[0, rows, vcols]
            s = (_dot_nt(qn, kn) * decay).astype(BF16)
            normed_out(jnp.dot(s, vn, preferred_element_type=F32), rows, vcols)
            k32 = kn.astype(F32)
            if emit_state:
                sfin_ref[0, 0, hh] = _dot_tn((k32 * kdf).astype(BF16), vn)
                sfin_ref[0, 1, hh] = _dot_tn((k32 * kdb).astype(BF16), vn)
            continue

        qdf = jnp.exp(lgf * (p + 1.0))
        qdb = jnp.exp(lgb * (c - p))
        cdf = jnp.exp(lgf * c)
        cdb = jnp.exp(lgb * c)
        if has_state:
            sf_scr[...] = s0_ref[0, 0, hh]
            sb_scr[...] = s0_ref[0, 1, hh]
        else:
            sf_scr[...] = jnp.zeros_like(sf_scr)
            sb_scr[...] = jnp.zeros_like(sb_scr)

        def rev_body(idx, carry):
            n = nc - 1 - idx
            rows = rows_of(n)
            sbh_scr[n] = sb_scr[...].astype(BF16)
            kb = (k_ref[0, rows, qcols].astype(F32) * kdb).astype(BF16)
            sb_scr[...] = cdb * sb_scr[...] + _dot_tn(kb, v_ref[0, rows, vcols])
            return carry

        lax.fori_loop(0, nc, rev_body, 0, unroll=True)

        def fwd_body(n, carry):
            rows = rows_of(n)
            qn = q_ref[0, rows, qcols]
            kn = k_ref[0, rows, qcols]
            vn = v_ref[0, rows, vcols]
            s = (_dot_nt(qn, kn) * decay).astype(BF16)
            q32 = qn.astype(F32)
            o = (jnp.dot(s, vn, preferred_element_type=F32)
                 + jnp.dot((q32 * qdf).astype(BF16), sf_scr[...].astype(BF16), preferred_element_type=F32)
                 + jnp.dot((q32 * qdb).astype(BF16), sbh_scr[n], preferred_element_type=F32))
            normed_out(o, rows, vcols)
            kf = (kn.astype(F32) * kdf).astype(BF16)
            sf_scr[...] = cdf * sf_scr[...] + _dot_tn(kf, vn)
            return carry

        lax.fori_loop(0, nc, fwd_body, 0, unroll=True)

        if emit_state:
            sfin_ref[0, 0, hh] = sf_scr[...]
            sfin_ref[0, 1, hh] = sb_scr[...]


def _retention(z3, decay_logit, s0, emit_state, hp):
    b, l, _ = z3.shape
    nc = l // RET_CHUNK
    has_state = s0 is not None
    kq, kv_ = OFF_K // (hp * DK), OFF_V // (hp * DV)
    kg = OFF_GRET // (hp * DV)
    in_specs = [pl.BlockSpec(memory_space=pltpu.SMEM),
                pl.BlockSpec((1, l, hp * DK), lambda i, h: (i, 0, h)),
                pl.BlockSpec((1, l, hp * DK), lambda i, h: (i, 0, kq + h)),
                pl.BlockSpec((1, l, hp * DV), lambda i, h: (i, 0, kv_ + h)),
                pl.BlockSpec((1, l, hp * DV), lambda i, h: (i, 0, kg + h))]
    args = [decay_logit, z3, z3, z3, z3]
    if has_state:
        in_specs.append(pl.BlockSpec((1, 2, hp, DK, DV), lambda i, h: (i, 0, h, 0, 0)))
        args.append(s0)
    out_specs = [pl.BlockSpec((1, l, hp * DV), lambda i, h: (i, 0, h))]
    out_shape = [jax.ShapeDtypeStruct((b, l, D_V), BF16)]
    if emit_state:
        out_specs.append(pl.BlockSpec((1, 2, hp, DK, DV), lambda i, h: (i, 0, h, 0, 0)))
        out_shape.append(jax.ShapeDtypeStruct((b, 2, N_HEADS, DK, DV), F32))
    return pl.pallas_call(
        functools.partial(_ret_kernel, nc=nc, hp=hp, has_state=has_state, emit_state=emit_state),
        grid=(b, N_HEADS // hp),
        in_specs=in_specs,
        out_specs=out_specs,
        out_shape=out_shape,
        scratch_shapes=[pltpu.VMEM((DK, DV), F32), pltpu.VMEM((DK, DV), F32), pltpu.VMEM((nc, DK, DV), BF16)],
        compiler_params=pltpu.CompilerParams(
            dimension_semantics=("parallel", "parallel"), vmem_limit_bytes=VMEM_LIMIT),
        name="retention",
    )(*args)


def _sqrt_unit(x):
    return x * lax.rsqrt(jnp.maximum(x, 1e-30))


def _lru_kernel(xs_ref, sg_ref, cw_ref, cb_ref, w_ref, ba_ref, bx_ref, ap_ref, h0_ref, pt_ref, out_ref, fin_ref,
                xp_scr, s1_scr, wm_scr, a0_scr, b0_scr, a1_scr, b1_scr, *, slabs, seq_len, width, nblk):
    r8 = LRU_ROWS
    tc = LRU_TC
    rows = PERM_ROWS
    cw = nblk * LRU_BLOCK
    nchunks = slabs // tc
    assert width == seq_len or width == tc

    def roll4(v):
        return pltpu.roll(v, LRU_SEQS, axis=0)

    xp_scr[pl.ds(0, 2 * r8), :] = jnp.zeros((2 * r8, cw), F32)
    xp_scr[pl.ds(2 * r8, slabs * r8), :] = xs_ref[0].astype(F32)
    xp_scr[pl.ds((slabs + 2) * r8, r8), :] = roll4(xs_ref[0, pl.ds((slabs - 1) * r8, r8), :].astype(F32))
    xp_scr[pl.ds((slabs + 3) * r8, r8), :] = roll4(xs_ref[0, pl.ds((slabs - 2) * r8, r8), :].astype(F32))

    row_cw = lax.broadcasted_iota(jnp.int32, (rows, cw), 0)
    step_in_chunk = row_cw >> 3
    low_cw = (row_cw & (r8 - 1)) < LRU_SEQS

    def row_pattern(lo, hi):
        n = lo.shape[-1]
        return jnp.where(lax.broadcasted_iota(jnp.int32, (r8, n), 0) < LRU_SEQS, lo, hi)

    def tap_rows(s):
        zero = jnp.zeros((1, cw), F32)
        lo = cw_ref[pl.ds(s + CONV_LEFT, 1), :] if 0 <= s + CONV_LEFT < CONV_W else zero
        hi = cw_ref[pl.ds(CONV_LEFT - s, 1), :] if 0 <= CONV_LEFT - s < CONV_W else zero
        return 0.5 * row_pattern(lo, hi)

    def tap_table(s):
        tap = jnp.broadcast_to(tap_rows(s)[None], (tc, r8, cw)).reshape(rows, cw)
        if width != seq_len:
            tap = jnp.where((step_in_chunk + s >= 0) & (step_in_chunk + s < tc), tap, 0.0)
        return tap

    wm_scr[0] = tap_table(-2) + tap_table(2)
    for s in range(-1, 2):
        wm_scr[s + 2] = tap_table(s)

    ap8 = row_pattern(ap_ref[pl.ds(0, 1), :], ap_ref[pl.ds(1, 1), :])
    c1_lo = (-0.5 * LRU_C / np.log(2.0)) * _softplus(-ap8)
    cb_half = 0.5 * cb_ref[...]

    def half_bias(j):
        cols = slice(j * LRU_BLOCK, (j + 1) * LRU_BLOCK)
        return 0.5 * jnp.concatenate(
            [row_pattern(ba_ref[pl.ds(0, 1), cols], ba_ref[pl.ds(1, 1), cols]),
             row_pattern(bx_ref[pl.ds(0, 1), cols], bx_ref[pl.ds(1, 1), cols])], axis=1)

    bias_lo = [half_bias(j) for j in range(nblk)]
    low_rows = (lax.broadcasted_iota(jnp.int32, (rows, LRU_BLOCK), 0) & (r8 - 1)) < LRU_SEQS
    high_rows = jnp.logical_not(low_rows)

    def conv_half(t0):
        def shifted(s):
            return xp_scr[pl.ds(pl.multiple_of((t0 + s + 2) * r8, r8), rows), :]

        xh = jnp.where(low_cw, shifted(-2), shifted(2)) * wm_scr[0] + cb_half
        for s in range(-1, 2):
            xh = xh + shifted(s) * wm_scr[s + 2]
        xp_scr[pl.ds(pl.multiple_of(t0 * r8, rows), rows), :] = xh
        return xh

    def gates(xh, sweep2, a_scr, b_scr):
        fj = high_rows if sweep2 else low_rows
        c1 = roll4(c1_lo) if sweep2 else c1_lo
        for j in range(nblk):
            cols = slice(j * LRU_BLOCK, (j + 1) * LRU_BLOCK)
            xj = xh[:, cols]
            lhs = jnp.concatenate([jnp.where(fj, xj, 0.0), jnp.where(fj, 0.0, xj)], axis=1)
            pre = jnp.dot(lhs.astype(BF16), w_ref[j], preferred_element_type=F32)
            bj = roll4(bias_lo[j]) if sweep2 else bias_lo[j]
            pre = pre.reshape(tc, r8, 2 * LRU_BLOCK) + bj[None]
            tr = jnp.tanh(pre[:, :, :LRU_BLOCK])
            tg = jnp.tanh(pre[:, :, LRU_BLOCK:])
            c1j = c1[:, cols][None]
            a = jnp.exp2(c1j * tr + c1j)
            bco = _sqrt_unit(1.0 - a * a) * ((tg + 1.0) * xj.reshape(tc, r8, LRU_BLOCK))
            a_scr[:, cols] = a.reshape(rows, LRU_BLOCK)
            b_scr[:, cols] = bco.reshape(rows, LRU_BLOCK)

    bufs = ((a0_scr, b0_scr), (a1_scr, b1_scr))

    def chunk_rows(t0):
        return pl.ds(pl.multiple_of(t0 * r8, rows), rows)

    def scan(buf, h, dst_ref, t0, descending):
        a_ref, b_ref = bufs[buf]
        for i in range(tc):
            t = tc - 1 - i if descending else i
            h = a_ref[t * r8:(t + 1) * r8, :] * h + b_ref[t * r8:(t + 1) * r8, :]
            if dst_ref is None:
                b_ref[t * r8:(t + 1) * r8, :] = h
            else:
                dst_ref[pl.ds(pl.multiple_of(t0 * r8, rows) + t * r8, r8), :] = h
        return h

    def gates1(ci, buf):
        gates(conv_half(ci * tc), False, *bufs[buf])

    def scan1(ci, buf, h):
        return scan(buf, h, s1_scr, ci * tc, False)

    npairs = nchunks // 2
    assert nchunks == 2 * npairs
    gates1(0, 0)

    def sweep1_pair(i, h):
        gates1(2 * i + 1, 1)
        h = scan1(2 * i, 0, h)
        gates1(2 * i + 2, 0)
        return scan1(2 * i + 1, 1, h)

    h = lax.fori_loop(0, npairs - 1, sweep1_pair, h0_ref[0])
    gates1(nchunks - 1, 1)
    h = scan1(nchunks - 2, 0, h)
    h = scan1(nchunks - 1, 1, h)

    def gates2(p, buf):
        gates(xp_scr[chunk_rows((nchunks - 1 - p) * tc), :], True, *bufs[buf])

    def scan2(p, buf, h):
        return scan(buf, h, None, 0, True)

    def finish(p, buf):
        t0 = (nchunks - 1 - p) * tc
        crows = chunk_rows(t0)
        o = ((s1_scr[crows, :] + bufs[buf][1][...]) * sg_ref[0, crows, :].astype(F32)).astype(BF16)
        nat = jnp.dot(pt_ref[...], o, preferred_element_type=F32).astype(out_ref.dtype)
        lo = pl.multiple_of(t0, tc)
        hi = pl.multiple_of(seq_len - tc - t0, tc)
        for b in range(LRU_SEQS):
            out_ref[0, b, pl.ds(lo, tc), :] = nat[b * tc:(b + 1) * tc]
            out_ref[0, b, pl.ds(hi, tc), :] = nat[(LRU_SEQS + b) * tc:(LRU_SEQS + b + 1) * tc]

    h = roll4(h)
    gates2(0, 0)
    gates2(1, 1)
    h = scan2(0, 0, h)

    def sweep2_pair(i, h):
        finish(2 * i, 0)
        gates2(2 * i + 2, 0)
        h = scan2(2 * i + 1, 1, h)
        finish(2 * i + 1, 1)
        gates2(2 * i + 3, 1)
        return scan2(2 * i + 2, 0, h)

    h = lax.fori_loop(0, npairs - 1, sweep2_pair, h)
    finish(nchunks - 2, 0)
    h = scan2(nchunks - 1, 1, h)
    finish(nchunks - 1, 1)
    fin_ref[0] = h


def _lru(xs, gs, conv_w, conv_b, w_blk, ba, bx, a_param, h0, perm_t, seq_len, width, nblk):
    groups = xs.shape[0]
    r8 = LRU_ROWS
    slabs = seq_len // 2
    cw = nblk * LRU_BLOCK
    per_dir = pl.BlockSpec((2, cw), lambda g, c: (0, c))
    return pl.pallas_call(
        functools.partial(_lru_kernel, slabs=slabs, seq_len=seq_len, width=width, nblk=nblk),
        grid=(groups, N_LRU_BLOCKS // nblk),
        in_specs=[pl.BlockSpec((1, slabs * r8, cw), lambda g, c: (g, 0, c)),
                  pl.BlockSpec((1, slabs * r8, cw), lambda g, c: (g, 0, c)),
                  pl.BlockSpec((CONV_W, cw), lambda g, c: (0, c)),
                  pl.BlockSpec((1, cw), lambda g, c: (0, c)),
                  pl.BlockSpec((nblk, 2 * LRU_BLOCK, 2 * LRU_BLOCK), lambda g, c: (c, 0, 0)),
                  per_dir, per_dir, per_dir,
                  pl.BlockSpec((1, r8, cw), lambda g, c: (g, 0, c)),
                  pl.BlockSpec((PERM_ROWS, PERM_ROWS), lambda g, c: (0, 0))],
        out_specs=[pl.BlockSpec((1, LRU_SEQS, seq_len, cw), lambda g, c: (g, 0, 0, c)),
                   pl.BlockSpec((1, r8, cw), lambda g, c: (g, 0, c))],
        out_shape=[jax.ShapeDtypeStruct((groups, LRU_SEQS, seq_len, D_LRU), BF16),
                   jax.ShapeDtypeStruct((groups, r8, D_LRU), F32)],
        scratch_shapes=[pltpu.VMEM(((slabs + 4) * r8, cw), F32),
                        pltpu.VMEM((slabs * r8, cw), F32),
                        pltpu.VMEM((4, PERM_ROWS, cw), F32),
                        pltpu.VMEM((PERM_ROWS, cw), F32),
                        pltpu.VMEM((PERM_ROWS, cw), F32),
                        pltpu.VMEM((PERM_ROWS, cw), F32),
                        pltpu.VMEM((PERM_ROWS, cw), F32)],
        compiler_params=pltpu.CompilerParams(
            dimension_semantics=("parallel", "parallel"), vmem_limit_bytes=VMEM_LIMIT),
        name="lru",
    )(xs, gs, conv_w, conv_b.reshape(1, -1), w_blk, ba, bx, a_param, h0, perm_t)


def _lru_gate_weights(wa, wx):
    return jnp.concatenate([jnp.concatenate([wa[0], wx[0]], axis=2),
                            jnp.concatenate([wa[1], wx[1]], axis=2)], axis=1).astype(BF16)


def _tail_kernel(x_ref, o_ref, l_ref, mr_ref, ml_ref, gate_ref, fnw_ref, wrd_ref, wld_ref, wo_ref, y_ref):
    ret_out = jnp.dot(o_ref[...], wrd_ref[...], preferred_element_type=F32)
    lru_out = jnp.dot(l_ref[...], wld_ref[...], preferred_element_type=F32)
    merged = _sigmoid(mr_ref[...].astype(F32)) * ret_out + _sigmoid(ml_ref[...].astype(F32)) * lru_out
    out = jnp.dot(merged.astype(BF16), wo_ref[...], preferred_element_type=F32)
    y = x_ref[...] + gate_ref[0] * out
    ms = jnp.mean(y * y, axis=-1, keepdims=True)
    y_ref[...] = y * lax.rsqrt(ms + EPS) * fnw_ref[...]


TAIL_TM = 512


def _tail(x2d, o2d, l2d, z2d, mod3, cond_row, fnw, wrd_b, wld_b, wo_b):
    m = x2d.shape[0]
    tm = TAIL_TM
    kmr, kml = MAIN_MRET // D_MODEL, MAIN_MLRU // D_MODEL
    const = lambda i: (0, 0)
    return pl.pallas_call(
        _tail_kernel,
        grid=(m // tm,),
        in_specs=[pl.BlockSpec((tm, D_MODEL), lambda i: (i, 0)),
                  pl.BlockSpec((tm, D_V), lambda i: (i, 0)),
                  pl.BlockSpec((tm, D_LRU), lambda i: (i, 0)),
                  pl.BlockSpec((tm, D_MODEL), lambda i: (i, kmr)),
                  pl.BlockSpec((tm, D_MODEL), lambda i: (i, kml)),
                  pl.BlockSpec((1, 1, D_MODEL), lambda i: (cond_row(i), 0, MOD_GATE)),
                  pl.BlockSpec((1, D_MODEL), const),
                  pl.BlockSpec((D_V, D_MODEL), const),
                  pl.BlockSpec((D_LRU, D_MODEL), const),
                  pl.BlockSpec((D_MODEL, D_MODEL), const)],
        out_specs=pl.BlockSpec((tm, D_MODEL), lambda i: (i, 0)),
        out_shape=jax.ShapeDtypeStruct((m, D_MODEL), F32),
        compiler_params=pltpu.CompilerParams(
            dimension_semantics=("parallel",), vmem_limit_bytes=VMEM_LIMIT),
        name="tail",
    )(x2d, o2d, l2d, z2d, z2d, mod3, fnw.reshape(1, -1), wrd_b, wld_b, wo_b)


def _trunk(x, mod3, cond_row0, n_cond, s0_ret, h0_lru, width, lru_nblk, params, final_norm_w, emit_state):
    (norm_w, w_all, decay_logit, wrd_b, conv_w, conv_b, w_blk, ba, bx, a_param, wld_b, wo_b, perm, perm_t) = params
    b, l, _ = x.shape
    groups = b // LRU_SEQS
    x2d = x.reshape(b * l, D_MODEL)
    tokens_per_cond = b * l // n_cond

    def cond_row(tile_tokens):
        return lambda i: cond_row0 + i // (tokens_per_cond // tile_tokens)

    z2d = _inproj(x2d, norm_w, mod3, cond_row(INPROJ_TM), w_all)
    z3 = z2d.reshape(b, l, D_MAIN)
    ret = _retention(z3, decay_logit, s0_ret, emit_state, N_HEADS if l == RET_CHUNK else 1)

    xs, gs = _inproj_lru(x.reshape(groups, LRU_SEQS, l, D_MODEL), norm_w, mod3, cond_row0, n_cond, perm, w_all)
    if h0_lru is None:
        h0 = jnp.zeros((groups, LRU_ROWS, D_LRU), F32)
    else:
        h0 = h0_lru.reshape(groups, LRU_SEQS, 2, D_LRU).transpose(0, 2, 1, 3).reshape(groups, LRU_ROWS, D_LRU)
    lru_pre, fin = _lru(xs, gs, conv_w, conv_b, w_blk, ba, bx, a_param, h0, perm_t, l, width, lru_nblk)
    fin = fin.reshape(groups, 2, LRU_SEQS, D_LRU)
    lru_fin = jnp.stack([fin[:, 1], fin[:, 0]], axis=2).reshape(b, 2, D_LRU)

    y = _tail(x2d, ret[0].reshape(b * l, D_V), lru_pre.reshape(b * l, D_LRU), z2d, mod3, cond_row(TAIL_TM),
              final_norm_w, wrd_b, wld_b, wo_b)
    return y.reshape(b, l, D_MODEL), (ret[1] if emit_state else None), lru_fin


def kernel(x_prompt, x_sample, state_ret, state_lru, c, c_ctx, norm_w, w_ada, b_ada, w_in, ret_decay_logit,
           ret_gn_w, w_ret_down, conv_w, conv_b, lru_wa, lru_ba, lru_wx, lru_bx, lru_a_param, w_lru_down,
           w_out, final_norm_w):
    assert norm_w.shape[0] == 1, "single-layer step"
    n_dec = c.shape[0]
    cond8 = jnp.concatenate([c.astype(F32), c_ctx.astype(F32)[None],
                             jnp.zeros((8 - n_dec - 1, D_MODEL), F32)], axis=0)
    mod = _ada(cond8, w_ada[0], b_ada[0])
    col = jnp.arange(D_IN)
    col_scale = jnp.where((col >= OFF_GRET) & (col < OFF_XLRU), 0.5, 1.0).astype(F32)
    w_all = (w_in[0] * col_scale[None, :]).astype(BF16)
    wrd_b = (ret_gn_w[0][:, None] * w_ret_down[0]).astype(BF16)
    perm = jnp.asarray(_slab_permutation(PERM_UNIT), BF16)
    perm_t = jnp.asarray(_slab_permutation(LRU_TC).T, BF16)
    params = (norm_w[0], w_all, ret_decay_logit[0], wrd_b, conv_w[0], conv_b[0],
              _lru_gate_weights(lru_wa[0], lru_wx[0]), lru_ba[0], lru_bx[0], lru_a_param[0],
              w_lru_down[0].astype(BF16), w_out[0].astype(BF16), perm, perm_t)
    mod3 = mod.reshape(mod.shape[0], 1, 3 * D_MODEL)
    y_prompt, new_ret, new_lru = _trunk(x_prompt.astype(F32), mod3, n_dec, 1, None, None,
                                        x_prompt.shape[1], 5, params, final_norm_w, True)
    y_sample, _, _ = _trunk(x_sample.astype(F32), mod3, 0, n_dec, state_ret[:, 0], state_lru[:, 0],
                            GRID_W, 2, params, final_norm_w, False)
    return (y_prompt.astype(x_prompt.dtype), y_sample.astype(x_sample.dtype),
            new_ret[:, None].astype(state_ret.dtype), new_lru[:, None].astype(state_lru.dtype))
```

```python
import functools

import jax
import jax.numpy as jnp
import numpy as np
from jax import lax
from jax.experimental import pallas as pl
from jax.experimental.pallas import tpu as pltpu

F32 = jnp.float32
BF16 = jnp.bfloat16

D_MODEL = 1024
N_HEADS = 4
DK = 256
DV = 512
D_QK = N_HEADS * DK
D_V = N_HEADS * DV
N_LRU_BLOCKS = 10
LRU_BLOCK = 128
D_LRU = N_LRU_BLOCKS * LRU_BLOCK
LRU_C = 8.0
CONV_W = 4
CONV_LEFT = 2
GRID_W = 64
EPS = 1e-6

OFF_Q = 0
OFF_K = OFF_Q + D_QK
OFF_V = OFF_K + D_QK
OFF_GRET = OFF_V + D_V
OFF_XLRU = OFF_GRET + D_V
OFF_GLRU = OFF_XLRU + D_LRU
OFF_MRET = OFF_GLRU + D_LRU
OFF_MLRU = OFF_MRET + D_MODEL
D_IN = OFF_MLRU + D_MODEL
D_MAIN = OFF_XLRU + 2 * D_MODEL
MAIN_MRET = OFF_XLRU
MAIN_MLRU = OFF_XLRU + D_MODEL

RET_CHUNK = 256
LRU_SEQS = 4
LRU_ROWS = 2 * LRU_SEQS
LRU_TC = 64
PERM_ROWS = LRU_TC * LRU_ROWS
PERM_UNIT = 32
INPROJ_LRU_SLABS = 128
VMEM_LIMIT = 56 * 1024 * 1024


def _sigmoid(x):
    return 0.5 * jnp.tanh(0.5 * x) + 0.5


def _silu(x):
    return x * _sigmoid(x)


def _softplus(x):
    return jnp.maximum(x, 0.0) + jnp.log1p(jnp.exp(-jnp.abs(x)))


def _slab_permutation(tc):
    n = tc * LRU_ROWS
    p = np.zeros((n, n), np.float32)
    for t in range(tc):
        for r in range(LRU_ROWS):
            if r < LRU_SEQS:
                src = r * tc + t
            else:
                src = LRU_SEQS * tc + (r - LRU_SEQS) * tc + (tc - 1 - t)
            p[t * LRU_ROWS + r, src] = 1.0
    return p


def _ada_kernel(c_ref, w_ref, b_ref, o_ref):
    cond = _silu(c_ref[...]).astype(BF16)
    o_ref[...] = jnp.dot(cond, w_ref[...].astype(BF16), preferred_element_type=F32) + b_ref[...]


def _ada(cond8, w_ada, b_ada):
    tn = 768
    return pl.pallas_call(
        _ada_kernel,
        grid=(3 * D_MODEL // tn,),
        in_specs=[pl.BlockSpec((8, D_MODEL), lambda j: (0, 0)),
                  pl.BlockSpec((D_MODEL, tn), lambda j: (0, j)),
                  pl.BlockSpec((1, tn), lambda j: (0, j))],
        out_specs=pl.BlockSpec((8, tn), lambda j: (0, j)),
        out_shape=jax.ShapeDtypeStruct((8, 3 * D_MODEL), F32),
        name="ada",
    )(cond8, w_ada, b_ada.reshape(1, -1))


def _modulated_norm(x, nw, scale, shift):
    ms = jnp.mean(x * x, axis=-1, keepdims=True)
    return (x * lax.rsqrt(ms + EPS)) * (nw * (1.0 + scale)) + shift


INPROJ_TM = 1024
INPROJ_TN = 2048


def _inproj_kernel(x_ref, nw_ref, sc_ref, sh_ref, w_ref, z_ref, h_scr):
    @pl.when(pl.program_id(1) == 0)
    def _():
        h_scr[...] = _modulated_norm(x_ref[...], nw_ref[...], sc_ref[0], sh_ref[0]).astype(BF16)

    z_ref[...] = jnp.dot(h_scr[...], w_ref[...], preferred_element_type=F32).astype(z_ref.dtype)


MOD_SHIFT, MOD_SCALE, MOD_GATE = 0, 1, 2


def _inproj(x2d, norm_w, mod3, cond_row, w_all):
    m = x2d.shape[0]
    tm, tn = INPROJ_TM, INPROJ_TN
    n_lead = OFF_XLRU // tn
    n_tiles = D_MAIN // tn
    assert OFF_XLRU == n_lead * tn and D_MAIN == n_tiles * tn

    def w_col(j):
        off = j * tn
        for t in range(n_lead, n_tiles):
            off = jnp.where(j == t, OFF_MRET + (t - n_lead) * tn, off)
        return off

    return pl.pallas_call(
        _inproj_kernel,
        grid=(m // tm, D_MAIN // tn),
        in_specs=[pl.BlockSpec((tm, D_MODEL), lambda i, j: (i, 0)),
                  pl.BlockSpec((1, D_MODEL), lambda i, j: (0, 0)),
                  pl.BlockSpec((1, 1, D_MODEL), lambda i, j: (cond_row(i), 0, MOD_SCALE)),
                  pl.BlockSpec((1, 1, D_MODEL), lambda i, j: (cond_row(i), 0, MOD_SHIFT)),
                  pl.BlockSpec((pl.Element(D_MODEL), pl.Element(tn)), lambda i, j: (0, w_col(j)))],
        out_specs=pl.BlockSpec((tm, tn), lambda i, j: (i, j)),
        out_shape=jax.ShapeDtypeStruct((m, D_MAIN), BF16),
        scratch_shapes=[pltpu.VMEM((tm, D_MODEL), BF16)],
        compiler_params=pltpu.CompilerParams(
            dimension_semantics=("parallel", "arbitrary"), vmem_limit_bytes=VMEM_LIMIT),
        name="inproj",
    )(x2d, norm_w.reshape(1, -1), mod3, mod3, w_all)


def _inproj_lru_kernel(xa_ref, xb_ref, nw_ref, sc_ref, sh_ref, p_ref, w_ref, xs_ref, gs_ref):
    nw = nw_ref[...]
    sc = sc_ref[...]
    sh = sh_ref[...]
    ha = _modulated_norm(xa_ref[0], nw, sc, sh).astype(BF16)
    hb = _modulated_norm(xb_ref[0], nw, sc, sh).astype(BF16)
    units = INPROJ_LRU_SLABS // PERM_UNIT
    pieces = []
    for u in range(units):
        lo, hi = u * PERM_UNIT, (units - 1 - u) * PERM_UNIT
        src = jnp.concatenate([ha[b, lo:lo + PERM_UNIT] for b in range(LRU_SEQS)]
                              + [hb[b, hi:hi + PERM_UNIT] for b in range(LRU_SEQS)], axis=0)
        pieces.append(jnp.dot(p_ref[...], src, preferred_element_type=F32).astype(BF16))
    hp = jnp.concatenate(pieces, axis=0)
    z = jnp.dot(hp, w_ref[...], preferred_element_type=F32)
    xs_ref[0] = z[:, :D_LRU].astype(xs_ref.dtype)
    gs_ref[0] = _silu(z[:, D_LRU:]).astype(gs_ref.dtype)


def _inproj_lru(x4, norm_w, mod3, row0, nrows, perm, w_all):
    groups, _, l, _ = x4.shape
    assert row0 % nrows == 0 and nrows in (1, LRU_SEQS)
    ts = INPROJ_LRU_SLABS
    nt = l // 2 // ts
    ntb = l // ts
    out_sds = jax.ShapeDtypeStruct((groups, l // 2 * LRU_ROWS, D_LRU), BF16)
    return pl.pallas_call(
        _inproj_lru_kernel,
        grid=(groups, nt),
        in_specs=[pl.BlockSpec((1, LRU_SEQS, ts, D_MODEL), lambda g, i: (g, 0, i, 0)),
                  pl.BlockSpec((1, LRU_SEQS, ts, D_MODEL), lambda g, i: (g, 0, ntb - 1 - i, 0)),
                  pl.BlockSpec((1, D_MODEL), lambda g, i: (0, 0)),
                  pl.BlockSpec((nrows, 1, D_MODEL), lambda g, i: (row0 // nrows, 0, MOD_SCALE)),
                  pl.BlockSpec((nrows, 1, D_MODEL), lambda g, i: (row0 // nrows, 0, MOD_SHIFT)),
                  pl.BlockSpec((PERM_UNIT * LRU_ROWS, PERM_UNIT * LRU_ROWS), lambda g, i: (0, 0)),
                  pl.BlockSpec((pl.Element(D_MODEL), pl.Element(2 * D_LRU)), lambda g, i: (0, OFF_XLRU))],
        out_specs=[pl.BlockSpec((1, ts * LRU_ROWS, D_LRU), lambda g, i: (g, i, 0)),
                   pl.BlockSpec((1, ts * LRU_ROWS, D_LRU), lambda g, i: (g, i, 0))],
        out_shape=[out_sds, out_sds],
        compiler_params=pltpu.CompilerParams(
            dimension_semantics=("parallel", "parallel"), vmem_limit_bytes=VMEM_LIMIT),
        name="inproj_lru",
    )(x4, x4, norm_w.reshape(1, -1), mod3, mod3, perm, w_all)


def _dot_tn(a, b):
    return lax.dot_general(a, b, (((0,), (0,)), ((), ())), preferred_element_type=F32)


def _dot_nt(a, b):
    return lax.dot_general(a, b, (((1,), (1,)), ((), ())), preferred_element_type=F32)


def _ret_kernel(*refs, nc, hp, has_state, emit_state):
    dl_ref, q_ref, k_ref, v_ref, hg_ref = refs[:5]
    pos = 5
    s0_ref = None
    if has_state:
        s0_ref = refs[pos]
        pos += 1
    o_ref = refs[pos]
    pos += 1
    sfin_ref = None
    if emit_state:
        sfin_ref = refs[pos]
        pos += 1
    sf_scr, sb_scr, sbh_scr = refs[pos:pos + 3]

    c = RET_CHUNK
    carry_states = has_state or nc > 1
    ii = lax.broadcasted_iota(jnp.int32, (c, c), 0)
    jj = lax.broadcasted_iota(jnp.int32, (c, c), 1)
    diff = (ii - jj).astype(F32)
    p = lax.broadcasted_iota(jnp.int32, (c, 1), 0).astype(F32)
    kscale = DK ** -0.5

    def rows_of(n):
        return pl.ds(pl.multiple_of(n * c, c), c)

    def normed_out(o, rows, vcols):
        ms = jnp.mean(o * o, axis=-1, keepdims=True)
        on = o * lax.rsqrt(ms + EPS)
        hg = hg_ref[0, rows, vcols]
        gate = hg * (jnp.tanh(hg) + 1.0)
        o_ref[0, rows, vcols] = on.astype(o_ref.dtype) * gate

    for hh in range(hp):
        head = pl.program_id(1) * hp + hh
        qcols = slice(hh * DK, (hh + 1) * DK)
        vcols = slice(hh * DV, (hh + 1) * DV)
        lgf = -_softplus(-jnp.full((1, 1), dl_ref[0, head], F32))
        lgb = -_softplus(-jnp.full((1, 1), dl_ref[1, head], F32))
        decay = jnp.exp(jnp.where(diff >= 0, lgf * diff, -lgb * diff)) * kscale
        kdf = jnp.exp(lgf * (c - 1.0 - p)) * kscale
        kdb = jnp.exp(lgb * p) * kscale

        if not carry_states:
            rows = pl.ds(0, c)
            qn = q_ref[0, rows, qcols]
            kn = k_ref[0, rows, qcols]
            vn = v_ref[0, rows, vcols]
            s = (_dot_nt(qn, kn) * decay).astype(BF16)
            normed_out(jnp.dot(s, vn, preferred_element_type=F32), rows, vcols)
            k32 = kn.astype(F32)
            if emit_state:
                sfin_ref[0, 0, hh] = _dot_tn((k32 * kdf).astype(BF16), vn)
                sfin_ref[0, 1, hh] = _dot_tn((k32 * kdb).astype(BF16), vn)
            continue

        qdf = jnp.exp(lgf * (p + 1.0))
        qdb = jnp.exp(lgb * (c - p))
        cdf = jnp.exp(lgf * c)
        cdb = jnp.exp(lgb * c)
        if has_state:
            sf_scr[...] = s0_ref[0, 0, hh]
            sb_scr[...] = s0_ref[0, 1, hh]
        else:
            sf_scr[...] = jnp.zeros_like(sf_scr)
            sb_scr[...] = jnp.zeros_like(sb_scr)

        def rev_body(idx, carry):
            n = nc - 1 - idx
            rows = rows_of(n)
            sbh_scr[n] = sb_scr[...].astype(BF16)
            kb = (k_ref[0, rows, qcols].astype(F32) * kdb).astype(BF16)
            sb_scr[...] = cdb * sb_scr[...] + _dot_tn(kb, v_ref[0, rows, vcols])
            return carry

        lax.fori_loop(0, nc, rev_body, 0, unroll=True)

        def fwd_body(n, carry):
            rows = rows_of(n)
            qn = q_ref[0, rows, qcols]
            kn = k_ref[0, rows, qcols]
            vn = v_ref[0, rows, vcols]
            s = (_dot_nt(qn, kn) * decay).astype(BF16)
            q32 = qn.astype(F32)
            o = (jnp.dot(s, vn, preferred_element_type=F32)
                 + jnp.dot((q32 * qdf).astype(BF16), sf_scr[...].astype(BF16), preferred_element_type=F32)
                 + jnp.dot((q32 * qdb).astype(BF16), sbh_scr[n], preferred_element_type=F32))
            normed_out(o, rows, vcols)
            kf = (kn.astype(F32) * kdf).astype(BF16)
            sf_scr[...] = cdf * sf_scr[...] + _dot_tn(kf, vn)
            return carry

        lax.fori_loop(0, nc, fwd_body, 0, unroll=True)

        if emit_state:
            sfin_ref[0, 0, hh] = sf_scr[...]
            sfin_ref[0, 1, hh] = sb_scr[...]


def _retention(z3, decay_logit, s0, emit_state, hp):
    b, l, _ = z3.shape
    nc = l // RET_CHUNK
    has_state = s0 is not None
    kq, kv_ = OFF_K // (hp * DK), OFF_V // (hp * DV)
    kg = OFF_GRET // (hp * DV)
    in_specs = [pl.BlockSpec(memory_space=pltpu.SMEM),
                pl.BlockSpec((1, l, hp * DK), lambda i, h: (i, 0, h)),
                pl.BlockSpec((1, l, hp * DK), lambda i, h: (i, 0, kq + h)),
                pl.BlockSpec((1, l, hp * DV), lambda i, h: (i, 0, kv_ + h)),
                pl.BlockSpec((1, l, hp * DV), lambda i, h: (i, 0, kg + h))]
    args = [decay_logit, z3, z3, z3, z3]
    if has_state:
        in_specs.append(pl.BlockSpec((1, 2, hp, DK, DV), lambda i, h: (i, 0, h, 0, 0)))
        args.append(s0)
    out_specs = [pl.BlockSpec((1, l, hp * DV), lambda i, h: (i, 0, h))]
    out_shape = [jax.ShapeDtypeStruct((b, l, D_V), BF16)]
    if emit_state:
        out_specs.append(pl.BlockSpec((1, 2, hp, DK, DV), lambda i, h: (i, 0, h, 0, 0)))
        out_shape.append(jax.ShapeDtypeStruct((b, 2, N_HEADS, DK, DV), F32))
    return pl.pallas_call(
        functools.partial(_ret_kernel, nc=nc, hp=hp, has_state=has_state, emit_state=emit_state),
        grid=(b, N_HEADS // hp),
        in_specs=in_specs,
        out_specs=out_specs,
        out_shape=out_shape,
        scratch_shapes=[pltpu.VMEM((DK, DV), F32), pltpu.VMEM((DK, DV), F32), pltpu.VMEM((nc, DK, DV), BF16)],
        compiler_params=pltpu.CompilerParams(
            dimension_semantics=("parallel", "parallel"), vmem_limit_bytes=VMEM_LIMIT),
        name="retention",
    )(*args)


def _sqrt_unit(x):
    return x * lax.rsqrt(jnp.maximum(x, 1e-30))


def _lru_kernel(xs_ref, sg_ref, cw_ref, cb_ref, w_ref, ba_ref, bx_ref, ap_ref, h0_ref, pt_ref, out_ref, fin_ref,
                xp_scr, s1_scr, wm_scr, a0_scr, b0_scr, a1_scr, b1_scr, *, slabs, seq_len, width, nblk):
    r8 = LRU_ROWS
    tc = LRU_TC
    rows = PERM_ROWS
    cw = nblk * LRU_BLOCK
    nchunks = slabs // tc
    assert width == seq_len or width == tc

    def roll4(v):
        return pltpu.roll(v, LRU_SEQS, axis=0)

    xp_scr[pl.ds(0, 2 * r8), :] = jnp.zeros((2 * r8, cw), F32)
    xp_scr[pl.ds(2 * r8, slabs * r8), :] = xs_ref[0].astype(F32)
    xp_scr[pl.ds((slabs + 2) * r8, r8), :] = roll4(xs_ref[0, pl.ds((slabs - 1) * r8, r8), :].astype(F32))
    xp_scr[pl.ds((slabs + 3) * r8, r8), :] = roll4(xs_ref[0, pl.ds((slabs - 2) * r8, r8), :].astype(F32))

    row_cw = lax.broadcasted_iota(jnp.int32, (rows, cw), 0)
    step_in_chunk = row_cw >> 3
    low_cw = (row_cw & (r8 - 1)) < LRU_SEQS

    def row_pattern(lo, hi):
        n = lo.shape[-1]
        return jnp.where(lax.broadcasted_iota(jnp.int32, (r8, n), 0) < LRU_SEQS, lo, hi)

    def tap_rows(s):
        zero = jnp.zeros((1, cw), F32)
        lo = cw_ref[pl.ds(s + CONV_LEFT, 1), :] if 0 <= s + CONV_LEFT < CONV_W else zero
        hi = cw_ref[pl.ds(CONV_LEFT - s, 1), :] if 0 <= CONV_LEFT - s < CONV_W else zero
        return 0.5 * row_pattern(lo, hi)

    def tap_table(s):
        tap = jnp.broadcast_to(tap_rows(s)[None], (tc, r8, cw)).reshape(rows, cw)
        if width != seq_len:
            tap = jnp.where((step_in_chunk + s >= 0) & (step_in_chunk + s < tc), tap, 0.0)
        return tap

    wm_scr[0] = tap_table(-2) + tap_table(2)
    for s in range(-1, 2):
        wm_scr[s + 2] = tap_table(s)

    ap8 = row_pattern(ap_ref[pl.ds(0, 1), :], ap_ref[pl.ds(1, 1), :])
    c1_lo = (-0.5 * LRU_C / np.log(2.0)) * _softplus(-ap8)
    cb_half = 0.5 * cb_ref[...]

    def half_bias(j):
        cols = slice(j * LRU_BLOCK, (j + 1) * LRU_BLOCK)
        return 0.5 * jnp.concatenate(
            [row_pattern(ba_ref[pl.ds(0, 1), cols], ba_ref[pl.ds(1, 1), cols]),
             row_pattern(bx_ref[pl.ds(0, 1), cols], bx_ref[pl.ds(1, 1), cols])], axis=1)

    bias_lo = [half_bias(j) for j in range(nblk)]
    low_rows = (lax.broadcasted_iota(jnp.int32, (rows, LRU_BLOCK), 0) & (r8 - 1)) < LRU_SEQS
    high_rows = jnp.logical_not(low_rows)

    def conv_half(t0):
        def shifted(s):
            return xp_scr[pl.ds(pl.multiple_of((t0 + s + 2) * r8, r8), rows), :]

        xh = jnp.where(low_cw, shifted(-2), shifted(2)) * wm_scr[0] + cb_half
        for s in range(-1, 2):
            xh = xh + shifted(s) * wm_scr[s + 2]
        xp_scr[pl.ds(pl.multiple_of(t0 * r8, rows), rows), :] = xh
        return xh

    def gates(xh, sweep2, a_scr, b_scr):
        fj = high_rows if sweep2 else low_rows
        c1 = roll4(c1_lo) if sweep2 else c1_lo
        for j in range(nblk):
            cols = slice(j * LRU_BLOCK, (j + 1) * LRU_BLOCK)
            xj = xh[:, cols]
            lhs = jnp.concatenate([jnp.where(fj, xj, 0.0), jnp.where(fj, 0.0, xj)], axis=1)
            pre = jnp.dot(lhs.astype(BF16), w_ref[j], preferred_element_type=F32)
            bj = roll4(bias_lo[j]) if sweep2 else bias_lo[j]
            pre = pre.reshape(tc, r8, 2 * LRU_BLOCK) + bj[None]
            tr = jnp.tanh(pre[:, :, :LRU_BLOCK])
            tg = jnp.tanh(pre[:, :, LRU_BLOCK:])
            c1j = c1[:, cols][None]
            a = jnp.exp2(c1j * tr + c1j)
            bco = _sqrt_unit(1.0 - a * a) * ((tg + 1.0) * xj.reshape(tc, r8, LRU_BLOCK))
            a_scr[:, cols] = a.reshape(rows, LRU_BLOCK)
            b_scr[:, cols] = bco.reshape(rows, LRU_BLOCK)

    bufs = ((a0_scr, b0_scr), (a1_scr, b1_scr))

    def chunk_rows(t0):
        return pl.ds(pl.multiple_of(t0 * r8, rows), rows)

    def scan(buf, h, dst_ref, t0, descending):
        a_ref, b_ref = bufs[buf]
        for i in range(tc):
            t = tc - 1 - i if descending else i
            h = a_ref[t * r8:(t + 1) * r8, :] * h + b_ref[t * r8:(t + 1) * r8, :]
            if dst_ref is None:
                b_ref[t * r8:(t + 1) * r8, :] = h
            else:
                dst_ref[pl.ds(pl.multiple_of(t0 * r8, rows) + t * r8, r8), :] = h
        return h

    def gates1(ci, buf):
        gates(conv_half(ci * tc), False, *bufs[buf])

    def scan1(ci, buf, h):
        return scan(buf, h, s1_scr, ci * tc, False)

    npairs = nchunks // 2
    assert nchunks == 2 * npairs
    gates1(0, 0)

    def sweep1_pair(i, h):
        gates1(2 * i + 1, 1)
        h = scan1(2 * i, 0, h)
        gates1(2 * i + 2, 0)
        return scan1(2 * i + 1, 1, h)

    h = lax.fori_loop(0, npairs - 1, sweep1_pair, h0_ref[0])
    gates1(nchunks - 1, 1)
    h = scan1(nchunks - 2, 0, h)
    h = scan1(nchunks - 1, 1, h)

    def gates2(p, buf):
        gates(xp_scr[chunk_rows((nchunks - 1 - p) * tc), :], True, *bufs[buf])

    def scan2(p, buf, h):
        return scan(buf, h, None, 0, True)

    def finish(p, buf):
        t0 = (nchunks - 1 - p) * tc
        crows = chunk_rows(t0)
        o = (s1_scr[crows, :] + bufs[buf][1][...]).astype(BF16) * sg_ref[0, crows, :]
        nat = jnp.dot(pt_ref[...], o, preferred_element_type=F32).astype(out_ref.dtype)
        lo = pl.multiple_of(t0, tc)
        hi = pl.multiple_of(seq_len - tc - t0, tc)
        for b in range(LRU_SEQS):
            out_ref[0, b, pl.ds(lo, tc), :] = nat[b * tc:(b + 1) * tc]
            out_ref[0, b, pl.ds(hi, tc), :] = nat[(LRU_SEQS + b) * tc:(LRU_SEQS + b + 1) * tc]

    h = roll4(h)
    gates2(0, 0)
    gates2(1, 1)
    h = scan2(0, 0, h)

    def sweep2_pair(i, h):
        finish(2 * i, 0)
        gates2(2 * i + 2, 0)
        h = scan2(2 * i + 1, 1, h)
        finish(2 * i + 1, 1)
        gates2(2 * i + 3, 1)
        return scan2(2 * i + 2, 0, h)

    h = lax.fori_loop(0, npairs - 1, sweep2_pair, h)
    finish(nchunks - 2, 0)
    h = scan2(nchunks - 1, 1, h)
    finish(nchunks - 1, 1)
    fin_ref[0] = h


def _lru(xs, gs, conv_w, conv_b, w_blk, ba, bx, a_param, h0, perm_t, seq_len, width, nblk):
    groups = xs.shape[0]
    r8 = LRU_ROWS
    slabs = seq_len // 2
    cw = nblk * LRU_BLOCK
    per_dir = pl.BlockSpec((2, cw), lambda g, c: (0, c))
    return pl.pallas_call(
        functools.partial(_lru_kernel, slabs=slabs, seq_len=seq_len, width=width, nblk=nblk),
        grid=(groups, N_LRU_BLOCKS // nblk),
        in_specs=[pl.BlockSpec((1, slabs * r8, cw), lambda g, c: (g, 0, c)),
                  pl.BlockSpec((1, slabs * r8, cw), lambda g, c: (g, 0, c)),
                  pl.BlockSpec((CONV_W, cw), lambda g, c: (0, c)),
                  pl.BlockSpec((1, cw), lambda g, c: (0, c)),
                  pl.BlockSpec((nblk, 2 * LRU_BLOCK, 2 * LRU_BLOCK), lambda g, c: (c, 0, 0)),
                  per_dir, per_dir, per_dir,
                  pl.BlockSpec((1, r8, cw), lambda g, c: (g, 0, c)),
                  pl.BlockSpec((PERM_ROWS, PERM_ROWS), lambda g, c: (0, 0))],
        out_specs=[pl.BlockSpec((1, LRU_SEQS, seq_len, cw), lambda g, c: (g, 0, 0, c)),
                   pl.BlockSpec((1, r8, cw), lambda g, c: (g, 0, c))],
        out_shape=[jax.ShapeDtypeStruct((groups, LRU_SEQS, seq_len, D_LRU), BF16),
                   jax.ShapeDtypeStruct((groups, r8, D_LRU), F32)],
        scratch_shapes=[pltpu.VMEM(((slabs + 4) * r8, cw), F32),
                        pltpu.VMEM((slabs * r8, cw), F32),
                        pltpu.VMEM((4, PERM_ROWS, cw), F32),
                        pltpu.VMEM((PERM_ROWS, cw), F32),
                        pltpu.VMEM((PERM_ROWS, cw), F32),
                        pltpu.VMEM((PERM_ROWS, cw), F32),
                        pltpu.VMEM((PERM_ROWS, cw), F32)],
        compiler_params=pltpu.CompilerParams(
            dimension_semantics=("parallel", "parallel"), vmem_limit_bytes=VMEM_LIMIT),
        name="lru",
    )(xs, gs, conv_w, conv_b.reshape(1, -1), w_blk, ba, bx, a_param, h0, perm_t)


def _lru_gate_weights(wa, wx):
    return jnp.concatenate([jnp.concatenate([wa[0], wx[0]], axis=2),
                            jnp.concatenate([wa[1], wx[1]], axis=2)], axis=1).astype(BF16)


def _tail_kernel(x_ref, o_ref, l_ref, mr_ref, ml_ref, gate_ref, fnw_ref, wrd_ref, wld_ref, wo_ref, y_ref):
    ret_out = jnp.dot(o_ref[...], wrd_ref[...], preferred_element_type=F32)
    lru_out = jnp.dot(l_ref[...], wld_ref[...], preferred_element_type=F32)
    merged = _sigmoid(mr_ref[...].astype(F32)) * ret_out + _sigmoid(ml_ref[...].astype(F32)) * lru_out
    out = jnp.dot(merged.astype(BF16), wo_ref[...], preferred_element_type=F32)
    y = x_ref[...] + gate_ref[0] * out
    ms = jnp.mean(y * y, axis=-1, keepdims=True)
    y_ref[...] = y * lax.rsqrt(ms + EPS) * fnw_ref[...]


TAIL_TM = 512


def _tail(x2d, o2d, l2d, z2d, mod3, cond_row, fnw, wrd_b, wld_b, wo_b):
    m = x2d.shape[0]
    tm = TAIL_TM
    kmr, kml = MAIN_MRET // D_MODEL, MAIN_MLRU // D_MODEL
    const = lambda i: (0, 0)
    return pl.pallas_call(
        _tail_kernel,
        grid=(m // tm,),
        in_specs=[pl.BlockSpec((tm, D_MODEL), lambda i: (i, 0)),
                  pl.BlockSpec((tm, D_V), lambda i: (i, 0)),
                  pl.BlockSpec((tm, D_LRU), lambda i: (i, 0)),
                  pl.BlockSpec((tm, D_MODEL), lambda i: (i, kmr)),
                  pl.BlockSpec((tm, D_MODEL), lambda i: (i, kml)),
                  pl.BlockSpec((1, 1, D_MODEL), lambda i: (cond_row(i), 0, MOD_GATE)),
                  pl.BlockSpec((1, D_MODEL), const),
                  pl.BlockSpec((D_V, D_MODEL), const),
                  pl.BlockSpec((D_LRU, D_MODEL), const),
                  pl.BlockSpec((D_MODEL, D_MODEL), const)],
        out_specs=pl.BlockSpec((tm, D_MODEL), lambda i: (i, 0)),
        out_shape=jax.ShapeDtypeStruct((m, D_MODEL), F32),
        compiler_params=pltpu.CompilerParams(
            dimension_semantics=("parallel",), vmem_limit_bytes=VMEM_LIMIT),
        name="tail",
    )(x2d, o2d, l2d, z2d, z2d, mod3, fnw.reshape(1, -1), wrd_b, wld_b, wo_b)


def _trunk(x, mod3, cond_row0, n_cond, s0_ret, h0_lru, width, lru_nblk, params, final_norm_w, emit_state):
    (norm_w, w_all, decay_logit, wrd_b, conv_w, conv_b, w_blk, ba, bx, a_param, wld_b, wo_b, perm, perm_t) = params
    b, l, _ = x.shape
    groups = b // LRU_SEQS
    x2d = x.reshape(b * l, D_MODEL)
    tokens_per_cond = b * l // n_cond

    def cond_row(tile_tokens):
        return lambda i: cond_row0 + i // (tokens_per_cond // tile_tokens)

    z2d = _inproj(x2d, norm_w, mod3, cond_row(INPROJ_TM), w_all)
    z3 = z2d.reshape(b, l, D_MAIN)
    ret = _retention(z3, decay_logit, s0_ret, emit_state, N_HEADS if l == RET_CHUNK else 1)

    xs, gs = _inproj_lru(x.reshape(groups, LRU_SEQS, l, D_MODEL), norm_w, mod3, cond_row0, n_cond, perm, w_all)
    if h0_lru is None:
        h0 = jnp.zeros((groups, LRU_ROWS, D_LRU), F32)
    else:
        h0 = h0_lru.reshape(groups, LRU_SEQS, 2, D_LRU).transpose(0, 2, 1, 3).reshape(groups, LRU_ROWS, D_LRU)
    lru_pre, fin = _lru(xs, gs, conv_w, conv_b, w_blk, ba, bx, a_param, h0, perm_t, l, width, lru_nblk)
    fin = fin.reshape(groups, 2, LRU_SEQS, D_LRU)
    lru_fin = jnp.stack([fin[:, 1], fin[:, 0]], axis=2).reshape(b, 2, D_LRU)

    y = _tail(x2d, ret[0].reshape(b * l, D_V), lru_pre.reshape(b * l, D_LRU), z2d, mod3, cond_row(TAIL_TM),
              final_norm_w, wrd_b, wld_b, wo_b)
    return y.reshape(b, l, D_MODEL), (ret[1] if emit_state else None), lru_fin


def kernel(x_prompt, x_sample, state_ret, state_lru, c, c_ctx, norm_w, w_ada, b_ada, w_in, ret_decay_logit,
           ret_gn_w, w_ret_down, conv_w, conv_b, lru_wa, lru_ba, lru_wx, lru_bx, lru_a_param, w_lru_down,
           w_out, final_norm_w):
    assert norm_w.shape[0] == 1, "single-layer step"
    n_dec = c.shape[0]
    cond8 = jnp.concatenate([c.astype(F32), c_ctx.astype(F32)[None],
                             jnp.zeros((8 - n_dec - 1, D_MODEL), F32)], axis=0)
    mod = _ada(cond8, w_ada[0], b_ada[0])
    col = jnp.arange(D_IN)
    col_scale = jnp.where((col >= OFF_GRET) & (col < OFF_XLRU), 0.5, 1.0).astype(F32)
    w_all = (w_in[0] * col_scale[None, :]).astype(BF16)
    wrd_b = (ret_gn_w[0][:, None] * w_ret_down[0]).astype(BF16)
    perm = jnp.asarray(_slab_permutation(PERM_UNIT), BF16)
    perm_t = jnp.asarray(_slab_permutation(LRU_TC).T, BF16)
    params = (norm_w[0], w_all, ret_decay_logit[0], wrd_b, conv_w[0], conv_b[0],
              _lru_gate_weights(lru_wa[0], lru_wx[0]), lru_ba[0], lru_bx[0], lru_a_param[0],
              w_lru_down[0].astype(BF16), w_out[0].astype(BF16), perm, perm_t)
    mod3 = mod.reshape(mod.shape[0], 1, 3 * D_MODEL)
    y_prompt, new_ret, new_lru = _trunk(x_prompt.astype(F32), mod3, n_dec, 1, None, None,
                                        x_prompt.shape[1], 5, params, final_norm_w, True)
    y_sample, _, _ = _trunk(x_sample.astype(F32), mod3, 0, n_dec, state_ret[:, 0], state_lru[:, 0],
                            GRID_W, 2, params, final_norm_w, False)
    return (y_prompt.astype(x_prompt.dtype), y_sample.astype(x_sample.dtype),
            new_ret[:, None].astype(state_ret.dtype), new_lru[:, None].astype(state_lru.dtype))
```

```python
import functools

import jax
import jax.numpy as jnp
import numpy as np
from jax import lax
from jax.experimental import pallas as pl
from jax.experimental.pallas import tpu as pltpu

F32 = jnp.float32
BF16 = jnp.bfloat16

D_MODEL = 1024
N_HEADS = 4
DK = 256
DV = 512
D_QK = N_HEADS * DK
D_V = N_HEADS * DV
N_LRU_BLOCKS = 10
LRU_BLOCK = 128
D_LRU = N_LRU_BLOCKS * LRU_BLOCK
LRU_C = 8.0
CONV_W = 4
CONV_LEFT = 2
GRID_W = 64
EPS = 1e-6

OFF_Q = 0
OFF_K = OFF_Q + D_QK
OFF_V = OFF_K + D_QK
OFF_GRET = OFF_V + D_V
OFF_XLRU = OFF_GRET + D_V
OFF_GLRU = OFF_XLRU + D_LRU
OFF_MRET = OFF_GLRU + D_LRU
OFF_MLRU = OFF_MRET + D_MODEL
D_IN = OFF_MLRU + D_MODEL
D_MAIN = OFF_XLRU + 2 * D_MODEL
MAIN_MRET = OFF_XLRU
MAIN_MLRU = OFF_XLRU + D_MODEL

RET_CHUNK = 256
LRU_SEQS = 4
LRU_ROWS = 2 * LRU_SEQS
LRU_TC = 64
PERM_ROWS = LRU_TC * LRU_ROWS
PERM_UNIT = 32
INPROJ_LRU_SLABS = 128
VMEM_LIMIT = 56 * 1024 * 1024


def _sigmoid(x):
    return 0.5 * jnp.tanh(0.5 * x) + 0.5


def _silu(x):
    return x * _sigmoid(x)


def _softplus(x):
    return jnp.maximum(x, 0.0) + jnp.log1p(jnp.exp(-jnp.abs(x)))


def _slab_permutation(tc):
    n = tc * LRU_ROWS
    p = np.zeros((n, n), np.float32)
    for t in range(tc):
        for r in range(LRU_ROWS):
            if r < LRU_SEQS:
                src = r * tc + t
            else:
                src = LRU_SEQS * tc + (r - LRU_SEQS) * tc + (tc - 1 - t)
            p[t * LRU_ROWS + r, src] = 1.0
    return p


def _ada_kernel(c_ref, w_ref, b_ref, o_ref):
    cond = _silu(c_ref[...]).astype(BF16)
    o_ref[...] = jnp.dot(cond, w_ref[...].astype(BF16), preferred_element_type=F32) + b_ref[...]


def _ada(cond8, w_ada, b_ada):
    tn = 768
    return pl.pallas_call(
        _ada_kernel,
        grid=(3 * D_MODEL // tn,),
        in_specs=[pl.BlockSpec((8, D_MODEL), lambda j: (0, 0)),
                  pl.BlockSpec((D_MODEL, tn), lambda j: (0, j)),
                  pl.BlockSpec((1, tn), lambda j: (0, j))],
        out_specs=pl.BlockSpec((8, tn), lambda j: (0, j)),
        out_shape=jax.ShapeDtypeStruct((8, 3 * D_MODEL), F32),
        name="ada",
    )(cond8, w_ada, b_ada.reshape(1, -1))


def _modulated_norm(x, nw, scale, shift):
    ms = jnp.mean(x * x, axis=-1, keepdims=True)
    return (x * lax.rsqrt(ms + EPS)) * (nw * (1.0 + scale)) + shift


INPROJ_TM = 1024
INPROJ_TN = 2048


def _inproj_kernel(x_ref, nw_ref, sc_ref, sh_ref, w_ref, z_ref, h_scr):
    @pl.when(pl.program_id(1) == 0)
    def _():
        h_scr[...] = _modulated_norm(x_ref[...], nw_ref[...], sc_ref[0], sh_ref[0]).astype(BF16)

    z_ref[...] = jnp.dot(h_scr[...], w_ref[...], preferred_element_type=F32).astype(z_ref.dtype)


MOD_SHIFT, MOD_SCALE, MOD_GATE = 0, 1, 2


def _inproj(x2d, norm_w, mod3, cond_row, w_all):
    m = x2d.shape[0]
    tm, tn = INPROJ_TM, INPROJ_TN
    n_lead = OFF_XLRU // tn
    n_tiles = D_MAIN // tn
    assert OFF_XLRU == n_lead * tn and D_MAIN == n_tiles * tn

    def w_col(j):
        off = j * tn
        for t in range(n_lead, n_tiles):
            off = jnp.where(j == t, OFF_MRET + (t - n_lead) * tn, off)
        return off

    return pl.pallas_call(
        _inproj_kernel,
        grid=(m // tm, D_MAIN // tn),
        in_specs=[pl.BlockSpec((tm, D_MODEL), lambda i, j: (i, 0)),
                  pl.BlockSpec((1, D_MODEL), lambda i, j: (0, 0)),
                  pl.BlockSpec((1, 1, D_MODEL), lambda i, j: (cond_row(i), 0, MOD_SCALE)),
                  pl.BlockSpec((1, 1, D_MODEL), lambda i, j: (cond_row(i), 0, MOD_SHIFT)),
                  pl.BlockSpec((pl.Element(D_MODEL), pl.Element(tn)), lambda i, j: (0, w_col(j)))],
        out_specs=pl.BlockSpec((tm, tn), lambda i, j: (i, j)),
        out_shape=jax.ShapeDtypeStruct((m, D_MAIN), BF16),
        scratch_shapes=[pltpu.VMEM((tm, D_MODEL), BF16)],
        compiler_params=pltpu.CompilerParams(
            dimension_semantics=("parallel", "arbitrary"), vmem_limit_bytes=VMEM_LIMIT),
        name="inproj",
    )(x2d, norm_w.reshape(1, -1), mod3, mod3, w_all)


def _inproj_lru_kernel(xa_ref, xb_ref, nw_ref, sc_ref, sh_ref, p_ref, w_ref, xs_ref, gs_ref):
    nw = nw_ref[...]
    sc = sc_ref[...]
    sh = sh_ref[...]
    ha = _modulated_norm(xa_ref[0], nw, sc, sh).astype(BF16)
    hb = _modulated_norm(xb_ref[0], nw, sc, sh).astype(BF16)
    units = INPROJ_LRU_SLABS // PERM_UNIT
    pieces = []
    for u in range(units):
        lo, hi = u * PERM_UNIT, (units - 1 - u) * PERM_UNIT
        src = jnp.concatenate([ha[b, lo:lo + PERM_UNIT] for b in range(LRU_SEQS)]
                              + [hb[b, hi:hi + PERM_UNIT] for b in range(LRU_SEQS)], axis=0)
        pieces.append(jnp.dot(p_ref[...], src, preferred_element_type=F32).astype(BF16))
    hp = jnp.concatenate(pieces, axis=0)
    z = jnp.dot(hp, w_ref[...], preferred_element_type=F32)
    xs_ref[0] = z[:, :D_LRU].astype(xs_ref.dtype)
    gs_ref[0] = _silu(z[:, D_LRU:]).astype(gs_ref.dtype)


def _inproj_lru(x4, norm_w, mod3, row0, nrows, perm, w_all):
    groups, _, l, _ = x4.shape
    assert row0 % nrows == 0 and nrows in (1, LRU_SEQS)
    ts = INPROJ_LRU_SLABS
    nt = l // 2 // ts
    ntb = l // ts
    out_sds = jax.ShapeDtypeStruct((groups, l // 2 * LRU_ROWS, D_LRU), BF16)
    return pl.pallas_call(
        _inproj_lru_kernel,
        grid=(groups, nt),
        in_specs=[pl.BlockSpec((1, LRU_SEQS, ts, D_MODEL), lambda g, i: (g, 0, i, 0)),
                  pl.BlockSpec((1, LRU_SEQS, ts, D_MODEL), lambda g, i: (g, 0, ntb - 1 - i, 0)),
                  pl.BlockSpec((1, D_MODEL), lambda g, i: (0, 0)),
                  pl.BlockSpec((nrows, 1, D_MODEL), lambda g, i: (row0 // nrows, 0, MOD_SCALE)),
                  pl.BlockSpec((nrows, 1, D_MODEL), lambda g, i: (row0 // nrows, 0, MOD_SHIFT)),
                  pl.BlockSpec((PERM_UNIT * LRU_ROWS, PERM_UNIT * LRU_ROWS), lambda g, i: (0, 0)),
                  pl.BlockSpec((pl.Element(D_MODEL), pl.Element(2 * D_LRU)), lambda g, i: (0, OFF_XLRU))],
        out_specs=[pl.BlockSpec((1, ts * LRU_ROWS, D_LRU), lambda g, i: (g, i, 0)),
                   pl.BlockSpec((1, ts * LRU_ROWS, D_LRU), lambda g, i: (g, i, 0))],
        out_shape=[out_sds, out_sds],
        compiler_params=pltpu.CompilerParams(
            dimension_semantics=("parallel", "parallel"), vmem_limit_bytes=VMEM_LIMIT),
        name="inproj_lru",
    )(x4, x4, norm_w.reshape(1, -1), mod3, mod3, perm, w_all)


def _dot_tn(a, b):
    return lax.dot_general(a, b, (((0,), (0,)), ((), ())), preferred_element_type=F32)


def _dot_nt(a, b):
    return lax.dot_general(a, b, (((1,), (1,)), ((), ())), preferred_element_type=F32)


def _ret_kernel(*refs, nc, hp, has_state, emit_state):
    dl_ref, q_ref, k_ref, v_ref, hg_ref = refs[:5]
    pos = 5
    s0_ref = None
    if has_state:
        s0_ref = refs[pos]
        pos += 1
    o_ref = refs[pos]
    pos += 1
    sfin_ref = None
    if emit_state:
        sfin_ref = refs[pos]
        pos += 1
    sf_scr, sb_scr, sbh_scr = refs[pos:pos + 3]

    c = RET_CHUNK
    carry_states = has_state or nc > 1
    ii = lax.broadcasted_iota(jnp.int32, (c, c), 0)
    jj = lax.broadcasted_iota(jnp.int32, (c, c), 1)
    diff = (ii - jj).astype(F32)
    p = lax.broadcasted_iota(jnp.int32, (c, 1), 0).astype(F32)
    kscale = DK ** -0.5

    def rows_of(n):
        return pl.ds(pl.multiple_of(n * c, c), c)

    def normed_out(o, rows, vcols):
        ms = jnp.mean(o * o, axis=-1, keepdims=True)
        on = o * lax.rsqrt(ms + EPS)
        hg = hg_ref[0, rows, vcols]
        gate = hg * (jnp.tanh(hg) + 1.0)
        o_ref[0, rows, vcols] = on.astype(o_ref.dtype) * gate

    for hh in range(hp):
        head = pl.program_id(1) * hp + hh
        qcols = slice(hh * DK, (hh + 1) * DK)
        vcols = slice(hh * DV, (hh + 1) * DV)
        lgf = -_softplus(-jnp.full((1, 1), dl_ref[0, head], F32))
        lgb = -_softplus(-jnp.full((1, 1), dl_ref[1, head], F32))
        decay = jnp.exp(jnp.where(diff >= 0, lgf * diff, -lgb * diff)) * kscale
        kdf = jnp.exp(lgf * (c - 1.0 - p)) * kscale
        kdb = jnp.exp(lgb * p) * kscale

        if not carry_states:
            rows = pl.ds(0, c)
            qn = q_ref[0, rows, qcols]
            kn = k_ref[0, rows, qcols]
            vn = v_ref[0, rows, vcols]
            s = (_dot_nt(qn, kn) * decay).astype(BF16)
            normed_out(jnp.dot(s, vn, preferred_element_type=F32), rows, vcols)
            k32 = kn.astype(F32)
            if emit_state:
                sfin_ref[0, 0, hh] = _dot_tn((k32 * kdf).astype(BF16), vn)
                sfin_ref[0, 1, hh] = _dot_tn((k32 * kdb).astype(BF16), vn)
            continue

        def row_table(col):
            return jnp.broadcast_to(col, (c, DK)).astype(BF16)

        qdf_t = row_table(jnp.exp(lgf * (p + 1.0)))
        qdb_t = row_table(jnp.exp(lgb * (c - p)))
        kdf_t = row_table(kdf)
        kdb_t = row_table(kdb)
        cdf = jnp.exp(lgf * c)
        cdb = jnp.exp(lgb * c)
        if has_state:
            sf_scr[...] = s0_ref[0, 0, hh]
            sb_scr[...] = s0_ref[0, 1, hh]
        else:
            sf_scr[...] = jnp.zeros_like(sf_scr)
            sb_scr[...] = jnp.zeros_like(sb_scr)

        def rev_body(idx, carry):
            n = nc - 1 - idx
            rows = rows_of(n)
            sbh_scr[n] = sb_scr[...].astype(BF16)
            kb = k_ref[0, rows, qcols] * kdb_t
            sb_scr[...] = cdb * sb_scr[...] + _dot_tn(kb, v_ref[0, rows, vcols])
            return carry

        lax.fori_loop(0, nc, rev_body, 0, unroll=True)

        def fwd_body(n, carry):
            rows = rows_of(n)
            qn = q_ref[0, rows, qcols]
            kn = k_ref[0, rows, qcols]
            vn = v_ref[0, rows, vcols]
            s = (_dot_nt(qn, kn) * decay).astype(BF16)
            o = (jnp.dot(s, vn, preferred_element_type=F32)
                 + jnp.dot(qn * qdf_t, sf_scr[...].astype(BF16), preferred_element_type=F32)
                 + jnp.dot(qn * qdb_t, sbh_scr[n], preferred_element_type=F32))
            normed_out(o, rows, vcols)
            sf_scr[...] = cdf * sf_scr[...] + _dot_tn(kn * kdf_t, vn)
            return carry

        lax.fori_loop(0, nc, fwd_body, 0, unroll=True)

        if emit_state:
            sfin_ref[0, 0, hh] = sf_scr[...]
            sfin_ref[0, 1, hh] = sb_scr[...]


def _retention(z3, decay_logit, s0, emit_state, hp):
    b, l, _ = z3.shape
    nc = l // RET_CHUNK
    has_state = s0 is not None
    kq, kv_ = OFF_K // (hp * DK), OFF_V // (hp * DV)
    kg = OFF_GRET // (hp * DV)
    in_specs = [pl.BlockSpec(memory_space=pltpu.SMEM),
                pl.BlockSpec((1, l, hp * DK), lambda i, h: (i, 0, h)),
                pl.BlockSpec((1, l, hp * DK), lambda i, h: (i, 0, kq + h)),
                pl.BlockSpec((1, l, hp * DV), lambda i, h: (i, 0, kv_ + h)),
                pl.BlockSpec((1, l, hp * DV), lambda i, h: (i, 0, kg + h))]
    args = [decay_logit, z3, z3, z3, z3]
    if has_state:
        in_specs.append(pl.BlockSpec((1, 2, hp, DK, DV), lambda i, h: (i, 0, h, 0, 0)))
        args.append(s0)
    out_specs = [pl.BlockSpec((1, l, hp * DV), lambda i, h: (i, 0, h))]
    out_shape = [jax.ShapeDtypeStruct((b, l, D_V), BF16)]
    if emit_state:
        out_specs.append(pl.BlockSpec((1, 2, hp, DK, DV), lambda i, h: (i, 0, h, 0, 0)))
        out_shape.append(jax.ShapeDtypeStruct((b, 2, N_HEADS, DK, DV), F32))
    return pl.pallas_call(
        functools.partial(_ret_kernel, nc=nc, hp=hp, has_state=has_state, emit_state=emit_state),
        grid=(b, N_HEADS // hp),
        in_specs=in_specs,
        out_specs=out_specs,
        out_shape=out_shape,
        scratch_shapes=[pltpu.VMEM((DK, DV), F32), pltpu.VMEM((DK, DV), F32), pltpu.VMEM((nc, DK, DV), BF16)],
        compiler_params=pltpu.CompilerParams(
            dimension_semantics=("parallel", "parallel"), vmem_limit_bytes=VMEM_LIMIT),
        name="retention",
    )(*args)


def _sqrt_unit(x):
    return x * lax.rsqrt(jnp.maximum(x, 1e-30))


def _lru_kernel(xs_ref, sg_ref, cw_ref, cb_ref, w_ref, ba_ref, bx_ref, ap_ref, h0_ref, pt_ref, out_ref, fin_ref,
                xp_scr, s1_scr, wm_scr, a0_scr, b0_scr, a1_scr, b1_scr, *, slabs, seq_len, width, nblk):
    r8 = LRU_ROWS
    tc = LRU_TC
    rows = PERM_ROWS
    cw = nblk * LRU_BLOCK
    nchunks = slabs // tc
    assert width == seq_len or width == tc

    def roll4(v):
        return pltpu.roll(v, LRU_SEQS, axis=0)

    xp_scr[pl.ds(0, 2 * r8), :] = jnp.zeros((2 * r8, cw), F32)
    xp_scr[pl.ds(2 * r8, slabs * r8), :] = xs_ref[0].astype(F32)
    xp_scr[pl.ds((slabs + 2) * r8, r8), :] = roll4(xs_ref[0, pl.ds((slabs - 1) * r8, r8), :].astype(F32))
    xp_scr[pl.ds((slabs + 3) * r8, r8), :] = roll4(xs_ref[0, pl.ds((slabs - 2) * r8, r8), :].astype(F32))

    row_cw = lax.broadcasted_iota(jnp.int32, (rows, cw), 0)
    step_in_chunk = row_cw >> 3
    low_cw = (row_cw & (r8 - 1)) < LRU_SEQS

    def row_pattern(lo, hi):
        n = lo.shape[-1]
        return jnp.where(lax.broadcasted_iota(jnp.int32, (r8, n), 0) < LRU_SEQS, lo, hi)

    def tap_rows(s):
        zero = jnp.zeros((1, cw), F32)
        lo = cw_ref[pl.ds(s + CONV_LEFT, 1), :] if 0 <= s + CONV_LEFT < CONV_W else zero
        hi = cw_ref[pl.ds(CONV_LEFT - s, 1), :] if 0 <= CONV_LEFT - s < CONV_W else zero
        return 0.5 * row_pattern(lo, hi)

    def tap_table(s):
        tap = jnp.broadcast_to(tap_rows(s)[None], (tc, r8, cw)).reshape(rows, cw)
        if width != seq_len:
            tap = jnp.where((step_in_chunk + s >= 0) & (step_in_chunk + s < tc), tap, 0.0)
        return tap

    wm_scr[0] = tap_table(-2) + tap_table(2)
    for s in range(-1, 2):
        wm_scr[s + 2] = tap_table(s)

    ap8 = row_pattern(ap_ref[pl.ds(0, 1), :], ap_ref[pl.ds(1, 1), :])
    c1_lo = (-0.5 * LRU_C / np.log(2.0)) * _softplus(-ap8)
    cb_half = 0.5 * cb_ref[...]

    def half_bias(j):
        cols = slice(j * LRU_BLOCK, (j + 1) * LRU_BLOCK)
        return 0.5 * jnp.concatenate(
            [row_pattern(ba_ref[pl.ds(0, 1), cols], ba_ref[pl.ds(1, 1), cols]),
             row_pattern(bx_ref[pl.ds(0, 1), cols], bx_ref[pl.ds(1, 1), cols])], axis=1)

    bias_lo = [half_bias(j) for j in range(nblk)]
    low_rows = (lax.broadcasted_iota(jnp.int32, (rows, LRU_BLOCK), 0) & (r8 - 1)) < LRU_SEQS
    high_rows = jnp.logical_not(low_rows)

    def conv_half(t0):
        def shifted(s):
            return xp_scr[pl.ds(pl.multiple_of((t0 + s + 2) * r8, r8), rows), :]

        xh = jnp.where(low_cw, shifted(-2), shifted(2)) * wm_scr[0] + cb_half
        for s in range(-1, 2):
            xh = xh + shifted(s) * wm_scr[s + 2]
        xp_scr[pl.ds(pl.multiple_of(t0 * r8, rows), rows), :] = xh
        return xh

    def gates(xh, sweep2, a_scr, b_scr):
        fj = high_rows if sweep2 else low_rows
        c1 = roll4(c1_lo) if sweep2 else c1_lo
        for j in range(nblk):
            cols = slice(j * LRU_BLOCK, (j + 1) * LRU_BLOCK)
            xj = xh[:, cols]
            lhs = jnp.concatenate([jnp.where(fj, xj, 0.0), jnp.where(fj, 0.0, xj)], axis=1)
            pre = jnp.dot(lhs.astype(BF16), w_ref[j], preferred_element_type=F32)
            bj = roll4(bias_lo[j]) if sweep2 else bias_lo[j]
            pre = pre.reshape(tc, r8, 2 * LRU_BLOCK) + bj[None]
            tr = jnp.tanh(pre[:, :, :LRU_BLOCK])
            tg = jnp.tanh(pre[:, :, LRU_BLOCK:])
            c1j = c1[:, cols][None]
            a = jnp.exp2(c1j * tr + c1j)
            bco = _sqrt_unit(1.0 - a * a) * ((tg + 1.0) * xj.reshape(tc, r8, LRU_BLOCK))
            a_scr[:, cols] = a.reshape(rows, LRU_BLOCK)
            b_scr[:, cols] = bco.reshape(rows, LRU_BLOCK)

    bufs = ((a0_scr, b0_scr), (a1_scr, b1_scr))

    def chunk_rows(t0):
        return pl.ds(pl.multiple_of(t0 * r8, rows), rows)

    def scan(buf, h, dst_ref, t0, descending):
        a_ref, b_ref = bufs[buf]
        for i in range(tc):
            t = tc - 1 - i if descending else i
            h = a_ref[t * r8:(t + 1) * r8, :] * h + b_ref[t * r8:(t + 1) * r8, :]
            if dst_ref is None:
                b_ref[t * r8:(t + 1) * r8, :] = h
            else:
                dst_ref[pl.ds(pl.multiple_of(t0 * r8, rows) + t * r8, r8), :] = h
        return h

    def gates1(ci, buf):
        gates(conv_half(ci * tc), False, *bufs[buf])

    def scan1(ci, buf, h):
        return scan(buf, h, s1_scr, ci * tc, False)

    npairs = nchunks // 2
    assert nchunks == 2 * npairs
    gates1(0, 0)

    def sweep1_pair(i, h):
        gates1(2 * i + 1, 1)
        h = scan1(2 * i, 0, h)
        gates1(2 * i + 2, 0)
        return scan1(2 * i + 1, 1, h)

    h = lax.fori_loop(0, npairs - 1, sweep1_pair, h0_ref[0])
    gates1(nchunks - 1, 1)
    h = scan1(nchunks - 2, 0, h)
    h = scan1(nchunks - 1, 1, h)

    def gates2(p, buf):
        gates(xp_scr[chunk_rows((nchunks - 1 - p) * tc), :], True, *bufs[buf])

    def scan2(p, buf, h):
        return scan(buf, h, None, 0, True)

    def finish(p, buf):
        t0 = (nchunks - 1 - p) * tc
        crows = chunk_rows(t0)
        o = (s1_scr[crows, :] + bufs[buf][1][...]).astype(BF16) * sg_ref[0, crows, :]
        nat = jnp.dot(pt_ref[...], o, preferred_element_type=F32).astype(out_ref.dtype)
        lo = pl.multiple_of(t0, tc)
        hi = pl.multiple_of(seq_len - tc - t0, tc)
        for b in range(LRU_SEQS):
            out_ref[0, b, pl.ds(lo, tc), :] = nat[b * tc:(b + 1) * tc]
            out_ref[0, b, pl.ds(hi, tc), :] = nat[(LRU_SEQS + b) * tc:(LRU_SEQS + b + 1) * tc]

    h = roll4(h)
    gates2(0, 0)
    gates2(1, 1)
    h = scan2(0, 0, h)

    def sweep2_pair(i, h):
        finish(2 * i, 0)
        gates2(2 * i + 2, 0)
        h = scan2(2 * i + 1, 1, h)
        finish(2 * i + 1, 1)
        gates2(2 * i + 3, 1)
        return scan2(2 * i + 2, 0, h)

    h = lax.fori_loop(0, npairs - 1, sweep2_pair, h)
    finish(nchunks - 2, 0)
    h = scan2(nchunks - 1, 1, h)
    finish(nchunks - 1, 1)
    fin_ref[0] = h


def _lru(xs, gs, conv_w, conv_b, w_blk, ba, bx, a_param, h0, perm_t, seq_len, width, nblk):
    groups = xs.shape[0]
    r8 = LRU_ROWS
    slabs = seq_len // 2
    cw = nblk * LRU_BLOCK
    per_dir = pl.BlockSpec((2, cw), lambda g, c: (0, c))
    return pl.pallas_call(
        functools.partial(_lru_kernel, slabs=slabs, seq_len=seq_len, width=width, nblk=nblk),
        grid=(groups, N_LRU_BLOCKS // nblk),
        in_specs=[pl.BlockSpec((1, slabs * r8, cw), lambda g, c: (g, 0, c)),
                  pl.BlockSpec((1, slabs * r8, cw), lambda g, c: (g, 0, c)),
                  pl.BlockSpec((CONV_W, cw), lambda g, c: (0, c)),
                  pl.BlockSpec((1, cw), lambda g, c: (0, c)),
                  pl.BlockSpec((nblk, 2 * LRU_BLOCK, 2 * LRU_BLOCK), lambda g, c: (c, 0, 0)),
                  per_dir, per_dir, per_dir,
                  pl.BlockSpec((1, r8, cw), lambda g, c: (g, 0, c)),
                  pl.BlockSpec((PERM_ROWS, PERM_ROWS), lambda g, c: (0, 0))],
        out_specs=[pl.BlockSpec((1, LRU_SEQS, seq_len, cw), lambda g, c: (g, 0, 0, c)),
                   pl.BlockSpec((1, r8, cw), lambda g, c: (g, 0, c))],
        out_shape=[jax.ShapeDtypeStruct((groups, LRU_SEQS, seq_len, D_LRU), BF16),
                   jax.ShapeDtypeStruct((groups, r8, D_LRU), F32)],
        scratch_shapes=[pltpu.VMEM(((slabs + 4) * r8, cw), F32),
                        pltpu.VMEM((slabs * r8, cw), F32),
                        pltpu.VMEM((4, PERM_ROWS, cw), F32),
                        pltpu.VMEM((PERM_ROWS, cw), F32),
                        pltpu.VMEM((PERM_ROWS, cw), F32),
                        pltpu.VMEM((PERM_ROWS, cw), F32),
                        pltpu.VMEM((PERM_ROWS, cw), F32)],
        compiler_params=pltpu.CompilerParams(
            dimension_semantics=("parallel", "parallel"), vmem_limit_bytes=VMEM_LIMIT),
        name="lru",
    )(xs, gs, conv_w, conv_b.reshape(1, -1), w_blk, ba, bx, a_param, h0, perm_t)


def _lru_gate_weights(wa, wx):
    return jnp.concatenate([jnp.concatenate([wa[0], wx[0]], axis=2),
                            jnp.concatenate([wa[1], wx[1]], axis=2)], axis=1).astype(BF16)


def _tail_kernel(x_ref, o_ref, l_ref, mr_ref, ml_ref, gate_ref, fnw_ref, wrd_ref, wld_ref, wo_ref, y_ref):
    ret_out = jnp.dot(o_ref[...], wrd_ref[...], preferred_element_type=F32)
    lru_out = jnp.dot(l_ref[...], wld_ref[...], preferred_element_type=F32)
    merged = _sigmoid(mr_ref[...].astype(F32)) * ret_out + _sigmoid(ml_ref[...].astype(F32)) * lru_out
    out = jnp.dot(merged.astype(BF16), wo_ref[...], preferred_element_type=F32)
    y = x_ref[...] + gate_ref[0] * out
    ms = jnp.mean(y * y, axis=-1, keepdims=True)
    y_ref[...] = y * lax.rsqrt(ms + EPS) * fnw_ref[...]


TAIL_TM = 512


def _tail(x2d, o2d, l2d, z2d, mod3, cond_row, fnw, wrd_b, wld_b, wo_b):
    m = x2d.shape[0]
    tm = TAIL_TM
    kmr, kml = MAIN_MRET // D_MODEL, MAIN_MLRU // D_MODEL
    const = lambda i: (0, 0)
    return pl.pallas_call(
        _tail_kernel,
        grid=(m // tm,),
        in_specs=[pl.BlockSpec((tm, D_MODEL), lambda i: (i, 0)),
                  pl.BlockSpec((tm, D_V), lambda i: (i, 0)),
                  pl.BlockSpec((tm, D_LRU), lambda i: (i, 0)),
                  pl.BlockSpec((tm, D_MODEL), lambda i: (i, kmr)),
                  pl.BlockSpec((tm, D_MODEL), lambda i: (i, kml)),
                  pl.BlockSpec((1, 1, D_MODEL), lambda i: (cond_row(i), 0, MOD_GATE)),
                  pl.BlockSpec((1, D_MODEL), const),
                  pl.BlockSpec((D_V, D_MODEL), const),
                  pl.BlockSpec((D_LRU, D_MODEL), const),
                  pl.BlockSpec((D_MODEL, D_MODEL), const)],
        out_specs=pl.BlockSpec((tm, D_MODEL), lambda i: (i, 0)),
        out_shape=jax.ShapeDtypeStruct((m, D_MODEL), F32),
        compiler_params=pltpu.CompilerParams(
            dimension_semantics=("parallel",), vmem_limit_bytes=VMEM_LIMIT),
        name="tail",
    )(x2d, o2d, l2d, z2d, z2d, mod3, fnw.reshape(1, -1), wrd_b, wld_b, wo_b)


def _trunk(x, mod3, cond_row0, n_cond, s0_ret, h0_lru, width, lru_nblk, params, final_norm_w, emit_state):
    (norm_w, w_all, decay_logit, wrd_b, conv_w, conv_b, w_blk, ba, bx, a_param, wld_b, wo_b, perm, perm_t) = params
    b, l, _ = x.shape
    groups = b // LRU_SEQS
    x2d = x.reshape(b * l, D_MODEL)
    tokens_per_cond = b * l // n_cond

    def cond_row(tile_tokens):
        return lambda i: cond_row0 + i // (tokens_per_cond // tile_tokens)

    z2d = _inproj(x2d, norm_w, mod3, cond_row(INPROJ_TM), w_all)
    z3 = z2d.reshape(b, l, D_MAIN)
    ret = _retention(z3, decay_logit, s0_ret, emit_state, N_HEADS if l == RET_CHUNK else 1)

    xs, gs = _inproj_lru(x.reshape(groups, LRU_SEQS, l, D_MODEL), norm_w, mod3, cond_row0, n_cond, perm, w_all)
    if h0_lru is None:
        h0 = jnp.zeros((groups, LRU_ROWS, D_LRU), F32)
    else:
        h0 = h0_lru.reshape(groups, LRU_SEQS, 2, D_LRU).transpose(0, 2, 1, 3).reshape(groups, LRU_ROWS, D_LRU)
    lru_pre, fin = _lru(xs, gs, conv_w, conv_b, w_blk, ba, bx, a_param, h0, perm_t, l, width, lru_nblk)
    fin = fin.reshape(groups, 2, LRU_SEQS, D_LRU)
    lru_fin = jnp.stack([fin[:, 1], fin[:, 0]], axis=2).reshape(b, 2, D_LRU)

    y = _tail(x2d, ret[0].reshape(b * l, D_V), lru_pre.reshape(b * l, D_LRU), z2d, mod3, cond_row(TAIL_TM),
              final_norm_w, wrd_b, wld_b, wo_b)
    return y.reshape(b, l, D_MODEL), (ret[1] if emit_state else None), lru_fin


def kernel(x_prompt, x_sample, state_ret, state_lru, c, c_ctx, norm_w, w_ada, b_ada, w_in, ret_decay_logit,
           ret_gn_w, w_ret_down, conv_w, conv_b, lru_wa, lru_ba, lru_wx, lru_bx, lru_a_param, w_lru_down,
           w_out, final_norm_w):
    assert norm_w.shape[0] == 1, "single-layer step"
    n_dec = c.shape[0]
    cond8 = jnp.concatenate([c.astype(F32), c_ctx.astype(F32)[None],
                             jnp.zeros((8 - n_dec - 1, D_MODEL), F32)], axis=0)
    mod = _ada(cond8, w_ada[0], b_ada[0])
    col = jnp.arange(D_IN)
    col_scale = jnp.where((col >= OFF_GRET) & (col < OFF_XLRU), 0.5, 1.0).astype(F32)
    w_all = (w_in[0] * col_scale[None, :]).astype(BF16)
    wrd_b = (ret_gn_w[0][:, None] * w_ret_down[0]).astype(BF16)
    perm = jnp.asarray(_slab_permutation(PERM_UNIT), BF16)
    perm_t = jnp.asarray(_slab_permutation(LRU_TC).T, BF16)
    params = (norm_w[0], w_all, ret_decay_logit[0], wrd_b, conv_w[0], conv_b[0],
              _lru_gate_weights(lru_wa[0], lru_wx[0]), lru_ba[0], lru_bx[0], lru_a_param[0],
              w_lru_down[0].astype(BF16), w_out[0].astype(BF16), perm, perm_t)
    mod3 = mod.reshape(mod.shape[0], 1, 3 * D_MODEL)
    y_prompt, new_ret, new_lru = _trunk(x_prompt.astype(F32), mod3, n_dec, 1, None, None,
                                        x_prompt.shape[1], 5, params, final_norm_w, True)
    y_sample, _, _ = _trunk(x_sample.astype(F32), mod3, 0, n_dec, state_ret[:, 0], state_lru[:, 0],
                            GRID_W, 2, params, final_norm_w, False)
    return (y_prompt.astype(x_prompt.dtype), y_sample.astype(x_sample.dtype),
            new_ret[:, None].astype(state_ret.dtype), new_lru[:, None].astype(state_lru.dtype))
```

```python
import functools

import jax
import jax.numpy as jnp
import numpy as np
from jax import lax
from jax.experimental import pallas as pl
from jax.experimental.pallas import tpu as pltpu

F32 = jnp.float32
BF16 = jnp.bfloat16

D_MODEL = 1024
N_HEADS = 4
DK = 256
DV = 512
D_QK = N_HEADS * DK
D_V = N_HEADS * DV
N_LRU_BLOCKS = 10
LRU_BLOCK = 128
D_LRU = N_LRU_BLOCKS * LRU_BLOCK
LRU_C = 8.0
CONV_W = 4
CONV_LEFT = 2
GRID_W = 64
EPS = 1e-6

OFF_Q = 0
OFF_K = OFF_Q + D_QK
OFF_V = OFF_K + D_QK
OFF_GRET = OFF_V + D_V
OFF_XLRU = OFF_GRET + D_V
OFF_GLRU = OFF_XLRU + D_LRU
OFF_MRET = OFF_GLRU + D_LRU
OFF_MLRU = OFF_MRET + D_MODEL
D_IN = OFF_MLRU + D_MODEL
D_MAIN = OFF_XLRU + 2 * D_MODEL
MAIN_MRET = OFF_XLRU
MAIN_MLRU = OFF_XLRU + D_MODEL

RET_CHUNK = 256
LRU_SEQS = 4
LRU_ROWS = 2 * LRU_SEQS
LRU_TC = 64
PERM_ROWS = LRU_TC * LRU_ROWS
PERM_UNIT = 32
INPROJ_LRU_SLABS = 128
VMEM_LIMIT = 56 * 1024 * 1024


def _sigmoid(x):
    return 0.5 * jnp.tanh(0.5 * x) + 0.5


def _silu(x):
    return x * _sigmoid(x)


def _softplus(x):
    return jnp.maximum(x, 0.0) + jnp.log1p(jnp.exp(-jnp.abs(x)))


def _slab_permutation(tc):
    n = tc * LRU_ROWS
    p = np.zeros((n, n), np.float32)
    for t in range(tc):
        for r in range(LRU_ROWS):
            if r < LRU_SEQS:
                src = r * tc + t
            else:
                src = LRU_SEQS * tc + (r - LRU_SEQS) * tc + (tc - 1 - t)
            p[t * LRU_ROWS + r, src] = 1.0
    return p


def _ada_kernel(c_ref, w_ref, b_ref, o_ref):
    cond = _silu(c_ref[...]).astype(BF16)
    mod = jnp.dot(cond, w_ref[...].astype(BF16), preferred_element_type=F32) + b_ref[...]
    o_ref[...] = mod[:, None, :]


def _ada(cond8, w_ada, b_ada):
    tn = 1536
    return pl.pallas_call(
        _ada_kernel,
        grid=(3 * D_MODEL // tn,),
        in_specs=[pl.BlockSpec((8, D_MODEL), lambda j: (0, 0)),
                  pl.BlockSpec((D_MODEL, tn), lambda j: (0, j)),
                  pl.BlockSpec((1, tn), lambda j: (0, j))],
        out_specs=pl.BlockSpec((8, 1, tn), lambda j: (0, 0, j)),
        out_shape=jax.ShapeDtypeStruct((8, 1, 3 * D_MODEL), F32),
        name="ada",
    )(cond8, w_ada, b_ada.reshape(1, -1))


def _modulated_norm(x, nw, scale, shift):
    ms = jnp.mean(x * x, axis=-1, keepdims=True)
    return (x * lax.rsqrt(ms + EPS)) * (nw * (1.0 + scale)) + shift


INPROJ_TM = 1024
INPROJ_TN = 2048


def _inproj_kernel(x_ref, nw_ref, sc_ref, sh_ref, w_ref, z_ref, h_scr):
    @pl.when(pl.program_id(1) == 0)
    def _():
        h_scr[...] = _modulated_norm(x_ref[...], nw_ref[...], sc_ref[0], sh_ref[0]).astype(BF16)

    z_ref[...] = jnp.dot(h_scr[...], w_ref[...], preferred_element_type=F32).astype(z_ref.dtype)


MOD_SHIFT, MOD_SCALE, MOD_GATE = 0, 1, 2


def _inproj(x2d, norm_w, mod3, cond_row, w_all):
    m = x2d.shape[0]
    tm, tn = INPROJ_TM, INPROJ_TN
    n_lead = OFF_XLRU // tn
    n_tiles = D_MAIN // tn
    assert OFF_XLRU == n_lead * tn and D_MAIN == n_tiles * tn

    def w_col(j):
        off = j * tn
        for t in range(n_lead, n_tiles):
            off = jnp.where(j == t, OFF_MRET + (t - n_lead) * tn, off)
        return off

    return pl.pallas_call(
        _inproj_kernel,
        grid=(m // tm, D_MAIN // tn),
        in_specs=[pl.BlockSpec((tm, D_MODEL), lambda i, j: (i, 0)),
                  pl.BlockSpec((1, D_MODEL), lambda i, j: (0, 0)),
                  pl.BlockSpec((1, 1, D_MODEL), lambda i, j: (cond_row(i), 0, MOD_SCALE)),
                  pl.BlockSpec((1, 1, D_MODEL), lambda i, j: (cond_row(i), 0, MOD_SHIFT)),
                  pl.BlockSpec((pl.Element(D_MODEL), pl.Element(tn)), lambda i, j: (0, w_col(j)))],
        out_specs=pl.BlockSpec((tm, tn), lambda i, j: (i, j)),
        out_shape=jax.ShapeDtypeStruct((m, D_MAIN), BF16),
        scratch_shapes=[pltpu.VMEM((tm, D_MODEL), BF16)],
        compiler_params=pltpu.CompilerParams(
            dimension_semantics=("parallel", "arbitrary"), vmem_limit_bytes=VMEM_LIMIT),
        name="inproj",
    )(x2d, norm_w.reshape(1, -1), mod3, mod3, w_all)


def _inproj_lru_kernel(xa_ref, xb_ref, nw_ref, sc_ref, sh_ref, p_ref, w_ref, xs_ref, gs_ref):
    nw = nw_ref[...]
    sc = sc_ref[...]
    sh = sh_ref[...]
    ha = _modulated_norm(xa_ref[0], nw, sc, sh).astype(BF16)
    hb = _modulated_norm(xb_ref[0], nw, sc, sh).astype(BF16)
    units = INPROJ_LRU_SLABS // PERM_UNIT
    pieces = []
    for u in range(units):
        lo, hi = u * PERM_UNIT, (units - 1 - u) * PERM_UNIT
        src = jnp.concatenate([ha[b, lo:lo + PERM_UNIT] for b in range(LRU_SEQS)]
                              + [hb[b, hi:hi + PERM_UNIT] for b in range(LRU_SEQS)], axis=0)
        pieces.append(jnp.dot(p_ref[...], src, preferred_element_type=F32).astype(BF16))
    hp = jnp.concatenate(pieces, axis=0)
    z = jnp.dot(hp, w_ref[...], preferred_element_type=F32)
    xs_ref[0] = z[:, :D_LRU].astype(xs_ref.dtype)
    gs_ref[0] = _silu(z[:, D_LRU:]).astype(gs_ref.dtype)


def _inproj_lru(x4, norm_w, mod3, row0, nrows, perm, w_all):
    groups, _, l, _ = x4.shape
    assert row0 % nrows == 0 and nrows in (1, LRU_SEQS)
    ts = INPROJ_LRU_SLABS
    nt = l // 2 // ts
    ntb = l // ts
    out_sds = jax.ShapeDtypeStruct((groups, l // 2 * LRU_ROWS, D_LRU), BF16)
    return pl.pallas_call(
        _inproj_lru_kernel,
        grid=(groups, nt),
        in_specs=[pl.BlockSpec((1, LRU_SEQS, ts, D_MODEL), lambda g, i: (g, 0, i, 0)),
                  pl.BlockSpec((1, LRU_SEQS, ts, D_MODEL), lambda g, i: (g, 0, ntb - 1 - i, 0)),
                  pl.BlockSpec((1, D_MODEL), lambda g, i: (0, 0)),
                  pl.BlockSpec((nrows, 1, D_MODEL), lambda g, i: (row0 // nrows, 0, MOD_SCALE)),
                  pl.BlockSpec((nrows, 1, D_MODEL), lambda g, i: (row0 // nrows, 0, MOD_SHIFT)),
                  pl.BlockSpec((PERM_UNIT * LRU_ROWS, PERM_UNIT * LRU_ROWS), lambda g, i: (0, 0)),
                  pl.BlockSpec((pl.Element(D_MODEL), pl.Element(2 * D_LRU)), lambda g, i: (0, OFF_XLRU))],
        out_specs=[pl.BlockSpec((1, ts * LRU_ROWS, D_LRU), lambda g, i: (g, i, 0)),
                   pl.BlockSpec((1, ts * LRU_ROWS, D_LRU), lambda g, i: (g, i, 0))],
        out_shape=[out_sds, out_sds],
        compiler_params=pltpu.CompilerParams(
            dimension_semantics=("parallel", "parallel"), vmem_limit_bytes=VMEM_LIMIT),
        name="inproj_lru",
    )(x4, x4, norm_w.reshape(1, -1), mod3, mod3, perm, w_all)


def _dot_tn(a, b):
    return lax.dot_general(a, b, (((0,), (0,)), ((), ())), preferred_element_type=F32)


def _dot_nt(a, b):
    return lax.dot_general(a, b, (((1,), (1,)), ((), ())), preferred_element_type=F32)


def _ret_kernel(*refs, nc, hp, has_state, emit_state):
    dl_ref, q_ref, k_ref, v_ref, hg_ref = refs[:5]
    pos = 5
    s0_ref = None
    if has_state:
        s0_ref = refs[pos]
        pos += 1
    o_ref = refs[pos]
    pos += 1
    sfin_ref = None
    if emit_state:
        sfin_ref = refs[pos]
        pos += 1
    sf_scr, sb_scr, sbh_scr = refs[pos:pos + 3]

    c = RET_CHUNK
    carry_states = has_state or nc > 1
    ii = lax.broadcasted_iota(jnp.int32, (c, c), 0)
    jj = lax.broadcasted_iota(jnp.int32, (c, c), 1)
    diff = (ii - jj).astype(F32)
    p = lax.broadcasted_iota(jnp.int32, (c, 1), 0).astype(F32)
    kscale = DK ** -0.5

    def rows_of(n):
        return pl.ds(pl.multiple_of(n * c, c), c)

    def normed_out(o, rows, vcols):
        ms = jnp.mean(o * o, axis=-1, keepdims=True)
        on = o * lax.rsqrt(ms + EPS)
        hg = hg_ref[0, rows, vcols]
        gate = hg * (jnp.tanh(hg) + 1.0)
        o_ref[0, rows, vcols] = on.astype(o_ref.dtype) * gate

    for hh in range(hp):
        head = pl.program_id(1) * hp + hh
        qcols = slice(hh * DK, (hh + 1) * DK)
        vcols = slice(hh * DV, (hh + 1) * DV)
        lgf = -_softplus(-jnp.full((1, 1), dl_ref[0, head], F32))
        lgb = -_softplus(-jnp.full((1, 1), dl_ref[1, head], F32))
        decay = jnp.exp(jnp.where(diff >= 0, lgf * diff, -lgb * diff)) * kscale
        kdf = jnp.exp(lgf * (c - 1.0 - p)) * kscale
        kdb = jnp.exp(lgb * p) * kscale

        if not carry_states:
            rows = pl.ds(0, c)
            qn = q_ref[0, rows, qcols]
            kn = k_ref[0, rows, qcols]
            vn = v_ref[0, rows, vcols]
            s = (_dot_nt(qn, kn) * decay).astype(BF16)
            normed_out(jnp.dot(s, vn, preferred_element_type=F32), rows, vcols)
            k32 = kn.astype(F32)
            if emit_state:
                sfin_ref[0, 0, hh] = _dot_tn((k32 * kdf).astype(BF16), vn)
                sfin_ref[0, 1, hh] = _dot_tn((k32 * kdb).astype(BF16), vn)
            continue

        def row_table(col):
            return jnp.broadcast_to(col, (c, DK)).astype(BF16)

        qdf_t = row_table(jnp.exp(lgf * (p + 1.0)))
        qdb_t = row_table(jnp.exp(lgb * (c - p)))
        kdf_t = row_table(kdf)
        kdb_t = row_table(kdb)
        cdf = jnp.exp(lgf * c)
        cdb = jnp.exp(lgb * c)
        if has_state:
            sf_scr[...] = s0_ref[0, 0, hh]
            sb_scr[...] = s0_ref[0, 1, hh]
        else:
            sf_scr[...] = jnp.zeros_like(sf_scr)
            sb_scr[...] = jnp.zeros_like(sb_scr)

        def rev_body(idx, carry):
            n = nc - 1 - idx
            rows = rows_of(n)
            sbh_scr[n] = sb_scr[...].astype(BF16)
            kb = k_ref[0, rows, qcols] * kdb_t
            sb_scr[...] = cdb * sb_scr[...] + _dot_tn(kb, v_ref[0, rows, vcols])
            return carry

        lax.fori_loop(0, nc, rev_body, 0, unroll=True)

        def fwd_body(n, carry):
            rows = rows_of(n)
            qn = q_ref[0, rows, qcols]
            kn = k_ref[0, rows, qcols]
            vn = v_ref[0, rows, vcols]
            s = (_dot_nt(qn, kn) * decay).astype(BF16)
            o = (jnp.dot(s, vn, preferred_element_type=F32)
                 + jnp.dot(qn * qdf_t, sf_scr[...].astype(BF16), preferred_element_type=F32)
                 + jnp.dot(qn * qdb_t, sbh_scr[n], preferred_element_type=F32))
            normed_out(o, rows, vcols)
            sf_scr[...] = cdf * sf_scr[...] + _dot_tn(kn * kdf_t, vn)
            return carry

        lax.fori_loop(0, nc, fwd_body, 0, unroll=True)

        if emit_state:
            sfin_ref[0, 0, hh] = sf_scr[...]
            sfin_ref[0, 1, hh] = sb_scr[...]


def _retention(z3, decay_logit, s0, emit_state, hp):
    b, l, _ = z3.shape
    nc = l // RET_CHUNK
    has_state = s0 is not None
    kq, kv_ = OFF_K // (hp * DK), OFF_V // (hp * DV)
    kg = OFF_GRET // (hp * DV)
    in_specs = [pl.BlockSpec(memory_space=pltpu.SMEM),
                pl.BlockSpec((1, l, hp * DK), lambda i, h: (i, 0, h)),
                pl.BlockSpec((1, l, hp * DK), lambda i, h: (i, 0, kq + h)),
                pl.BlockSpec((1, l, hp * DV), lambda i, h: (i, 0, kv_ + h)),
                pl.BlockSpec((1, l, hp * DV), lambda i, h: (i, 0, kg + h))]
    args = [decay_logit, z3, z3, z3, z3]
    if has_state:
        in_specs.append(pl.BlockSpec((1, 2, hp, DK, DV), lambda i, h: (i, 0, h, 0, 0)))
        args.append(s0)
    out_specs = [pl.BlockSpec((1, l, hp * DV), lambda i, h: (i, 0, h))]
    out_shape = [jax.ShapeDtypeStruct((b, l, D_V), BF16)]
    if emit_state:
        out_specs.append(pl.BlockSpec((1, 2, hp, DK, DV), lambda i, h: (i, 0, h, 0, 0)))
        out_shape.append(jax.ShapeDtypeStruct((b, 2, N_HEADS, DK, DV), F32))
    return pl.pallas_call(
        functools.partial(_ret_kernel, nc=nc, hp=hp, has_state=has_state, emit_state=emit_state),
        grid=(b, N_HEADS // hp),
        in_specs=in_specs,
        out_specs=out_specs,
        out_shape=out_shape,
        scratch_shapes=[pltpu.VMEM((DK, DV), F32), pltpu.VMEM((DK, DV), F32), pltpu.VMEM((nc, DK, DV), BF16)],
        compiler_params=pltpu.CompilerParams(
            dimension_semantics=("parallel", "parallel"), vmem_limit_bytes=VMEM_LIMIT),
        name="retention",
    )(*args)


def _sqrt_unit(x):
    return x * lax.rsqrt(jnp.maximum(x, 1e-30))


def _lru_kernel(xs_ref, sg_ref, cw_ref, cb_ref, w_ref, ba_ref, bx_ref, ap_ref, h0_ref, pt_ref, out_ref, fin_ref,
                xp_scr, s1_scr, wm_scr, a0_scr, b0_scr, a1_scr, b1_scr, *, slabs, seq_len, width, nblk):
    r8 = LRU_ROWS
    tc = LRU_TC
    rows = PERM_ROWS
    cw = nblk * LRU_BLOCK
    nchunks = slabs // tc
    assert width == seq_len or width == tc

    def roll4(v):
        return pltpu.roll(v, LRU_SEQS, axis=0)

    xp_scr[pl.ds(0, 2 * r8), :] = jnp.zeros((2 * r8, cw), F32)
    xp_scr[pl.ds(2 * r8, slabs * r8), :] = xs_ref[0].astype(F32)
    xp_scr[pl.ds((slabs + 2) * r8, r8), :] = roll4(xs_ref[0, pl.ds((slabs - 1) * r8, r8), :].astype(F32))
    xp_scr[pl.ds((slabs + 3) * r8, r8), :] = roll4(xs_ref[0, pl.ds((slabs - 2) * r8, r8), :].astype(F32))

    row_cw = lax.broadcasted_iota(jnp.int32, (rows, cw), 0)
    step_in_chunk = row_cw >> 3
    low_cw = (row_cw & (r8 - 1)) < LRU_SEQS

    def row_pattern(lo, hi):
        n = lo.shape[-1]
        return jnp.where(lax.broadcasted_iota(jnp.int32, (r8, n), 0) < LRU_SEQS, lo, hi)

    def tap_rows(s):
        zero = jnp.zeros((1, cw), F32)
        lo = cw_ref[pl.ds(s + CONV_LEFT, 1), :] if 0 <= s + CONV_LEFT < CONV_W else zero
        hi = cw_ref[pl.ds(CONV_LEFT - s, 1), :] if 0 <= CONV_LEFT - s < CONV_W else zero
        return 0.5 * row_pattern(lo, hi)

    def tap_table(s):
        tap = jnp.broadcast_to(tap_rows(s)[None], (tc, r8, cw)).reshape(rows, cw)
        if width != seq_len:
            tap = jnp.where((step_in_chunk + s >= 0) & (step_in_chunk + s < tc), tap, 0.0)
        return tap

    wm_scr[0] = tap_table(-2) + tap_table(2)
    for s in range(-1, 2):
        wm_scr[s + 2] = tap_table(s)

    ap8 = row_pattern(ap_ref[pl.ds(0, 1), :], ap_ref[pl.ds(1, 1), :])
    c1_lo = (-0.5 * LRU_C / np.log(2.0)) * _softplus(-ap8)
    cb_half = 0.5 * cb_ref[...]

    def half_bias(j):
        cols = slice(j * LRU_BLOCK, (j + 1) * LRU_BLOCK)
        return 0.5 * jnp.concatenate(
            [row_pattern(ba_ref[pl.ds(0, 1), cols], ba_ref[pl.ds(1, 1), cols]),
             row_pattern(bx_ref[pl.ds(0, 1), cols], bx_ref[pl.ds(1, 1), cols])], axis=1)

    bias_lo = [half_bias(j) for j in range(nblk)]
    low_rows = (lax.broadcasted_iota(jnp.int32, (rows, LRU_BLOCK), 0) & (r8 - 1)) < LRU_SEQS
    high_rows = jnp.logical_not(low_rows)

    def conv_half(t0):
        def shifted(s):
            return xp_scr[pl.ds(pl.multiple_of((t0 + s + 2) * r8, r8), rows), :]

        xh = jnp.where(low_cw, shifted(-2), shifted(2)) * wm_scr[0] + cb_half
        for s in range(-1, 2):
            xh = xh + shifted(s) * wm_scr[s + 2]
        xp_scr[pl.ds(pl.multiple_of(t0 * r8, rows), rows), :] = xh
        return xh

    def gates(xh, sweep2, a_scr, b_scr):
        fj = high_rows if sweep2 else low_rows
        c1 = roll4(c1_lo) if sweep2 else c1_lo
        for j in range(nblk):
            cols = slice(j * LRU_BLOCK, (j + 1) * LRU_BLOCK)
            xj = xh[:, cols]
            lhs = jnp.concatenate([jnp.where(fj, xj, 0.0), jnp.where(fj, 0.0, xj)], axis=1)
            pre = jnp.dot(lhs.astype(BF16), w_ref[j], preferred_element_type=F32)
            bj = roll4(bias_lo[j]) if sweep2 else bias_lo[j]
            pre = pre.reshape(tc, r8, 2 * LRU_BLOCK) + bj[None]
            tr = jnp.tanh(pre[:, :, :LRU_BLOCK])
            tg = jnp.tanh(pre[:, :, LRU_BLOCK:])
            c1j = c1[:, cols][None]
            a = jnp.exp2(c1j * tr + c1j)
            bco = _sqrt_unit(1.0 - a * a) * ((tg + 1.0) * xj.reshape(tc, r8, LRU_BLOCK))
            a_scr[:, cols] = a.reshape(rows, LRU_BLOCK)
            b_scr[:, cols] = bco.reshape(rows, LRU_BLOCK)

    bufs = ((a0_scr, b0_scr), (a1_scr, b1_scr))

    def chunk_rows(t0):
        return pl.ds(pl.multiple_of(t0 * r8, rows), rows)

    def scan(buf, h, dst_ref, t0, descending):
        a_ref, b_ref = bufs[buf]
        for i in range(tc):
            t = tc - 1 - i if descending else i
            h = a_ref[t * r8:(t + 1) * r8, :] * h + b_ref[t * r8:(t + 1) * r8, :]
            if dst_ref is None:
                b_ref[t * r8:(t + 1) * r8, :] = h
            else:
                dst_ref[pl.ds(pl.multiple_of(t0 * r8, rows) + t * r8, r8), :] = h
        return h

    def gates1(ci, buf):
        gates(conv_half(ci * tc), False, *bufs[buf])

    def scan1(ci, buf, h):
        return scan(buf, h, s1_scr, ci * tc, False)

    npairs = nchunks // 2
    assert nchunks == 2 * npairs
    gates1(0, 0)

    def sweep1_pair(i, h):
        gates1(2 * i + 1, 1)
        h = scan1(2 * i, 0, h)
        gates1(2 * i + 2, 0)
        return scan1(2 * i + 1, 1, h)

    h = lax.fori_loop(0, npairs - 1, sweep1_pair, h0_ref[0])
    gates1(nchunks - 1, 1)
    h = scan1(nchunks - 2, 0, h)
    h = scan1(nchunks - 1, 1, h)

    def gates2(p, buf):
        gates(xp_scr[chunk_rows((nchunks - 1 - p) * tc), :], True, *bufs[buf])

    def scan2(p, buf, h):
        return scan(buf, h, None, 0, True)

    def finish(p, buf):
        t0 = (nchunks - 1 - p) * tc
        crows = chunk_rows(t0)
        o = (s1_scr[crows, :] + bufs[buf][1][...]).astype(BF16) * sg_ref[0, crows, :]
        nat = jnp.dot(pt_ref[...], o, preferred_element_type=F32).astype(out_ref.dtype)
        lo = pl.multiple_of(t0, tc)
        hi = pl.multiple_of(seq_len - tc - t0, tc)
        for b in range(LRU_SEQS):
            out_ref[0, b, pl.ds(lo, tc), :] = nat[b * tc:(b + 1) * tc]
            out_ref[0, b, pl.ds(hi, tc), :] = nat[(LRU_SEQS + b) * tc:(LRU_SEQS + b + 1) * tc]

    h = roll4(h)
    gates2(0, 0)
    gates2(1, 1)
    h = scan2(0, 0, h)

    def sweep2_pair(i, h):
        finish(2 * i, 0)
        gates2(2 * i + 2, 0)
        h = scan2(2 * i + 1, 1, h)
        finish(2 * i + 1, 1)
        gates2(2 * i + 3, 1)
        return scan2(2 * i + 2, 0, h)

    h = lax.fori_loop(0, npairs - 1, sweep2_pair, h)
    finish(nchunks - 2, 0)
    h = scan2(nchunks - 1, 1, h)
    finish(nchunks - 1, 1)
    fin_ref[0] = h


def _lru(xs, gs, conv_w, conv_b, w_blk, ba, bx, a_param, h0, perm_t, seq_len, width, nblk):
    groups = xs.shape[0]
    r8 = LRU_ROWS
    slabs = seq_len // 2
    cw = nblk * LRU_BLOCK
    per_dir = pl.BlockSpec((2, cw), lambda g, c: (0, c))
    return pl.pallas_call(
        functools.partial(_lru_kernel, slabs=slabs, seq_len=seq_len, width=width, nblk=nblk),
        grid=(groups, N_LRU_BLOCKS // nblk),
        in_specs=[pl.BlockSpec((1, slabs * r8, cw), lambda g, c: (g, 0, c)),
                  pl.BlockSpec((1, slabs * r8, cw), lambda g, c: (g, 0, c)),
                  pl.BlockSpec((CONV_W, cw), lambda g, c: (0, c)),
                  pl.BlockSpec((1, cw), lambda g, c: (0, c)),
                  pl.BlockSpec((nblk, 2 * LRU_BLOCK, 2 * LRU_BLOCK), lambda g, c: (c, 0, 0)),
                  per_dir, per_dir, per_dir,
                  pl.BlockSpec((1, r8, cw), lambda g, c: (g, 0, c)),
                  pl.BlockSpec((PERM_ROWS, PERM_ROWS), lambda g, c: (0, 0))],
        out_specs=[pl.BlockSpec((1, LRU_SEQS, seq_len, cw), lambda g, c: (g, 0, 0, c)),
                   pl.BlockSpec((1, r8, cw), lambda g, c: (g, 0, c))],
        out_shape=[jax.ShapeDtypeStruct((groups, LRU_SEQS, seq_len, D_LRU), BF16),
                   jax.ShapeDtypeStruct((groups, r8, D_LRU), F32)],
        scratch_shapes=[pltpu.VMEM(((slabs + 4) * r8, cw), F32),
                        pltpu.VMEM((slabs * r8, cw), F32),
                        pltpu.VMEM((4, PERM_ROWS, cw), F32),
                        pltpu.VMEM((PERM_ROWS, cw), F32),
                        pltpu.VMEM((PERM_ROWS, cw), F32),
                        pltpu.VMEM((PERM_ROWS, cw), F32),
                        pltpu.VMEM((PERM_ROWS, cw), F32)],
        compiler_params=pltpu.CompilerParams(
            dimension_semantics=("parallel", "parallel"), vmem_limit_bytes=VMEM_LIMIT),
        name="lru",
    )(xs, gs, conv_w, conv_b.reshape(1, -1), w_blk, ba, bx, a_param, h0, perm_t)


def _lru_gate_weights(wa, wx):
    return jnp.concatenate([jnp.concatenate([wa[0], wx[0]], axis=2),
                            jnp.concatenate([wa[1], wx[1]], axis=2)], axis=1).astype(BF16)


def _tail_kernel(x_ref, o_ref, l_ref, mr_ref, ml_ref, gate_ref, fnw_ref, wrd_ref, wld_ref, wo_ref, y_ref):
    ret_out = jnp.dot(o_ref[...], wrd_ref[...], preferred_element_type=F32)
    lru_out = jnp.dot(l_ref[...], wld_ref[...], preferred_element_type=F32)
    merged = _sigmoid(mr_ref[...].astype(F32)) * ret_out + _sigmoid(ml_ref[...].astype(F32)) * lru_out
    out = jnp.dot(merged.astype(BF16), wo_ref[...], preferred_element_type=F32)
    y = x_ref[...] + gate_ref[0] * out
    ms = jnp.mean(y * y, axis=-1, keepdims=True)
    y_ref[...] = y * lax.rsqrt(ms + EPS) * fnw_ref[...]


TAIL_TM = 512


def _tail(x2d, o2d, l2d, z2d, mod3, cond_row, fnw, wrd_b, wld_b, wo_b):
    m = x2d.shape[0]
    tm = TAIL_TM
    kmr, kml = MAIN_MRET // D_MODEL, MAIN_MLRU // D_MODEL
    const = lambda i: (0, 0)
    return pl.pallas_call(
        _tail_kernel,
        grid=(m // tm,),
        in_specs=[pl.BlockSpec((tm, D_MODEL), lambda i: (i, 0)),
                  pl.BlockSpec((tm, D_V), lambda i: (i, 0)),
                  pl.BlockSpec((tm, D_LRU), lambda i: (i, 0)),
                  pl.BlockSpec((tm, D_MODEL), lambda i: (i, kmr)),
                  pl.BlockSpec((tm, D_MODEL), lambda i: (i, kml)),
                  pl.BlockSpec((1, 1, D_MODEL), lambda i: (cond_row(i), 0, MOD_GATE)),
                  pl.BlockSpec((1, D_MODEL), const),
                  pl.BlockSpec((D_V, D_MODEL), const),
                  pl.BlockSpec((D_LRU, D_MODEL), const),
                  pl.BlockSpec((D_MODEL, D_MODEL), const)],
        out_specs=pl.BlockSpec((tm, D_MODEL), lambda i: (i, 0)),
        out_shape=jax.ShapeDtypeStruct((m, D_MODEL), F32),
        compiler_params=pltpu.CompilerParams(
            dimension_semantics=("parallel",), vmem_limit_bytes=VMEM_LIMIT),
        name="tail",
    )(x2d, o2d, l2d, z2d, z2d, mod3, fnw.reshape(1, -1), wrd_b, wld_b, wo_b)


def _trunk(x, mod3, cond_row0, n_cond, s0_ret, h0_lru, width, lru_nblk, params, final_norm_w, emit_state):
    (norm_w, w_all, decay_logit, wrd_b, conv_w, conv_b, w_blk, ba, bx, a_param, wld_b, wo_b, perm, perm_t) = params
    b, l, _ = x.shape
    groups = b // LRU_SEQS
    x2d = x.reshape(b * l, D_MODEL)
    tokens_per_cond = b * l // n_cond

    def cond_row(tile_tokens):
        return lambda i: cond_row0 + i // (tokens_per_cond // tile_tokens)

    z2d = _inproj(x2d, norm_w, mod3, cond_row(INPROJ_TM), w_all)
    z3 = z2d.reshape(b, l, D_MAIN)
    ret = _retention(z3, decay_logit, s0_ret, emit_state, N_HEADS if l == RET_CHUNK else 1)

    xs, gs = _inproj_lru(x.reshape(groups, LRU_SEQS, l, D_MODEL), norm_w, mod3, cond_row0, n_cond, perm, w_all)
    if h0_lru is None:
        h0 = jnp.zeros((groups, LRU_ROWS, D_LRU), F32)
    else:
        h0 = h0_lru.reshape(groups, LRU_SEQS, 2, D_LRU).transpose(0, 2, 1, 3).reshape(groups, LRU_ROWS, D_LRU)
    lru_pre, fin = _lru(xs, gs, conv_w, conv_b, w_blk, ba, bx, a_param, h0, perm_t, l, width, lru_nblk)
    fin = fin.reshape(groups, 2, LRU_SEQS, D_LRU)
    lru_fin = jnp.stack([fin[:, 1], fin[:, 0]], axis=2).reshape(b, 2, D_LRU)

    y = _tail(x2d, ret[0].reshape(b * l, D_V), lru_pre.reshape(b * l, D_LRU), z2d, mod3, cond_row(TAIL_TM),
              final_norm_w, wrd_b, wld_b, wo_b)
    return y.reshape(b, l, D_MODEL), (ret[1] if emit_state else None), lru_fin


def kernel(x_prompt, x_sample, state_ret, state_lru, c, c_ctx, norm_w, w_ada, b_ada, w_in, ret_decay_logit,
           ret_gn_w, w_ret_down, conv_w, conv_b, lru_wa, lru_ba, lru_wx, lru_bx, lru_a_param, w_lru_down,
           w_out, final_norm_w):
    assert norm_w.shape[0] == 1, "single-layer step"
    n_dec = c.shape[0]
    cond8 = jnp.concatenate([c.astype(F32), c_ctx.astype(F32)[None],
                             jnp.zeros((8 - n_dec - 1, D_MODEL), F32)], axis=0)
    mod = _ada(cond8, w_ada[0], b_ada[0])
    col = jnp.arange(D_IN)
    col_scale = jnp.where((col >= OFF_GRET) & (col < OFF_XLRU), 0.5, 1.0).astype(F32)
    w_all = (w_in[0] * col_scale[None, :]).astype(BF16)
    wrd_b = (ret_gn_w[0][:, None] * w_ret_down[0]).astype(BF16)
    perm = jnp.asarray(_slab_permutation(PERM_UNIT), BF16)
    perm_t = jnp.asarray(_slab_permutation(LRU_TC).T, BF16)
    params = (norm_w[0], w_all, ret_decay_logit[0], wrd_b, conv_w[0], conv_b[0],
              _lru_gate_weights(lru_wa[0], lru_wx[0]), lru_ba[0], lru_bx[0], lru_a_param[0],
              w_lru_down[0].astype(BF16), w_out[0].astype(BF16), perm, perm_t)
    mod3 = mod
    y_prompt, new_ret, new_lru = _trunk(x_prompt.astype(F32), mod3, n_dec, 1, None, None,
                                        x_prompt.shape[1], 5, params, final_norm_w, True)
    y_sample, _, _ = _trunk(x_sample.astype(F32), mod3, 0, n_dec, state_ret[:, 0], state_lru[:, 0],
                            GRID_W, 2, params, final_norm_w, False)
    return (y_prompt.astype(x_prompt.dtype), y_sample.astype(x_sample.dtype),
            new_ret[:, None].astype(state_ret.dtype), new_lru[:, None].astype(state_lru.dtype))
```

```python
import functools

import jax
import jax.numpy as jnp
import numpy as np
from jax import lax
from jax.experimental import pallas as pl
from jax.experimental.pallas import tpu as pltpu

F32 = jnp.float32
BF16 = jnp.bfloat16

D_MODEL = 1024
N_HEADS = 4
DK = 256
DV = 512
D_QK = N_HEADS * DK
D_V = N_HEADS * DV
N_LRU_BLOCKS = 10
LRU_BLOCK = 128
D_LRU = N_LRU_BLOCKS * LRU_BLOCK
LRU_C = 8.0
CONV_W = 4
CONV_LEFT = 2
GRID_W = 64
EPS = 1e-6

OFF_Q = 0
OFF_K = OFF_Q + D_QK
OFF_V = OFF_K + D_QK
OFF_GRET = OFF_V + D_V
OFF_XLRU = OFF_GRET + D_V
OFF_GLRU = OFF_XLRU + D_LRU
OFF_MRET = OFF_GLRU + D_LRU
OFF_MLRU = OFF_MRET + D_MODEL
D_IN = OFF_MLRU + D_MODEL
D_MAIN = OFF_XLRU + 2 * D_MODEL
MAIN_MRET = OFF_XLRU
MAIN_MLRU = OFF_XLRU + D_MODEL

RET_CHUNK = 256
LRU_SEQS = 4
LRU_ROWS = 2 * LRU_SEQS
LRU_TC = 64
PERM_ROWS = LRU_TC * LRU_ROWS
PERM_UNIT = 32
INPROJ_LRU_SLABS = 128
VMEM_LIMIT = 56 * 1024 * 1024


def _sigmoid(x):
    return 0.5 * jnp.tanh(0.5 * x) + 0.5


def _silu(x):
    return x * _sigmoid(x)


def _softplus(x):
    return jnp.maximum(x, 0.0) + jnp.log1p(jnp.exp(-jnp.abs(x)))


def _slab_permutation(tc):
    n = tc * LRU_ROWS
    p = np.zeros((n, n), np.float32)
    for t in range(tc):
        for r in range(LRU_ROWS):
            if r < LRU_SEQS:
                src = r * tc + t
            else:
                src = LRU_SEQS * tc + (r - LRU_SEQS) * tc + (tc - 1 - t)
            p[t * LRU_ROWS + r, src] = 1.0
    return p


def _ada_kernel(c_ref, w_ref, b_ref, o_ref):
    cond = _silu(c_ref[...]).astype(BF16)
    mod = jnp.dot(cond, w_ref[...].astype(BF16), preferred_element_type=F32) + b_ref[...]
    o_ref[...] = mod[:, None, :]


def _ada(cond8, w_ada, b_ada):
    tn = 1536
    return pl.pallas_call(
        _ada_kernel,
        grid=(3 * D_MODEL // tn,),
        in_specs=[pl.BlockSpec((8, D_MODEL), lambda j: (0, 0)),
                  pl.BlockSpec((D_MODEL, tn), lambda j: (0, j)),
                  pl.BlockSpec((1, tn), lambda j: (0, j))],
        out_specs=pl.BlockSpec((8, 1, tn), lambda j: (0, 0, j)),
        out_shape=jax.ShapeDtypeStruct((8, 1, 3 * D_MODEL), F32),
        name="ada",
    )(cond8, w_ada, b_ada.reshape(1, -1))


def _modulated_norm(x, nw, scale, shift):
    ms = jnp.mean(x * x, axis=-1, keepdims=True)
    return (x * lax.rsqrt(ms + EPS)) * (nw * (1.0 + scale)) + shift


INPROJ_TM = 1024
INPROJ_TN = 2048


def _inproj_kernel(x_ref, nw_ref, sc_ref, sh_ref, w_ref, z_ref, h_scr):
    @pl.when(pl.program_id(1) == 0)
    def _():
        h_scr[...] = _modulated_norm(x_ref[...], nw_ref[...], sc_ref[0], sh_ref[0]).astype(BF16)

    z_ref[...] = jnp.dot(h_scr[...], w_ref[...], preferred_element_type=F32).astype(z_ref.dtype)


MOD_SHIFT, MOD_SCALE, MOD_GATE = 0, 1, 2


def _inproj(x2d, norm_w, mod3, cond_row, w_all):
    m = x2d.shape[0]
    tm, tn = INPROJ_TM, INPROJ_TN
    n_lead = OFF_XLRU // tn
    n_tiles = D_MAIN // tn
    assert OFF_XLRU == n_lead * tn and D_MAIN == n_tiles * tn

    def w_col(j):
        off = j * tn
        for t in range(n_lead, n_tiles):
            off = jnp.where(j == t, OFF_MRET + (t - n_lead) * tn, off)
        return off

    return pl.pallas_call(
        _inproj_kernel,
        grid=(m // tm, D_MAIN // tn),
        in_specs=[pl.BlockSpec((tm, D_MODEL), lambda i, j: (i, 0)),
                  pl.BlockSpec((1, D_MODEL), lambda i, j: (0, 0)),
                  pl.BlockSpec((1, 1, D_MODEL), lambda i, j: (cond_row(i), 0, MOD_SCALE)),
                  pl.BlockSpec((1, 1, D_MODEL), lambda i, j: (cond_row(i), 0, MOD_SHIFT)),
                  pl.BlockSpec((pl.Element(D_MODEL), pl.Element(tn)), lambda i, j: (0, w_col(j)))],
        out_specs=pl.BlockSpec((tm, tn), lambda i, j: (i, j)),
        out_shape=jax.ShapeDtypeStruct((m, D_MAIN), BF16),
        scratch_shapes=[pltpu.VMEM((tm, D_MODEL), BF16)],
        compiler_params=pltpu.CompilerParams(
            dimension_semantics=("parallel", "arbitrary"), vmem_limit_bytes=VMEM_LIMIT),
        name="inproj",
    )(x2d, norm_w.reshape(1, -1), mod3, mod3, w_all)


def _inproj_lru_kernel(xa_ref, xb_ref, nw_ref, sc_ref, sh_ref, p_ref, w_ref, xs_ref, gs_ref):
    nw = nw_ref[...]
    sc = sc_ref[...]
    sh = sh_ref[...]
    ha = _modulated_norm(xa_ref[0], nw, sc, sh).astype(BF16)
    hb = _modulated_norm(xb_ref[0], nw, sc, sh).astype(BF16)
    units = INPROJ_LRU_SLABS // PERM_UNIT
    pieces = []
    for u in range(units):
        lo, hi = u * PERM_UNIT, (units - 1 - u) * PERM_UNIT
        src = jnp.concatenate([ha[b, lo:lo + PERM_UNIT] for b in range(LRU_SEQS)]
                              + [hb[b, hi:hi + PERM_UNIT] for b in range(LRU_SEQS)], axis=0)
        pieces.append(jnp.dot(p_ref[...], src, preferred_element_type=F32).astype(BF16))
    hp = jnp.concatenate(pieces, axis=0)
    z = jnp.dot(hp, w_ref[...], preferred_element_type=F32)
    xs_ref[0] = z[:, :D_LRU].astype(xs_ref.dtype)
    gs_ref[0] = _silu(z[:, D_LRU:]).astype(gs_ref.dtype)


def _inproj_lru(x4, norm_w, mod3, row0, nrows, perm, w_all):
    groups, _, l, _ = x4.shape
    assert row0 % nrows == 0 and nrows in (1, LRU_SEQS)
    ts = INPROJ_LRU_SLABS
    nt = l // 2 // ts
    ntb = l // ts
    out_sds = jax.ShapeDtypeStruct((groups, l // 2 * LRU_ROWS, D_LRU), BF16)
    return pl.pallas_call(
        _inproj_lru_kernel,
        grid=(groups, nt),
        in_specs=[pl.BlockSpec((1, LRU_SEQS, ts, D_MODEL), lambda g, i: (g, 0, i, 0)),
                  pl.BlockSpec((1, LRU_SEQS, ts, D_MODEL), lambda g, i: (g, 0, ntb - 1 - i, 0)),
                  pl.BlockSpec((1, D_MODEL), lambda g, i: (0, 0)),
                  pl.BlockSpec((nrows, 1, D_MODEL), lambda g, i: (row0 // nrows, 0, MOD_SCALE)),
                  pl.BlockSpec((nrows, 1, D_MODEL), lambda g, i: (row0 // nrows, 0, MOD_SHIFT)),
                  pl.BlockSpec((PERM_UNIT * LRU_ROWS, PERM_UNIT * LRU_ROWS), lambda g, i: (0, 0)),
                  pl.BlockSpec((pl.Element(D_MODEL), pl.Element(2 * D_LRU)), lambda g, i: (0, OFF_XLRU))],
        out_specs=[pl.BlockSpec((1, ts * LRU_ROWS, D_LRU), lambda g, i: (g, i, 0)),
                   pl.BlockSpec((1, ts * LRU_ROWS, D_LRU), lambda g, i: (g, i, 0))],
        out_shape=[out_sds, out_sds],
        compiler_params=pltpu.CompilerParams(
            dimension_semantics=("parallel", "parallel"), vmem_limit_bytes=VMEM_LIMIT),
        name="inproj_lru",
    )(x4, x4, norm_w.reshape(1, -1), mod3, mod3, perm, w_all)


def _dot_tn(a, b):
    return lax.dot_general(a, b, (((0,), (0,)), ((), ())), preferred_element_type=F32)


def _dot_nt(a, b):
    return lax.dot_general(a, b, (((1,), (1,)), ((), ())), preferred_element_type=F32)


def _tail_math(x, o, lru_pre, m_ret, m_lru, gate, fnw, wrd, wld, wo):
    ret_out = jnp.dot(o, wrd, preferred_element_type=F32)
    lru_out = jnp.dot(lru_pre, wld, preferred_element_type=F32)
    merged = _sigmoid(m_ret.astype(F32)) * ret_out + _sigmoid(m_lru.astype(F32)) * lru_out
    out = jnp.dot(merged.astype(BF16), wo, preferred_element_type=F32)
    y = x + gate * out
    ms = jnp.mean(y * y, axis=-1, keepdims=True)
    return y * lax.rsqrt(ms + EPS) * fnw


def _ret_kernel(*refs, nc, hp, has_state, emit_state, fuse_tail):
    dl_ref, q_ref, k_ref, v_ref, hg_ref = refs[:5]
    pos = 5
    s0_ref = None
    if has_state:
        s0_ref = refs[pos]
        pos += 1
    tail_refs = None
    if fuse_tail:
        tail_refs = refs[pos:pos + 9]
        pos += 9
    out_ref = refs[pos]
    pos += 1
    sfin_ref = None
    if emit_state:
        sfin_ref = refs[pos]
        pos += 1
    sf_scr, sb_scr, sbh_scr = refs[pos:pos + 3]
    o_scr = refs[pos + 3] if fuse_tail else None

    c = RET_CHUNK
    carry_states = has_state or nc > 1
    ii = lax.broadcasted_iota(jnp.int32, (c, c), 0)
    jj = lax.broadcasted_iota(jnp.int32, (c, c), 1)
    diff = (ii - jj).astype(F32)
    p = lax.broadcasted_iota(jnp.int32, (c, 1), 0).astype(F32)
    kscale = DK ** -0.5

    def rows_of(n):
        return pl.ds(pl.multiple_of(n * c, c), c)

    def normed_out(o, rows, vcols):
        ms = jnp.mean(o * o, axis=-1, keepdims=True)
        on = o * lax.rsqrt(ms + EPS)
        hg = hg_ref[0, rows, vcols]
        gate = hg * (jnp.tanh(hg) + 1.0)
        if fuse_tail:
            o_scr[rows, vcols] = on.astype(BF16) * gate
        else:
            out_ref[0, rows, vcols] = on.astype(out_ref.dtype) * gate

    for hh in range(hp):
        head = pl.program_id(1) * hp + hh
        qcols = slice(hh * DK, (hh + 1) * DK)
        vcols = slice(hh * DV, (hh + 1) * DV)
        lgf = -_softplus(-jnp.full((1, 1), dl_ref[0, head], F32))
        lgb = -_softplus(-jnp.full((1, 1), dl_ref[1, head], F32))
        decay = jnp.exp(jnp.where(diff >= 0, lgf * diff, -lgb * diff)) * kscale
        kdf = jnp.exp(lgf * (c - 1.0 - p)) * kscale
        kdb = jnp.exp(lgb * p) * kscale

        if not carry_states:
            rows = pl.ds(0, c)
            qn = q_ref[0, rows, qcols]
            kn = k_ref[0, rows, qcols]
            vn = v_ref[0, rows, vcols]
            s = (_dot_nt(qn, kn) * decay).astype(BF16)
            normed_out(jnp.dot(s, vn, preferred_element_type=F32), rows, vcols)
            k32 = kn.astype(F32)
            if emit_state:
                sfin_ref[0, 0, hh] = _dot_tn((k32 * kdf).astype(BF16), vn)
                sfin_ref[0, 1, hh] = _dot_tn((k32 * kdb).astype(BF16), vn)
            continue

        def row_table(col):
            return jnp.broadcast_to(col, (c, DK)).astype(BF16)

        qdf_t = row_table(jnp.exp(lgf * (p + 1.0)))
        qdb_t = row_table(jnp.exp(lgb * (c - p)))
        kdf_t = row_table(kdf)
        kdb_t = row_table(kdb)
        cdf = jnp.exp(lgf * c)
        cdb = jnp.exp(lgb * c)
        if has_state:
            sf_scr[...] = s0_ref[0, 0, hh]
            sb_scr[...] = s0_ref[0, 1, hh]
        else:
            sf_scr[...] = jnp.zeros_like(sf_scr)
            sb_scr[...] = jnp.zeros_like(sb_scr)

        def rev_body(idx, carry):
            n = nc - 1 - idx
            rows = rows_of(n)
            sbh_scr[n] = sb_scr[...].astype(BF16)
            kb = k_ref[0, rows, qcols] * kdb_t
            sb_scr[...] = cdb * sb_scr[...] + _dot_tn(kb, v_ref[0, rows, vcols])
            return carry

        lax.fori_loop(0, nc, rev_body, 0, unroll=True)

        def fwd_body(n, carry):
            rows = rows_of(n)
            qn = q_ref[0, rows, qcols]
            kn = k_ref[0, rows, qcols]
            vn = v_ref[0, rows, vcols]
            s = (_dot_nt(qn, kn) * decay).astype(BF16)
            o = (jnp.dot(s, vn, preferred_element_type=F32)
                 + jnp.dot(qn * qdf_t, sf_scr[...].astype(BF16), preferred_element_type=F32)
                 + jnp.dot(qn * qdb_t, sbh_scr[n], preferred_element_type=F32))
            normed_out(o, rows, vcols)
            sf_scr[...] = cdf * sf_scr[...] + _dot_tn(kn * kdf_t, vn)
            return carry

        lax.fori_loop(0, nc, fwd_body, 0, unroll=True)

        if emit_state:
            sfin_ref[0, 0, hh] = sf_scr[...]
            sfin_ref[0, 1, hh] = sb_scr[...]

    if fuse_tail:
        x_ref, l_ref, mr_ref, ml_ref, gate_ref, fnw_ref, wrd_ref, wld_ref, wo_ref = tail_refs
        out_ref[0] = _tail_math(x_ref[0], o_scr[...], l_ref[0], mr_ref[0], ml_ref[0], gate_ref[0], fnw_ref[...],
                                wrd_ref[...], wld_ref[...], wo_ref[...])


def _retention(z3, decay_logit, s0, emit_state, hp, tail=None):
    b, l, _ = z3.shape
    nc = l // RET_CHUNK
    has_state = s0 is not None
    fuse_tail = tail is not None
    assert not fuse_tail or hp == N_HEADS
    kq, kv_ = OFF_K // (hp * DK), OFF_V // (hp * DV)
    kg = OFF_GRET // (hp * DV)
    in_specs = [pl.BlockSpec(memory_space=pltpu.SMEM),
                pl.BlockSpec((1, l, hp * DK), lambda i, h: (i, 0, h)),
                pl.BlockSpec((1, l, hp * DK), lambda i, h: (i, 0, kq + h)),
                pl.BlockSpec((1, l, hp * DV), lambda i, h: (i, 0, kv_ + h)),
                pl.BlockSpec((1, l, hp * DV), lambda i, h: (i, 0, kg + h))]
    args = [decay_logit, z3, z3, z3, z3]
    if has_state:
        in_specs.append(pl.BlockSpec((1, 2, hp, DK, DV), lambda i, h: (i, 0, h, 0, 0)))
        args.append(s0)
    scratch = [pltpu.VMEM((DK, DV), F32), pltpu.VMEM((DK, DV), F32), pltpu.VMEM((nc, DK, DV), BF16)]
    if fuse_tail:
        x3, l3, mod3, cond_row, fnw, wrd_b, wld_b, wo_b = tail
        const = lambda i, h: (0, 0)
        in_specs += [pl.BlockSpec((1, l, D_MODEL), lambda i, h: (i, 0, 0)),
                     pl.BlockSpec((1, l, D_LRU), lambda i, h: (i, 0, 0)),
                     pl.BlockSpec((1, l, D_MODEL), lambda i, h: (i, 0, MAIN_MRET // D_MODEL)),
                     pl.BlockSpec((1, l, D_MODEL), lambda i, h: (i, 0, MAIN_MLRU // D_MODEL)),
                     pl.BlockSpec((1, 1, D_MODEL), lambda i, h: (cond_row(i), 0, MOD_GATE)),
                     pl.BlockSpec((1, D_MODEL), const),
                     pl.BlockSpec((D_V, D_MODEL), const),
                     pl.BlockSpec((D_LRU, D_MODEL), const),
                     pl.BlockSpec((D_MODEL, D_MODEL), const)]
        args += [x3, l3, z3, z3, mod3, fnw.reshape(1, -1), wrd_b, wld_b, wo_b]
        out_specs = [pl.BlockSpec((1, l, D_MODEL), lambda i, h: (i, 0, 0))]
        out_shape = [jax.ShapeDtypeStruct((b, l, D_MODEL), F32)]
        scratch.append(pltpu.VMEM((l, D_V), BF16))
    else:
        out_specs = [pl.BlockSpec((1, l, hp * DV), lambda i, h: (i, 0, h))]
        out_shape = [jax.ShapeDtypeStruct((b, l, D_V), BF16)]
    if emit_state:
        out_specs.append(pl.BlockSpec((1, 2, hp, DK, DV), lambda i, h: (i, 0, h, 0, 0)))
        out_shape.append(jax.ShapeDtypeStruct((b, 2, N_HEADS, DK, DV), F32))
    return pl.pallas_call(
        functools.partial(_ret_kernel, nc=nc, hp=hp, has_state=has_state, emit_state=emit_state,
                          fuse_tail=fuse_tail),
        grid=(b, N_HEADS // hp),
        in_specs=in_specs,
        out_specs=out_specs,
        out_shape=out_shape,
        scratch_shapes=scratch,
        compiler_params=pltpu.CompilerParams(
            dimension_semantics=("parallel", "parallel"), vmem_limit_bytes=VMEM_LIMIT),
        name="retention_tail" if fuse_tail else "retention",
    )(*args)


def _sqrt_unit(x):
    return x * lax.rsqrt(jnp.maximum(x, 1e-30))


def _lru_kernel(xs_ref, sg_ref, cw_ref, cb_ref, w_ref, ba_ref, bx_ref, ap_ref, h0_ref, pt_ref, out_ref, fin_ref,
                xp_scr, s1_scr, wm_scr, a0_scr, b0_scr, a1_scr, b1_scr, *, slabs, seq_len, width, nblk):
    r8 = LRU_ROWS
    tc = LRU_TC
    rows = PERM_ROWS
    cw = nblk * LRU_BLOCK
    nchunks = slabs // tc
    assert width == seq_len or width == tc

    def roll4(v):
        return pltpu.roll(v, LRU_SEQS, axis=0)

    xp_scr[pl.ds(0, 2 * r8), :] = jnp.zeros((2 * r8, cw), F32)
    xp_scr[pl.ds(2 * r8, slabs * r8), :] = xs_ref[0].astype(F32)
    xp_scr[pl.ds((slabs + 2) * r8, r8), :] = roll4(xs_ref[0, pl.ds((slabs - 1) * r8, r8), :].astype(F32))
    xp_scr[pl.ds((slabs + 3) * r8, r8), :] = roll4(xs_ref[0, pl.ds((slabs - 2) * r8, r8), :].astype(F32))

    row_cw = lax.broadcasted_iota(jnp.int32, (rows, cw), 0)
    step_in_chunk = row_cw >> 3
    low_cw = (row_cw & (r8 - 1)) < LRU_SEQS

    def row_pattern(lo, hi):
        n = lo.shape[-1]
        return jnp.where(lax.broadcasted_iota(jnp.int32, (r8, n), 0) < LRU_SEQS, lo, hi)

    def tap_rows(s):
        zero = jnp.zeros((1, cw), F32)
        lo = cw_ref[pl.ds(s + CONV_LEFT, 1), :] if 0 <= s + CONV_LEFT < CONV_W else zero
        hi = cw_ref[pl.ds(CONV_LEFT - s, 1), :] if 0 <= CONV_LEFT - s < CONV_W else zero
        return 0.5 * row_pattern(lo, hi)

    def tap_table(s):
        tap = jnp.broadcast_to(tap_rows(s)[None], (tc, r8, cw)).reshape(rows, cw)
        if width != seq_len:
            tap = jnp.where((step_in_chunk + s >= 0) & (step_in_chunk + s < tc), tap, 0.0)
        return tap

    wm_scr[0] = tap_table(-2) + tap_table(2)
    for s in range(-1, 2):
        wm_scr[s + 2] = tap_table(s)

    ap8 = row_pattern(ap_ref[pl.ds(0, 1), :], ap_ref[pl.ds(1, 1), :])
    c1_lo = (-0.5 * LRU_C / np.log(2.0)) * _softplus(-ap8)
    cb_half = 0.5 * cb_ref[...]

    def half_bias(j):
        cols = slice(j * LRU_BLOCK, (j + 1) * LRU_BLOCK)
        return 0.5 * jnp.concatenate(
            [row_pattern(ba_ref[pl.ds(0, 1), cols], ba_ref[pl.ds(1, 1), cols]),
             row_pattern(bx_ref[pl.ds(0, 1), cols], bx_ref[pl.ds(1, 1), cols])], axis=1)

    bias_lo = [half_bias(j) for j in range(nblk)]
    low_rows = (lax.broadcasted_iota(jnp.int32, (rows, LRU_BLOCK), 0) & (r8 - 1)) < LRU_SEQS
    high_rows = jnp.logical_not(low_rows)

    def conv_half(t0):
        def shifted(s):
            return xp_scr[pl.ds(pl.multiple_of((t0 + s + 2) * r8, r8), rows), :]

        xh = jnp.where(low_cw, shifted(-2), shifted(2)) * wm_scr[0] + cb_half
        for s in range(-1, 2):
            xh = xh + shifted(s) * wm_scr[s + 2]
        xp_scr[pl.ds(pl.multiple_of(t0 * r8, rows), rows), :] = xh
        return xh

    def gates(xh, sweep2, a_scr, b_scr):
        fj = high_rows if sweep2 else low_rows
        c1 = roll4(c1_lo) if sweep2 else c1_lo
        for j in range(nblk):
            cols = slice(j * LRU_BLOCK, (j + 1) * LRU_BLOCK)
            xj = xh[:, cols]
            lhs = jnp.concatenate([jnp.where(fj, xj, 0.0), jnp.where(fj, 0.0, xj)], axis=1)
            pre = jnp.dot(lhs.astype(BF16), w_ref[j], preferred_element_type=F32)
            bj = roll4(bias_lo[j]) if sweep2 else bias_lo[j]
            pre = pre.reshape(tc, r8, 2 * LRU_BLOCK) + bj[None]
            tr = jnp.tanh(pre[:, :, :LRU_BLOCK])
            tg = jnp.tanh(pre[:, :, LRU_BLOCK:])
            c1j = c1[:, cols][None]
            a = jnp.exp2(c1j * tr + c1j)
            bco = _sqrt_unit(1.0 - a * a) * ((tg + 1.0) * xj.reshape(tc, r8, LRU_BLOCK))
            a_scr[:, cols] = a.reshape(rows, LRU_BLOCK)
            b_scr[:, cols] = bco.reshape(rows, LRU_BLOCK)

    bufs = ((a0_scr, b0_scr), (a1_scr, b1_scr))

    def chunk_rows(t0):
        return pl.ds(pl.multiple_of(t0 * r8, rows), rows)

    def scan(buf, h, dst_ref, t0, descending):
        a_ref, b_ref = bufs[buf]
        for i in range(tc):
            t = tc - 1 - i if descending else i
            h = a_ref[t * r8:(t + 1) * r8, :] * h + b_ref[t * r8:(t + 1) * r8, :]
            if dst_ref is None:
                b_ref[t * r8:(t + 1) * r8, :] = h
            else:
                dst_ref[pl.ds(pl.multiple_of(t0 * r8, rows) + t * r8, r8), :] = h
        return h

    def gates1(ci, buf):
        gates(conv_half(ci * tc), False, *bufs[buf])

    def scan1(ci, buf, h):
        return scan(buf, h, s1_scr, ci * tc, False)

    npairs = nchunks // 2
    assert nchunks == 2 * npairs
    gates1(0, 0)

    def sweep1_pair(i, h):
        gates1(2 * i + 1, 1)
        h = scan1(2 * i, 0, h)
        gates1(2 * i + 2, 0)
        return scan1(2 * i + 1, 1, h)

    h = lax.fori_loop(0, npairs - 1, sweep1_pair, h0_ref[0])
    gates1(nchunks - 1, 1)
    h = scan1(nchunks - 2, 0, h)
    h = scan1(nchunks - 1, 1, h)

    def gates2(p, buf):
        gates(xp_scr[chunk_rows((nchunks - 1 - p) * tc), :], True, *bufs[buf])

    def scan2(p, buf, h):
        return scan(buf, h, None, 0, True)

    def finish(p, buf):
        t0 = (nchunks - 1 - p) * tc
        crows = chunk_rows(t0)
        o = (s1_scr[crows, :] + bufs[buf][1][...]).astype(BF16) * sg_ref[0, crows, :]
        nat = jnp.dot(pt_ref[...], o, preferred_element_type=F32).astype(out_ref.dtype)
        lo = pl.multiple_of(t0, tc)
        hi = pl.multiple_of(seq_len - tc - t0, tc)
        for b in range(LRU_SEQS):
            out_ref[0, b, pl.ds(lo, tc), :] = nat[b * tc:(b + 1) * tc]
            out_ref[0, b, pl.ds(hi, tc), :] = nat[(LRU_SEQS + b) * tc:(LRU_SEQS + b + 1) * tc]

    h = roll4(h)
    gates2(0, 0)
    gates2(1, 1)
    h = scan2(0, 0, h)

    def sweep2_pair(i, h):
        finish(2 * i, 0)
        gates2(2 * i + 2, 0)
        h = scan2(2 * i + 1, 1, h)
        finish(2 * i + 1, 1)
        gates2(2 * i + 3, 1)
        return scan2(2 * i + 2, 0, h)

    h = lax.fori_loop(0, npairs - 1, sweep2_pair, h)
    finish(nchunks - 2, 0)
    h = scan2(nchunks - 1, 1, h)
    finish(nchunks - 1, 1)
    fin_ref[0] = h


def _lru(xs, gs, conv_w, conv_b, w_blk, ba, bx, a_param, h0, perm_t, seq_len, width, nblk):
    groups = xs.shape[0]
    r8 = LRU_ROWS
    slabs = seq_len // 2
    cw = nblk * LRU_BLOCK
    per_dir = pl.BlockSpec((2, cw), lambda g, c: (0, c))
    return pl.pallas_call(
        functools.partial(_lru_kernel, slabs=slabs, seq_len=seq_len, width=width, nblk=nblk),
        grid=(groups, N_LRU_BLOCKS // nblk),
        in_specs=[pl.BlockSpec((1, slabs * r8, cw), lambda g, c: (g, 0, c)),
                  pl.BlockSpec((1, slabs * r8, cw), lambda g, c: (g, 0, c)),
                  pl.BlockSpec((CONV_W, cw), lambda g, c: (0, c)),
                  pl.BlockSpec((1, cw), lambda g, c: (0, c)),
                  pl.BlockSpec((nblk, 2 * LRU_BLOCK, 2 * LRU_BLOCK), lambda g, c: (c, 0, 0)),
                  per_dir, per_dir, per_dir,
                  pl.BlockSpec((1, r8, cw), lambda g, c: (g, 0, c)),
                  pl.BlockSpec((PERM_ROWS, PERM_ROWS), lambda g, c: (0, 0))],
        out_specs=[pl.BlockSpec((1, LRU_SEQS, seq_len, cw), lambda g, c: (g, 0, 0, c)),
                   pl.BlockSpec((1, r8, cw), lambda g, c: (g, 0, c))],
        out_shape=[jax.ShapeDtypeStruct((groups, LRU_SEQS, seq_len, D_LRU), BF16),
                   jax.ShapeDtypeStruct((groups, r8, D_LRU), F32)],
        scratch_shapes=[pltpu.VMEM(((slabs + 4) * r8, cw), F32),
                        pltpu.VMEM((slabs * r8, cw), F32),
                        pltpu.VMEM((4, PERM_ROWS, cw), F32),
                        pltpu.VMEM((PERM_ROWS, cw), F32),
                        pltpu.VMEM((PERM_ROWS, cw), F32),
                        pltpu.VMEM((PERM_ROWS, cw), F32),
                        pltpu.VMEM((PERM_ROWS, cw), F32)],
        compiler_params=pltpu.CompilerParams(
            dimension_semantics=("parallel", "parallel"), vmem_limit_bytes=VMEM_LIMIT),
        name="lru",
    )(xs, gs, conv_w, conv_b.reshape(1, -1), w_blk, ba, bx, a_param, h0, perm_t)


def _lru_gate_weights(wa, wx):
    return jnp.concatenate([jnp.concatenate([wa[0], wx[0]], axis=2),
                            jnp.concatenate([wa[1], wx[1]], axis=2)], axis=1).astype(BF16)


def _tail_kernel(x_ref, o_ref, l_ref, mr_ref, ml_ref, gate_ref, fnw_ref, wrd_ref, wld_ref, wo_ref, y_ref):
    y_ref[...] = _tail_math(x_ref[...], o_ref[...], l_ref[...], mr_ref[...], ml_ref[...], gate_ref[0],
                            fnw_ref[...], wrd_ref[...], wld_ref[...], wo_ref[...])


TAIL_TM = 512


def _tail(x2d, o2d, l2d, z2d, mod3, cond_row, fnw, wrd_b, wld_b, wo_b):
    m = x2d.shape[0]
    tm = TAIL_TM
    kmr, kml = MAIN_MRET // D_MODEL, MAIN_MLRU // D_MODEL
    const = lambda i: (0, 0)
    return pl.pallas_call(
        _tail_kernel,
        grid=(m // tm,),
        in_specs=[pl.BlockSpec((tm, D_MODEL), lambda i: (i, 0)),
                  pl.BlockSpec((tm, D_V), lambda i: (i, 0)),
                  pl.BlockSpec((tm, D_LRU), lambda i: (i, 0)),
                  pl.BlockSpec((tm, D_MODEL), lambda i: (i, kmr)),
                  pl.BlockSpec((tm, D_MODEL), lambda i: (i, kml)),
                  pl.BlockSpec((1, 1, D_MODEL), lambda i: (cond_row(i), 0, MOD_GATE)),
                  pl.BlockSpec((1, D_MODEL), const),
                  pl.BlockSpec((D_V, D_MODEL), const),
                  pl.BlockSpec((D_LRU, D_MODEL), const),
                  pl.BlockSpec((D_MODEL, D_MODEL), const)],
        out_specs=pl.BlockSpec((tm, D_MODEL), lambda i: (i, 0)),
        out_shape=jax.ShapeDtypeStruct((m, D_MODEL), F32),
        compiler_params=pltpu.CompilerParams(
            dimension_semantics=("parallel",), vmem_limit_bytes=VMEM_LIMIT),
        name="tail",
    )(x2d, o2d, l2d, z2d, z2d, mod3, fnw.reshape(1, -1), wrd_b, wld_b, wo_b)


def _trunk(x, mod3, cond_row0, n_cond, s0_ret, h0_lru, width, lru_nblk, params, final_norm_w, emit_state):
    (norm_w, w_all, decay_logit, wrd_b, conv_w, conv_b, w_blk, ba, bx, a_param, wld_b, wo_b, perm, perm_t) = params
    b, l, _ = x.shape
    groups = b // LRU_SEQS
    x2d = x.reshape(b * l, D_MODEL)
    tokens_per_cond = b * l // n_cond

    def cond_row(tile_tokens):
        return lambda i: cond_row0 + i // (tokens_per_cond // tile_tokens)

    z2d = _inproj(x2d, norm_w, mod3, cond_row(INPROJ_TM), w_all)
    z3 = z2d.reshape(b, l, D_MAIN)

    xs, gs = _inproj_lru(x.reshape(groups, LRU_SEQS, l, D_MODEL), norm_w, mod3, cond_row0, n_cond, perm, w_all)
    if h0_lru is None:
        h0 = jnp.zeros((groups, LRU_ROWS, D_LRU), F32)
    else:
        h0 = h0_lru.reshape(groups, LRU_SEQS, 2, D_LRU).transpose(0, 2, 1, 3).reshape(groups, LRU_ROWS, D_LRU)
    lru_pre, fin = _lru(xs, gs, conv_w, conv_b, w_blk, ba, bx, a_param, h0, perm_t, l, width, lru_nblk)
    fin = fin.reshape(groups, 2, LRU_SEQS, D_LRU)
    lru_fin = jnp.stack([fin[:, 1], fin[:, 0]], axis=2).reshape(b, 2, D_LRU)

    if l == RET_CHUNK:
        ret = _retention(z3, decay_logit, s0_ret, emit_state, N_HEADS,
                         tail=(x, lru_pre.reshape(b, l, D_LRU), mod3, cond_row(l), final_norm_w,
                               wrd_b, wld_b, wo_b))
        y = ret[0]
    else:
        ret = _retention(z3, decay_logit, s0_ret, emit_state, 1)
        y = _tail(x2d, ret[0].reshape(b * l, D_V), lru_pre.reshape(b * l, D_LRU), z2d, mod3, cond_row(TAIL_TM),
                  final_norm_w, wrd_b, wld_b, wo_b).reshape(b, l, D_MODEL)
    return y, (ret[1] if emit_state else None), lru_fin


def kernel(x_prompt, x_sample, state_ret, state_lru, c, c_ctx, norm_w, w_ada, b_ada, w_in, ret_decay_logit,
           ret_gn_w, w_ret_down, conv_w, conv_b, lru_wa, lru_ba, lru_wx, lru_bx, lru_a_param, w_lru_down,
           w_out, final_norm_w):
    assert norm_w.shape[0] == 1, "single-layer step"
    n_dec = c.shape[0]
    cond8 = jnp.concatenate([c.astype(F32), c_ctx.astype(F32)[None],
                             jnp.zeros((8 - n_dec - 1, D_MODEL), F32)], axis=0)
    mod = _ada(cond8, w_ada[0], b_ada[0])
    col = jnp.arange(D_IN)
    col_scale = jnp.where((col >= OFF_GRET) & (col < OFF_XLRU), 0.5, 1.0).astype(F32)
    w_all = (w_in[0] * col_scale[None, :]).astype(BF16)
    wrd_b = (ret_gn_w[0][:, None] * w_ret_down[0]).astype(BF16)
    perm = jnp.asarray(_slab_permutation(PERM_UNIT), BF16)
    perm_t = jnp.asarray(_slab_permutation(LRU_TC).T, BF16)
    params = (norm_w[0], w_all, ret_decay_logit[0], wrd_b, conv_w[0], conv_b[0],
              _lru_gate_weights(lru_wa[0], lru_wx[0]), lru_ba[0], lru_bx[0], lru_a_param[0],
              w_lru_down[0].astype(BF16), w_out[0].astype(BF16), perm, perm_t)
    mod3 = mod
    y_prompt, new_ret, new_lru = _trunk(x_prompt.astype(F32), mod3, n_dec, 1, None, None,
                                        x_prompt.shape[1], 5, params, final_norm_w, True)
    y_sample, _, _ = _trunk(x_sample.astype(F32), mod3, 0, n_dec, state_ret[:, 0], state_lru[:, 0],
                            GRID_W, 2, params, final_norm_w, False)
    return (y_prompt.astype(x_prompt.dtype), y_sample.astype(x_sample.dtype),
            new_ret[:, None].astype(state_ret.dtype), new_lru[:, None].astype(state_lru.dtype))
```

```python
import functools

import jax
import jax.numpy as jnp
import numpy as np
from jax import lax
from jax.experimental import pallas as pl
from jax.experimental.pallas import tpu as pltpu

F32 = jnp.float32
BF16 = jnp.bfloat16

D_MODEL = 1024
N_HEADS = 4
DK = 256
DV = 512
D_QK = N_HEADS * DK
D_V = N_HEADS * DV
N_LRU_BLOCKS = 10
LRU_BLOCK = 128
D_LRU = N_LRU_BLOCKS * LRU_BLOCK
LRU_C = 8.0
CONV_W = 4
CONV_LEFT = 2
GRID_W = 64
EPS = 1e-6

OFF_Q = 0
OFF_K = OFF_Q + D_QK
OFF_V = OFF_K + D_QK
OFF_GRET = OFF_V + D_V
OFF_XLRU = OFF_GRET + D_V
OFF_GLRU = OFF_XLRU + D_LRU
OFF_MRET = OFF_GLRU + D_LRU
OFF_MLRU = OFF_MRET + D_MODEL
D_IN = OFF_MLRU + D_MODEL
D_MAIN = OFF_XLRU + 2 * D_MODEL
MAIN_MRET = OFF_XLRU
MAIN_MLRU = OFF_XLRU + D_MODEL

RET_CHUNK = 256
LRU_SEQS = 4
LRU_ROWS = 2 * LRU_SEQS
LRU_TC = 64
PERM_ROWS = LRU_TC * LRU_ROWS
PERM_UNIT = 32
INPROJ_LRU_SLABS = 128
VMEM_LIMIT = 56 * 1024 * 1024


def _sigmoid(x):
    return 0.5 * jnp.tanh(0.5 * x) + 0.5


def _silu(x):
    return x * _sigmoid(x)


def _softplus(x):
    return jnp.maximum(x, 0.0) + jnp.log1p(jnp.exp(-jnp.abs(x)))


def _slab_permutation(tc):
    n = tc * LRU_ROWS
    p = np.zeros((n, n), np.float32)
    for t in range(tc):
        for r in range(LRU_ROWS):
            if r < LRU_SEQS:
                src = r * tc + t
            else:
                src = LRU_SEQS * tc + (r - LRU_SEQS) * tc + (tc - 1 - t)
            p[t * LRU_ROWS + r, src] = 1.0
    return p


def _ada_kernel(c_ref, w_ref, b_ref, o_ref):
    cond = _silu(c_ref[...]).astype(BF16)
    mod = jnp.dot(cond, w_ref[...].astype(BF16), preferred_element_type=F32) + b_ref[...]
    o_ref[...] = mod[:, None, :]


def _ada(cond8, w_ada, b_ada):
    tn = 1536
    return pl.pallas_call(
        _ada_kernel,
        grid=(3 * D_MODEL // tn,),
        in_specs=[pl.BlockSpec((8, D_MODEL), lambda j: (0, 0)),
                  pl.BlockSpec((D_MODEL, tn), lambda j: (0, j)),
                  pl.BlockSpec((1, tn), lambda j: (0, j))],
        out_specs=pl.BlockSpec((8, 1, tn), lambda j: (0, 0, j)),
        out_shape=jax.ShapeDtypeStruct((8, 1, 3 * D_MODEL), F32),
        name="ada",
    )(cond8, w_ada, b_ada.reshape(1, -1))


def _modulated_norm(x, nw, scale, shift):
    ms = jnp.mean(x * x, axis=-1, keepdims=True)
    return (x * lax.rsqrt(ms + EPS)) * (nw * (1.0 + scale)) + shift


INPROJ_TM = 1024
INPROJ_TN = 2048


def _inproj_kernel(x_ref, nw_ref, sc_ref, sh_ref, w_ref, *rest):
    if len(rest) == 2:
        z_ref, h_scr = rest
    else:
        gn_ref, wrd_ref, wld_ref, wo_ref, z_ref, wrd_out, wld_out, wo_out, h_scr = rest
        wrd_out[...] = (gn_ref[...] * wrd_ref[...]).astype(BF16)
        wld_out[...] = wld_ref[...].astype(BF16)
        wo_out[...] = wo_ref[...].astype(BF16)

    @pl.when(pl.program_id(1) == 0)
    def _():
        h_scr[...] = _modulated_norm(x_ref[...], nw_ref[...], sc_ref[0], sh_ref[0]).astype(BF16)

    z_ref[...] = jnp.dot(h_scr[...], w_ref[...], preferred_element_type=F32).astype(z_ref.dtype)


MOD_SHIFT, MOD_SCALE, MOD_GATE = 0, 1, 2


def _inproj(x2d, norm_w, mod3, cond_row, w_all, tail_weights=None):
    m = x2d.shape[0]
    tm, tn = INPROJ_TM, INPROJ_TN
    n_lead = OFF_XLRU // tn
    n_tiles = D_MAIN // tn
    assert OFF_XLRU == n_lead * tn and D_MAIN == n_tiles * tn

    def w_col(j):
        off = j * tn
        for t in range(n_lead, n_tiles):
            off = jnp.where(j == t, OFF_MRET + (t - n_lead) * tn, off)
        return off

    in_specs = [pl.BlockSpec((tm, D_MODEL), lambda i, j: (i, 0)),
                pl.BlockSpec((1, D_MODEL), lambda i, j: (0, 0)),
                pl.BlockSpec((1, 1, D_MODEL), lambda i, j: (cond_row(i), 0, MOD_SCALE)),
                pl.BlockSpec((1, 1, D_MODEL), lambda i, j: (cond_row(i), 0, MOD_SHIFT)),
                pl.BlockSpec((pl.Element(D_MODEL), pl.Element(tn)), lambda i, j: (0, w_col(j)))]
    args = [x2d, norm_w.reshape(1, -1), mod3, mod3, w_all]
    out_specs = [pl.BlockSpec((tm, tn), lambda i, j: (i, j))]
    out_shape = [jax.ShapeDtypeStruct((m, D_MAIN), BF16)]
    if tail_weights is not None:
        gn_w, wrd, wld, wo = tail_weights
        steps = (m // tm) * n_tiles

        def rows_per_step(w):
            assert w.shape[0] % (steps * 16) == 0
            return w.shape[0] // steps

        def step_rows(w, cols):
            return pl.BlockSpec((rows_per_step(w), cols), lambda i, j: (i * n_tiles + j, 0))

        in_specs += [step_rows(wrd, 1), step_rows(wrd, D_MODEL), step_rows(wld, D_MODEL), step_rows(wo, D_MODEL)]
        args += [gn_w.reshape(-1, 1), wrd, wld, wo]
        for w in (wrd, wld, wo):
            out_specs.append(step_rows(w, D_MODEL))
            out_shape.append(jax.ShapeDtypeStruct(w.shape, BF16))
    return pl.pallas_call(
        _inproj_kernel,
        grid=(m // tm, n_tiles),
        in_specs=in_specs,
        out_specs=out_specs,
        out_shape=out_shape,
        scratch_shapes=[pltpu.VMEM((tm, D_MODEL), BF16)],
        compiler_params=pltpu.CompilerParams(
            dimension_semantics=("arbitrary", "arbitrary"), vmem_limit_bytes=VMEM_LIMIT),
        name="inproj",
    )(*args)


def _inproj_lru_kernel(xa_ref, xb_ref, nw_ref, sc_ref, sh_ref, p_ref, w_ref, xs_ref, gs_ref):
    nw = nw_ref[...]
    sc = sc_ref[...]
    sh = sh_ref[...]
    ha = _modulated_norm(xa_ref[0], nw, sc, sh).astype(BF16)
    hb = _modulated_norm(xb_ref[0], nw, sc, sh).astype(BF16)
    units = INPROJ_LRU_SLABS // PERM_UNIT
    pieces = []
    for u in range(units):
        lo, hi = u * PERM_UNIT, (units - 1 - u) * PERM_UNIT
        src = jnp.concatenate([ha[b, lo:lo + PERM_UNIT] for b in range(LRU_SEQS)]
                              + [hb[b, hi:hi + PERM_UNIT] for b in range(LRU_SEQS)], axis=0)
        pieces.append(jnp.dot(p_ref[...], src, preferred_element_type=F32).astype(BF16))
    hp = jnp.concatenate(pieces, axis=0)
    z = jnp.dot(hp, w_ref[...], preferred_element_type=F32)
    xs_ref[0] = z[:, :D_LRU].astype(xs_ref.dtype)
    gs_ref[0] = _silu(z[:, D_LRU:]).astype(gs_ref.dtype)


def _inproj_lru(x4, norm_w, mod3, row0, nrows, perm, w_all):
    groups, _, l, _ = x4.shape
    assert row0 % nrows == 0 and nrows in (1, LRU_SEQS)
    ts = INPROJ_LRU_SLABS
    nt = l // 2 // ts
    ntb = l // ts
    out_sds = jax.ShapeDtypeStruct((groups, l // 2 * LRU_ROWS, D_LRU), BF16)
    return pl.pallas_call(
        _inproj_lru_kernel,
        grid=(groups, nt),
        in_specs=[pl.BlockSpec((1, LRU_SEQS, ts, D_MODEL), lambda g, i: (g, 0, i, 0)),
                  pl.BlockSpec((1, LRU_SEQS, ts, D_MODEL), lambda g, i: (g, 0, ntb - 1 - i, 0)),
                  pl.BlockSpec((1, D_MODEL), lambda g, i: (0, 0)),
                  pl.BlockSpec((nrows, 1, D_MODEL), lambda g, i: (row0 // nrows, 0, MOD_SCALE)),
                  pl.BlockSpec((nrows, 1, D_MODEL), lambda g, i: (row0 // nrows, 0, MOD_SHIFT)),
                  pl.BlockSpec((PERM_UNIT * LRU_ROWS, PERM_UNIT * LRU_ROWS), lambda g, i: (0, 0)),
                  pl.BlockSpec((pl.Element(D_MODEL), pl.Element(2 * D_LRU)), lambda g, i: (0, OFF_XLRU))],
        out_specs=[pl.BlockSpec((1, ts * LRU_ROWS, D_LRU), lambda g, i: (g, i, 0)),
                   pl.BlockSpec((1, ts * LRU_ROWS, D_LRU), lambda g, i: (g, i, 0))],
        out_shape=[out_sds, out_sds],
        compiler_params=pltpu.CompilerParams(
            dimension_semantics=("parallel", "parallel"), vmem_limit_bytes=VMEM_LIMIT),
        name="inproj_lru",
    )(x4, x4, norm_w.reshape(1, -1), mod3, mod3, perm, w_all)


def _dot_tn(a, b):
    return lax.dot_general(a, b, (((0,), (0,)), ((), ())), preferred_element_type=F32)


def _dot_nt(a, b):
    return lax.dot_general(a, b, (((1,), (1,)), ((), ())), preferred_element_type=F32)


def _tail_math(x, o, lru_pre, m_ret, m_lru, gate, fnw, wrd, wld, wo):
    ret_out = jnp.dot(o, wrd, preferred_element_type=F32)
    lru_out = jnp.dot(lru_pre, wld, preferred_element_type=F32)
    merged = _sigmoid(m_ret.astype(F32)) * ret_out + _sigmoid(m_lru.astype(F32)) * lru_out
    out = jnp.dot(merged.astype(BF16), wo, preferred_element_type=F32)
    y = x + gate * out
    ms = jnp.mean(y * y, axis=-1, keepdims=True)
    return y * lax.rsqrt(ms + EPS) * fnw


def _ret_kernel(*refs, nc, hp, has_state, emit_state, fuse_tail):
    dl_ref, q_ref, k_ref, v_ref, hg_ref = refs[:5]
    pos = 5
    s0_ref = None
    if has_state:
        s0_ref = refs[pos]
        pos += 1
    tail_refs = None
    if fuse_tail:
        tail_refs = refs[pos:pos + 9]
        pos += 9
    out_ref = refs[pos]
    pos += 1
    sfin_ref = None
    if emit_state:
        sfin_ref = refs[pos]
        pos += 1
    sf_scr, sb_scr, sbh_scr = refs[pos:pos + 3]
    o_scr = refs[pos + 3] if fuse_tail else None

    c = RET_CHUNK
    carry_states = has_state or nc > 1
    ii = lax.broadcasted_iota(jnp.int32, (c, c), 0)
    jj = lax.broadcasted_iota(jnp.int32, (c, c), 1)
    diff = (ii - jj).astype(F32)
    p = lax.broadcasted_iota(jnp.int32, (c, 1), 0).astype(F32)
    kscale = DK ** -0.5

    def rows_of(n):
        return pl.ds(pl.multiple_of(n * c, c), c)

    def normed_out(o, rows, vcols):
        ms = jnp.mean(o * o, axis=-1, keepdims=True)
        on = o * lax.rsqrt(ms + EPS)
        hg = hg_ref[0, rows, vcols]
        gate = hg * (jnp.tanh(hg) + 1.0)
        if fuse_tail:
            o_scr[rows, vcols] = on.astype(BF16) * gate
        else:
            out_ref[0, rows, vcols] = on.astype(out_ref.dtype) * gate

    for hh in range(hp):
        head = pl.program_id(1) * hp + hh
        qcols = slice(hh * DK, (hh + 1) * DK)
        vcols = slice(hh * DV, (hh + 1) * DV)
        lgf = -_softplus(-jnp.full((1, 1), dl_ref[0, head], F32))
        lgb = -_softplus(-jnp.full((1, 1), dl_ref[1, head], F32))
        decay = jnp.exp(jnp.where(diff >= 0, lgf * diff, -lgb * diff)) * kscale
        kdf = jnp.exp(lgf * (c - 1.0 - p)) * kscale
        kdb = jnp.exp(lgb * p) * kscale

        if not carry_states:
            rows = pl.ds(0, c)
            qn = q_ref[0, rows, qcols]
            kn = k_ref[0, rows, qcols]
            vn = v_ref[0, rows, vcols]
            s = (_dot_nt(qn, kn) * decay).astype(BF16)
            normed_out(jnp.dot(s, vn, preferred_element_type=F32), rows, vcols)
            k32 = kn.astype(F32)
            if emit_state:
                sfin_ref[0, 0, hh] = _dot_tn((k32 * kdf).astype(BF16), vn)
                sfin_ref[0, 1, hh] = _dot_tn((k32 * kdb).astype(BF16), vn)
            continue

        def row_table(col):
            return jnp.broadcast_to(col, (c, DK)).astype(BF16)

        qdf_t = row_table(jnp.exp(lgf * (p + 1.0)))
        qdb_t = row_table(jnp.exp(lgb * (c - p)))
        kdf_t = row_table(kdf)
        kdb_t = row_table(kdb)
        cdf = jnp.exp(lgf * c)
        cdb = jnp.exp(lgb * c)
        if has_state:
            sf_scr[...] = s0_ref[0, 0, hh]
            sb_scr[...] = s0_ref[0, 1, hh]
        else:
            sf_scr[...] = jnp.zeros_like(sf_scr)
            sb_scr[...] = jnp.zeros_like(sb_scr)

        def rev_body(idx, carry):
            n = nc - 1 - idx
            rows = rows_of(n)
            sbh_scr[n] = sb_scr[...].astype(BF16)
            kb = k_ref[0, rows, qcols] * kdb_t
            sb_scr[...] = cdb * sb_scr[...] + _dot_tn(kb, v_ref[0, rows, vcols])
            return carry

        lax.fori_loop(0, nc, rev_body, 0, unroll=True)

        def fwd_body(n, carry):
            rows = rows_of(n)
            qn = q_ref[0, rows, qcols]
            kn = k_ref[0, rows, qcols]
            vn = v_ref[0, rows, vcols]
            s = (_dot_nt(qn, kn) * decay).astype(BF16)
            o = (jnp.dot(s, vn, preferred_element_type=F32)
                 + jnp.dot(qn * qdf_t, sf_scr[...].astype(BF16), preferred_element_type=F32)
                 + jnp.dot(qn * qdb_t, sbh_scr[n], preferred_element_type=F32))
            normed_out(o, rows, vcols)
            sf_scr[...] = cdf * sf_scr[...] + _dot_tn(kn * kdf_t, vn)
            return carry

        lax.fori_loop(0, nc, fwd_body, 0, unroll=True)

        if emit_state:
            sfin_ref[0, 0, hh] = sf_scr[...]
            sfin_ref[0, 1, hh] = sb_scr[...]

    if fuse_tail:
        x_ref, l_ref, mr_ref, ml_ref, gate_ref, fnw_ref, wrd_ref, wld_ref, wo_ref = tail_refs
        out_ref[0] = _tail_math(x_ref[0], o_scr[...], l_ref[0], mr_ref[0], ml_ref[0], gate_ref[0], fnw_ref[...],
                                wrd_ref[...], wld_ref[...], wo_ref[...])


def _retention(z3, decay_logit, s0, emit_state, hp, tail=None):
    b, l, _ = z3.shape
    nc = l // RET_CHUNK
    has_state = s0 is not None
    fuse_tail = tail is not None
    assert not fuse_tail or hp == N_HEADS
    kq, kv_ = OFF_K // (hp * DK), OFF_V // (hp * DV)
    kg = OFF_GRET // (hp * DV)
    in_specs = [pl.BlockSpec(memory_space=pltpu.SMEM),
                pl.BlockSpec((1, l, hp * DK), lambda i, h: (i, 0, h)),
                pl.BlockSpec((1, l, hp * DK), lambda i, h: (i, 0, kq + h)),
                pl.BlockSpec((1, l, hp * DV), lambda i, h: (i, 0, kv_ + h)),
                pl.BlockSpec((1, l, hp * DV), lambda i, h: (i, 0, kg + h))]
    args = [decay_logit, z3, z3, z3, z3]
    if has_state:
        in_specs.append(pl.BlockSpec((1, 2, hp, DK, DV), lambda i, h: (i, 0, h, 0, 0)))
        args.append(s0)
    scratch = [pltpu.VMEM((DK, DV), F32), pltpu.VMEM((DK, DV), F32), pltpu.VMEM((nc, DK, DV), BF16)]
    if fuse_tail:
        x3, l3, mod3, cond_row, fnw, wrd_b, wld_b, wo_b = tail
        const = lambda i, h: (0, 0)
        in_specs += [pl.BlockSpec((1, l, D_MODEL), lambda i, h: (i, 0, 0)),
                     pl.BlockSpec((1, l, D_LRU), lambda i, h: (i, 0, 0)),
                     pl.BlockSpec((1, l, D_MODEL), lambda i, h: (i, 0, MAIN_MRET // D_MODEL)),
                     pl.BlockSpec((1, l, D_MODEL), lambda i, h: (i, 0, MAIN_MLRU // D_MODEL)),
                     pl.BlockSpec((1, 1, D_MODEL), lambda i, h: (cond_row(i), 0, MOD_GATE)),
                     pl.BlockSpec((1, D_MODEL), const),
                     pl.BlockSpec((D_V, D_MODEL), const),
                     pl.BlockSpec((D_LRU, D_MODEL), const),
                     pl.BlockSpec((D_MODEL, D_MODEL), const)]
        args += [x3, l3, z3, z3, mod3, fnw.reshape(1, -1), wrd_b, wld_b, wo_b]
        out_specs = [pl.BlockSpec((1, l, D_MODEL), lambda i, h: (i, 0, 0))]
        out_shape = [jax.ShapeDtypeStruct((b, l, D_MODEL), F32)]
        scratch.append(pltpu.VMEM((l, D_V), BF16))
    else:
        out_specs = [pl.BlockSpec((1, l, hp * DV), lambda i, h: (i, 0, h))]
        out_shape = [jax.ShapeDtypeStruct((b, l, D_V), BF16)]
    if emit_state:
        out_specs.append(pl.BlockSpec((1, 2, hp, DK, DV), lambda i, h: (i, 0, h, 0, 0)))
        out_shape.append(jax.ShapeDtypeStruct((b, 2, N_HEADS, DK, DV), F32))
    return pl.pallas_call(
        functools.partial(_ret_kernel, nc=nc, hp=hp, has_state=has_state, emit_state=emit_state,
                          fuse_tail=fuse_tail),
        grid=(b, N_HEADS // hp),
        in_specs=in_specs,
        out_specs=out_specs,
        out_shape=out_shape,
        scratch_shapes=scratch,
        compiler_params=pltpu.CompilerParams(
            dimension_semantics=("parallel", "parallel"), vmem_limit_bytes=VMEM_LIMIT),
        name="retention_tail" if fuse_tail else "retention",
    )(*args)


def _sqrt_unit(x):
    return x * lax.rsqrt(jnp.maximum(x, 1e-30))


def _lru_kernel(xs_ref, sg_ref, cw_ref, cb_ref, w_ref, ba_ref, bx_ref, ap_ref, h0_ref, pt_ref, out_ref, fin_ref,
                xp_scr, s1_scr, wm_scr, a0_scr, b0_scr, a1_scr, b1_scr, *, slabs, seq_len, width, nblk):
    r8 = LRU_ROWS
    tc = LRU_TC
    rows = PERM_ROWS
    cw = nblk * LRU_BLOCK
    nchunks = slabs // tc
    assert width == seq_len or width == tc

    def roll4(v):
        return pltpu.roll(v, LRU_SEQS, axis=0)

    xp_scr[pl.ds(0, 2 * r8), :] = jnp.zeros((2 * r8, cw), F32)
    xp_scr[pl.ds(2 * r8, slabs * r8), :] = xs_ref[0].astype(F32)
    xp_scr[pl.ds((slabs + 2) * r8, r8), :] = roll4(xs_ref[0, pl.ds((slabs - 1) * r8, r8), :].astype(F32))
    xp_scr[pl.ds((slabs + 3) * r8, r8), :] = roll4(xs_ref[0, pl.ds((slabs - 2) * r8, r8), :].astype(F32))

    row_cw = lax.broadcasted_iota(jnp.int32, (rows, cw), 0)
    step_in_chunk = row_cw >> 3
    low_cw = (row_cw & (r8 - 1)) < LRU_SEQS

    def row_pattern(lo, hi):
        n = lo.shape[-1]
        return jnp.where(lax.broadcasted_iota(jnp.int32, (r8, n), 0) < LRU_SEQS, lo, hi)

    def tap_rows(s):
        zero = jnp.zeros((1, cw), F32)
        lo = cw_ref[pl.ds(s + CONV_LEFT, 1), :] if 0 <= s + CONV_LEFT < CONV_W else zero
        hi = cw_ref[pl.ds(CONV_LEFT - s, 1), :] if 0 <= CONV_LEFT - s < CONV_W else zero
        return 0.5 * row_pattern(lo, hi)

    def tap_table(s):
        tap = jnp.broadcast_to(tap_rows(s)[None], (tc, r8, cw)).reshape(rows, cw)
        if width != seq_len:
            tap = jnp.where((step_in_chunk + s >= 0) & (step_in_chunk + s < tc), tap, 0.0)
        return tap

    wm_scr[0] = tap_table(-2) + tap_table(2)
    for s in range(-1, 2):
        wm_scr[s + 2] = tap_table(s)

    ap8 = row_pattern(ap_ref[pl.ds(0, 1), :], ap_ref[pl.ds(1, 1), :])
    c1_lo = (-0.5 * LRU_C / np.log(2.0)) * _softplus(-ap8)
    cb_half = 0.5 * cb_ref[...]

    def half_bias(j):
        cols = slice(j * LRU_BLOCK, (j + 1) * LRU_BLOCK)
        return 0.5 * jnp.concatenate(
            [row_pattern(ba_ref[pl.ds(0, 1), cols], ba_ref[pl.ds(1, 1), cols]),
             row_pattern(bx_ref[pl.ds(0, 1), cols], bx_ref[pl.ds(1, 1), cols])], axis=1)

    bias_lo = [half_bias(j) for j in range(nblk)]
    low_rows = (lax.broadcasted_iota(jnp.int32, (rows, LRU_BLOCK), 0) & (r8 - 1)) < LRU_SEQS
    high_rows = jnp.logical_not(low_rows)

    def conv_half(t0):
        def shifted(s):
            return xp_scr[pl.ds(pl.multiple_of((t0 + s + 2) * r8, r8), rows), :]

        xh = jnp.where(low_cw, shifted(-2), shifted(2)) * wm_scr[0] + cb_half
        for s in range(-1, 2):
            xh = xh + shifted(s) * wm_scr[s + 2]
        xp_scr[pl.ds(pl.multiple_of(t0 * r8, rows), rows), :] = xh
        return xh

    def gates(xh, sweep2, a_scr, b_scr):
        fj = high_rows if sweep2 else low_rows
        c1 = roll4(c1_lo) if sweep2 else c1_lo
        for j in range(nblk):
            cols = slice(j * LRU_BLOCK, (j + 1) * LRU_BLOCK)
            xj = xh[:, cols]
            lhs = jnp.concatenate([jnp.where(fj, xj, 0.0), jnp.where(fj, 0.0, xj)], axis=1)
            pre = jnp.dot(lhs.astype(BF16), w_ref[j], preferred_element_type=F32)
            bj = roll4(bias_lo[j]) if sweep2 else bias_lo[j]
            pre = pre.reshape(tc, r8, 2 * LRU_BLOCK) + bj[None]
            tr = jnp.tanh(pre[:, :, :LRU_BLOCK])
            tg = jnp.tanh(pre[:, :, LRU_BLOCK:])
            c1j = c1[:, cols][None]
            a = jnp.exp2(c1j * tr + c1j)
            bco = _sqrt_unit(1.0 - a * a) * ((tg + 1.0) * xj.reshape(tc, r8, LRU_BLOCK))
            a_scr[:, cols] = a.reshape(rows, LRU_BLOCK)
            b_scr[:, cols] = bco.reshape(rows, LRU_BLOCK)

    bufs = ((a0_scr, b0_scr), (a1_scr, b1_scr))

    def chunk_rows(t0):
        return pl.ds(pl.multiple_of(t0 * r8, rows), rows)

    def scan(buf, h, dst_ref, t0, descending):
        a_ref, b_ref = bufs[buf]
        for i in range(tc):
            t = tc - 1 - i if descending else i
            h = a_ref[t * r8:(t + 1) * r8, :] * h + b_ref[t * r8:(t + 1) * r8, :]
            if dst_ref is None:
                b_ref[t * r8:(t + 1) * r8, :] = h
            else:
                dst_ref[pl.ds(pl.multiple_of(t0 * r8, rows) + t * r8, r8), :] = h
        return h

    def gates1(ci, buf):
        gates(conv_half(ci * tc), False, *bufs[buf])

    def scan1(ci, buf, h):
        return scan(buf, h, s1_scr, ci * tc, False)

    npairs = nchunks // 2
    assert nchunks == 2 * npairs
    gates1(0, 0)

    def sweep1_pair(i, h):
        gates1(2 * i + 1, 1)
        h = scan1(2 * i, 0, h)
        gates1(2 * i + 2, 0)
        return scan1(2 * i + 1, 1, h)

    h = lax.fori_loop(0, npairs - 1, sweep1_pair, h0_ref[0])
    gates1(nchunks - 1, 1)
    h = scan1(nchunks - 2, 0, h)
    h = scan1(nchunks - 1, 1, h)

    def gates2(p, buf):
        gates(xp_scr[chunk_rows((nchunks - 1 - p) * tc), :], True, *bufs[buf])

    def scan2(p, buf, h):
        return scan(buf, h, None, 0, True)

    def finish(p, buf):
        t0 = (nchunks - 1 - p) * tc
        crows = chunk_rows(t0)
        o = (s1_scr[crows, :] + bufs[buf][1][...]).astype(BF16) * sg_ref[0, crows, :]
        nat = jnp.dot(pt_ref[...], o, preferred_element_type=F32).astype(out_ref.dtype)
        lo = pl.multiple_of(t0, tc)
        hi = pl.multiple_of(seq_len - tc - t0, tc)
        for b in range(LRU_SEQS):
            out_ref[0, b, pl.ds(lo, tc), :] = nat[b * tc:(b + 1) * tc]
            out_ref[0, b, pl.ds(hi, tc), :] = nat[(LRU_SEQS + b) * tc:(LRU_SEQS + b + 1) * tc]

    h = roll4(h)
    gates2(0, 0)
    gates2(1, 1)
    h = scan2(0, 0, h)

    def sweep2_pair(i, h):
        finish(2 * i, 0)
        gates2(2 * i + 2, 0)
        h = scan2(2 * i + 1, 1, h)
        finish(2 * i + 1, 1)
        gates2(2 * i + 3, 1)
        return scan2(2 * i + 2, 0, h)

    h = lax.fori_loop(0, npairs - 1, sweep2_pair, h)
    finish(nchunks - 2, 0)
    h = scan2(nchunks - 1, 1, h)
    finish(nchunks - 1, 1)
    fin_ref[0] = h


def _lru(xs, gs, conv_w, conv_b, w_blk, ba, bx, a_param, h0, perm_t, seq_len, width, nblk):
    groups = xs.shape[0]
    r8 = LRU_ROWS
    slabs = seq_len // 2
    cw = nblk * LRU_BLOCK
    per_dir = pl.BlockSpec((2, cw), lambda g, c: (0, c))
    return pl.pallas_call(
        functools.partial(_lru_kernel, slabs=slabs, seq_len=seq_len, width=width, nblk=nblk),
        grid=(groups, N_LRU_BLOCKS // nblk),
        in_specs=[pl.BlockSpec((1, slabs * r8, cw), lambda g, c: (g, 0, c)),
                  pl.BlockSpec((1, slabs * r8, cw), lambda g, c: (g, 0, c)),
                  pl.BlockSpec((CONV_W, cw), lambda g, c: (0, c)),
                  pl.BlockSpec((1, cw), lambda g, c: (0, c)),
                  pl.BlockSpec((nblk, 2 * LRU_BLOCK, 2 * LRU_BLOCK), lambda g, c: (c, 0, 0)),
                  per_dir, per_dir, per_dir,
                  pl.BlockSpec((1, r8, cw), lambda g, c: (g, 0, c)),
                  pl.BlockSpec((PERM_ROWS, PERM_ROWS), lambda g, c: (0, 0))],
        out_specs=[pl.BlockSpec((1, LRU_SEQS, seq_len, cw), lambda g, c: (g, 0, 0, c)),
                   pl.BlockSpec((1, r8, cw), lambda g, c: (g, 0, c))],
        out_shape=[jax.ShapeDtypeStruct((groups, LRU_SEQS, seq_len, D_LRU), BF16),
                   jax.ShapeDtypeStruct((groups, r8, D_LRU), F32)],
        scratch_shapes=[pltpu.VMEM(((slabs + 4) * r8, cw), F32),
                        pltpu.VMEM((slabs * r8, cw), F32),
                        pltpu.VMEM((4, PERM_ROWS, cw), F32),
                        pltpu.VMEM((PERM_ROWS, cw), F32),
                        pltpu.VMEM((PERM_ROWS, cw), F32),
                        pltpu.VMEM((PERM_ROWS, cw), F32),
                        pltpu.VMEM((PERM_ROWS, cw), F32)],
        compiler_params=pltpu.CompilerParams(
            dimension_semantics=("parallel", "parallel"), vmem_limit_bytes=VMEM_LIMIT),
        name="lru",
    )(xs, gs, conv_w, conv_b.reshape(1, -1), w_blk, ba, bx, a_param, h0, perm_t)


def _lru_gate_weights(wa, wx):
    return jnp.concatenate([jnp.concatenate([wa[0], wx[0]], axis=2),
                            jnp.concatenate([wa[1], wx[1]], axis=2)], axis=1).astype(BF16)


def _tail_kernel(x_ref, o_ref, l_ref, mr_ref, ml_ref, gate_ref, fnw_ref, wrd_ref, wld_ref, wo_ref, y_ref):
    y_ref[...] = _tail_math(x_ref[...], o_ref[...], l_ref[...], mr_ref[...], ml_ref[...], gate_ref[0],
                            fnw_ref[...], wrd_ref[...], wld_ref[...], wo_ref[...])


TAIL_TM = 512


def _tail(x2d, o2d, l2d, z2d, mod3, cond_row, fnw, wrd_b, wld_b, wo_b):
    m = x2d.shape[0]
    tm = TAIL_TM
    kmr, kml = MAIN_MRET // D_MODEL, MAIN_MLRU // D_MODEL
    const = lambda i: (0, 0)
    return pl.pallas_call(
        _tail_kernel,
        grid=(m // tm,),
        in_specs=[pl.BlockSpec((tm, D_MODEL), lambda i: (i, 0)),
                  pl.BlockSpec((tm, D_V), lambda i: (i, 0)),
                  pl.BlockSpec((tm, D_LRU), lambda i: (i, 0)),
                  pl.BlockSpec((tm, D_MODEL), lambda i: (i, kmr)),
                  pl.BlockSpec((tm, D_MODEL), lambda i: (i, kml)),
                  pl.BlockSpec((1, 1, D_MODEL), lambda i: (cond_row(i), 0, MOD_GATE)),
                  pl.BlockSpec((1, D_MODEL), const),
                  pl.BlockSpec((D_V, D_MODEL), const),
                  pl.BlockSpec((D_LRU, D_MODEL), const),
                  pl.BlockSpec((D_MODEL, D_MODEL), const)],
        out_specs=pl.BlockSpec((tm, D_MODEL), lambda i: (i, 0)),
        out_shape=jax.ShapeDtypeStruct((m, D_MODEL), F32),
        compiler_params=pltpu.CompilerParams(
            dimension_semantics=("parallel",), vmem_limit_bytes=VMEM_LIMIT),
        name="tail",
    )(x2d, o2d, l2d, z2d, z2d, mod3, fnw.reshape(1, -1), wrd_b, wld_b, wo_b)


def _trunk(x, mod3, cond_row0, n_cond, s0_ret, h0_lru, width, lru_nblk, params, tail_w, final_norm_w,
           emit_state):
    (norm_w, w_all, decay_logit, conv_w, conv_b, w_blk, ba, bx, a_param, perm, perm_t) = params
    b, l, _ = x.shape
    groups = b // LRU_SEQS
    x2d = x.reshape(b * l, D_MODEL)
    tokens_per_cond = b * l // n_cond

    def cond_row(tile_tokens):
        return lambda i: cond_row0 + i // (tokens_per_cond // tile_tokens)

    if len(tail_w) == 4:
        z2d, *tail_w = _inproj(x2d, norm_w, mod3, cond_row(INPROJ_TM), w_all, tail_weights=tail_w)
    else:
        z2d, = _inproj(x2d, norm_w, mod3, cond_row(INPROJ_TM), w_all)
    wrd_b, wld_b, wo_b = tail_w
    z3 = z2d.reshape(b, l, D_MAIN)

    xs, gs = _inproj_lru(x.reshape(groups, LRU_SEQS, l, D_MODEL), norm_w, mod3, cond_row0, n_cond, perm, w_all)
    if h0_lru is None:
        h0 = jnp.zeros((groups, LRU_ROWS, D_LRU), F32)
    else:
        h0 = h0_lru.reshape(groups, LRU_SEQS, 2, D_LRU).transpose(0, 2, 1, 3).reshape(groups, LRU_ROWS, D_LRU)
    lru_pre, fin = _lru(xs, gs, conv_w, conv_b, w_blk, ba, bx, a_param, h0, perm_t, l, width, lru_nblk)
    fin = fin.reshape(groups, 2, LRU_SEQS, D_LRU)
    lru_fin = jnp.stack([fin[:, 1], fin[:, 0]], axis=2).reshape(b, 2, D_LRU)

    if l == RET_CHUNK:
        ret = _retention(z3, decay_logit, s0_ret, emit_state, N_HEADS,
                         tail=(x, lru_pre.reshape(b, l, D_LRU), mod3, cond_row(l), final_norm_w,
                               wrd_b, wld_b, wo_b))
        y = ret[0]
    else:
        ret = _retention(z3, decay_logit, s0_ret, emit_state, 1)
        y = _tail(x2d, ret[0].reshape(b * l, D_V), lru_pre.reshape(b * l, D_LRU), z2d, mod3, cond_row(TAIL_TM),
                  final_norm_w, wrd_b, wld_b, wo_b).reshape(b, l, D_MODEL)
    return y, (ret[1] if emit_state else None), lru_fin, (wrd_b, wld_b, wo_b)


def kernel(x_prompt, x_sample, state_ret, state_lru, c, c_ctx, norm_w, w_ada, b_ada, w_in, ret_decay_logit,
           ret_gn_w, w_ret_down, conv_w, conv_b, lru_wa, lru_ba, lru_wx, lru_bx, lru_a_param, w_lru_down,
           w_out, final_norm_w):
    assert norm_w.shape[0] == 1, "single-layer step"
    n_dec = c.shape[0]
    cond8 = jnp.concatenate([c.astype(F32), c_ctx.astype(F32)[None],
                             jnp.zeros((8 - n_dec - 1, D_MODEL), F32)], axis=0)
    mod = _ada(cond8, w_ada[0], b_ada[0])
    col = jnp.arange(D_IN)
    col_scale = jnp.where((col >= OFF_GRET) & (col < OFF_XLRU), 0.5, 1.0).astype(F32)
    w_all = (w_in[0] * col_scale[None, :]).astype(BF16)
    perm = jnp.asarray(_slab_permutation(PERM_UNIT), BF16)
    perm_t = jnp.asarray(_slab_permutation(LRU_TC).T, BF16)
    params = (norm_w[0], w_all, ret_decay_logit[0], conv_w[0], conv_b[0],
              _lru_gate_weights(lru_wa[0], lru_wx[0]), lru_ba[0], lru_bx[0], lru_a_param[0], perm, perm_t)
    mod3 = mod
    tail_f32 = (ret_gn_w[0], w_ret_down[0], w_lru_down[0], w_out[0])
    y_prompt, new_ret, new_lru, tail_b = _trunk(x_prompt.astype(F32), mod3, n_dec, 1, None, None,
                                                x_prompt.shape[1], 5, params, tail_f32, final_norm_w, True)
    y_sample, _, _, _ = _trunk(x_sample.astype(F32), mod3, 0, n_dec, state_ret[:, 0], state_lru[:, 0],
                               GRID_W, 2, params, tail_b, final_norm_w, False)
    return (y_prompt.astype(x_prompt.dtype), y_sample.astype(x_sample.dtype),
            new_ret[:, None].astype(state_ret.dtype), new_lru[:, None].astype(state_lru.dtype))
```

```python
import functools

import jax
import jax.numpy as jnp
import numpy as np
from jax import lax
from jax.experimental import pallas as pl
from jax.experimental.pallas import tpu as pltpu

F32 = jnp.float32
BF16 = jnp.bfloat16

D_MODEL = 1024
N_HEADS = 4
DK = 256
DV = 512
D_QK = N_HEADS * DK
D_V = N_HEADS * DV
N_LRU_BLOCKS = 10
LRU_BLOCK = 128
D_LRU = N_LRU_BLOCKS * LRU_BLOCK
LRU_C = 8.0
CONV_W = 4
CONV_LEFT = 2
GRID_W = 64
EPS = 1e-6

OFF_Q = 0
OFF_K = OFF_Q + D_QK
OFF_V = OFF_K + D_QK
OFF_GRET = OFF_V + D_V
OFF_XLRU = OFF_GRET + D_V
OFF_GLRU = OFF_XLRU + D_LRU
OFF_MRET = OFF_GLRU + D_LRU
OFF_MLRU = OFF_MRET + D_MODEL
D_IN = OFF_MLRU + D_MODEL
D_MAIN = OFF_XLRU + 2 * D_MODEL
MAIN_MRET = OFF_XLRU
MAIN_MLRU = OFF_XLRU + D_MODEL

RET_CHUNK = 256
LRU_SEQS = 4
LRU_ROWS = 2 * LRU_SEQS
LRU_TC = 64
PERM_ROWS = LRU_TC * LRU_ROWS
PERM_UNIT = 32
INPROJ_LRU_SLABS = 128
VMEM_LIMIT = 56 * 1024 * 1024


def _sigmoid(x):
    return 0.5 * jnp.tanh(0.5 * x) + 0.5


def _silu(x):
    return x * _sigmoid(x)


def _softplus(x):
    return jnp.maximum(x, 0.0) + jnp.log1p(jnp.exp(-jnp.abs(x)))


def _slab_permutation(tc):
    n = tc * LRU_ROWS
    p = np.zeros((n, n), np.float32)
    for t in range(tc):
        for r in range(LRU_ROWS):
            if r < LRU_SEQS:
                src = r * tc + t
            else:
                src = LRU_SEQS * tc + (r - LRU_SEQS) * tc + (tc - 1 - t)
            p[t * LRU_ROWS + r, src] = 1.0
    return p


def _ada_kernel(c_ref, w_ref, b_ref, o_ref):
    cond = _silu(c_ref[...]).astype(BF16)
    mod = jnp.dot(cond, w_ref[...].astype(BF16), preferred_element_type=F32) + b_ref[...]
    o_ref[...] = mod[:, None, :]


def _ada(cond8, w_ada, b_ada):
    tn = 1536
    return pl.pallas_call(
        _ada_kernel,
        grid=(3 * D_MODEL // tn,),
        in_specs=[pl.BlockSpec((8, D_MODEL), lambda j: (0, 0)),
                  pl.BlockSpec((D_MODEL, tn), lambda j: (0, j)),
                  pl.BlockSpec((1, tn), lambda j: (0, j))],
        out_specs=pl.BlockSpec((8, 1, tn), lambda j: (0, 0, j)),
        out_shape=jax.ShapeDtypeStruct((8, 1, 3 * D_MODEL), F32),
        name="ada",
    )(cond8, w_ada, b_ada.reshape(1, -1))


def _modulated_norm(x, nw, scale, shift):
    ms = jnp.mean(x * x, axis=-1, keepdims=True)
    return (x * lax.rsqrt(ms + EPS)) * (nw * (1.0 + scale)) + shift


INPROJ_TM = 1024
INPROJ_TN = 2048


def _inproj_kernel(x_ref, nw_ref, sc_ref, sh_ref, w_ref, *rest):
    if len(rest) == 2:
        z_ref, h_scr = rest
    else:
        gn_ref, wrd_ref, wld_ref, wo_ref, z_ref, wrd_out, wld_out, wo_out, h_scr = rest
        wrd_out[...] = (gn_ref[...] * wrd_ref[...]).astype(BF16)
        wld_out[...] = wld_ref[...].astype(BF16)
        wo_out[...] = wo_ref[...].astype(BF16)

    @pl.when(pl.program_id(1) == 0)
    def _():
        h_scr[...] = _modulated_norm(x_ref[...], nw_ref[...], sc_ref[0], sh_ref[0]).astype(BF16)

    z_ref[...] = jnp.dot(h_scr[...], w_ref[...], preferred_element_type=F32).astype(z_ref.dtype)


MOD_SHIFT, MOD_SCALE, MOD_GATE = 0, 1, 2


def _inproj(x2d, norm_w, mod3, cond_row, w_all, tail_weights=None):
    m = x2d.shape[0]
    tm, tn = INPROJ_TM, INPROJ_TN
    n_lead = OFF_XLRU // tn
    n_tiles = D_MAIN // tn
    assert OFF_XLRU == n_lead * tn and D_MAIN == n_tiles * tn

    def w_col(j):
        off = j * tn
        for t in range(n_lead, n_tiles):
            off = jnp.where(j == t, OFF_MRET + (t - n_lead) * tn, off)
        return off

    in_specs = [pl.BlockSpec((tm, D_MODEL), lambda i, j: (i, 0)),
                pl.BlockSpec((1, D_MODEL), lambda i, j: (0, 0)),
                pl.BlockSpec((1, 1, D_MODEL), lambda i, j: (cond_row(i), 0, MOD_SCALE)),
                pl.BlockSpec((1, 1, D_MODEL), lambda i, j: (cond_row(i), 0, MOD_SHIFT)),
                pl.BlockSpec((pl.Element(D_MODEL), pl.Element(tn)), lambda i, j: (0, w_col(j)))]
    args = [x2d, norm_w.reshape(1, -1), mod3, mod3, w_all]
    out_specs = [pl.BlockSpec((tm, tn), lambda i, j: (i, j))]
    out_shape = [jax.ShapeDtypeStruct((m, D_MAIN), BF16)]
    if tail_weights is not None:
        gn_w, wrd, wld, wo = tail_weights
        steps = (m // tm) * n_tiles

        def rows_per_step(w):
            assert w.shape[0] % (steps * 16) == 0
            return w.shape[0] // steps

        def step_rows(w, cols):
            return pl.BlockSpec((rows_per_step(w), cols), lambda i, j: (i * n_tiles + j, 0))

        in_specs += [step_rows(wrd, 1), step_rows(wrd, D_MODEL), step_rows(wld, D_MODEL), step_rows(wo, D_MODEL)]
        args += [gn_w.reshape(-1, 1), wrd, wld, wo]
        for w in (wrd, wld, wo):
            out_specs.append(step_rows(w, D_MODEL))
            out_shape.append(jax.ShapeDtypeStruct(w.shape, BF16))
    return pl.pallas_call(
        _inproj_kernel,
        grid=(m // tm, n_tiles),
        in_specs=in_specs,
        out_specs=out_specs,
        out_shape=out_shape,
        scratch_shapes=[pltpu.VMEM((tm, D_MODEL), BF16)],
        compiler_params=pltpu.CompilerParams(
            dimension_semantics=("arbitrary", "arbitrary"), vmem_limit_bytes=VMEM_LIMIT),
        name="inproj",
    )(*args)


def _inproj_lru_kernel(xa_ref, xb_ref, nw_ref, sc_ref, sh_ref, p_ref, w_ref, xs_ref, gs_ref):
    nw = nw_ref[...]
    sc = sc_ref[...]
    sh = sh_ref[...]
    ha = _modulated_norm(xa_ref[0], nw, sc, sh).astype(BF16)
    hb = _modulated_norm(xb_ref[0], nw, sc, sh).astype(BF16)
    units = INPROJ_LRU_SLABS // PERM_UNIT
    pieces = []
    for u in range(units):
        lo, hi = u * PERM_UNIT, (units - 1 - u) * PERM_UNIT
        src = jnp.concatenate([ha[b, lo:lo + PERM_UNIT] for b in range(LRU_SEQS)]
                              + [hb[b, hi:hi + PERM_UNIT] for b in range(LRU_SEQS)], axis=0)
        pieces.append(jnp.dot(p_ref[...], src, preferred_element_type=F32).astype(BF16))
    hp = jnp.concatenate(pieces, axis=0)
    z = jnp.dot(hp, w_ref[...], preferred_element_type=F32)
    xs_ref[0] = z[:, :D_LRU].astype(xs_ref.dtype)
    gs_ref[0] = _silu(z[:, D_LRU:]).astype(gs_ref.dtype)


def _inproj_lru(x4, norm_w, mod3, row0, nrows, perm, w_all):
    groups, _, l, _ = x4.shape
    assert row0 % nrows == 0 and nrows in (1, LRU_SEQS)
    ts = INPROJ_LRU_SLABS
    nt = l // 2 // ts
    ntb = l // ts
    out_sds = jax.ShapeDtypeStruct((groups, l // 2 * LRU_ROWS, D_LRU), BF16)
    return pl.pallas_call(
        _inproj_lru_kernel,
        grid=(groups, nt),
        in_specs=[pl.BlockSpec((1, LRU_SEQS, ts, D_MODEL), lambda g, i: (g, 0, i, 0)),
                  pl.BlockSpec((1, LRU_SEQS, ts, D_MODEL), lambda g, i: (g, 0, ntb - 1 - i, 0)),
                  pl.BlockSpec((1, D_MODEL), lambda g, i: (0, 0)),
                  pl.BlockSpec((nrows, 1, D_MODEL), lambda g, i: (row0 // nrows, 0, MOD_SCALE)),
                  pl.BlockSpec((nrows, 1, D_MODEL), lambda g, i: (row0 // nrows, 0, MOD_SHIFT)),
                  pl.BlockSpec((PERM_UNIT * LRU_ROWS, PERM_UNIT * LRU_ROWS), lambda g, i: (0, 0)),
                  pl.BlockSpec((pl.Element(D_MODEL), pl.Element(2 * D_LRU)), lambda g, i: (0, OFF_XLRU))],
        out_specs=[pl.BlockSpec((1, ts * LRU_ROWS, D_LRU), lambda g, i: (g, i, 0)),
                   pl.BlockSpec((1, ts * LRU_ROWS, D_LRU), lambda g, i: (g, i, 0))],
        out_shape=[out_sds, out_sds],
        compiler_params=pltpu.CompilerParams(
            dimension_semantics=("parallel", "parallel"), vmem_limit_bytes=VMEM_LIMIT),
        name="inproj_lru",
    )(x4, x4, norm_w.reshape(1, -1), mod3, mod3, perm, w_all)


def _dot_tn(a, b):
    return lax.dot_general(a, b, (((0,), (0,)), ((), ())), preferred_element_type=F32)


def _dot_nt(a, b):
    return lax.dot_general(a, b, (((1,), (1,)), ((), ())), preferred_element_type=F32)


def _tail_math(x, o, lru_pre, m_ret, m_lru, gate, fnw, wrd, wld, wo):
    ret_out = jnp.dot(o, wrd, preferred_element_type=F32)
    lru_out = jnp.dot(lru_pre, wld, preferred_element_type=F32)
    merged = _sigmoid(m_ret.astype(F32)) * ret_out + _sigmoid(m_lru.astype(F32)) * lru_out
    out = jnp.dot(merged.astype(BF16), wo, preferred_element_type=F32)
    y = x + gate * out
    ms = jnp.mean(y * y, axis=-1, keepdims=True)
    return y * lax.rsqrt(ms + EPS) * fnw


RING_SLOTS = 3


def _ret_kernel(*refs, nc, hp, has_state, emit_state, fuse_tail, ring):
    dl_ref = refs[0]
    if ring:
        z_hbm = refs[1]
        pos = 2
    else:
        q_ref, k_ref, v_ref, hg_ref = refs[1:5]
        pos = 5
    s0_ref = None
    if has_state:
        s0_ref = refs[pos]
        pos += 1
    tail_refs = None
    if fuse_tail:
        tail_refs = refs[pos:pos + 9]
        pos += 9
    out_ref = refs[pos]
    pos += 1
    sfin_ref = None
    if emit_state:
        sfin_ref = refs[pos]
        pos += 1
    sf_scr, sb_scr, sbh_scr = refs[pos:pos + 3]
    pos += 3
    o_scr = None
    if fuse_tail:
        o_scr = refs[pos]
        pos += 1

    if ring:
        assert hp == 1 and not fuse_tail
        bufs = refs[pos:pos + 4]
        sem = refs[pos + 4]
        nh = pl.num_programs(1)
        n_steps = pl.num_programs(0) * nh
        step = pl.program_id(0) * nh + pl.program_id(1)
        streams = ((OFF_Q, DK), (OFF_K, DK), (OFF_V, DV), (OFF_GRET, DV))

        def ring_copies(st, slot):
            seq, hd = st // nh, st % nh
            return [pltpu.make_async_copy(
                        z_hbm.at[seq, :, pl.ds(pl.multiple_of(off + hd * w, 128), w)],
                        buf.at[slot], sem.at[slot, k])
                    for k, ((off, w), buf) in enumerate(zip(streams, bufs))]

        @pl.when(step == 0)
        def _():
            for st in range(RING_SLOTS - 1):
                for cp in ring_copies(st, st):
                    cp.start()

        @pl.when(step + RING_SLOTS - 1 < n_steps)
        def _():
            for cp in ring_copies(step + RING_SLOTS - 1, (step + RING_SLOTS - 1) % RING_SLOTS):
                cp.start()

        slot = step % RING_SLOTS
        for cp in ring_copies(step, slot):
            cp.wait()
        q_at = lambda rows, cols: bufs[0][slot, rows, :]
        k_at = lambda rows, cols: bufs[1][slot, rows, :]
        v_at = lambda rows, cols: bufs[2][slot, rows, :]
        hg_at = lambda rows, cols: bufs[3][slot, rows, :]
    else:
        q_at = lambda rows, cols: q_ref[0, rows, cols]
        k_at = lambda rows, cols: k_ref[0, rows, cols]
        v_at = lambda rows, cols: v_ref[0, rows, cols]
        hg_at = lambda rows, cols: hg_ref[0, rows, cols]

    c = RET_CHUNK
    carry_states = has_state or nc > 1
    ii = lax.broadcasted_iota(jnp.int32, (c, c), 0)
    jj = lax.broadcasted_iota(jnp.int32, (c, c), 1)
    diff = (ii - jj).astype(F32)
    p = lax.broadcasted_iota(jnp.int32, (c, 1), 0).astype(F32)
    kscale = DK ** -0.5

    def rows_of(n):
        return pl.ds(pl.multiple_of(n * c, c), c)

    def normed_out(o, rows, vcols):
        ms = jnp.mean(o * o, axis=-1, keepdims=True)
        on = o * lax.rsqrt(ms + EPS)
        hg = hg_at(rows, vcols)
        gate = hg * (jnp.tanh(hg) + 1.0)
        if fuse_tail:
            o_scr[rows, vcols] = on.astype(BF16) * gate
        else:
            out_ref[0, rows, vcols] = on.astype(out_ref.dtype) * gate

    for hh in range(hp):
        head = pl.program_id(1) * hp + hh
        qcols = slice(hh * DK, (hh + 1) * DK)
        vcols = slice(hh * DV, (hh + 1) * DV)
        lgf = -_softplus(-jnp.full((1, 1), dl_ref[0, head], F32))
        lgb = -_softplus(-jnp.full((1, 1), dl_ref[1, head], F32))
        decay = jnp.exp(jnp.where(diff >= 0, lgf * diff, -lgb * diff)) * kscale
        kdf = jnp.exp(lgf * (c - 1.0 - p)) * kscale
        kdb = jnp.exp(lgb * p) * kscale

        if not carry_states:
            rows = pl.ds(0, c)
            qn = q_at(rows, qcols)
            kn = k_at(rows, qcols)
            vn = v_at(rows, vcols)
            s = (_dot_nt(qn, kn) * decay).astype(BF16)
            normed_out(jnp.dot(s, vn, preferred_element_type=F32), rows, vcols)
            k32 = kn.astype(F32)
            if emit_state:
                sfin_ref[0, 0, hh] = _dot_tn((k32 * kdf).astype(BF16), vn)
                sfin_ref[0, 1, hh] = _dot_tn((k32 * kdb).astype(BF16), vn)
            continue

        def row_table(col):
            return jnp.broadcast_to(col, (c, DK)).astype(BF16)

        qdf_t = row_table(jnp.exp(lgf * (p + 1.0)))
        qdb_t = row_table(jnp.exp(lgb * (c - p)))
        kdf_t = row_table(kdf)
        kdb_t = row_table(kdb)
        cdf = jnp.exp(lgf * c)
        cdb = jnp.exp(lgb * c)
        if has_state:
            sf_scr[...] = s0_ref[0, 0, hh]
            sb_scr[...] = s0_ref[0, 1, hh]
        else:
            sf_scr[...] = jnp.zeros_like(sf_scr)
            sb_scr[...] = jnp.zeros_like(sb_scr)

        def rev_body(idx, carry):
            n = nc - 1 - idx
            rows = rows_of(n)
            sbh_scr[n] = sb_scr[...].astype(BF16)
            kb = k_at(rows, qcols) * kdb_t
            sb_scr[...] = cdb * sb_scr[...] + _dot_tn(kb, v_at(rows, vcols))
            return carry

        lax.fori_loop(0, nc, rev_body, 0, unroll=True)

        def fwd_body(n, carry):
            rows = rows_of(n)
            qn = q_at(rows, qcols)
            kn = k_at(rows, qcols)
            vn = v_at(rows, vcols)
            s = (_dot_nt(qn, kn) * decay).astype(BF16)
            o = (jnp.dot(s, vn, preferred_element_type=F32)
                 + jnp.dot(qn * qdf_t, sf_scr[...].astype(BF16), preferred_element_type=F32)
                 + jnp.dot(qn * qdb_t, sbh_scr[n], preferred_element_type=F32))
            normed_out(o, rows, vcols)
            sf_scr[...] = cdf * sf_scr[...] + _dot_tn(kn * kdf_t, vn)
            return carry

        lax.fori_loop(0, nc, fwd_body, 0, unroll=True)

        if emit_state:
            sfin_ref[0, 0, hh] = sf_scr[...]
            sfin_ref[0, 1, hh] = sb_scr[...]

    if fuse_tail:
        x_ref, l_ref, mr_ref, ml_ref, gate_ref, fnw_ref, wrd_ref, wld_ref, wo_ref = tail_refs
        out_ref[0] = _tail_math(x_ref[0], o_scr[...], l_ref[0], mr_ref[0], ml_ref[0], gate_ref[0], fnw_ref[...],
                                wrd_ref[...], wld_ref[...], wo_ref[...])


def _retention(z3, decay_logit, s0, emit_state, hp, tail=None):
    b, l, _ = z3.shape
    nc = l // RET_CHUNK
    has_state = s0 is not None
    fuse_tail = tail is not None
    assert not fuse_tail or hp == N_HEADS
    ring = hp == 1 and not fuse_tail and b * N_HEADS >= RING_SLOTS
    kq, kv_ = OFF_K // (hp * DK), OFF_V // (hp * DV)
    kg = OFF_GRET // (hp * DV)
    if ring:
        in_specs = [pl.BlockSpec(memory_space=pltpu.SMEM), pl.BlockSpec(memory_space=pl.ANY)]
        args = [decay_logit, z3]
    else:
        in_specs = [pl.BlockSpec(memory_space=pltpu.SMEM),
                    pl.BlockSpec((1, l, hp * DK), lambda i, h: (i, 0, h)),
                    pl.BlockSpec((1, l, hp * DK), lambda i, h: (i, 0, kq + h)),
                    pl.BlockSpec((1, l, hp * DV), lambda i, h: (i, 0, kv_ + h)),
                    pl.BlockSpec((1, l, hp * DV), lambda i, h: (i, 0, kg + h))]
        args = [decay_logit, z3, z3, z3, z3]
    if has_state:
        in_specs.append(pl.BlockSpec((1, 2, hp, DK, DV), lambda i, h: (i, 0, h, 0, 0)))
        args.append(s0)
    scratch = [pltpu.VMEM((DK, DV), F32), pltpu.VMEM((DK, DV), F32), pltpu.VMEM((nc, DK, DV), BF16)]
    if fuse_tail:
        x3, l3, mod3, cond_row, fnw, wrd_b, wld_b, wo_b = tail
        const = lambda i, h: (0, 0)
        in_specs += [pl.BlockSpec((1, l, D_MODEL), lambda i, h: (i, 0, 0)),
                     pl.BlockSpec((1, l, D_LRU), lambda i, h: (i, 0, 0)),
                     pl.BlockSpec((1, l, D_MODEL), lambda i, h: (i, 0, MAIN_MRET // D_MODEL)),
                     pl.BlockSpec((1, l, D_MODEL), lambda i, h: (i, 0, MAIN_MLRU // D_MODEL)),
                     pl.BlockSpec((1, 1, D_MODEL), lambda i, h: (cond_row(i), 0, MOD_GATE)),
                     pl.BlockSpec((1, D_MODEL), const),
                     pl.BlockSpec((D_V, D_MODEL), const),
                     pl.BlockSpec((D_LRU, D_MODEL), const),
                     pl.BlockSpec((D_MODEL, D_MODEL), const)]
        args += [x3, l3, z3, z3, mod3, fnw.reshape(1, -1), wrd_b, wld_b, wo_b]
        out_specs = [pl.BlockSpec((1, l, D_MODEL), lambda i, h: (i, 0, 0))]
        out_shape = [jax.ShapeDtypeStruct((b, l, D_MODEL), F32)]
        scratch.append(pltpu.VMEM((l, D_V), BF16))
    else:
        out_specs = [pl.BlockSpec((1, l, hp * DV), lambda i, h: (i, 0, h))]
        out_shape = [jax.ShapeDtypeStruct((b, l, D_V), BF16)]
    if emit_state:
        out_specs.append(pl.BlockSpec((1, 2, hp, DK, DV), lambda i, h: (i, 0, h, 0, 0)))
        out_shape.append(jax.ShapeDtypeStruct((b, 2, N_HEADS, DK, DV), F32))
    if ring:
        scratch += [pltpu.VMEM((RING_SLOTS, l, w), BF16) for w in (DK, DK, DV, DV)]
        scratch.append(pltpu.SemaphoreType.DMA((RING_SLOTS, 4)))
    return pl.pallas_call(
        functools.partial(_ret_kernel, nc=nc, hp=hp, has_state=has_state, emit_state=emit_state,
                          fuse_tail=fuse_tail, ring=ring),
        grid=(b, N_HEADS // hp),
        in_specs=in_specs,
        out_specs=out_specs,
        out_shape=out_shape,
        scratch_shapes=scratch,
        compiler_params=pltpu.CompilerParams(
            dimension_semantics=("arbitrary", "arbitrary") if ring else ("parallel", "parallel"),
            vmem_limit_bytes=VMEM_LIMIT),
        name="retention_tail" if fuse_tail else "retention",
    )(*args)


def _sqrt_unit(x):
    return x * lax.rsqrt(jnp.maximum(x, 1e-30))


def _lru_kernel(xs_ref, sg_ref, cw_ref, cb_ref, w_ref, ba_ref, bx_ref, ap_ref, h0_ref, pt_ref, out_ref, fin_ref,
                xp_scr, s1_scr, wm_scr, a0_scr, b0_scr, a1_scr, b1_scr, *, slabs, seq_len, width, nblk):
    r8 = LRU_ROWS
    tc = LRU_TC
    rows = PERM_ROWS
    cw = nblk * LRU_BLOCK
    nchunks = slabs // tc
    assert width == seq_len or width == tc

    def roll4(v):
        return pltpu.roll(v, LRU_SEQS, axis=0)

    xp_scr[pl.ds(0, 2 * r8), :] = jnp.zeros((2 * r8, cw), F32)
    xp_scr[pl.ds(2 * r8, slabs * r8), :] = xs_ref[0].astype(F32)
    xp_scr[pl.ds((slabs + 2) * r8, r8), :] = roll4(xs_ref[0, pl.ds((slabs - 1) * r8, r8), :].astype(F32))
    xp_scr[pl.ds((slabs + 3) * r8, r8), :] = roll4(xs_ref[0, pl.ds((slabs - 2) * r8, r8), :].astype(F32))

    row_cw = lax.broadcasted_iota(jnp.int32, (rows, cw), 0)
    step_in_chunk = row_cw >> 3
    low_cw = (row_cw & (r8 - 1)) < LRU_SEQS

    def row_pattern(lo, hi):
        n = lo.shape[-1]
        return jnp.where(lax.broadcasted_iota(jnp.int32, (r8, n), 0) < LRU_SEQS, lo, hi)

    def tap_rows(s):
        zero = jnp.zeros((1, cw), F32)
        lo = cw_ref[pl.ds(s + CONV_LEFT, 1), :] if 0 <= s + CONV_LEFT < CONV_W else zero
        hi = cw_ref[pl.ds(CONV_LEFT - s, 1), :] if 0 <= CONV_LEFT - s < CONV_W else zero
        return 0.5 * row_pattern(lo, hi)

    def tap_table(s):
        tap = jnp.broadcast_to(tap_rows(s)[None], (tc, r8, cw)).reshape(rows, cw)
        if width != seq_len:
            tap = jnp.where((step_in_chunk + s >= 0) & (step_in_chunk + s < tc), tap, 0.0)
        return tap

    wm_scr[0] = tap_table(-2) + tap_table(2)
    for s in range(-1, 2):
        wm_scr[s + 2] = tap_table(s)

    ap8 = row_pattern(ap_ref[pl.ds(0, 1), :], ap_ref[pl.ds(1, 1), :])
    c1_lo = (-0.5 * LRU_C / np.log(2.0)) * _softplus(-ap8)
    cb_half = 0.5 * cb_ref[...]

    def half_bias(j):
        cols = slice(j * LRU_BLOCK, (j + 1) * LRU_BLOCK)
        return 0.5 * jnp.concatenate(
            [row_pattern(ba_ref[pl.ds(0, 1), cols], ba_ref[pl.ds(1, 1), cols]),
             row_pattern(bx_ref[pl.ds(0, 1), cols], bx_ref[pl.ds(1, 1), cols])], axis=1)

    bias_lo = [half_bias(j) for j in range(nblk)]
    low_rows = (lax.broadcasted_iota(jnp.int32, (rows, LRU_BLOCK), 0) & (r8 - 1)) < LRU_SEQS
    high_rows = jnp.logical_not(low_rows)

    def conv_half(t0):
        def shifted(s):
            return xp_scr[pl.ds(pl.multiple_of((t0 + s + 2) * r8, r8), rows), :]

        xh = jnp.where(low_cw, shifted(-2), shifted(2)) * wm_scr[0] + cb_half
        for s in range(-1, 2):
            xh = xh + shifted(s) * wm_scr[s + 2]
        xp_scr[pl.ds(pl.multiple_of(t0 * r8, rows), rows), :] = xh
        return xh

    def gates(xh, sweep2, a_scr, b_scr):
        fj = high_rows if sweep2 else low_rows
        c1 = roll4(c1_lo) if sweep2 else c1_lo
        for j in range(nblk):
            cols = slice(j * LRU_BLOCK, (j + 1) * LRU_BLOCK)
            xj = xh[:, cols]
            lhs = jnp.concatenate([jnp.where(fj, xj, 0.0), jnp.where(fj, 0.0, xj)], axis=1)
            pre = jnp.dot(lhs.astype(BF16), w_ref[j], preferred_element_type=F32)
            bj = roll4(bias_lo[j]) if sweep2 else bias_lo[j]
            pre = pre.reshape(tc, r8, 2 * LRU_BLOCK) + bj[None]
            tr = jnp.tanh(pre[:, :, :LRU_BLOCK])
            tg = jnp.tanh(pre[:, :, LRU_BLOCK:])
            c1j = c1[:, cols][None]
            a = jnp.exp2(c1j * tr + c1j)
            bco = _sqrt_unit(1.0 - a * a) * ((tg + 1.0) * xj.reshape(tc, r8, LRU_BLOCK))
            a_scr[:, cols] = a.reshape(rows, LRU_BLOCK)
            b_scr[:, cols] = bco.reshape(rows, LRU_BLOCK)

    bufs = ((a0_scr, b0_scr), (a1_scr, b1_scr))

    def chunk_rows(t0):
        return pl.ds(pl.multiple_of(t0 * r8, rows), rows)

    def scan(buf, h, dst_ref, t0, descending):
        a_ref, b_ref = bufs[buf]
        for i in range(tc):
            t = tc - 1 - i if descending else i
            h = a_ref[t * r8:(t + 1) * r8, :] * h + b_ref[t * r8:(t + 1) * r8, :]
            if dst_ref is None:
                b_ref[t * r8:(t + 1) * r8, :] = h
            else:
                dst_ref[pl.ds(pl.multiple_of(t0 * r8, rows) + t * r8, r8), :] = h
        return h

    def gates1(ci, buf):
        gates(conv_half(ci * tc), False, *bufs[buf])

    def scan1(ci, buf, h):
        return scan(buf, h, s1_scr, ci * tc, False)

    npairs = nchunks // 2
    assert nchunks == 2 * npairs
    gates1(0, 0)

    def sweep1_pair(i, h):
        gates1(2 * i + 1, 1)
        h = scan1(2 * i, 0, h)
        gates1(2 * i + 2, 0)
        return scan1(2 * i + 1, 1, h)

    h = lax.fori_loop(0, npairs - 1, sweep1_pair, h0_ref[0])
    gates1(nchunks - 1, 1)
    h = scan1(nchunks - 2, 0, h)
    h = scan1(nchunks - 1, 1, h)

    def gates2(p, buf):
        gates(xp_scr[chunk_rows((nchunks - 1 - p) * tc), :], True, *bufs[buf])

    def scan2(p, buf, h):
        return scan(buf, h, None, 0, True)

    def finish(p, buf):
        t0 = (nchunks - 1 - p) * tc
        crows = chunk_rows(t0)
        o = (s1_scr[crows, :] + bufs[buf][1][...]).astype(BF16) * sg_ref[0, crows, :]
        nat = jnp.dot(pt_ref[...], o, preferred_element_type=F32).astype(out_ref.dtype)
        lo = pl.multiple_of(t0, tc)
        hi = pl.multiple_of(seq_len - tc - t0, tc)
        for b in range(LRU_SEQS):
            out_ref[0, b, pl.ds(lo, tc), :] = nat[b * tc:(b + 1) * tc]
            out_ref[0, b, pl.ds(hi, tc), :] = nat[(LRU_SEQS + b) * tc:(LRU_SEQS + b + 1) * tc]

    h = roll4(h)
    gates2(0, 0)
    gates2(1, 1)
    h = scan2(0, 0, h)

    def sweep2_pair(i, h):
        finish(2 * i, 0)
        gates2(2 * i + 2, 0)
        h = scan2(2 * i + 1, 1, h)
        finish(2 * i + 1, 1)
        gates2(2 * i + 3, 1)
        return scan2(2 * i + 2, 0, h)

    h = lax.fori_loop(0, npairs - 1, sweep2_pair, h)
    finish(nchunks - 2, 0)
    h = scan2(nchunks - 1, 1, h)
    finish(nchunks - 1, 1)
    fin_ref[0] = h


def _lru(xs, gs, conv_w, conv_b, w_blk, ba, bx, a_param, h0, perm_t, seq_len, width, nblk):
    groups = xs.shape[0]
    r8 = LRU_ROWS
    slabs = seq_len // 2
    cw = nblk * LRU_BLOCK
    per_dir = pl.BlockSpec((2, cw), lambda g, c: (0, c))
    return pl.pallas_call(
        functools.partial(_lru_kernel, slabs=slabs, seq_len=seq_len, width=width, nblk=nblk),
        grid=(groups, N_LRU_BLOCKS // nblk),
        in_specs=[pl.BlockSpec((1, slabs * r8, cw), lambda g, c: (g, 0, c)),
                  pl.BlockSpec((1, slabs * r8, cw), lambda g, c: (g, 0, c)),
                  pl.BlockSpec((CONV_W, cw), lambda g, c: (0, c)),
                  pl.BlockSpec((1, cw), lambda g, c: (0, c)),
                  pl.BlockSpec((nblk, 2 * LRU_BLOCK, 2 * LRU_BLOCK), lambda g, c: (c, 0, 0)),
                  per_dir, per_dir, per_dir,
                  pl.BlockSpec((1, r8, cw), lambda g, c: (g, 0, c)),
                  pl.BlockSpec((PERM_ROWS, PERM_ROWS), lambda g, c: (0, 0))],
        out_specs=[pl.BlockSpec((1, LRU_SEQS, seq_len, cw), lambda g, c: (g, 0, 0, c)),
                   pl.BlockSpec((1, r8, cw), lambda g, c: (g, 0, c))],
        out_shape=[jax.ShapeDtypeStruct((groups, LRU_SEQS, seq_len, D_LRU), BF16),
                   jax.ShapeDtypeStruct((groups, r8, D_LRU), F32)],
        scratch_shapes=[pltpu.VMEM(((slabs + 4) * r8, cw), F32),
                        pltpu.VMEM((slabs * r8, cw), F32),
                        pltpu.VMEM((4, PERM_ROWS, cw), F32),
                        pltpu.VMEM((PERM_ROWS, cw), F32),
                        pltpu.VMEM((PERM_ROWS, cw), F32),
                        pltpu.VMEM((PERM_ROWS, cw), F32),
                        pltpu.VMEM((PERM_ROWS, cw), F32)],
        compiler_params=pltpu.CompilerParams(
            dimension_semantics=("parallel", "parallel"), vmem_limit_bytes=VMEM_LIMIT),
        name="lru",
    )(xs, gs, conv_w, conv_b.reshape(1, -1), w_blk, ba, bx, a_param, h0, perm_t)


def _lru_gate_weights(wa, wx):
    return jnp.concatenate([jnp.concatenate([wa[0], wx[0]], axis=2),
                            jnp.concatenate([wa[1], wx[1]], axis=2)], axis=1).astype(BF16)


def _tail_kernel(x_ref, o_ref, l_ref, mr_ref, ml_ref, gate_ref, fnw_ref, wrd_ref, wld_ref, wo_ref, y_ref):
    y_ref[...] = _tail_math(x_ref[...], o_ref[...], l_ref[...], mr_ref[...], ml_ref[...], gate_ref[0],
                            fnw_ref[...], wrd_ref[...], wld_ref[...], wo_ref[...])


TAIL_TM = 512


def _tail(x2d, o2d, l2d, z2d, mod3, cond_row, fnw, wrd_b, wld_b, wo_b):
    m = x2d.shape[0]
    tm = TAIL_TM
    kmr, kml = MAIN_MRET // D_MODEL, MAIN_MLRU // D_MODEL
    const = lambda i: (0, 0)
    return pl.pallas_call(
        _tail_kernel,
        grid=(m // tm,),
        in_specs=[pl.BlockSpec((tm, D_MODEL), lambda i: (i, 0)),
                  pl.BlockSpec((tm, D_V), lambda i: (i, 0)),
                  pl.BlockSpec((tm, D_LRU), lambda i: (i, 0)),
                  pl.BlockSpec((tm, D_MODEL), lambda i: (i, kmr)),
                  pl.BlockSpec((tm, D_MODEL), lambda i: (i, kml)),
                  pl.BlockSpec((1, 1, D_MODEL), lambda i: (cond_row(i), 0, MOD_GATE)),
                  pl.BlockSpec((1, D_MODEL), const),
                  pl.BlockSpec((D_V, D_MODEL), const),
                  pl.BlockSpec((D_LRU, D_MODEL), const),
                  pl.BlockSpec((D_MODEL, D_MODEL), const)],
        out_specs=pl.BlockSpec((tm, D_MODEL), lambda i: (i, 0)),
        out_shape=jax.ShapeDtypeStruct((m, D_MODEL), F32),
        compiler_params=pltpu.CompilerParams(
            dimension_semantics=("parallel",), vmem_limit_bytes=VMEM_LIMIT),
        name="tail",
    )(x2d, o2d, l2d, z2d, z2d, mod3, fnw.reshape(1, -1), wrd_b, wld_b, wo_b)


def _trunk(x, mod3, cond_row0, n_cond, s0_ret, h0_lru, width, lru_nblk, params, tail_w, final_norm_w,
           emit_state):
    (norm_w, w_all, decay_logit, conv_w, conv_b, w_blk, ba, bx, a_param, perm, perm_t) = params
    b, l, _ = x.shape
    groups = b // LRU_SEQS
    x2d = x.reshape(b * l, D_MODEL)
    tokens_per_cond = b * l // n_cond

    def cond_row(tile_tokens):
        return lambda i: cond_row0 + i // (tokens_per_cond // tile_tokens)

    if len(tail_w) == 4:
        z2d, *tail_w = _inproj(x2d, norm_w, mod3, cond_row(INPROJ_TM), w_all, tail_weights=tail_w)
    else:
        z2d, = _inproj(x2d, norm_w, mod3, cond_row(INPROJ_TM), w_all)
    wrd_b, wld_b, wo_b = tail_w
    z3 = z2d.reshape(b, l, D_MAIN)

    xs, gs = _inproj_lru(x.reshape(groups, LRU_SEQS, l, D_MODEL), norm_w, mod3, cond_row0, n_cond, perm, w_all)
    if h0_lru is None:
        h0 = jnp.zeros((groups, LRU_ROWS, D_LRU), F32)
    else:
        h0 = h0_lru.reshape(groups, LRU_SEQS, 2, D_LRU).transpose(0, 2, 1, 3).reshape(groups, LRU_ROWS, D_LRU)
    lru_pre, fin = _lru(xs, gs, conv_w, conv_b, w_blk, ba, bx, a_param, h0, perm_t, l, width, lru_nblk)
    fin = fin.reshape(groups, 2, LRU_SEQS, D_LRU)
    lru_fin = jnp.stack([fin[:, 1], fin[:, 0]], axis=2).reshape(b, 2, D_LRU)

    if l == RET_CHUNK:
        ret = _retention(z3, decay_logit, s0_ret, emit_state, N_HEADS,
                         tail=(x, lru_pre.reshape(b, l, D_LRU), mod3, cond_row(l), final_norm_w,
                               wrd_b, wld_b, wo_b))
        y = ret[0]
    else:
        ret = _retention(z3, decay_logit, s0_ret, emit_state, 1)
        y = _tail(x2d, ret[0].reshape(b * l, D_V), lru_pre.reshape(b * l, D_LRU), z2d, mod3, cond_row(TAIL_TM),
                  final_norm_w, wrd_b, wld_b, wo_b).reshape(b, l, D_MODEL)
    return y, (ret[1] if emit_state else None), lru_fin, (wrd_b, wld_b, wo_b)


def kernel(x_prompt, x_sample, state_ret, state_lru, c, c_ctx, norm_w, w_ada, b_ada, w_in, ret_decay_logit,
           ret_gn_w, w_ret_down, conv_w, conv_b, lru_wa, lru_ba, lru_wx, lru_bx, lru_a_param, w_lru_down,
           w_out, final_norm_w):
    assert norm_w.shape[0] == 1, "single-layer step"
    n_dec = c.shape[0]
    cond8 = jnp.concatenate([c.astype(F32), c_ctx.astype(F32)[None],
                             jnp.zeros((8 - n_dec - 1, D_MODEL), F32)], axis=0)
    mod = _ada(cond8, w_ada[0], b_ada[0])
    col = jnp.arange(D_IN)
    col_scale = jnp.where((col >= OFF_GRET) & (col < OFF_XLRU), 0.5, 1.0).astype(F32)
    w_all = (w_in[0] * col_scale[None, :]).astype(BF16)
    perm = jnp.asarray(_slab_permutation(PERM_UNIT), BF16)
    perm_t = jnp.asarray(_slab_permutation(LRU_TC).T, BF16)
    params = (norm_w[0], w_all, ret_decay_logit[0], conv_w[0], conv_b[0],
              _lru_gate_weights(lru_wa[0], lru_wx[0]), lru_ba[0], lru_bx[0], lru_a_param[0], perm, perm_t)
    mod3 = mod
    tail_f32 = (ret_gn_w[0], w_ret_down[0], w_lru_down[0], w_out[0])
    y_prompt, new_ret, new_lru, tail_b = _trunk(x_prompt.astype(F32), mod3, n_dec, 1, None, None,
                                                x_prompt.shape[1], 5, params, tail_f32, final_norm_w, True)
    y_sample, _, _, _ = _trunk(x_sample.astype(F32), mod3, 0, n_dec, state_ret[:, 0], state_lru[:, 0],
                               GRID_W, 2, params, tail_b, final_norm_w, False)
    return (y_prompt.astype(x_prompt.dtype), y_sample.astype(x_sample.dtype),
            new_ret[:, None].astype(state_ret.dtype), new_lru[:, None].astype(state_lru.dtype))
```

```python
import functools

import jax
import jax.numpy as jnp
import numpy as np
from jax import lax
from jax.experimental import pallas as pl
from jax.experimental.pallas import tpu as pltpu

F32 = jnp.float32
BF16 = jnp.bfloat16

D_MODEL = 1024
N_HEADS = 4
DK = 256
DV = 512
D_QK = N_HEADS * DK
D_V = N_HEADS * DV
N_LRU_BLOCKS = 10
LRU_BLOCK = 128
D_LRU = N_LRU_BLOCKS * LRU_BLOCK
LRU_C = 8.0
CONV_W = 4
CONV_LEFT = 2
GRID_W = 64
EPS = 1e-6

OFF_Q = 0
OFF_K = OFF_Q + D_QK
OFF_V = OFF_K + D_QK
OFF_GRET = OFF_V + D_V
OFF_XLRU = OFF_GRET + D_V
OFF_GLRU = OFF_XLRU + D_LRU
OFF_MRET = OFF_GLRU + D_LRU
OFF_MLRU = OFF_MRET + D_MODEL
D_IN = OFF_MLRU + D_MODEL
D_MAIN = OFF_XLRU + 2 * D_MODEL
MAIN_MRET = OFF_XLRU
MAIN_MLRU = OFF_XLRU + D_MODEL

RET_CHUNK = 256
LRU_SEQS = 4
LRU_ROWS = 2 * LRU_SEQS
LRU_TC = 64
PERM_ROWS = LRU_TC * LRU_ROWS
PERM_UNIT = 32
INPROJ_LRU_SLABS = 128
VMEM_LIMIT = 56 * 1024 * 1024


def _sigmoid(x):
    return 0.5 * jnp.tanh(0.5 * x) + 0.5


def _silu(x):
    return x * _sigmoid(x)


def _softplus(x):
    return jnp.maximum(x, 0.0) + jnp.log1p(jnp.exp(-jnp.abs(x)))


def _slab_permutation(tc):
    n = tc * LRU_ROWS
    p = np.zeros((n, n), np.float32)
    for t in range(tc):
        for r in range(LRU_ROWS):
            if r < LRU_SEQS:
                src = r * tc + t
            else:
                src = LRU_SEQS * tc + (r - LRU_SEQS) * tc + (tc - 1 - t)
            p[t * LRU_ROWS + r, src] = 1.0
    return p


def _ada_kernel(c_ref, w_ref, b_ref, o_ref):
    cond = _silu(c_ref[...]).astype(BF16)
    mod = jnp.dot(cond, w_ref[...].astype(BF16), preferred_element_type=F32) + b_ref[...]
    o_ref[...] = mod[:, None, :]


def _ada(cond8, w_ada, b_ada):
    tn = 1536
    return pl.pallas_call(
        _ada_kernel,
        grid=(3 * D_MODEL // tn,),
        in_specs=[pl.BlockSpec((8, D_MODEL), lambda j: (0, 0)),
                  pl.BlockSpec((D_MODEL, tn), lambda j: (0, j)),
                  pl.BlockSpec((1, tn), lambda j: (0, j))],
        out_specs=pl.BlockSpec((8, 1, tn), lambda j: (0, 0, j)),
        out_shape=jax.ShapeDtypeStruct((8, 1, 3 * D_MODEL), F32),
        name="ada",
    )(cond8, w_ada, b_ada.reshape(1, -1))


def _modulated_norm(x, nw, scale, shift):
    ms = jnp.mean(x * x, axis=-1, keepdims=True)
    return (x * lax.rsqrt(ms + EPS)) * (nw * (1.0 + scale)) + shift


INPROJ_TM = 1024
INPROJ_TN = 2048


def _inproj_kernel(x_ref, nw_ref, sc_ref, sh_ref, w_ref, *rest):
    if len(rest) == 2:
        z_ref, h_scr = rest
    else:
        gn_ref, wrd_ref, wld_ref, wo_ref, z_ref, wrd_out, wld_out, wo_out, h_scr = rest
        wrd_out[...] = (gn_ref[...] * wrd_ref[...]).astype(BF16)
        wld_out[...] = wld_ref[...].astype(BF16)
        wo_out[...] = wo_ref[...].astype(BF16)

    @pl.when(pl.program_id(1) == 0)
    def _():
        h_scr[...] = _modulated_norm(x_ref[...], nw_ref[...], sc_ref[0], sh_ref[0]).astype(BF16)

    z_ref[...] = jnp.dot(h_scr[...], w_ref[...], preferred_element_type=F32).astype(z_ref.dtype)


MOD_SHIFT, MOD_SCALE, MOD_GATE = 0, 1, 2


def _inproj(x2d, norm_w, mod3, cond_row, w_all, tail_weights=None):
    m = x2d.shape[0]
    tm, tn = INPROJ_TM, INPROJ_TN
    n_lead = OFF_XLRU // tn
    n_tiles = D_MAIN // tn
    assert OFF_XLRU == n_lead * tn and D_MAIN == n_tiles * tn

    def w_col(j):
        off = j * tn
        for t in range(n_lead, n_tiles):
            off = jnp.where(j == t, OFF_MRET + (t - n_lead) * tn, off)
        return off

    in_specs = [pl.BlockSpec((tm, D_MODEL), lambda i, j: (i, 0)),
                pl.BlockSpec((1, D_MODEL), lambda i, j: (0, 0)),
                pl.BlockSpec((1, 1, D_MODEL), lambda i, j: (cond_row(i), 0, MOD_SCALE)),
                pl.BlockSpec((1, 1, D_MODEL), lambda i, j: (cond_row(i), 0, MOD_SHIFT)),
                pl.BlockSpec((pl.Element(D_MODEL), pl.Element(tn)), lambda i, j: (0, w_col(j)))]
    args = [x2d, norm_w.reshape(1, -1), mod3, mod3, w_all]
    out_specs = [pl.BlockSpec((tm, tn), lambda i, j: (i, j))]
    out_shape = [jax.ShapeDtypeStruct((m, D_MAIN), BF16)]
    if tail_weights is not None:
        gn_w, wrd, wld, wo = tail_weights
        steps = (m // tm) * n_tiles

        def rows_per_step(w):
            assert w.shape[0] % (steps * 16) == 0
            return w.shape[0] // steps

        def step_rows(w, cols):
            return pl.BlockSpec((rows_per_step(w), cols), lambda i, j: (i * n_tiles + j, 0))

        in_specs += [step_rows(wrd, 1), step_rows(wrd, D_MODEL), step_rows(wld, D_MODEL), step_rows(wo, D_MODEL)]
        args += [gn_w.reshape(-1, 1), wrd, wld, wo]
        for w in (wrd, wld, wo):
            out_specs.append(step_rows(w, D_MODEL))
            out_shape.append(jax.ShapeDtypeStruct(w.shape, BF16))
    return pl.pallas_call(
        _inproj_kernel,
        grid=(m // tm, n_tiles),
        in_specs=in_specs,
        out_specs=out_specs,
        out_shape=out_shape,
        scratch_shapes=[pltpu.VMEM((tm, D_MODEL), BF16)],
        compiler_params=pltpu.CompilerParams(
            dimension_semantics=("arbitrary", "arbitrary"), vmem_limit_bytes=VMEM_LIMIT),
        name="inproj",
    )(*args)


def _inproj_lru_kernel(xa_ref, xb_ref, nw_ref, sc_ref, sh_ref, p_ref, w_ref, xs_ref, gs_ref):
    nw = nw_ref[...]
    sc = sc_ref[...]
    sh = sh_ref[...]
    ha = _modulated_norm(xa_ref[0], nw, sc, sh).astype(BF16)
    hb = _modulated_norm(xb_ref[0], nw, sc, sh).astype(BF16)
    units = INPROJ_LRU_SLABS // PERM_UNIT
    pieces = []
    for u in range(units):
        lo, hi = u * PERM_UNIT, (units - 1 - u) * PERM_UNIT
        src = jnp.concatenate([ha[b, lo:lo + PERM_UNIT] for b in range(LRU_SEQS)]
                              + [hb[b, hi:hi + PERM_UNIT] for b in range(LRU_SEQS)], axis=0)
        pieces.append(jnp.dot(p_ref[...], src, preferred_element_type=F32).astype(BF16))
    hp = jnp.concatenate(pieces, axis=0)
    z = jnp.dot(hp, w_ref[...], preferred_element_type=F32)
    xs_ref[0] = z[:, :D_LRU].astype(xs_ref.dtype)
    gs_ref[0] = _silu(z[:, D_LRU:]).astype(gs_ref.dtype)


def _inproj_lru(x4, norm_w, mod3, row0, nrows, perm, w_all):
    groups, _, l, _ = x4.shape
    assert row0 % nrows == 0 and nrows in (1, LRU_SEQS)
    ts = INPROJ_LRU_SLABS
    nt = l // 2 // ts
    ntb = l // ts
    out_sds = jax.ShapeDtypeStruct((groups, l // 2 * LRU_ROWS, D_LRU), BF16)
    return pl.pallas_call(
        _inproj_lru_kernel,
        grid=(groups, nt),
        in_specs=[pl.BlockSpec((1, LRU_SEQS, ts, D_MODEL), lambda g, i: (g, 0, i, 0)),
                  pl.BlockSpec((1, LRU_SEQS, ts, D_MODEL), lambda g, i: (g, 0, ntb - 1 - i, 0)),
                  pl.BlockSpec((1, D_MODEL), lambda g, i: (0, 0)),
                  pl.BlockSpec((nrows, 1, D_MODEL), lambda g, i: (row0 // nrows, 0, MOD_SCALE)),
                  pl.BlockSpec((nrows, 1, D_MODEL), lambda g, i: (row0 // nrows, 0, MOD_SHIFT)),
                  pl.BlockSpec((PERM_UNIT * LRU_ROWS, PERM_UNIT * LRU_ROWS), lambda g, i: (0, 0)),
                  pl.BlockSpec((pl.Element(D_MODEL), pl.Element(2 * D_LRU)), lambda g, i: (0, OFF_XLRU))],
        out_specs=[pl.BlockSpec((1, ts * LRU_ROWS, D_LRU), lambda g, i: (g, i, 0)),
                   pl.BlockSpec((1, ts * LRU_ROWS, D_LRU), lambda g, i: (g, i, 0))],
        out_shape=[out_sds, out_sds],
        compiler_params=pltpu.CompilerParams(
            dimension_semantics=("parallel", "parallel"), vmem_limit_bytes=VMEM_LIMIT),
        name="inproj_lru",
    )(x4, x4, norm_w.reshape(1, -1), mod3, mod3, perm, w_all)


def _dot_tn(a, b):
    return lax.dot_general(a, b, (((0,), (0,)), ((), ())), preferred_element_type=F32)


def _dot_nt(a, b):
    return lax.dot_general(a, b, (((1,), (1,)), ((), ())), preferred_element_type=F32)


def _tail_math(x, o, lru_pre, m_ret, m_lru, gate, fnw, wrd, wld, wo):
    ret_out = jnp.dot(o, wrd, preferred_element_type=F32)
    lru_out = jnp.dot(lru_pre, wld, preferred_element_type=F32)
    merged = _sigmoid(m_ret.astype(F32)) * ret_out + _sigmoid(m_lru.astype(F32)) * lru_out
    out = jnp.dot(merged.astype(BF16), wo, preferred_element_type=F32)
    y = x + gate * out
    ms = jnp.mean(y * y, axis=-1, keepdims=True)
    return y * lax.rsqrt(ms + EPS) * fnw


def _ret_kernel(*refs, nc, hp, has_state, emit_state, fuse_tail):
    dl_ref, q_ref, k_ref, v_ref, hg_ref = refs[:5]
    pos = 5
    s0_ref = None
    if has_state:
        s0_ref = refs[pos]
        pos += 1
    tail_refs = None
    if fuse_tail:
        tail_refs = refs[pos:pos + 9]
        pos += 9
    out_ref = refs[pos]
    pos += 1
    sfin_ref = None
    if emit_state:
        sfin_ref = refs[pos]
        pos += 1
    sf_scr, sb_scr, sbh_scr = refs[pos:pos + 3]
    o_scr = refs[pos + 3] if fuse_tail else None

    c = RET_CHUNK
    carry_states = has_state or nc > 1
    ii = lax.broadcasted_iota(jnp.int32, (c, c), 0)
    jj = lax.broadcasted_iota(jnp.int32, (c, c), 1)
    diff = (ii - jj).astype(F32)
    p = lax.broadcasted_iota(jnp.int32, (c, 1), 0).astype(F32)
    kscale = DK ** -0.5

    def rows_of(n):
        return pl.ds(pl.multiple_of(n * c, c), c)

    def normed_out(o, rows, vcols):
        ms = jnp.mean(o * o, axis=-1, keepdims=True)
        on = o * lax.rsqrt(ms + EPS)
        hg = hg_ref[0, rows, vcols]
        gate = hg * (jnp.tanh(hg) + 1.0)
        if fuse_tail:
            o_scr[rows, vcols] = on.astype(BF16) * gate
        else:
            out_ref[0, rows, vcols] = on.astype(out_ref.dtype) * gate

    for hh in range(hp):
        head = pl.program_id(1) * hp + hh
        qcols = slice(hh * DK, (hh + 1) * DK)
        vcols = slice(hh * DV, (hh + 1) * DV)
        lgf = -_softplus(-jnp.full((1, 1), dl_ref[0, head], F32))
        lgb = -_softplus(-jnp.full((1, 1), dl_ref[1, head], F32))
        decay = jnp.exp(jnp.where(diff >= 0, lgf * diff, -lgb * diff)) * kscale
        kdf = jnp.exp(lgf * (c - 1.0 - p)) * kscale
        kdb = jnp.exp(lgb * p) * kscale

        if not carry_states:
            rows = pl.ds(0, c)
            qn = q_ref[0, rows, qcols]
            kn = k_ref[0, rows, qcols]
            vn = v_ref[0, rows, vcols]
            s = (_dot_nt(qn, kn) * decay).astype(BF16)
            normed_out(jnp.dot(s, vn, preferred_element_type=F32), rows, vcols)
            k32 = kn.astype(F32)
            if emit_state:
                sfin_ref[0, 0, hh] = _dot_tn((k32 * kdf).astype(BF16), vn)
                sfin_ref[0, 1, hh] = _dot_tn((k32 * kdb).astype(BF16), vn)
            continue

        def row_table(col):
            return jnp.broadcast_to(col, (c, DK)).astype(BF16)

        qdf_t = row_table(jnp.exp(lgf * (p + 1.0)))
        qdb_t = row_table(jnp.exp(lgb * (c - p)))
        kdf_t = row_table(kdf)
        kdb_t = row_table(kdb)
        cdf = jnp.exp(lgf * c)
        cdb = jnp.exp(lgb * c)
        if has_state:
            sf_scr[...] = s0_ref[0, 0, hh]
            sb_scr[...] = s0_ref[0, 1, hh]
        else:
            sf_scr[...] = jnp.zeros_like(sf_scr)
            sb_scr[...] = jnp.zeros_like(sb_scr)

        def rev_body(idx, carry):
            n = nc - 1 - idx
            rows = rows_of(n)
            sbh_scr[n] = sb_scr[...].astype(BF16)
            kb = k_ref[0, rows, qcols] * kdb_t
            sb_scr[...] = cdb * sb_scr[...] + _dot_tn(kb, v_ref[0, rows, vcols])
            return carry

        lax.fori_loop(0, nc, rev_body, 0, unroll=True)

        def fwd_body(n, carry):
            rows = rows_of(n)
            qn = q_ref[0, rows, qcols]
            kn = k_ref[0, rows, qcols]
            vn = v_ref[0, rows, vcols]
            s = (_dot_nt(qn, kn) * decay).astype(BF16)
            o = (jnp.dot(s, vn, preferred_element_type=F32)
                 + jnp.dot(qn * qdf_t, sf_scr[...].astype(BF16), preferred_element_type=F32)
                 + jnp.dot(qn * qdb_t, sbh_scr[n], preferred_element_type=F32))
            normed_out(o, rows, vcols)
            sf_scr[...] = cdf * sf_scr[...] + _dot_tn(kn * kdf_t, vn)
            return carry

        lax.fori_loop(0, nc, fwd_body, 0, unroll=True)

        if emit_state:
            sfin_ref[0, 0, hh] = sf_scr[...]
            sfin_ref[0, 1, hh] = sb_scr[...]

    if fuse_tail:
        x_ref, l_ref, mr_ref, ml_ref, gate_ref, fnw_ref, wrd_ref, wld_ref, wo_ref = tail_refs
        out_ref[0] = _tail_math(x_ref[0], o_scr[...], l_ref[0], mr_ref[0], ml_ref[0], gate_ref[0], fnw_ref[...],
                                wrd_ref[...], wld_ref[...], wo_ref[...])


def _retention(z3, decay_logit, s0, emit_state, hp, tail=None):
    b, l, _ = z3.shape
    nc = l // RET_CHUNK
    has_state = s0 is not None
    fuse_tail = tail is not None
    assert not fuse_tail or hp == N_HEADS
    kq, kv_ = OFF_K // (hp * DK), OFF_V // (hp * DV)
    kg = OFF_GRET // (hp * DV)
    in_specs = [pl.BlockSpec(memory_space=pltpu.SMEM),
                pl.BlockSpec((1, l, hp * DK), lambda i, h: (i, 0, h)),
                pl.BlockSpec((1, l, hp * DK), lambda i, h: (i, 0, kq + h)),
                pl.BlockSpec((1, l, hp * DV), lambda i, h: (i, 0, kv_ + h)),
                pl.BlockSpec((1, l, hp * DV), lambda i, h: (i, 0, kg + h))]
    args = [decay_logit, z3, z3, z3, z3]
    if has_state:
        in_specs.append(pl.BlockSpec((1, 2, hp, DK, DV), lambda i, h: (i, 0, h, 0, 0)))
        args.append(s0)
    scratch = [pltpu.VMEM((DK, DV), F32), pltpu.VMEM((DK, DV), F32), pltpu.VMEM((nc, DK, DV), BF16)]
    if fuse_tail:
        x3, l3, mod3, cond_row, fnw, wrd_b, wld_b, wo_b = tail
        const = lambda i, h: (0, 0)
        in_specs += [pl.BlockSpec((1, l, D_MODEL), lambda i, h: (i, 0, 0)),
                     pl.BlockSpec((1, l, D_LRU), lambda i, h: (i, 0, 0)),
                     pl.BlockSpec((1, l, D_MODEL), lambda i, h: (i, 0, MAIN_MRET // D_MODEL)),
                     pl.BlockSpec((1, l, D_MODEL), lambda i, h: (i, 0, MAIN_MLRU // D_MODEL)),
                     pl.BlockSpec((1, 1, D_MODEL), lambda i, h: (cond_row(i), 0, MOD_GATE)),
                     pl.BlockSpec((1, D_MODEL), const),
                     pl.BlockSpec((D_V, D_MODEL), const),
                     pl.BlockSpec((D_LRU, D_MODEL), const),
                     pl.BlockSpec((D_MODEL, D_MODEL), const)]
        args += [x3, l3, z3, z3, mod3, fnw.reshape(1, -1), wrd_b, wld_b, wo_b]
        out_specs = [pl.BlockSpec((1, l, D_MODEL), lambda i, h: (i, 0, 0))]
        out_shape = [jax.ShapeDtypeStruct((b, l, D_MODEL), F32)]
        scratch.append(pltpu.VMEM((l, D_V), BF16))
    else:
        out_specs = [pl.BlockSpec((1, l, hp * DV), lambda i, h: (i, 0, h))]
        out_shape = [jax.ShapeDtypeStruct((b, l, D_V), BF16)]
    if emit_state:
        out_specs.append(pl.BlockSpec((1, 2, hp, DK, DV), lambda i, h: (i, 0, h, 0, 0)))
        out_shape.append(jax.ShapeDtypeStruct((b, 2, N_HEADS, DK, DV), F32))
    return pl.pallas_call(
        functools.partial(_ret_kernel, nc=nc, hp=hp, has_state=has_state, emit_state=emit_state,
                          fuse_tail=fuse_tail),
        grid=(b, N_HEADS // hp),
        in_specs=in_specs,
        out_specs=out_specs,
        out_shape=out_shape,
        scratch_shapes=scratch,
        compiler_params=pltpu.CompilerParams(
            dimension_semantics=("parallel", "parallel"), vmem_limit_bytes=VMEM_LIMIT),
        name="retention_tail" if fuse_tail else "retention",
    )(*args)


def _sqrt_unit(x):
    return x * lax.rsqrt(jnp.maximum(x, 1e-30))


def _lru_kernel(xs_ref, sg_ref, cw_ref, cb_ref, w_ref, ba_ref, bx_ref, ap_ref, h0_ref, pt_ref, out_ref, fin_ref,
                xp_scr, s1_scr, wm_scr, a0_scr, b0_scr, a1_scr, b1_scr, *, slabs, seq_len, width, nblk, has_h0):
    r8 = LRU_ROWS
    tc = LRU_TC
    rows = PERM_ROWS
    cw = nblk * LRU_BLOCK
    nchunks = slabs // tc
    assert width == seq_len or width == tc

    def roll4(v):
        return pltpu.roll(v, LRU_SEQS, axis=0)

    def permute_rows(v, src):
        row = lax.broadcasted_iota(jnp.int32, v.shape, 0)
        rolled = {}
        out = None
        for r, s in enumerate(src):
            shift = (r - s) % r8
            if shift not in rolled:
                rolled[shift] = pltpu.roll(v, shift, axis=0) if shift else v
            out = rolled[shift] if out is None else jnp.where(row == r, rolled[shift], out)
        return out

    xp_scr[pl.ds(0, 2 * r8), :] = jnp.zeros((2 * r8, cw), F32)
    xp_scr[pl.ds(2 * r8, slabs * r8), :] = xs_ref[0].astype(F32)
    xp_scr[pl.ds((slabs + 2) * r8, r8), :] = roll4(xs_ref[0, pl.ds((slabs - 1) * r8, r8), :].astype(F32))
    xp_scr[pl.ds((slabs + 3) * r8, r8), :] = roll4(xs_ref[0, pl.ds((slabs - 2) * r8, r8), :].astype(F32))

    row_cw = lax.broadcasted_iota(jnp.int32, (rows, cw), 0)
    step_in_chunk = row_cw >> 3
    low_cw = (row_cw & (r8 - 1)) < LRU_SEQS

    def row_pattern(lo, hi):
        n = lo.shape[-1]
        return jnp.where(lax.broadcasted_iota(jnp.int32, (r8, n), 0) < LRU_SEQS, lo, hi)

    def tap_rows(s):
        zero = jnp.zeros((1, cw), F32)
        lo = cw_ref[pl.ds(s + CONV_LEFT, 1), :] if 0 <= s + CONV_LEFT < CONV_W else zero
        hi = cw_ref[pl.ds(CONV_LEFT - s, 1), :] if 0 <= CONV_LEFT - s < CONV_W else zero
        return 0.5 * row_pattern(lo, hi)

    def tap_table(s):
        tap = jnp.broadcast_to(tap_rows(s)[None], (tc, r8, cw)).reshape(rows, cw)
        if width != seq_len:
            tap = jnp.where((step_in_chunk + s >= 0) & (step_in_chunk + s < tc), tap, 0.0)
        return tap

    wm_scr[0] = tap_table(-2) + tap_table(2)
    for s in range(-1, 2):
        wm_scr[s + 2] = tap_table(s)

    ap8 = row_pattern(ap_ref[pl.ds(0, 1), :], ap_ref[pl.ds(1, 1), :])
    c1_lo = (-0.5 * LRU_C / np.log(2.0)) * _softplus(-ap8)
    cb_half = 0.5 * cb_ref[...]

    def half_bias(j):
        cols = slice(j * LRU_BLOCK, (j + 1) * LRU_BLOCK)
        return 0.5 * jnp.concatenate(
            [row_pattern(ba_ref[pl.ds(0, 1), cols], ba_ref[pl.ds(1, 1), cols]),
             row_pattern(bx_ref[pl.ds(0, 1), cols], bx_ref[pl.ds(1, 1), cols])], axis=1)

    bias_lo = [half_bias(j) for j in range(nblk)]
    low_rows = (lax.broadcasted_iota(jnp.int32, (rows, LRU_BLOCK), 0) & (r8 - 1)) < LRU_SEQS
    high_rows = jnp.logical_not(low_rows)

    def conv_half(t0):
        def shifted(s):
            return xp_scr[pl.ds(pl.multiple_of((t0 + s + 2) * r8, r8), rows), :]

        xh = jnp.where(low_cw, shifted(-2), shifted(2)) * wm_scr[0] + cb_half
        for s in range(-1, 2):
            xh = xh + shifted(s) * wm_scr[s + 2]
        xp_scr[pl.ds(pl.multiple_of(t0 * r8, rows), rows), :] = xh
        return xh

    def gates(xh, sweep2, a_scr, b_scr):
        fj = high_rows if sweep2 else low_rows
        c1 = roll4(c1_lo) if sweep2 else c1_lo
        for j in range(nblk):
            cols = slice(j * LRU_BLOCK, (j + 1) * LRU_BLOCK)
            xj = xh[:, cols]
            lhs = jnp.concatenate([jnp.where(fj, xj, 0.0), jnp.where(fj, 0.0, xj)], axis=1)
            pre = jnp.dot(lhs.astype(BF16), w_ref[j], preferred_element_type=F32)
            bj = roll4(bias_lo[j]) if sweep2 else bias_lo[j]
            pre = pre.reshape(tc, r8, 2 * LRU_BLOCK) + bj[None]
            tr = jnp.tanh(pre[:, :, :LRU_BLOCK])
            tg = jnp.tanh(pre[:, :, LRU_BLOCK:])
            c1j = c1[:, cols][None]
            a = jnp.exp2(c1j * tr + c1j)
            bco = _sqrt_unit(1.0 - a * a) * ((tg + 1.0) * xj.reshape(tc, r8, LRU_BLOCK))
            a_scr[:, cols] = a.reshape(rows, LRU_BLOCK)
            b_scr[:, cols] = bco.reshape(rows, LRU_BLOCK)

    bufs = ((a0_scr, b0_scr), (a1_scr, b1_scr))

    def chunk_rows(t0):
        return pl.ds(pl.multiple_of(t0 * r8, rows), rows)

    def scan(buf, h, dst_ref, t0, descending):
        a_ref, b_ref = bufs[buf]
        for i in range(tc):
            t = tc - 1 - i if descending else i
            h = a_ref[t * r8:(t + 1) * r8, :] * h + b_ref[t * r8:(t + 1) * r8, :]
            if dst_ref is None:
                b_ref[t * r8:(t + 1) * r8, :] = h
            else:
                dst_ref[pl.ds(pl.multiple_of(t0 * r8, rows) + t * r8, r8), :] = h
        return h

    def gates1(ci, buf):
        gates(conv_half(ci * tc), False, *bufs[buf])

    def scan1(ci, buf, h):
        return scan(buf, h, s1_scr, ci * tc, False)

    npairs = nchunks // 2
    assert nchunks == 2 * npairs
    gates1(0, 0)

    def sweep1_pair(i, h):
        gates1(2 * i + 1, 1)
        h = scan1(2 * i, 0, h)
        gates1(2 * i + 2, 0)
        return scan1(2 * i + 1, 1, h)

    h_init = (permute_rows(h0_ref[0], [2 * (r % LRU_SEQS) + r // LRU_SEQS for r in range(r8)]) if has_h0
              else jnp.zeros((r8, cw), F32))
    h = lax.fori_loop(0, npairs - 1, sweep1_pair, h_init)
    gates1(nchunks - 1, 1)
    h = scan1(nchunks - 2, 0, h)
    h = scan1(nchunks - 1, 1, h)

    def gates2(p, buf):
        gates(xp_scr[chunk_rows((nchunks - 1 - p) * tc), :], True, *bufs[buf])

    def scan2(p, buf, h):
        return scan(buf, h, None, 0, True)

    def finish(p, buf):
        t0 = (nchunks - 1 - p) * tc
        crows = chunk_rows(t0)
        o = (s1_scr[crows, :] + bufs[buf][1][...]).astype(BF16) * sg_ref[0, crows, :]
        nat = jnp.dot(pt_ref[...], o, preferred_element_type=F32).astype(out_ref.dtype)
        lo = pl.multiple_of(t0, tc)
        hi = pl.multiple_of(seq_len - tc - t0, tc)
        for b in range(LRU_SEQS):
            out_ref[0, b, pl.ds(lo, tc), :] = nat[b * tc:(b + 1) * tc]
            out_ref[0, b, pl.ds(hi, tc), :] = nat[(LRU_SEQS + b) * tc:(LRU_SEQS + b + 1) * tc]

    h = roll4(h)
    gates2(0, 0)
    gates2(1, 1)
    h = scan2(0, 0, h)

    def sweep2_pair(i, h):
        finish(2 * i, 0)
        gates2(2 * i + 2, 0)
        h = scan2(2 * i + 1, 1, h)
        finish(2 * i + 1, 1)
        gates2(2 * i + 3, 1)
        return scan2(2 * i + 2, 0, h)

    h = lax.fori_loop(0, npairs - 1, sweep2_pair, h)
    finish(nchunks - 2, 0)
    h = scan2(nchunks - 1, 1, h)
    finish(nchunks - 1, 1)
    fin_ref[0] = permute_rows(h, [(r // 2) + (LRU_SEQS if r % 2 == 0 else 0) for r in range(r8)])


def _lru(xs, gs, conv_w, conv_b, w_blk, ba, bx, a_param, h0, perm_t, seq_len, width, nblk):
    groups = xs.shape[0]
    r8 = LRU_ROWS
    slabs = seq_len // 2
    cw = nblk * LRU_BLOCK
    per_dir = pl.BlockSpec((2, cw), lambda g, c: (0, c))
    state_spec = pl.BlockSpec((1, r8, cw), lambda g, c: (g, 0, c))
    has_h0 = h0 is not None
    if has_h0:
        h0 = h0.astype(F32).reshape(groups, r8, D_LRU)
    fin = pl.pallas_call(
        functools.partial(_lru_kernel, slabs=slabs, seq_len=seq_len, width=width, nblk=nblk, has_h0=has_h0),
        grid=(groups, N_LRU_BLOCKS // nblk),
        in_specs=[pl.BlockSpec((1, slabs * r8, cw), lambda g, c: (g, 0, c)),
                  pl.BlockSpec((1, slabs * r8, cw), lambda g, c: (g, 0, c)),
                  pl.BlockSpec((CONV_W, cw), lambda g, c: (0, c)),
                  pl.BlockSpec((1, cw), lambda g, c: (0, c)),
                  pl.BlockSpec((nblk, 2 * LRU_BLOCK, 2 * LRU_BLOCK), lambda g, c: (c, 0, 0)),
                  per_dir, per_dir, per_dir,
                  state_spec if has_h0 else per_dir,
                  pl.BlockSpec((PERM_ROWS, PERM_ROWS), lambda g, c: (0, 0))],
        out_specs=[pl.BlockSpec((1, LRU_SEQS, seq_len, cw), lambda g, c: (g, 0, 0, c)),
                   state_spec],
        out_shape=[jax.ShapeDtypeStruct((groups, LRU_SEQS, seq_len, D_LRU), BF16),
                   jax.ShapeDtypeStruct((groups, r8, D_LRU), F32)],
        scratch_shapes=[pltpu.VMEM(((slabs + 4) * r8, cw), F32),
                        pltpu.VMEM((slabs * r8, cw), F32),
                        pltpu.VMEM((4, PERM_ROWS, cw), F32),
                        pltpu.VMEM((PERM_ROWS, cw), F32),
                        pltpu.VMEM((PERM_ROWS, cw), F32),
                        pltpu.VMEM((PERM_ROWS, cw), F32),
                        pltpu.VMEM((PERM_ROWS, cw), F32)],
        compiler_params=pltpu.CompilerParams(
            dimension_semantics=("parallel", "parallel"), vmem_limit_bytes=VMEM_LIMIT),
        name="lru",
    )(xs, gs, conv_w, conv_b.reshape(1, -1), w_blk, ba, bx, a_param, h0 if has_h0 else ba, perm_t)
    return fin[0], fin[1].reshape(groups * LRU_SEQS, 2, D_LRU)


def _lru_gate_weights(wa, wx):
    return jnp.concatenate([jnp.concatenate([wa[0], wx[0]], axis=2),
                            jnp.concatenate([wa[1], wx[1]], axis=2)], axis=1).astype(BF16)


def _tail_kernel(x_ref, o_ref, l_ref, mr_ref, ml_ref, gate_ref, fnw_ref, wrd_ref, wld_ref, wo_ref, y_ref):
    y_ref[...] = _tail_math(x_ref[...], o_ref[...], l_ref[...], mr_ref[...], ml_ref[...], gate_ref[0],
                            fnw_ref[...], wrd_ref[...], wld_ref[...], wo_ref[...])


TAIL_TM = 512


def _tail(x2d, o2d, l2d, z2d, mod3, cond_row, fnw, wrd_b, wld_b, wo_b):
    m = x2d.shape[0]
    tm = TAIL_TM
    kmr, kml = MAIN_MRET // D_MODEL, MAIN_MLRU // D_MODEL
    const = lambda i: (0, 0)
    return pl.pallas_call(
        _tail_kernel,
        grid=(m // tm,),
        in_specs=[pl.BlockSpec((tm, D_MODEL), lambda i: (i, 0)),
                  pl.BlockSpec((tm, D_V), lambda i: (i, 0)),
                  pl.BlockSpec((tm, D_LRU), lambda i: (i, 0)),
                  pl.BlockSpec((tm, D_MODEL), lambda i: (i, kmr)),
                  pl.BlockSpec((tm, D_MODEL), lambda i: (i, kml)),
                  pl.BlockSpec((1, 1, D_MODEL), lambda i: (cond_row(i), 0, MOD_GATE)),
                  pl.BlockSpec((1, D_MODEL), const),
                  pl.BlockSpec((D_V, D_MODEL), const),
                  pl.BlockSpec((D_LRU, D_MODEL), const),
                  pl.BlockSpec((D_MODEL, D_MODEL), const)],
        out_specs=pl.BlockSpec((tm, D_MODEL), lambda i: (i, 0)),
        out_shape=jax.ShapeDtypeStruct((m, D_MODEL), F32),
        compiler_params=pltpu.CompilerParams(
            dimension_semantics=("parallel",), vmem_limit_bytes=VMEM_LIMIT),
        name="tail",
    )(x2d, o2d, l2d, z2d, z2d, mod3, fnw.reshape(1, -1), wrd_b, wld_b, wo_b)


def _trunk(x, mod3, cond_row0, n_cond, s0_ret, h0_lru, width, lru_nblk, params, tail_w, final_norm_w,
           emit_state):
    (norm_w, w_all, decay_logit, conv_w, conv_b, w_blk, ba, bx, a_param, perm, perm_t) = params
    b, l, _ = x.shape
    groups = b // LRU_SEQS
    x2d = x.reshape(b * l, D_MODEL)
    tokens_per_cond = b * l // n_cond

    def cond_row(tile_tokens):
        return lambda i: cond_row0 + i // (tokens_per_cond // tile_tokens)

    if len(tail_w) == 4:
        z2d, *tail_w = _inproj(x2d, norm_w, mod3, cond_row(INPROJ_TM), w_all, tail_weights=tail_w)
    else:
        z2d, = _inproj(x2d, norm_w, mod3, cond_row(INPROJ_TM), w_all)
    wrd_b, wld_b, wo_b = tail_w
    z3 = z2d.reshape(b, l, D_MAIN)

    xs, gs = _inproj_lru(x.reshape(groups, LRU_SEQS, l, D_MODEL), norm_w, mod3, cond_row0, n_cond, perm, w_all)
    lru_pre, lru_fin = _lru(xs, gs, conv_w, conv_b, w_blk, ba, bx, a_param, h0_lru, perm_t, l, width, lru_nblk)

    if l == RET_CHUNK:
        ret = _retention(z3, decay_logit, s0_ret, emit_state, N_HEADS,
                         tail=(x, lru_pre.reshape(b, l, D_LRU), mod3, cond_row(l), final_norm_w,
                               wrd_b, wld_b, wo_b))
        y = ret[0]
    else:
        ret = _retention(z3, decay_logit, s0_ret, emit_state, 1)
        y = _tail(x2d, ret[0].reshape(b * l, D_V), lru_pre.reshape(b * l, D_LRU), z2d, mod3, cond_row(TAIL_TM),
                  final_norm_w, wrd_b, wld_b, wo_b).reshape(b, l, D_MODEL)
    return y, (ret[1] if emit_state else None), lru_fin, (wrd_b, wld_b, wo_b)


def kernel(x_prompt, x_sample, state_ret, state_lru, c, c_ctx, norm_w, w_ada, b_ada, w_in, ret_decay_logit,
           ret_gn_w, w_ret_down, conv_w, conv_b, lru_wa, lru_ba, lru_wx, lru_bx, lru_a_param, w_lru_down,
           w_out, final_norm_w):
    assert norm_w.shape[0] == 1, "single-layer step"
    n_dec = c.shape[0]
    cond8 = jnp.concatenate([c.astype(F32), c_ctx.astype(F32)[None],
                             jnp.zeros((8 - n_dec - 1, D_MODEL), F32)], axis=0)
    mod = _ada(cond8, w_ada[0], b_ada[0])
    col = jnp.arange(D_IN)
    col_scale = jnp.where((col >= OFF_GRET) & (col < OFF_XLRU), 0.5, 1.0).astype(F32)
    w_all = (w_in[0] * col_scale[None, :]).astype(BF16)
    perm = jnp.asarray(_slab_permutation(PERM_UNIT), BF16)
    perm_t = jnp.asarray(_slab_permutation(LRU_TC).T, BF16)
    params = (norm_w[0], w_all, ret_decay_logit[0], conv_w[0], conv_b[0],
              _lru_gate_weights(lru_wa[0], lru_wx[0]), lru_ba[0], lru_bx[0], lru_a_param[0], perm, perm_t)
    mod3 = mod
    tail_f32 = (ret_gn_w[0], w_ret_down[0], w_lru_down[0], w_out[0])
    y_prompt, new_ret, new_lru, tail_b = _trunk(x_prompt.astype(F32), mod3, n_dec, 1, None, None,
                                                x_prompt.shape[1], 5, params, tail_f32, final_norm_w, True)
    y_sample, _, _, _ = _trunk(x_sample.astype(F32), mod3, 0, n_dec, state_ret[:, 0], state_lru[:, 0],
                               GRID_W, 2, params, tail_b, final_norm_w, False)
    return (y_prompt.astype(x_prompt.dtype), y_sample.astype(x_sample.dtype),
            new_ret[:, None].astype(state_ret.dtype), new_lru[:, None].astype(state_lru.dtype))
```

```python
import functools

import jax
import jax.numpy as jnp
import numpy as np
from jax import lax
from jax.experimental import pallas as pl
from jax.experimental.pallas import tpu as pltpu

F32 = jnp.float32
BF16 = jnp.bfloat16

D_MODEL = 1024
N_HEADS = 4
DK = 256
DV = 512
D_QK = N_HEADS * DK
D_V = N_HEADS * DV
N_LRU_BLOCKS = 10
LRU_BLOCK = 128
D_LRU = N_LRU_BLOCKS * LRU_BLOCK
LRU_C = 8.0
CONV_W = 4
CONV_LEFT = 2
GRID_W = 64
EPS = 1e-6

OFF_Q = 0
OFF_K = OFF_Q + D_QK
OFF_V = OFF_K + D_QK
OFF_GRET = OFF_V + D_V
OFF_XLRU = OFF_GRET + D_V
OFF_GLRU = OFF_XLRU + D_LRU
OFF_MRET = OFF_GLRU + D_LRU
OFF_MLRU = OFF_MRET + D_MODEL
D_IN = OFF_MLRU + D_MODEL
D_MAIN = OFF_XLRU + 2 * D_MODEL
MAIN_MRET = OFF_XLRU
MAIN_MLRU = OFF_XLRU + D_MODEL

RET_CHUNK = 256
LRU_SEQS = 4
LRU_ROWS = 2 * LRU_SEQS
LRU_TC = 64
PERM_ROWS = LRU_TC * LRU_ROWS
PERM_UNIT = 32
INPROJ_LRU_SLABS = 128
VMEM_LIMIT = 56 * 1024 * 1024


def _sigmoid(x):
    return 0.5 * jnp.tanh(0.5 * x) + 0.5


def _silu(x):
    return x * _sigmoid(x)


def _softplus(x):
    return jnp.maximum(x, 0.0) + jnp.log1p(jnp.exp(-jnp.abs(x)))


def _slab_permutation(tc):
    n = tc * LRU_ROWS
    p = np.zeros((n, n), np.float32)
    for t in range(tc):
        for r in range(LRU_ROWS):
            if r < LRU_SEQS:
                src = r * tc + t
            else:
                src = LRU_SEQS * tc + (r - LRU_SEQS) * tc + (tc - 1 - t)
            p[t * LRU_ROWS + r, src] = 1.0
    return p


def _ada_kernel(c_ref, w_ref, b_ref, o_ref):
    cond = _silu(c_ref[...]).astype(BF16)
    mod = jnp.dot(cond, w_ref[...].astype(BF16), preferred_element_type=F32) + b_ref[...]
    o_ref[...] = mod[:, None, :]


def _ada(cond8, w_ada, b_ada):
    tn = 1536
    return pl.pallas_call(
        _ada_kernel,
        grid=(3 * D_MODEL // tn,),
        in_specs=[pl.BlockSpec((8, D_MODEL), lambda j: (0, 0)),
                  pl.BlockSpec((D_MODEL, tn), lambda j: (0, j)),
                  pl.BlockSpec((1, tn), lambda j: (0, j))],
        out_specs=pl.BlockSpec((8, 1, tn), lambda j: (0, 0, j)),
        out_shape=jax.ShapeDtypeStruct((8, 1, 3 * D_MODEL), F32),
        name="ada",
    )(cond8, w_ada, b_ada.reshape(1, -1))


def _modulated_norm(x, nw, scale, shift):
    ms = jnp.mean(x * x, axis=-1, keepdims=True)
    return (x * lax.rsqrt(ms + EPS)) * (nw * (1.0 + scale)) + shift


INPROJ_TM = 1024
INPROJ_TN = 2048


def _inproj_kernel(x_ref, nw_ref, sc_ref, sh_ref, w_ref, *rest):
    if len(rest) == 2:
        z_ref, h_scr = rest
    else:
        gn_ref, wrd_ref, wld_ref, wo_ref, z_ref, wrd_out, wld_out, wo_out, h_scr = rest
        wrd_out[...] = (gn_ref[...] * wrd_ref[...]).astype(BF16)
        wld_out[...] = wld_ref[...].astype(BF16)
        wo_out[...] = wo_ref[...].astype(BF16)

    @pl.when(pl.program_id(1) == 0)
    def _():
        h_scr[...] = _modulated_norm(x_ref[...], nw_ref[...], sc_ref[0], sh_ref[0]).astype(BF16)

    z_ref[...] = jnp.dot(h_scr[...], w_ref[...], preferred_element_type=F32).astype(z_ref.dtype)


MOD_SHIFT, MOD_SCALE, MOD_GATE = 0, 1, 2


def _inproj(x2d, norm_w, mod3, cond_row, w_all, tail_weights=None):
    m = x2d.shape[0]
    tm, tn = INPROJ_TM, INPROJ_TN
    n_lead = OFF_XLRU // tn
    n_tiles = D_MAIN // tn
    assert OFF_XLRU == n_lead * tn and D_MAIN == n_tiles * tn

    def w_col(j):
        off = j * tn
        for t in range(n_lead, n_tiles):
            off = jnp.where(j == t, OFF_MRET + (t - n_lead) * tn, off)
        return off

    in_specs = [pl.BlockSpec((tm, D_MODEL), lambda i, j: (i, 0)),
                pl.BlockSpec((1, D_MODEL), lambda i, j: (0, 0)),
                pl.BlockSpec((1, 1, D_MODEL), lambda i, j: (cond_row(i), 0, MOD_SCALE)),
                pl.BlockSpec((1, 1, D_MODEL), lambda i, j: (cond_row(i), 0, MOD_SHIFT)),
                pl.BlockSpec((pl.Element(D_MODEL), pl.Element(tn)), lambda i, j: (0, w_col(j)))]
    args = [x2d, norm_w.reshape(1, -1), mod3, mod3, w_all]
    out_specs = [pl.BlockSpec((tm, tn), lambda i, j: (i, j))]
    out_shape = [jax.ShapeDtypeStruct((m, D_MAIN), BF16)]
    if tail_weights is not None:
        gn_w, wrd, wld, wo = tail_weights
        steps = (m // tm) * n_tiles

        def rows_per_step(w):
            assert w.shape[0] % (steps * 16) == 0
            return w.shape[0] // steps

        def step_rows(w, cols):
            return pl.BlockSpec((rows_per_step(w), cols), lambda i, j: (i * n_tiles + j, 0))

        in_specs += [step_rows(wrd, 1), step_rows(wrd, D_MODEL), step_rows(wld, D_MODEL), step_rows(wo, D_MODEL)]
        args += [gn_w.reshape(-1, 1), wrd, wld, wo]
        for w in (wrd, wld, wo):
            out_specs.append(step_rows(w, D_MODEL))
            out_shape.append(jax.ShapeDtypeStruct(w.shape, BF16))
    return pl.pallas_call(
        _inproj_kernel,
        grid=(m // tm, n_tiles),
        in_specs=in_specs,
        out_specs=out_specs,
        out_shape=out_shape,
        scratch_shapes=[pltpu.VMEM((tm, D_MODEL), BF16)],
        compiler_params=pltpu.CompilerParams(
            dimension_semantics=("arbitrary", "arbitrary"), vmem_limit_bytes=VMEM_LIMIT),
        name="inproj",
    )(*args)


def _inproj_lru_kernel(xa_ref, xb_ref, nw_ref, sc_ref, sh_ref, p_ref, w_ref, xs_ref, gs_ref):
    nw = nw_ref[...]
    sc = sc_ref[...]
    sh = sh_ref[...]
    ha = _modulated_norm(xa_ref[0], nw, sc, sh).astype(BF16)
    hb = _modulated_norm(xb_ref[0], nw, sc, sh).astype(BF16)
    units = INPROJ_LRU_SLABS // PERM_UNIT
    pieces = []
    for u in range(units):
        lo, hi = u * PERM_UNIT, (units - 1 - u) * PERM_UNIT
        src = jnp.concatenate([ha[b, lo:lo + PERM_UNIT] for b in range(LRU_SEQS)]
                              + [hb[b, hi:hi + PERM_UNIT] for b in range(LRU_SEQS)], axis=0)
        pieces.append(jnp.dot(p_ref[...], src, preferred_element_type=F32).astype(BF16))
    hp = jnp.concatenate(pieces, axis=0)
    z = jnp.dot(hp, w_ref[...], preferred_element_type=F32)
    xs_ref[0] = z[:, :D_LRU].astype(xs_ref.dtype)
    gs_ref[0] = _silu(z[:, D_LRU:]).astype(gs_ref.dtype)


def _inproj_lru(x4, norm_w, mod3, row0, nrows, perm, w_all):
    groups, _, l, _ = x4.shape
    assert row0 % nrows == 0 and nrows in (1, LRU_SEQS)
    ts = INPROJ_LRU_SLABS
    nt = l // 2 // ts
    ntb = l // ts
    out_sds = jax.ShapeDtypeStruct((groups, l // 2 * LRU_ROWS, D_LRU), BF16)
    return pl.pallas_call(
        _inproj_lru_kernel,
        grid=(groups, nt),
        in_specs=[pl.BlockSpec((1, LRU_SEQS, ts, D_MODEL), lambda g, i: (g, 0, i, 0)),
                  pl.BlockSpec((1, LRU_SEQS, ts, D_MODEL), lambda g, i: (g, 0, ntb - 1 - i, 0)),
                  pl.BlockSpec((1, D_MODEL), lambda g, i: (0, 0)),
                  pl.BlockSpec((nrows, 1, D_MODEL), lambda g, i: (row0 // nrows, 0, MOD_SCALE)),
                  pl.BlockSpec((nrows, 1, D_MODEL), lambda g, i: (row0 // nrows, 0, MOD_SHIFT)),
                  pl.BlockSpec((PERM_UNIT * LRU_ROWS, PERM_UNIT * LRU_ROWS), lambda g, i: (0, 0)),
                  pl.BlockSpec((pl.Element(D_MODEL), pl.Element(2 * D_LRU)), lambda g, i: (0, OFF_XLRU))],
        out_specs=[pl.BlockSpec((1, ts * LRU_ROWS, D_LRU), lambda g, i: (g, i, 0)),
                   pl.BlockSpec((1, ts * LRU_ROWS, D_LRU), lambda g, i: (g, i, 0))],
        out_shape=[out_sds, out_sds],
        compiler_params=pltpu.CompilerParams(
            dimension_semantics=("parallel", "parallel"), vmem_limit_bytes=VMEM_LIMIT),
        name="inproj_lru",
    )(x4, x4, norm_w.reshape(1, -1), mod3, mod3, perm, w_all)


def _dot_tn(a, b):
    return lax.dot_general(a, b, (((0,), (0,)), ((), ())), preferred_element_type=F32)


def _dot_nt(a, b):
    return lax.dot_general(a, b, (((1,), (1,)), ((), ())), preferred_element_type=F32)


def _tail_math(x, o, lru_pre, m_ret, m_lru, gate, fnw, wrd, wld, wo):
    ret_out = jnp.dot(o, wrd, preferred_element_type=F32)
    lru_out = jnp.dot(lru_pre, wld, preferred_element_type=F32)
    merged = _sigmoid(m_ret.astype(F32)) * ret_out + _sigmoid(m_lru.astype(F32)) * lru_out
    out = jnp.dot(merged.astype(BF16), wo, preferred_element_type=F32)
    y = x + gate * out
    ms = jnp.mean(y * y, axis=-1, keepdims=True)
    return y * lax.rsqrt(ms + EPS) * fnw


def _ret_kernel(*refs, nc, hp, has_state, emit_state, fuse_tail):
    dl_ref, q_ref, k_ref, v_ref, hg_ref = refs[:5]
    pos = 5
    s0_ref = None
    if has_state:
        s0_ref = refs[pos]
        pos += 1
    tail_refs = None
    if fuse_tail:
        tail_refs = refs[pos:pos + 9]
        pos += 9
    out_ref = refs[pos]
    pos += 1
    sfin_ref = None
    if emit_state:
        sfin_ref = refs[pos]
        pos += 1
    sf_scr, sb_scr, sbh_scr = refs[pos:pos + 3]
    o_scr = refs[pos + 3] if fuse_tail else None

    c = RET_CHUNK
    carry_states = has_state or nc > 1
    ii = lax.broadcasted_iota(jnp.int32, (c, c), 0)
    jj = lax.broadcasted_iota(jnp.int32, (c, c), 1)
    diff = (ii - jj).astype(F32)
    p = lax.broadcasted_iota(jnp.int32, (c, 1), 0).astype(F32)
    kscale = DK ** -0.5

    def rows_of(n):
        return pl.ds(pl.multiple_of(n * c, c), c)

    def normed_out(o, rows, vcols):
        ms = jnp.mean(o * o, axis=-1, keepdims=True)
        on = o * lax.rsqrt(ms + EPS)
        hg = hg_ref[0, rows, vcols]
        gate = hg * (jnp.tanh(hg) + 1.0)
        if fuse_tail:
            o_scr[rows, vcols] = on.astype(BF16) * gate
        else:
            out_ref[0, rows, vcols] = on.astype(out_ref.dtype) * gate

    for hh in range(hp):
        head = pl.program_id(1) * hp + hh
        qcols = slice(hh * DK, (hh + 1) * DK)
        vcols = slice(hh * DV, (hh + 1) * DV)
        lgf = -_softplus(-jnp.full((1, 1), dl_ref[0, head], F32))
        lgb = -_softplus(-jnp.full((1, 1), dl_ref[1, head], F32))
        decay = jnp.exp(jnp.where(diff >= 0, lgf * diff, -lgb * diff)) * kscale
        kdf = jnp.exp(lgf * (c - 1.0 - p)) * kscale
        kdb = jnp.exp(lgb * p) * kscale

        if not carry_states:
            rows = pl.ds(0, c)
            qn = q_ref[0, rows, qcols]
            kn = k_ref[0, rows, qcols]
            vn = v_ref[0, rows, vcols]
            s = (_dot_nt(qn, kn) * decay).astype(BF16)
            normed_out(jnp.dot(s, vn, preferred_element_type=F32), rows, vcols)
            k32 = kn.astype(F32)
            if emit_state:
                sfin_ref[0, 0, hh] = _dot_tn((k32 * kdf).astype(BF16), vn)
                sfin_ref[0, 1, hh] = _dot_tn((k32 * kdb).astype(BF16), vn)
            continue

        def row_table(col):
            return jnp.broadcast_to(col, (c, DK)).astype(BF16)

        qdf_t = row_table(jnp.exp(lgf * (p + 1.0)))
        qdb_t = row_table(jnp.exp(lgb * (c - p)))
        kdf_t = row_table(kdf)
        kdb_t = row_table(kdb)
        cdf = jnp.exp(lgf * c)
        cdb = jnp.exp(lgb * c)
        if has_state:
            sf_scr[...] = s0_ref[0, 0, hh]
            sb_scr[...] = s0_ref[0, 1, hh]
        else:
            sf_scr[...] = jnp.zeros_like(sf_scr)
            sb_scr[...] = jnp.zeros_like(sb_scr)

        def rev_body(idx, carry):
            n = nc - 1 - idx
            rows = rows_of(n)
            sbh_scr[n] = sb_scr[...].astype(BF16)
            kb = k_ref[0, rows, qcols] * kdb_t
            sb_scr[...] = cdb * sb_scr[...] + _dot_tn(kb, v_ref[0, rows, vcols])
            return carry

        lax.fori_loop(0, nc, rev_body, 0, unroll=True)

        def fwd_body(n, carry):
            rows = rows_of(n)
            qn = q_ref[0, rows, qcols]
            kn = k_ref[0, rows, qcols]
            vn = v_ref[0, rows, vcols]
            s = (_dot_nt(qn, kn) * decay).astype(BF16)
            o = (jnp.dot(s, vn, preferred_element_type=F32)
                 + jnp.dot(qn * qdf_t, sf_scr[...].astype(BF16), preferred_element_type=F32)
                 + jnp.dot(qn * qdb_t, sbh_scr[n], preferred_element_type=F32))
            normed_out(o, rows, vcols)
            sf_scr[...] = cdf * sf_scr[...] + _dot_tn(kn * kdf_t, vn)
            return carry

        lax.fori_loop(0, nc, fwd_body, 0, unroll=True)

        if emit_state:
            sfin_ref[0, 0, hh] = sf_scr[...]
            sfin_ref[0, 1, hh] = sb_scr[...]

    if fuse_tail:
        x_ref, l_ref, mr_ref, ml_ref, gate_ref, fnw_ref, wrd_ref, wld_ref, wo_ref = tail_refs
        out_ref[0] = _tail_math(x_ref[0], o_scr[...], l_ref[0], mr_ref[0], ml_ref[0], gate_ref[0], fnw_ref[...],
                                wrd_ref[...], wld_ref[...], wo_ref[...])


def _retention(z3, decay_logit, s0, emit_state, hp, tail=None):
    b, l, _ = z3.shape
    nc = l // RET_CHUNK
    has_state = s0 is not None
    fuse_tail = tail is not None
    assert not fuse_tail or hp == N_HEADS
    kq, kv_ = OFF_K // (hp * DK), OFF_V // (hp * DV)
    kg = OFF_GRET // (hp * DV)
    in_specs = [pl.BlockSpec(memory_space=pltpu.SMEM),
                pl.BlockSpec((1, l, hp * DK), lambda i, h: (i, 0, h)),
                pl.BlockSpec((1, l, hp * DK), lambda i, h: (i, 0, kq + h)),
                pl.BlockSpec((1, l, hp * DV), lambda i, h: (i, 0, kv_ + h)),
                pl.BlockSpec((1, l, hp * DV), lambda i, h: (i, 0, kg + h))]
    args = [decay_logit, z3, z3, z3, z3]
    if has_state:
        in_specs.append(pl.BlockSpec((1, 2, hp, DK, DV), lambda i, h: (i, 0, h, 0, 0)))
        args.append(s0)
    scratch = [pltpu.VMEM((DK, DV), F32), pltpu.VMEM((DK, DV), F32), pltpu.VMEM((nc, DK, DV), BF16)]
    if fuse_tail:
        x3, l3, mod3, cond_row, fnw, wrd_b, wld_b, wo_b = tail
        const = lambda i, h: (0, 0)
        in_specs += [pl.BlockSpec((1, l, D_MODEL), lambda i, h: (i, 0, 0)),
                     pl.BlockSpec((1, l, D_LRU), lambda i, h: (i, 0, 0)),
                     pl.BlockSpec((1, l, D_MODEL), lambda i, h: (i, 0, MAIN_MRET // D_MODEL)),
                     pl.BlockSpec((1, l, D_MODEL), lambda i, h: (i, 0, MAIN_MLRU // D_MODEL)),
                     pl.BlockSpec((1, 1, D_MODEL), lambda i, h: (cond_row(i), 0, MOD_GATE)),
                     pl.BlockSpec((1, D_MODEL), const),
                     pl.BlockSpec((D_V, D_MODEL), const),
                     pl.BlockSpec((D_LRU, D_MODEL), const),
                     pl.BlockSpec((D_MODEL, D_MODEL), const)]
        args += [x3, l3, z3, z3, mod3, fnw.reshape(1, -1), wrd_b, wld_b, wo_b]
        out_specs = [pl.BlockSpec((1, l, D_MODEL), lambda i, h: (i, 0, 0))]
        out_shape = [jax.ShapeDtypeStruct((b, l, D_MODEL), F32)]
        scratch.append(pltpu.VMEM((l, D_V), BF16))
    else:
        out_specs = [pl.BlockSpec((1, l, hp * DV), lambda i, h: (i, 0, h))]
        out_shape = [jax.ShapeDtypeStruct((b, l, D_V), BF16)]
    if emit_state:
        out_specs.append(pl.BlockSpec((1, 2, hp, DK, DV), lambda i, h: (i, 0, h, 0, 0)))
        out_shape.append(jax.ShapeDtypeStruct((b, 2, N_HEADS, DK, DV), F32))
    return pl.pallas_call(
        functools.partial(_ret_kernel, nc=nc, hp=hp, has_state=has_state, emit_state=emit_state,
                          fuse_tail=fuse_tail),
        grid=(b, N_HEADS // hp),
        in_specs=in_specs,
        out_specs=out_specs,
        out_shape=out_shape,
        scratch_shapes=scratch,
        compiler_params=pltpu.CompilerParams(
            dimension_semantics=("parallel", "parallel"), vmem_limit_bytes=VMEM_LIMIT),
        name="retention_tail" if fuse_tail else "retention",
    )(*args)


def _sqrt_unit(x):
    return x * lax.rsqrt(jnp.maximum(x, 1e-30))


def _lru_kernel(xs_ref, sg_ref, cw_ref, cb_ref, w_ref, ba_ref, bx_ref, ap_ref, h0_ref, pt_ref, out_ref, fin_ref,
                xp_scr, s1_scr, wm_scr, a0_scr, b0_scr, a1_scr, b1_scr, *, slabs, seq_len, width, nblk, has_h0):
    r8 = LRU_ROWS
    tc = LRU_TC
    rows = PERM_ROWS
    cw = nblk * LRU_BLOCK
    nchunks = slabs // tc
    assert width == seq_len or width == tc

    def roll4(v):
        return pltpu.roll(v, LRU_SEQS, axis=0)

    xp_scr[pl.ds(0, 2 * r8), :] = jnp.zeros((2 * r8, cw), F32)
    xp_scr[pl.ds(2 * r8, slabs * r8), :] = xs_ref[0].astype(F32)
    xp_scr[pl.ds((slabs + 2) * r8, r8), :] = roll4(xs_ref[0, pl.ds((slabs - 1) * r8, r8), :].astype(F32))
    xp_scr[pl.ds((slabs + 3) * r8, r8), :] = roll4(xs_ref[0, pl.ds((slabs - 2) * r8, r8), :].astype(F32))

    row_cw = lax.broadcasted_iota(jnp.int32, (rows, cw), 0)
    step_in_chunk = row_cw >> 3
    low_cw = (row_cw & (r8 - 1)) < LRU_SEQS

    def row_pattern(lo, hi):
        n = lo.shape[-1]
        return jnp.where(lax.broadcasted_iota(jnp.int32, (r8, n), 0) < LRU_SEQS, lo, hi)

    def tap_rows(s):
        zero = jnp.zeros((1, cw), F32)
        lo = cw_ref[pl.ds(s + CONV_LEFT, 1), :] if 0 <= s + CONV_LEFT < CONV_W else zero
        hi = cw_ref[pl.ds(CONV_LEFT - s, 1), :] if 0 <= CONV_LEFT - s < CONV_W else zero
        return 0.5 * row_pattern(lo, hi)

    def tap_table(s):
        tap = jnp.broadcast_to(tap_rows(s)[None], (tc, r8, cw)).reshape(rows, cw)
        if width != seq_len:
            tap = jnp.where((step_in_chunk + s >= 0) & (step_in_chunk + s < tc), tap, 0.0)
        return tap

    wm_scr[0] = tap_table(-2) + tap_table(2)
    for s in range(-1, 2):
        wm_scr[s + 2] = tap_table(s)

    ap8 = row_pattern(ap_ref[pl.ds(0, 1), :], ap_ref[pl.ds(1, 1), :])
    c1_lo = (-0.5 * LRU_C / np.log(2.0)) * _softplus(-ap8)
    cb_half = 0.5 * cb_ref[...]

    def half_bias(j):
        cols = slice(j * LRU_BLOCK, (j + 1) * LRU_BLOCK)
        return 0.5 * jnp.concatenate(
            [row_pattern(ba_ref[pl.ds(0, 1), cols], ba_ref[pl.ds(1, 1), cols]),
             row_pattern(bx_ref[pl.ds(0, 1), cols], bx_ref[pl.ds(1, 1), cols])], axis=1)

    bias_lo = [half_bias(j) for j in range(nblk)]
    low_rows = (lax.broadcasted_iota(jnp.int32, (rows, LRU_BLOCK), 0) & (r8 - 1)) < LRU_SEQS
    high_rows = jnp.logical_not(low_rows)

    def conv_half(t0):
        def shifted(s):
            return xp_scr[pl.ds(pl.multiple_of((t0 + s + 2) * r8, r8), rows), :]

        xh = jnp.where(low_cw, shifted(-2), shifted(2)) * wm_scr[0] + cb_half
        for s in range(-1, 2):
            xh = xh + shifted(s) * wm_scr[s + 2]
        xp_scr[pl.ds(pl.multiple_of(t0 * r8, rows), rows), :] = xh
        return xh

    def gates(xh, sweep2, a_scr, b_scr):
        fj = high_rows if sweep2 else low_rows
        c1 = roll4(c1_lo) if sweep2 else c1_lo
        for j in range(nblk):
            cols = slice(j * LRU_BLOCK, (j + 1) * LRU_BLOCK)
            xj = xh[:, cols]
            lhs = jnp.concatenate([jnp.where(fj, xj, 0.0), jnp.where(fj, 0.0, xj)], axis=1)
            pre = jnp.dot(lhs.astype(BF16), w_ref[j], preferred_element_type=F32)
            bj = roll4(bias_lo[j]) if sweep2 else bias_lo[j]
            pre = pre.reshape(tc, r8, 2 * LRU_BLOCK) + bj[None]
            tr = jnp.tanh(pre[:, :, :LRU_BLOCK])
            tg = jnp.tanh(pre[:, :, LRU_BLOCK:])
            c1j = c1[:, cols][None]
            a = jnp.exp2(c1j * tr + c1j)
            bco = _sqrt_unit(1.0 - a * a) * ((tg + 1.0) * xj.reshape(tc, r8, LRU_BLOCK))
            a_scr[:, cols] = a.reshape(rows, LRU_BLOCK)
            b_scr[:, cols] = bco.reshape(rows, LRU_BLOCK)

    bufs = ((a0_scr, b0_scr), (a1_scr, b1_scr))

    def chunk_rows(t0):
        return pl.ds(pl.multiple_of(t0 * r8, rows), rows)

    def scan(buf, h, dst_ref, t0, descending):
        a_ref, b_ref = bufs[buf]
        for i in range(tc):
            t = tc - 1 - i if descending else i
            h = a_ref[t * r8:(t + 1) * r8, :] * h + b_ref[t * r8:(t + 1) * r8, :]
            if dst_ref is None:
                b_ref[t * r8:(t + 1) * r8, :] = h
            else:
                dst_ref[pl.ds(pl.multiple_of(t0 * r8, rows) + t * r8, r8), :] = h
        return h

    def gates1(ci, buf):
        gates(conv_half(ci * tc), False, *bufs[buf])

    def scan1(ci, buf, h):
        return scan(buf, h, s1_scr, ci * tc, False)

    npairs = nchunks // 2
    assert nchunks == 2 * npairs
    gates1(0, 0)

    def sweep1_pair(i, h):
        gates1(2 * i + 1, 1)
        h = scan1(2 * i, 0, h)
        gates1(2 * i + 2, 0)
        return scan1(2 * i + 1, 1, h)

    h_init = jnp.zeros((r8, cw), F32)
    if has_h0:
        row8 = lax.broadcasted_iota(jnp.int32, (r8, cw), 0)
        for s in range(LRU_SEQS):
            a1_scr[pl.ds(r8 * s, 2), :] = h0_ref[s].astype(F32)
            t = a1_scr[pl.ds(r8 * s, r8), :]
            h_init = jnp.where(row8 == s, pltpu.roll(t, s, axis=0) if s else t, h_init)
            h_init = jnp.where(row8 == LRU_SEQS + s, pltpu.roll(t, LRU_SEQS - 1 + s, axis=0), h_init)
    h = lax.fori_loop(0, npairs - 1, sweep1_pair, h_init)
    gates1(nchunks - 1, 1)
    h = scan1(nchunks - 2, 0, h)
    h = scan1(nchunks - 1, 1, h)

    def gates2(p, buf):
        gates(xp_scr[chunk_rows((nchunks - 1 - p) * tc), :], True, *bufs[buf])

    def scan2(p, buf, h):
        return scan(buf, h, None, 0, True)

    def finish(p, buf):
        t0 = (nchunks - 1 - p) * tc
        crows = chunk_rows(t0)
        o = (s1_scr[crows, :] + bufs[buf][1][...]).astype(BF16) * sg_ref[0, crows, :]
        nat = jnp.dot(pt_ref[...], o, preferred_element_type=F32).astype(out_ref.dtype)
        lo = pl.multiple_of(t0, tc)
        hi = pl.multiple_of(seq_len - tc - t0, tc)
        for b in range(LRU_SEQS):
            out_ref[0, b, pl.ds(lo, tc), :] = nat[b * tc:(b + 1) * tc]
            out_ref[0, b, pl.ds(hi, tc), :] = nat[(LRU_SEQS + b) * tc:(LRU_SEQS + b + 1) * tc]

    h = roll4(h)
    gates2(0, 0)
    gates2(1, 1)
    h = scan2(0, 0, h)

    def sweep2_pair(i, h):
        finish(2 * i, 0)
        gates2(2 * i + 2, 0)
        h = scan2(2 * i + 1, 1, h)
        finish(2 * i + 1, 1)
        gates2(2 * i + 3, 1)
        return scan2(2 * i + 2, 0, h)

    h = lax.fori_loop(0, npairs - 1, sweep2_pair, h)
    finish(nchunks - 2, 0)
    h = scan2(nchunks - 1, 1, h)
    finish(nchunks - 1, 1)
    row8 = lax.broadcasted_iota(jnp.int32, (r8, cw), 0)
    hs = roll4(h)
    for s in range(LRU_SEQS):
        fwd = pltpu.roll(hs, r8 - s, axis=0) if s else hs
        bwd = pltpu.roll(h, (1 - s) % r8, axis=0) if s != 1 else h
        fin_ref[s] = jnp.where(row8 == 0, fwd, bwd)[0:2, :].astype(fin_ref.dtype)


def _lru(xs, gs, conv_w, conv_b, w_blk, ba, bx, a_param, h0, perm_t, seq_len, width, nblk):
    groups = xs.shape[0]
    r8 = LRU_ROWS
    slabs = seq_len // 2
    cw = nblk * LRU_BLOCK
    per_dir = pl.BlockSpec((2, cw), lambda g, c: (0, c))
    state_spec = pl.BlockSpec((LRU_SEQS, 2, cw), lambda g, c: (g, 0, c))
    has_h0 = h0 is not None
    return pl.pallas_call(
        functools.partial(_lru_kernel, slabs=slabs, seq_len=seq_len, width=width, nblk=nblk, has_h0=has_h0),
        grid=(groups, N_LRU_BLOCKS // nblk),
        in_specs=[pl.BlockSpec((1, slabs * r8, cw), lambda g, c: (g, 0, c)),
                  pl.BlockSpec((1, slabs * r8, cw), lambda g, c: (g, 0, c)),
                  pl.BlockSpec((CONV_W, cw), lambda g, c: (0, c)),
                  pl.BlockSpec((1, cw), lambda g, c: (0, c)),
                  pl.BlockSpec((nblk, 2 * LRU_BLOCK, 2 * LRU_BLOCK), lambda g, c: (c, 0, 0)),
                  per_dir, per_dir, per_dir,
                  state_spec if has_h0 else per_dir,
                  pl.BlockSpec((PERM_ROWS, PERM_ROWS), lambda g, c: (0, 0))],
        out_specs=[pl.BlockSpec((1, LRU_SEQS, seq_len, cw), lambda g, c: (g, 0, 0, c)),
                   state_spec],
        out_shape=[jax.ShapeDtypeStruct((groups, LRU_SEQS, seq_len, D_LRU), BF16),
                   jax.ShapeDtypeStruct((groups * LRU_SEQS, 2, D_LRU), F32)],
        scratch_shapes=[pltpu.VMEM(((slabs + 4) * r8, cw), F32),
                        pltpu.VMEM((slabs * r8, cw), F32),
                        pltpu.VMEM((4, PERM_ROWS, cw), F32),
                        pltpu.VMEM((PERM_ROWS, cw), F32),
                        pltpu.VMEM((PERM_ROWS, cw), F32),
                        pltpu.VMEM((PERM_ROWS, cw), F32),
                        pltpu.VMEM((PERM_ROWS, cw), F32)],
        compiler_params=pltpu.CompilerParams(
            dimension_semantics=("parallel", "parallel"), vmem_limit_bytes=VMEM_LIMIT),
        name="lru",
    )(xs, gs, conv_w, conv_b.reshape(1, -1), w_blk, ba, bx, a_param, h0 if has_h0 else ba, perm_t)


def _lru_gate_weights(wa, wx):
    return jnp.concatenate([jnp.concatenate([wa[0], wx[0]], axis=2),
                            jnp.concatenate([wa[1], wx[1]], axis=2)], axis=1).astype(BF16)


def _tail_kernel(x_ref, o_ref, l_ref, mr_ref, ml_ref, gate_ref, fnw_ref, wrd_ref, wld_ref, wo_ref, y_ref):
    y_ref[...] = _tail_math(x_ref[...], o_ref[...], l_ref[...], mr_ref[...], ml_ref[...], gate_ref[0],
                            fnw_ref[...], wrd_ref[...], wld_ref[...], wo_ref[...])


TAIL_TM = 512


def _tail(x2d, o2d, l2d, z2d, mod3, cond_row, fnw, wrd_b, wld_b, wo_b):
    m = x2d.shape[0]
    tm = TAIL_TM
    kmr, kml = MAIN_MRET // D_MODEL, MAIN_MLRU // D_MODEL
    const = lambda i: (0, 0)
    return pl.pallas_call(
        _tail_kernel,
        grid=(m // tm,),
        in_specs=[pl.BlockSpec((tm, D_MODEL), lambda i: (i, 0)),
                  pl.BlockSpec((tm, D_V), lambda i: (i, 0)),
                  pl.BlockSpec((tm, D_LRU), lambda i: (i, 0)),
                  pl.BlockSpec((tm, D_MODEL), lambda i: (i, kmr)),
                  pl.BlockSpec((tm, D_MODEL), lambda i: (i, kml)),
                  pl.BlockSpec((1, 1, D_MODEL), lambda i: (cond_row(i), 0, MOD_GATE)),
                  pl.BlockSpec((1, D_MODEL), const),
                  pl.BlockSpec((D_V, D_MODEL), const),
                  pl.BlockSpec((D_LRU, D_MODEL), const),
                  pl.BlockSpec((D_MODEL, D_MODEL), const)],
        out_specs=pl.BlockSpec((tm, D_MODEL), lambda i: (i, 0)),
        out_shape=jax.ShapeDtypeStruct((m, D_MODEL), F32),
        compiler_params=pltpu.CompilerParams(
            dimension_semantics=("parallel",), vmem_limit_bytes=VMEM_LIMIT),
        name="tail",
    )(x2d, o2d, l2d, z2d, z2d, mod3, fnw.reshape(1, -1), wrd_b, wld_b, wo_b)


def _trunk(x, mod3, cond_row0, n_cond, s0_ret, h0_lru, width, lru_nblk, params, tail_w, final_norm_w,
           emit_state):
    (norm_w, w_all, decay_logit, conv_w, conv_b, w_blk, ba, bx, a_param, perm, perm_t) = params
    b, l, _ = x.shape
    groups = b // LRU_SEQS
    x2d = x.reshape(b * l, D_MODEL)
    tokens_per_cond = b * l // n_cond

    def cond_row(tile_tokens):
        return lambda i: cond_row0 + i // (tokens_per_cond // tile_tokens)

    if len(tail_w) == 4:
        z2d, *tail_w = _inproj(x2d, norm_w, mod3, cond_row(INPROJ_TM), w_all, tail_weights=tail_w)
    else:
        z2d, = _inproj(x2d, norm_w, mod3, cond_row(INPROJ_TM), w_all)
    wrd_b, wld_b, wo_b = tail_w
    z3 = z2d.reshape(b, l, D_MAIN)

    xs, gs = _inproj_lru(x.reshape(groups, LRU_SEQS, l, D_MODEL), norm_w, mod3, cond_row0, n_cond, perm, w_all)
    lru_pre, lru_fin = _lru(xs, gs, conv_w, conv_b, w_blk, ba, bx, a_param, h0_lru, perm_t, l, width, lru_nblk)

    if l == RET_CHUNK:
        ret = _retention(z3, decay_logit, s0_ret, emit_state, N_HEADS,
                         tail=(x, lru_pre.reshape(b, l, D_LRU), mod3, cond_row(l), final_norm_w,
                               wrd_b, wld_b, wo_b))
        y = ret[0]
    else:
        ret = _retention(z3, decay_logit, s0_ret, emit_state, 1)
        y = _tail(x2d, ret[0].reshape(b * l, D_V), lru_pre.reshape(b * l, D_LRU), z2d, mod3, cond_row(TAIL_TM),
                  final_norm_w, wrd_b, wld_b, wo_b).reshape(b, l, D_MODEL)
    return y, (ret[1] if emit_state else None), lru_fin, (wrd_b, wld_b, wo_b)


def kernel(x_prompt, x_sample, state_ret, state_lru, c, c_ctx, norm_w, w_ada, b_ada, w_in, ret_decay_logit,
           ret_gn_w, w_ret_down, conv_w, conv_b, lru_wa, lru_ba, lru_wx, lru_bx, lru_a_param, w_lru_down,
           w_out, final_norm_w):
    assert norm_w.shape[0] == 1, "single-layer step"
    n_dec = c.shape[0]
    cond8 = jnp.concatenate([c.astype(F32), c_ctx.astype(F32)[None],
                             jnp.zeros((8 - n_dec - 1, D_MODEL), F32)], axis=0)
    mod = _ada(cond8, w_ada[0], b_ada[0])
    col = jnp.arange(D_IN)
    col_scale = jnp.where((col >= OFF_GRET) & (col < OFF_XLRU), 0.5, 1.0).astype(F32)
    w_all = (w_in[0] * col_scale[None, :]).astype(BF16)
    perm = jnp.asarray(_slab_permutation(PERM_UNIT), BF16)
    perm_t = jnp.asarray(_slab_permutation(LRU_TC).T, BF16)
    params = (norm_w[0], w_all, ret_decay_logit[0], conv_w[0], conv_b[0],
              _lru_gate_weights(lru_wa[0], lru_wx[0]), lru_ba[0], lru_bx[0], lru_a_param[0], perm, perm_t)
    mod3 = mod
    tail_f32 = (ret_gn_w[0], w_ret_down[0], w_lru_down[0], w_out[0])
    y_prompt, new_ret, new_lru, tail_b = _trunk(x_prompt.astype(F32), mod3, n_dec, 1, None, None,
                                                x_prompt.shape[1], 5, params, tail_f32, final_norm_w, True)
    y_sample, _, _, _ = _trunk(x_sample.astype(F32), mod3, 0, n_dec, state_ret[:, 0], state_lru[:, 0],
                               GRID_W, 2, params, tail_b, final_norm_w, False)
    return (y_prompt.astype(x_prompt.dtype), y_sample.astype(x_sample.dtype),
            new_ret[:, None].astype(state_ret.dtype), new_lru[:, None].astype(state_lru.dtype))
```

```python
import functools

import jax
import jax.numpy as jnp
import numpy as np
from jax import lax
from jax.experimental import pallas as pl
from jax.experimental.pallas import tpu as pltpu

F32 = jnp.float32
BF16 = jnp.bfloat16

D_MODEL = 1024
N_HEADS = 4
DK = 256
DV = 512
D_QK = N_HEADS * DK
D_V = N_HEADS * DV
N_LRU_BLOCKS = 10
LRU_BLOCK = 128
D_LRU = N_LRU_BLOCKS * LRU_BLOCK
LRU_C = 8.0
CONV_W = 4
CONV_LEFT = 2
GRID_W = 64
EPS = 1e-6

OFF_Q = 0
OFF_K = OFF_Q + D_QK
OFF_V = OFF_K + D_QK
OFF_GRET = OFF_V + D_V
OFF_XLRU = OFF_GRET + D_V
OFF_GLRU = OFF_XLRU + D_LRU
OFF_MRET = OFF_GLRU + D_LRU
OFF_MLRU = OFF_MRET + D_MODEL
D_IN = OFF_MLRU + D_MODEL
D_MAIN = OFF_XLRU + 2 * D_MODEL
MAIN_MRET = OFF_XLRU
MAIN_MLRU = OFF_XLRU + D_MODEL

RET_CHUNK = 256
LRU_SEQS = 4
LRU_ROWS = 2 * LRU_SEQS
LRU_TC = 64
PERM_ROWS = LRU_TC * LRU_ROWS
PERM_UNIT = 32
INPROJ_LRU_SLABS = 128
VMEM_LIMIT = 56 * 1024 * 1024


def _sigmoid(x):
    return 0.5 * jnp.tanh(0.5 * x) + 0.5


def _silu(x):
    return x * _sigmoid(x)


def _softplus(x):
    return jnp.maximum(x, 0.0) + jnp.log1p(jnp.exp(-jnp.abs(x)))


def _slab_permutation(tc):
    n = tc * LRU_ROWS
    p = np.zeros((n, n), np.float32)
    for t in range(tc):
        for r in range(LRU_ROWS):
            if r < LRU_SEQS:
                src = r * tc + t
            else:
                src = LRU_SEQS * tc + (r - LRU_SEQS) * tc + (tc - 1 - t)
            p[t * LRU_ROWS + r, src] = 1.0
    return p


def _ada_kernel(c_ref, w_ref, b_ref, o_ref):
    cond = _silu(c_ref[...]).astype(BF16)
    mod = jnp.dot(cond, w_ref[...].astype(BF16), preferred_element_type=F32) + b_ref[...]
    o_ref[...] = mod[:, None, :]


def _ada(cond8, w_ada, b_ada):
    tn = 1536
    return pl.pallas_call(
        _ada_kernel,
        grid=(3 * D_MODEL // tn,),
        in_specs=[pl.BlockSpec((8, D_MODEL), lambda j: (0, 0)),
                  pl.BlockSpec((D_MODEL, tn), lambda j: (0, j)),
                  pl.BlockSpec((1, tn), lambda j: (0, j))],
        out_specs=pl.BlockSpec((8, 1, tn), lambda j: (0, 0, j)),
        out_shape=jax.ShapeDtypeStruct((8, 1, 3 * D_MODEL), F32),
        name="ada",
    )(cond8, w_ada, b_ada.reshape(1, -1))


def _modulated_norm(x, nw, scale, shift):
    ms = jnp.mean(x * x, axis=-1, keepdims=True)
    return (x * lax.rsqrt(ms + EPS)) * (nw * (1.0 + scale)) + shift


INPROJ_TM = 1024
INPROJ_TN = 2048


def _inproj_kernel(x_ref, nw_ref, sc_ref, sh_ref, w_ref, *rest):
    if len(rest) == 2:
        z_ref, h_scr = rest
    else:
        gn_ref, wrd_ref, wld_ref, wo_ref, z_ref, wrd_out, wld_out, wo_out, h_scr = rest
        wrd_out[...] = (gn_ref[...] * wrd_ref[...]).astype(BF16)
        wld_out[...] = wld_ref[...].astype(BF16)
        wo_out[...] = wo_ref[...].astype(BF16)

    @pl.when(pl.program_id(1) == 0)
    def _():
        h_scr[...] = _modulated_norm(x_ref[...], nw_ref[...], sc_ref[0], sh_ref[0]).astype(BF16)

    z_ref[...] = jnp.dot(h_scr[...], w_ref[...], preferred_element_type=F32).astype(z_ref.dtype)


MOD_SHIFT, MOD_SCALE, MOD_GATE = 0, 1, 2


def _inproj(x2d, norm_w, mod3, cond_row, w_all, tail_weights=None):
    m = x2d.shape[0]
    tm, tn = INPROJ_TM, INPROJ_TN
    n_lead = OFF_XLRU // tn
    n_tiles = D_MAIN // tn
    assert OFF_XLRU == n_lead * tn and D_MAIN == n_tiles * tn

    def w_col(j):
        off = j * tn
        for t in range(n_lead, n_tiles):
            off = jnp.where(j == t, OFF_MRET + (t - n_lead) * tn, off)
        return off

    in_specs = [pl.BlockSpec((tm, D_MODEL), lambda i, j: (i, 0)),
                pl.BlockSpec((1, D_MODEL), lambda i, j: (0, 0)),
                pl.BlockSpec((1, 1, D_MODEL), lambda i, j: (cond_row(i), 0, MOD_SCALE)),
                pl.BlockSpec((1, 1, D_MODEL), lambda i, j: (cond_row(i), 0, MOD_SHIFT)),
                pl.BlockSpec((pl.Element(D_MODEL), pl.Element(tn)), lambda i, j: (0, w_col(j)))]
    args = [x2d, norm_w.reshape(1, -1), mod3, mod3, w_all]
    out_specs = [pl.BlockSpec((tm, tn), lambda i, j: (i, j))]
    out_shape = [jax.ShapeDtypeStruct((m, D_MAIN), BF16)]
    if tail_weights is not None:
        gn_w, wrd, wld, wo = tail_weights
        steps = (m // tm) * n_tiles

        def rows_per_step(w):
            assert w.shape[0] % (steps * 16) == 0
            return w.shape[0] // steps

        def step_rows(w, cols):
            return pl.BlockSpec((rows_per_step(w), cols), lambda i, j: (i * n_tiles + j, 0))

        in_specs += [step_rows(wrd, 1), step_rows(wrd, D_MODEL), step_rows(wld, D_MODEL), step_rows(wo, D_MODEL)]
        args += [gn_w.reshape(-1, 1), wrd, wld, wo]
        for w in (wrd, wld, wo):
            out_specs.append(step_rows(w, D_MODEL))
            out_shape.append(jax.ShapeDtypeStruct(w.shape, BF16))
    return pl.pallas_call(
        _inproj_kernel,
        grid=(m // tm, n_tiles),
        in_specs=in_specs,
        out_specs=out_specs,
        out_shape=out_shape,
        scratch_shapes=[pltpu.VMEM((tm, D_MODEL), BF16)],
        compiler_params=pltpu.CompilerParams(
            dimension_semantics=("arbitrary", "arbitrary"), vmem_limit_bytes=VMEM_LIMIT),
        name="inproj",
    )(*args)


def _inproj_lru_kernel(xa_ref, xb_ref, nw_ref, sc_ref, sh_ref, p_ref, w_ref, xs_ref, gs_ref):
    nw = nw_ref[...]
    sc = sc_ref[...]
    sh = sh_ref[...]
    ha = _modulated_norm(xa_ref[0], nw, sc, sh).astype(BF16)
    hb = _modulated_norm(xb_ref[0], nw, sc, sh).astype(BF16)
    units = INPROJ_LRU_SLABS // PERM_UNIT
    pieces = []
    for u in range(units):
        lo, hi = u * PERM_UNIT, (units - 1 - u) * PERM_UNIT
        src = jnp.concatenate([ha[b, lo:lo + PERM_UNIT] for b in range(LRU_SEQS)]
                              + [hb[b, hi:hi + PERM_UNIT] for b in range(LRU_SEQS)], axis=0)
        pieces.append(jnp.dot(p_ref[...], src, preferred_element_type=F32).astype(BF16))
    hp = jnp.concatenate(pieces, axis=0)
    z = jnp.dot(hp, w_ref[...], preferred_element_type=F32)
    xs_ref[0] = z[:, :D_LRU].astype(xs_ref.dtype)
    gs_ref[0] = _silu(z[:, D_LRU:]).astype(gs_ref.dtype)


def _inproj_lru(x4, norm_w, mod3, row0, nrows, perm, w_all):
    groups, _, l, _ = x4.shape
    assert row0 % nrows == 0 and nrows in (1, LRU_SEQS)
    ts = INPROJ_LRU_SLABS
    nt = l // 2 // ts
    ntb = l // ts
    out_sds = jax.ShapeDtypeStruct((groups, l // 2 * LRU_ROWS, D_LRU), BF16)
    return pl.pallas_call(
        _inproj_lru_kernel,
        grid=(groups, nt),
        in_specs=[pl.BlockSpec((1, LRU_SEQS, ts, D_MODEL), lambda g, i: (g, 0, i, 0)),
                  pl.BlockSpec((1, LRU_SEQS, ts, D_MODEL), lambda g, i: (g, 0, ntb - 1 - i, 0)),
                  pl.BlockSpec((1, D_MODEL), lambda g, i: (0, 0)),
                  pl.BlockSpec((nrows, 1, D_MODEL), lambda g, i: (row0 // nrows, 0, MOD_SCALE)),
                  pl.BlockSpec((nrows, 1, D_MODEL), lambda g, i: (row0 // nrows, 0, MOD_SHIFT)),
                  pl.BlockSpec((PERM_UNIT * LRU_ROWS, PERM_UNIT * LRU_ROWS), lambda g, i: (0, 0)),
                  pl.BlockSpec((pl.Element(D_MODEL), pl.Element(2 * D_LRU)), lambda g, i: (0, OFF_XLRU))],
        out_specs=[pl.BlockSpec((1, ts * LRU_ROWS, D_LRU), lambda g, i: (g, i, 0)),
                   pl.BlockSpec((1, ts * LRU_ROWS, D_LRU), lambda g, i: (g, i, 0))],
        out_shape=[out_sds, out_sds],
        compiler_params=pltpu.CompilerParams(
            dimension_semantics=("parallel", "parallel"), vmem_limit_bytes=VMEM_LIMIT),
        name="inproj_lru",
    )(x4, x4, norm_w.reshape(1, -1), mod3, mod3, perm, w_all)


def _dot_tn(a, b):
    return lax.dot_general(a, b, (((0,), (0,)), ((), ())), preferred_element_type=F32)


def _dot_nt(a, b):
    return lax.dot_general(a, b, (((1,), (1,)), ((), ())), preferred_element_type=F32)


def _tail_math(x, o, lru_pre, m_ret, m_lru, gate, fnw, wrd, wld, wo):
    ret_out = jnp.dot(o, wrd, preferred_element_type=F32)
    lru_out = jnp.dot(lru_pre, wld, preferred_element_type=F32)
    merged = _sigmoid(m_ret.astype(F32)) * ret_out + _sigmoid(m_lru.astype(F32)) * lru_out
    out = jnp.dot(merged.astype(BF16), wo, preferred_element_type=F32)
    y = x + gate * out
    ms = jnp.mean(y * y, axis=-1, keepdims=True)
    return y * lax.rsqrt(ms + EPS) * fnw


def _ret_kernel(*refs, nc, hp, has_state, emit_state, fuse_tail):
    dl_ref, q_ref, k_ref, v_ref, hg_ref = refs[:5]
    pos = 5
    s0_ref = None
    if has_state:
        s0_ref = refs[pos]
        pos += 1
    tail_refs = None
    if fuse_tail:
        tail_refs = refs[pos:pos + 9]
        pos += 9
    out_ref = refs[pos]
    pos += 1
    sfin_ref = None
    if emit_state:
        sfin_ref = refs[pos]
        pos += 1
    sf_scr, sb_scr, sbh_scr = refs[pos:pos + 3]
    o_scr = refs[pos + 3] if fuse_tail else None

    c = RET_CHUNK
    carry_states = has_state or nc > 1
    ii = lax.broadcasted_iota(jnp.int32, (c, c), 0)
    jj = lax.broadcasted_iota(jnp.int32, (c, c), 1)
    diff = (ii - jj).astype(F32)
    p = lax.broadcasted_iota(jnp.int32, (c, 1), 0).astype(F32)
    kscale = DK ** -0.5

    def rows_of(n):
        return pl.ds(pl.multiple_of(n * c, c), c)

    def normed_out(o, rows, vcols):
        ms = jnp.mean(o * o, axis=-1, keepdims=True)
        on = o * lax.rsqrt(ms + EPS)
        hg = hg_ref[0, rows, vcols]
        gate = hg * (jnp.tanh(hg) + 1.0)
        if fuse_tail:
            o_scr[rows, vcols] = on.astype(BF16) * gate
        else:
            out_ref[0, rows, vcols] = on.astype(out_ref.dtype) * gate

    for hh in range(hp):
        head = pl.program_id(1) * hp + hh
        qcols = slice(hh * DK, (hh + 1) * DK)
        vcols = slice(hh * DV, (hh + 1) * DV)
        lgf = -_softplus(-jnp.full((1, 1), dl_ref[0, head], F32))
        lgb = -_softplus(-jnp.full((1, 1), dl_ref[1, head], F32))
        decay = jnp.exp(jnp.where(diff >= 0, lgf * diff, -lgb * diff)) * kscale
        kdf = jnp.exp(lgf * (c - 1.0 - p)) * kscale
        kdb = jnp.exp(lgb * p) * kscale

        if not carry_states:
            rows = pl.ds(0, c)
            qn = q_ref[0, rows, qcols]
            kn = k_ref[0, rows, qcols]
            vn = v_ref[0, rows, vcols]
            s = (_dot_nt(qn, kn) * decay).astype(BF16)
            normed_out(jnp.dot(s, vn, preferred_element_type=F32), rows, vcols)
            k32 = kn.astype(F32)
            if emit_state:
                sfin_ref[0, 0, hh] = _dot_tn((k32 * kdf).astype(BF16), vn)
                sfin_ref[0, 1, hh] = _dot_tn((k32 * kdb).astype(BF16), vn)
            continue

        def row_table(col):
            return jnp.broadcast_to(col, (c, DK)).astype(BF16)

        qdf_t = row_table(jnp.exp(lgf * (p + 1.0)))
        qdb_t = row_table(jnp.exp(lgb * (c - p)))
        kdf_t = row_table(kdf)
        kdb_t = row_table(kdb)
        cdf = jnp.exp(lgf * c)
        cdb = jnp.exp(lgb * c)
        if has_state:
            sf_scr[...] = s0_ref[0, 0, hh]
            sb_scr[...] = s0_ref[0, 1, hh]
        else:
            sf_scr[...] = jnp.zeros_like(sf_scr)
            sb_scr[...] = jnp.zeros_like(sb_scr)

        def rev_body(idx, carry):
            n = nc - 1 - idx
            rows = rows_of(n)
            sbh_scr[n] = sb_scr[...].astype(BF16)
            kb = k_ref[0, rows, qcols] * kdb_t
            sb_scr[...] = cdb * sb_scr[...] + _dot_tn(kb, v_ref[0, rows, vcols])
            return carry

        lax.fori_loop(0, nc, rev_body, 0, unroll=True)

        def fwd_body(n, carry):
            rows = rows_of(n)
            qn = q_ref[0, rows, qcols]
            kn = k_ref[0, rows, qcols]
            vn = v_ref[0, rows, vcols]
            s = (_dot_nt(qn, kn) * decay).astype(BF16)
            o = (jnp.dot(s, vn, preferred_element_type=F32)
                 + jnp.dot(qn * qdf_t, sf_scr[...].astype(BF16), preferred_element_type=F32)
                 + jnp.dot(qn * qdb_t, sbh_scr[n], preferred_element_type=F32))
            normed_out(o, rows, vcols)
            sf_scr[...] = cdf * sf_scr[...] + _dot_tn(kn * kdf_t, vn)
            return carry

        lax.fori_loop(0, nc, fwd_body, 0, unroll=True)

        if emit_state:
            sfin_ref[0, 0, hh] = sf_scr[...]
            sfin_ref[0, 1, hh] = sb_scr[...]

    if fuse_tail:
        x_ref, l_ref, mr_ref, ml_ref, gate_ref, fnw_ref, wrd_ref, wld_ref, wo_ref = tail_refs
        out_ref[0] = _tail_math(x_ref[0], o_scr[...], l_ref[0], mr_ref[0], ml_ref[0], gate_ref[0], fnw_ref[...],
                                wrd_ref[...], wld_ref[...], wo_ref[...])


def _retention(z3, decay_logit, s0, emit_state, hp, tail=None):
    b, l, _ = z3.shape
    nc = l // RET_CHUNK
    has_state = s0 is not None
    fuse_tail = tail is not None
    assert not fuse_tail or hp == N_HEADS
    kq, kv_ = OFF_K // (hp * DK), OFF_V // (hp * DV)
    kg = OFF_GRET // (hp * DV)
    in_specs = [pl.BlockSpec(memory_space=pltpu.SMEM),
                pl.BlockSpec((1, l, hp * DK), lambda i, h: (i, 0, h)),
                pl.BlockSpec((1, l, hp * DK), lambda i, h: (i, 0, kq + h)),
                pl.BlockSpec((1, l, hp * DV), lambda i, h: (i, 0, kv_ + h)),
                pl.BlockSpec((1, l, hp * DV), lambda i, h: (i, 0, kg + h))]
    args = [decay_logit, z3, z3, z3, z3]
    if has_state:
        in_specs.append(pl.BlockSpec((1, 2, hp, DK, DV), lambda i, h: (i, 0, h, 0, 0)))
        args.append(s0)
    scratch = [pltpu.VMEM((DK, DV), F32), pltpu.VMEM((DK, DV), F32), pltpu.VMEM((nc, DK, DV), BF16)]
    if fuse_tail:
        x3, l3, mod3, cond_row, fnw, wrd_b, wld_b, wo_b = tail
        const = lambda i, h: (0, 0)
        in_specs += [pl.BlockSpec((1, l, D_MODEL), lambda i, h: (i, 0, 0)),
                     pl.BlockSpec((1, l, D_LRU), lambda i, h: (i, 0, 0)),
                     pl.BlockSpec((1, l, D_MODEL), lambda i, h: (i, 0, MAIN_MRET // D_MODEL)),
                     pl.BlockSpec((1, l, D_MODEL), lambda i, h: (i, 0, MAIN_MLRU // D_MODEL)),
                     pl.BlockSpec((1, 1, D_MODEL), lambda i, h: (cond_row(i), 0, MOD_GATE)),
                     pl.BlockSpec((1, D_MODEL), const),
                     pl.BlockSpec((D_V, D_MODEL), const),
                     pl.BlockSpec((D_LRU, D_MODEL), const),
                     pl.BlockSpec((D_MODEL, D_MODEL), const)]
        args += [x3, l3, z3, z3, mod3, fnw.reshape(1, -1), wrd_b, wld_b, wo_b]
        out_specs = [pl.BlockSpec((1, l, D_MODEL), lambda i, h: (i, 0, 0))]
        out_shape = [jax.ShapeDtypeStruct((b, l, D_MODEL), F32)]
        scratch.append(pltpu.VMEM((l, D_V), BF16))
    else:
        out_specs = [pl.BlockSpec((1, l, hp * DV), lambda i, h: (i, 0, h))]
        out_shape = [jax.ShapeDtypeStruct((b, l, D_V), BF16)]
    if emit_state:
        out_specs.append(pl.BlockSpec((1, 2, hp, DK, DV), lambda i, h: (i, 0, h, 0, 0)))
        out_shape.append(jax.ShapeDtypeStruct((b, 2, N_HEADS, DK, DV), F32))
    return pl.pallas_call(
        functools.partial(_ret_kernel, nc=nc, hp=hp, has_state=has_state, emit_state=emit_state,
                          fuse_tail=fuse_tail),
        grid=(b, N_HEADS // hp),
        in_specs=in_specs,
        out_specs=out_specs,
        out_shape=out_shape,
        scratch_shapes=scratch,
        compiler_params=pltpu.CompilerParams(
            dimension_semantics=("parallel", "parallel"), vmem_limit_bytes=VMEM_LIMIT),
        name="retention_tail" if fuse_tail else "retention",
    )(*args)


def _sqrt_unit(x):
    return x * lax.rsqrt(jnp.maximum(x, 1e-30))


def _lru_kernel(xs_ref, sg_ref, cw_ref, cb_ref, w_ref, ba_ref, bx_ref, ap_ref, h0_ref, pt_ref, out_ref, fin_ref,
                xp_scr, s1_scr, wm_scr, a0_scr, b0_scr, a1_scr, b1_scr, *, slabs, seq_len, width, nblk, has_h0):
    r8 = LRU_ROWS
    tc = LRU_TC
    rows = PERM_ROWS
    cw = nblk * LRU_BLOCK
    nchunks = slabs // tc
    assert width == seq_len or width == tc

    def roll4(v):
        return pltpu.roll(v, LRU_SEQS, axis=0)

    xp_scr[pl.ds(0, 2 * r8), :] = jnp.zeros((2 * r8, cw), F32)
    xp_scr[pl.ds(2 * r8, slabs * r8), :] = xs_ref[0].astype(F32)
    xp_scr[pl.ds((slabs + 2) * r8, r8), :] = roll4(xs_ref[0, pl.ds((slabs - 1) * r8, r8), :].astype(F32))
    xp_scr[pl.ds((slabs + 3) * r8, r8), :] = roll4(xs_ref[0, pl.ds((slabs - 2) * r8, r8), :].astype(F32))

    row_cw = lax.broadcasted_iota(jnp.int32, (rows, cw), 0)
    step_in_chunk = row_cw >> 3
    low_cw = (row_cw & (r8 - 1)) < LRU_SEQS

    def row_pattern(lo, hi):
        n = lo.shape[-1]
        return jnp.where(lax.broadcasted_iota(jnp.int32, (r8, n), 0) < LRU_SEQS, lo, hi)

    def tap_rows(s):
        zero = jnp.zeros((1, cw), F32)
        lo = cw_ref[pl.ds(s + CONV_LEFT, 1), :] if 0 <= s + CONV_LEFT < CONV_W else zero
        hi = cw_ref[pl.ds(CONV_LEFT - s, 1), :] if 0 <= CONV_LEFT - s < CONV_W else zero
        return 0.5 * row_pattern(lo, hi)

    def tap_table(s):
        tap = jnp.broadcast_to(tap_rows(s)[None], (tc, r8, cw)).reshape(rows, cw)
        if width != seq_len:
            tap = jnp.where((step_in_chunk + s >= 0) & (step_in_chunk + s < tc), tap, 0.0)
        return tap

    wm_scr[0] = tap_table(-2) + tap_table(2)
    for s in range(-1, 2):
        wm_scr[s + 2] = tap_table(s)

    ap8 = row_pattern(ap_ref[pl.ds(0, 1), :], ap_ref[pl.ds(1, 1), :])
    c1_lo = (-0.5 * LRU_C / np.log(2.0)) * _softplus(-ap8)
    cb_half = 0.5 * cb_ref[...]

    def half_bias(j):
        cols = slice(j * LRU_BLOCK, (j + 1) * LRU_BLOCK)
        return 0.5 * jnp.concatenate(
            [row_pattern(ba_ref[pl.ds(0, 1), cols], ba_ref[pl.ds(1, 1), cols]),
             row_pattern(bx_ref[pl.ds(0, 1), cols], bx_ref[pl.ds(1, 1), cols])], axis=1)

    bias_lo = [half_bias(j) for j in range(nblk)]
    low_rows = (lax.broadcasted_iota(jnp.int32, (rows, LRU_BLOCK), 0) & (r8 - 1)) < LRU_SEQS
    high_rows = jnp.logical_not(low_rows)

    def conv_half(t0):
        def shifted(s):
            return xp_scr[pl.ds(pl.multiple_of((t0 + s + 2) * r8, r8), rows), :]

        xh = jnp.where(low_cw, shifted(-2), shifted(2)) * wm_scr[0] + cb_half
        for s in range(-1, 2):
            xh = xh + shifted(s) * wm_scr[s + 2]
        xp_scr[pl.ds(pl.multiple_of(t0 * r8, rows), rows), :] = xh
        return xh

    def gates(xh, sweep2, a_scr, b_scr):
        fj = high_rows if sweep2 else low_rows
        c1 = roll4(c1_lo) if sweep2 else c1_lo
        for j in range(nblk):
            cols = slice(j * LRU_BLOCK, (j + 1) * LRU_BLOCK)
            xj = xh[:, cols]
            lhs = jnp.concatenate([jnp.where(fj, xj, 0.0), jnp.where(fj, 0.0, xj)], axis=1)
            pre = jnp.dot(lhs.astype(BF16), w_ref[j], preferred_element_type=F32)
            bj = roll4(bias_lo[j]) if sweep2 else bias_lo[j]
            pre = pre.reshape(tc, r8, 2 * LRU_BLOCK) + bj[None]
            tr = jnp.tanh(pre[:, :, :LRU_BLOCK])
            tg = jnp.tanh(pre[:, :, LRU_BLOCK:])
            c1j = c1[:, cols][None]
            a = jnp.exp2(c1j * tr + c1j)
            bco = _sqrt_unit(1.0 - a * a) * ((tg + 1.0) * xj.reshape(tc, r8, LRU_BLOCK))
            a_scr[:, cols] = a.reshape(rows, LRU_BLOCK)
            b_scr[:, cols] = bco.reshape(rows, LRU_BLOCK)

    bufs = ((a0_scr, b0_scr), (a1_scr, b1_scr))

    def chunk_rows(t0):
        return pl.ds(pl.multiple_of(t0 * r8, rows), rows)

    def scan(buf, h, dst_ref, t0, descending):
        a_ref, b_ref = bufs[buf]
        for i in range(tc):
            t = tc - 1 - i if descending else i
            h = a_ref[t * r8:(t + 1) * r8, :] * h + b_ref[t * r8:(t + 1) * r8, :]
            if dst_ref is None:
                b_ref[t * r8:(t + 1) * r8, :] = h
            else:
                dst_ref[pl.ds(pl.multiple_of(t0 * r8, rows) + t * r8, r8), :] = h
        return h

    def gates1(ci, buf):
        gates(conv_half(ci * tc), False, *bufs[buf])

    def scan1(ci, buf, h):
        return scan(buf, h, s1_scr, ci * tc, False)

    npairs = nchunks // 2
    assert nchunks == 2 * npairs
    gates1(0, 0)

    def sweep1_pair(i, h):
        gates1(2 * i + 1, 1)
        h = scan1(2 * i, 0, h)
        gates1(2 * i + 2, 0)
        return scan1(2 * i + 1, 1, h)

    h_init = jnp.zeros((r8, cw), F32)
    if has_h0:
        row8 = lax.broadcasted_iota(jnp.int32, (r8, cw), 0)
        for s in range(LRU_SEQS):
            for d in range(2):
                state = jnp.broadcast_to(h0_ref[s, d:d + 1, :].astype(F32), (r8, cw))
                h_init = jnp.where(row8 == d * LRU_SEQS + s, state, h_init)
    h = lax.fori_loop(0, npairs - 1, sweep1_pair, h_init)
    gates1(nchunks - 1, 1)
    h = scan1(nchunks - 2, 0, h)
    h = scan1(nchunks - 1, 1, h)

    def gates2(p, buf):
        gates(xp_scr[chunk_rows((nchunks - 1 - p) * tc), :], True, *bufs[buf])

    def scan2(p, buf, h):
        return scan(buf, h, None, 0, True)

    def finish(p, buf):
        t0 = (nchunks - 1 - p) * tc
        crows = chunk_rows(t0)
        o = (s1_scr[crows, :] + bufs[buf][1][...]).astype(BF16) * sg_ref[0, crows, :]
        nat = jnp.dot(pt_ref[...], o, preferred_element_type=F32).astype(out_ref.dtype)
        lo = pl.multiple_of(t0, tc)
        hi = pl.multiple_of(seq_len - tc - t0, tc)
        for b in range(LRU_SEQS):
            out_ref[0, b, pl.ds(lo, tc), :] = nat[b * tc:(b + 1) * tc]
            out_ref[0, b, pl.ds(hi, tc), :] = nat[(LRU_SEQS + b) * tc:(LRU_SEQS + b + 1) * tc]

    h = roll4(h)
    gates2(0, 0)
    gates2(1, 1)
    h = scan2(0, 0, h)

    def sweep2_pair(i, h):
        finish(2 * i, 0)
        gates2(2 * i + 2, 0)
        h = scan2(2 * i + 1, 1, h)
        finish(2 * i + 1, 1)
        gates2(2 * i + 3, 1)
        return scan2(2 * i + 2, 0, h)

    h = lax.fori_loop(0, npairs - 1, sweep2_pair, h)
    finish(nchunks - 2, 0)
    h = scan2(nchunks - 1, 1, h)
    finish(nchunks - 1, 1)
    row8 = lax.broadcasted_iota(jnp.int32, (r8, cw), 0)
    hs = roll4(h)
    for s in range(LRU_SEQS):
        fwd = pltpu.roll(hs, r8 - s, axis=0) if s else hs
        bwd = pltpu.roll(h, (1 - s) % r8, axis=0) if s != 1 else h
        fin_ref[s] = jnp.where(row8 == 0, fwd, bwd)[0:2, :].astype(fin_ref.dtype)


def _lru(xs, gs, conv_w, conv_b, w_blk, ba, bx, a_param, h0, perm_t, seq_len, width, nblk):
    groups = xs.shape[0]
    r8 = LRU_ROWS
    slabs = seq_len // 2
    cw = nblk * LRU_BLOCK
    per_dir = pl.BlockSpec((2, cw), lambda g, c: (0, c))
    state_spec = pl.BlockSpec((LRU_SEQS, 2, cw), lambda g, c: (g, 0, c))
    has_h0 = h0 is not None
    return pl.pallas_call(
        functools.partial(_lru_kernel, slabs=slabs, seq_len=seq_len, width=width, nblk=nblk, has_h0=has_h0),
        grid=(groups, N_LRU_BLOCKS // nblk),
        in_specs=[pl.BlockSpec((1, slabs * r8, cw), lambda g, c: (g, 0, c)),
                  pl.BlockSpec((1, slabs * r8, cw), lambda g, c: (g, 0, c)),
                  pl.BlockSpec((CONV_W, cw), lambda g, c: (0, c)),
                  pl.BlockSpec((1, cw), lambda g, c: (0, c)),
                  pl.BlockSpec((nblk, 2 * LRU_BLOCK, 2 * LRU_BLOCK), lambda g, c: (c, 0, 0)),
                  per_dir, per_dir, per_dir,
                  state_spec if has_h0 else per_dir,
                  pl.BlockSpec((PERM_ROWS, PERM_ROWS), lambda g, c: (0, 0))],
        out_specs=[pl.BlockSpec((1, LRU_SEQS, seq_len, cw), lambda g, c: (g, 0, 0, c)),
                   state_spec],
        out_shape=[jax.ShapeDtypeStruct((groups, LRU_SEQS, seq_len, D_LRU), BF16),
                   jax.ShapeDtypeStruct((groups * LRU_SEQS, 2, D_LRU), F32)],
        scratch_shapes=[pltpu.VMEM(((slabs + 4) * r8, cw), F32),
                        pltpu.VMEM((slabs * r8, cw), F32),
                        pltpu.VMEM((4, PERM_ROWS, cw), F32),
                        pltpu.VMEM((PERM_ROWS, cw), F32),
                        pltpu.VMEM((PERM_ROWS, cw), F32),
                        pltpu.VMEM((PERM_ROWS, cw), F32),
                        pltpu.VMEM((PERM_ROWS, cw), F32)],
        compiler_params=pltpu.CompilerParams(
            dimension_semantics=("parallel", "parallel"), vmem_limit_bytes=VMEM_LIMIT),
        name="lru",
    )(xs, gs, conv_w, conv_b.reshape(1, -1), w_blk, ba, bx, a_param, h0 if has_h0 else ba, perm_t)


def _lru_gate_weights(wa, wx):
    return jnp.concatenate([jnp.concatenate([wa[0], wx[0]], axis=2),
                            jnp.concatenate([wa[1], wx[1]], axis=2)], axis=1).astype(BF16)


def _tail_kernel(x_ref, o_ref, l_ref, mr_ref, ml_ref, gate_ref, fnw_ref, wrd_ref, wld_ref, wo_ref, y_ref):
    y_ref[...] = _tail_math(x_ref[...], o_ref[...], l_ref[...], mr_ref[...], ml_ref[...], gate_ref[0],
                            fnw_ref[...], wrd_ref[...], wld_ref[...], wo_ref[...])


TAIL_TM = 512


def _tail(x2d, o2d, l2d, z2d, mod3, cond_row, fnw, wrd_b, wld_b, wo_b):
    m = x2d.shape[0]
    tm = TAIL_TM
    kmr, kml = MAIN_MRET // D_MODEL, MAIN_MLRU // D_MODEL
    const = lambda i: (0, 0)
    return pl.pallas_call(
        _tail_kernel,
        grid=(m // tm,),
        in_specs=[pl.BlockSpec((tm, D_MODEL), lambda i: (i, 0)),
                  pl.BlockSpec((tm, D_V), lambda i: (i, 0)),
                  pl.BlockSpec((tm, D_LRU), lambda i: (i, 0)),
                  pl.BlockSpec((tm, D_MODEL), lambda i: (i, kmr)),
                  pl.BlockSpec((tm, D_MODEL), lambda i: (i, kml)),
                  pl.BlockSpec((1, 1, D_MODEL), lambda i: (cond_row(i), 0, MOD_GATE)),
                  pl.BlockSpec((1, D_MODEL), const),
                  pl.BlockSpec((D_V, D_MODEL), const),
                  pl.BlockSpec((D_LRU, D_MODEL), const),
                  pl.BlockSpec((D_MODEL, D_MODEL), const)],
        out_specs=pl.BlockSpec((tm, D_MODEL), lambda i: (i, 0)),
        out_shape=jax.ShapeDtypeStruct((m, D_MODEL), F32),
        compiler_params=pltpu.CompilerParams(
            dimension_semantics=("parallel",), vmem_limit_bytes=VMEM_LIMIT),
        name="tail",
    )(x2d, o2d, l2d, z2d, z2d, mod3, fnw.reshape(1, -1), wrd_b, wld_b, wo_b)


def _trunk(x, mod3, cond_row0, n_cond, s0_ret, h0_lru, width, lru_nblk, params, tail_w, final_norm_w,
           emit_state):
    (norm_w, w_all, decay_logit, conv_w, conv_b, w_blk, ba, bx, a_param, perm, perm_t) = params
    b, l, _ = x.shape
    groups = b // LRU_SEQS
    x2d = x.reshape(b * l, D_MODEL)
    tokens_per_cond = b * l // n_cond

    def cond_row(tile_tokens):
        return lambda i: cond_row0 + i // (tokens_per_cond // tile_tokens)

    if len(tail_w) == 4:
        z2d, *tail_w = _inproj(x2d, norm_w, mod3, cond_row(INPROJ_TM), w_all, tail_weights=tail_w)
    else:
        z2d, = _inproj(x2d, norm_w, mod3, cond_row(INPROJ_TM), w_all)
    wrd_b, wld_b, wo_b = tail_w
    z3 = z2d.reshape(b, l, D_MAIN)

    xs, gs = _inproj_lru(x.reshape(groups, LRU_SEQS, l, D_MODEL), norm_w, mod3, cond_row0, n_cond, perm, w_all)
    lru_pre, lru_fin = _lru(xs, gs, conv_w, conv_b, w_blk, ba, bx, a_param, h0_lru, perm_t, l, width, lru_nblk)

    if l == RET_CHUNK:
        ret = _retention(z3, decay_logit, s0_ret, emit_state, N_HEADS,
                         tail=(x, lru_pre.reshape(b, l, D_LRU), mod3, cond_row(l), final_norm_w,
                               wrd_b, wld_b, wo_b))
        y = ret[0]
    else:
        ret = _retention(z3, decay_logit, s0_ret, emit_state, 1)
        y = _tail(x2d, ret[0].reshape(b * l, D_V), lru_pre.reshape(b * l, D_LRU), z2d, mod3, cond_row(TAIL_TM),
                  final_norm_w, wrd_b, wld_b, wo_b).reshape(b, l, D_MODEL)
    return y, (ret[1] if emit_state else None), lru_fin, (wrd_b, wld_b, wo_b)


def kernel(x_prompt, x_sample, state_ret, state_lru, c, c_ctx, norm_w, w_ada, b_ada, w_in, ret_decay_logit,
           ret_gn_w, w_ret_down, conv_w, conv_b, lru_wa, lru_ba, lru_wx, lru_bx, lru_a_param, w_lru_down,
           w_out, final_norm_w):
    assert norm_w.shape[0] == 1, "single-layer step"
    n_dec = c.shape[0]
    cond8 = jnp.concatenate([c.astype(F32), c_ctx.astype(F32)[None],
                             jnp.zeros((8 - n_dec - 1, D_MODEL), F32)], axis=0)
    mod = _ada(cond8, w_ada[0], b_ada[0])
    col = jnp.arange(D_IN)
    col_scale = jnp.where((col >= OFF_GRET) & (col < OFF_XLRU), 0.5, 1.0).astype(F32)
    w_all = (w_in[0] * col_scale[None, :]).astype(BF16)
    perm = jnp.asarray(_slab_permutation(PERM_UNIT), BF16)
    perm_t = jnp.asarray(_slab_permutation(LRU_TC).T, BF16)
    params = (norm_w[0], w_all, ret_decay_logit[0], conv_w[0], conv_b[0],
              _lru_gate_weights(lru_wa[0], lru_wx[0]), lru_ba[0], lru_bx[0], lru_a_param[0], perm, perm_t)
    mod3 = mod
    tail_f32 = (ret_gn_w[0], w_ret_down[0], w_lru_down[0], w_out[0])
    y_prompt, new_ret, new_lru, tail_b = _trunk(x_prompt.astype(F32), mod3, n_dec, 1, None, None,
                                                x_prompt.shape[1], 5, params, tail_f32, final_norm_w, True)
    y_sample, _, _, _ = _trunk(x_sample.astype(F32), mod3, 0, n_dec, state_ret[:, 0], state_lru[:, 0],
                               GRID_W, 2, params, tail_b, final_norm_w, False)
    return (y_prompt.astype(x_prompt.dtype), y_sample.astype(x_sample.dtype),
            new_ret[:, None].astype(state_ret.dtype), new_lru[:, None].astype(state_lru.dtype))
```
